```python
import jax, jax.numpy as jnp
from jax import lax
import numpy as np

D_MODEL = 1024
BATCH = 4
SEQ = 8192
DEPTH = 4

CHUNK = 64
HEAD_DIM = 64
RWKV_WIDTH = D_MODEL // 2
RWKV_HEADS = RWKV_WIDTH // HEAD_DIM
RWKV_LORA_W = 64
RWKV_LORA_A = 64
RWKV_LORA_G = 128
DSA_WIDTH = D_MODEL - RWKV_WIDTH
DSA_HEADS = DSA_WIDTH // HEAD_DIM
IDX_HEADS = 4
IDX_DIM = 64
INDEX_TOPK = 256
Q_BLOCK = 128
ROPE_THETA = 500000.0
ROPE_DIM = HEAD_DIM // 4
POOL_WINDOWS = (2, 4, 8, 16)
POOL_GROUPS = len(POOL_WINDOWS)
POOL_WIDTH = D_MODEL // 2
POOL_GROUP_DIM = POOL_WIDTH // POOL_GROUPS
SG_WIDTH = D_MODEL - POOL_WIDTH
SG_GROUPS = 4
SG_GROUP_DIM = SG_WIDTH // SG_GROUPS
SG_CHUNK = 128
D_FF = 2816
CONV_WIDTH = 3
N_EVEN = (DEPTH + 1) // 2
N_ODD = DEPTH // 2
ALPHA = (2.0 * DEPTH) ** 0.25
BETA = (8.0 * DEPTH) ** -0.25
LN_EPS = 1e-5
GN_EPS = 64e-5
NEG_INF = -1e30
RWKV_SPLITS = (RWKV_WIDTH, RWKV_WIDTH, RWKV_WIDTH, RWKV_LORA_W, RWKV_LORA_A, RWKV_LORA_G)
DSA_SPLITS = (DSA_WIDTH, DSA_WIDTH, DSA_WIDTH, IDX_HEADS * IDX_DIM, IDX_DIM, IDX_HEADS)
RWKV_COLS = sum(RWKV_SPLITS)
DSA_COLS = sum(DSA_SPLITS)
EVEN_COLS = RWKV_COLS + DSA_COLS
ODD_COLS = POOL_WIDTH + 2 * SG_WIDTH

kernel_name = 'hybrid_rwkv7_dsa_pool_sgmlp_trunk'


def _split(p, sizes):
    cuts = [int(s) for s in np.cumsum(sizes)[:-1]]
    return jnp.split(p, cuts, axis=-1)


def _layer_norm(x, g, b, eps=LN_EPS):
    xf = x.astype(jnp.float32)
    mu = xf.mean(-1, keepdims=True)
    var = jnp.square(xf - mu).mean(-1, keepdims=True)
    return ((xf - mu) * lax.rsqrt(var + eps)).astype(x.dtype) * g + b


def _shift_prev(x):
    return jnp.pad(x, ((0, 0), (1, 0), (0, 0)))[:, :-1]


def _rope_tables(seq_len):
    inv = ROPE_THETA ** (-jnp.arange(0, ROPE_DIM, 2, dtype=jnp.float32) / ROPE_DIM)
    ang = jnp.arange(seq_len, dtype=jnp.float32)[:, None] * inv[None, :]
    return jnp.cos(ang), jnp.sin(ang)


def _partial_rope(x, cos, sin):
    half = ROPE_DIM // 2
    c = cos[None, :, None, :].astype(x.dtype)
    s = sin[None, :, None, :].astype(x.dtype)
    x1, x2, xp = x[..., :half], x[..., half:ROPE_DIM], x[..., ROPE_DIM:]
    return jnp.concatenate([x1 * c - x2 * s, x1 * s + x2 * c, xp], axis=-1)


def _rwkv7_mix(p, mu, w0, w2, a0, a2, g2, k_k, k_a, r_k, gn_g, gn_b):
    bsz, seq, _ = p.shape
    dt = p.dtype
    f32 = jnp.float32
    p = p + (_shift_prev(p) - p) * mu
    r, k, v, wd, ad, gd = _split(p, RWKV_SPLITS)
    w_log = -jax.nn.softplus(-(w0 + jnp.tanh(wd) @ w2)) - 0.5
    decay = jnp.exp(-jnp.exp(w_log.astype(f32)))
    a = jax.nn.sigmoid(a0 + ad @ a2)
    g = jax.nn.sigmoid(gd) @ g2
    hd = lambda t: t.astype(f32).reshape(bsz, seq, RWKV_HEADS, HEAD_DIM)
    kk = hd(k * k_k)
    kk = kk / jnp.maximum(jnp.sqrt(jnp.sum(kk * kk, -1, keepdims=True)), 1e-12)
    k = k * (1 + (a - 1) * k_a)
    r_h, k_h, v_h, a_h, w_h = hd(r), hd(k), hd(v), hd(a), hd(decay)
    xs = tuple(jnp.moveaxis(t, 1, 0) for t in (r_h, w_h, k_h, v_h, kk, a_h))

    def step(S, inp):
        r_t, w_t, k_t, v_t, kk_t, a_t = inp
        sa = jnp.einsum('bhij,bhj->bhi', S, -kk_t)
        S = S * w_t[:, :, None, :] + sa[..., None] * (kk_t * a_t)[:, :, None, :] + v_t[..., None] * k_t[:, :, None, :]
        return S, jnp.einsum('bhij,bhj->bhi', S, r_t)

    S0 = jnp.zeros((bsz, RWKV_HEADS, HEAD_DIM, HEAD_DIM), f32)
    _, o = lax.scan(step, S0, xs)
    o = jnp.moveaxis(o, 0, 1)
    mean = o.mean(-1, keepdims=True)
    var = jnp.square(o - mean).mean(-1, keepdims=True)
    o = ((o - mean) * lax.rsqrt(var + GN_EPS)).reshape(bsz, seq, RWKV_WIDTH) * gn_g + gn_b
    bonus = (jnp.sum(r_h * k_h * r_k, -1, keepdims=True) * v_h).reshape(bsz, seq, RWKV_WIDTH)
    return ((o + bonus) * g).astype(dt)


def _dsa_mix(p, cos, sin, ik_g, ik_b):
    bsz, seq, _ = p.shape
    dt = p.dtype
    f32 = jnp.float32
    q, k, v, qi, ki, wi = _split(p, DSA_SPLITS)
    hd = lambda t: t.reshape(bsz, seq, DSA_HEADS, HEAD_DIM)
    q = _partial_rope(hd(q), cos, sin)
    k = _partial_rope(hd(k), cos, sin)
    v = hd(v)
    qi = _partial_rope(qi.reshape(bsz, seq, IDX_HEADS, IDX_DIM), cos, sin)
    ki = _partial_rope(_layer_norm(ki, ik_g, ik_b)[:, :, None, :], cos, sin)[:, :, 0, :]
    wi = wi * (IDX_HEADS ** -0.5 * IDX_DIM ** -0.5)
    topk = min(INDEX_TOPK, seq // 4)
    nb = seq // Q_BLOCK
    key_chunk = jnp.arange(seq) // CHUNK
    scale = HEAD_DIM ** -0.5
    ki32 = ki.astype(f32)

    def block(args):
        qb, qib, wib, start = args
        t_chunk = (start + jnp.arange(Q_BLOCK)) // CHUNK
        admissible = key_chunk[None, :] <= t_chunk[:, None]
        s = jax.nn.relu(jnp.einsum('bqhd,bsd->bqhs', qib.astype(f32), ki32))
        score = jnp.einsum('bqh,bqhs->bqs', wib.astype(f32), s)
        score = jnp.where(admissible[None], score, NEG_INF)
        _, idx = lax.top_k(score, topk)
        valid = (idx // CHUNK) <= t_chunk[None, :, None]
        k_sel = jax.vmap(lambda kb, ib: kb[ib])(k, idx)
        v_sel = jax.vmap(lambda vb, ib: vb[ib])(v, idx)
        logits = jnp.einsum('bqhd,bqkhd->bhqk', qb, k_sel).astype(f32) * scale
        logits = jnp.where(valid[:, None], logits, NEG_INF)
        prob = jax.nn.softmax(logits, axis=-1).astype(dt)
        return jnp.einsum('bhqk,bqkhd->bqhd', prob, v_sel)

    blk = lambda t: jnp.moveaxis(t.reshape((bsz, nb, Q_BLOCK) + t.shape[2:]), 1, 0)
    starts = jnp.arange(nb, dtype=jnp.int32) * Q_BLOCK
    out = lax.map(block, (blk(q), blk(qi), blk(wi), starts))
    return jnp.moveaxis(out, 0, 1).reshape(bsz, seq, DSA_WIDTH)


def _pool_mix(xc, pool_w, pool_scale):
    bsz, seq, _ = xc.shape
    xg = xc.reshape(bsz, seq, POOL_GROUPS, POOL_GROUP_DIM)
    cs = jnp.pad(jnp.cumsum(xg.astype(jnp.float32), axis=1), ((0, 0), (1, 0), (0, 0), (0, 0)))
    t_idx = jnp.arange(seq, dtype=jnp.float32)
    pooled = []
    for gi, w in enumerate(POOL_WINDOWS):
        csg = cs[:, :, gi]
        upper = csg[:, 1:]
        lower = jnp.pad(csg, ((0, 0), (w - 1, 0), (0, 0)))[:, :seq]
        count = jnp.minimum(t_idx + 1.0, float(w))[None, :, None]
        pooled.append((upper - lower) / count)
    pooled = jnp.stack(pooled, axis=2).astype(xc.dtype) - xg
    y = jnp.einsum('btgc,gcd->btgd', pooled, pool_w).reshape(bsz, seq, POOL_WIDTH)
    return y * pool_scale


def _sgu_mix(u, v, ln_g, ln_b, w_s, b_s):
    bsz, seq, _ = u.shape
    u = jax.nn.gelu(u, approximate=False)
    v = _layer_norm(jax.nn.gelu(v, approximate=False), ln_g, ln_b)
    vb = v.reshape(bsz, seq // SG_CHUNK, SG_CHUNK, SG_GROUPS, SG_GROUP_DIM)
    ws = w_s * jnp.tril(jnp.ones((SG_CHUNK, SG_CHUNK), w_s.dtype))[None]
    z = jnp.einsum('gts,bnsgc->bntgc', ws, vb) + b_s.T[:, :, None]
    return u * z.reshape(bsz, seq, SG_WIDTH)


def _conv_ffn(h, w_up, conv_w, conv_b, w_down):
    seq = h.shape[1]
    u = h @ w_up
    up = jnp.pad(u, ((0, 0), (CONV_WIDTH - 1, 0), (0, 0)))
    u = conv_b + sum(up[:, i:i + seq] * conv_w[i] for i in range(CONV_WIDTH))
    gate, val = jnp.split(u, 2, axis=-1)
    return (jax.nn.silu(gate) * val) @ w_down


def setup_inputs(seed: int = 0) -> dict:
    key = jax.random.key(seed)
    ks = iter(jax.random.split(key, 48))
    nrm = lambda shape, s: jax.random.normal(next(ks), shape, jnp.float32) * s
    uni = lambda shape, lo, hi: jax.random.uniform(next(ks), shape, jnp.float32, lo, hi)
    E, O, L, D = N_EVEN, N_ODD, DEPTH, D_MODEL
    return {
        'x': nrm((BATCH, SEQ, D), 1.0),
        'c': nrm((BATCH, D), 1.0),
        'ada_w': nrm((L, D, 6 * D), 0.5 * D ** -0.5),
        'ada_b': nrm((L, 6 * D), 0.01),
        'ln_g': 1.0 + nrm((L, 2, D), 0.02),
        'ln_b': nrm((L, 2, D), 0.02),
        'ffn_w_up': nrm((L, D, 2 * D_FF), D ** -0.5),
        'ffn_conv_w': nrm((L, CONV_WIDTH, 2 * D_FF), CONV_WIDTH ** -0.5),
        'ffn_conv_b': nrm((L, 2 * D_FF), 0.02),
        'ffn_w_down': nrm((L, D_FF, D), BETA * D_FF ** -0.5),
        'ev_w_in': nrm((E, D, EVEN_COLS), D ** -0.5),
        'ev_w_out': nrm((E, D, D), BETA * D ** -0.5),
        'rw_mu': uni((E, RWKV_COLS), 0.0, 1.0),
        'rw_w0': uni((E, RWKV_WIDTH), -5.0, 1.0),
        'rw_w2': nrm((E, RWKV_LORA_W, RWKV_WIDTH), 0.5 * RWKV_LORA_W ** -0.5),
        'rw_a0': nrm((E, RWKV_WIDTH), 0.1),
        'rw_a2': nrm((E, RWKV_LORA_A, RWKV_WIDTH), RWKV_LORA_A ** -0.5),
        'rw_g2': nrm((E, RWKV_LORA_G, RWKV_WIDTH), RWKV_LORA_G ** -0.5),
        'rw_k_k': 0.85 + nrm((E, RWKV_WIDTH), 0.05),
        'rw_k_a': 1.0 + nrm((E, RWKV_WIDTH), 0.05),
        'rw_r_k': nrm((E, RWKV_HEADS, HEAD_DIM), 0.1),
        'rw_gn_g': 1.0 + nrm((E, RWKV_WIDTH), 0.02),
        'rw_gn_b': nrm((E, RWKV_WIDTH), 0.02),
        'ik_ln_g': 1.0 + nrm((E, IDX_DIM), 0.02),
        'ik_ln_b': nrm((E, IDX_DIM), 0.02),
        'od_w_in': nrm((O, D, ODD_COLS), D ** -0.5),
        'od_w_out': nrm((O, D, D), BETA * D ** -0.5),
        'pool_w': nrm((O, POOL_GROUPS, POOL_GROUP_DIM, POOL_GROUP_DIM), POOL_GROUP_DIM ** -0.5),
        'pool_scale': 1.0 + nrm((O, POOL_WIDTH), 0.1),
        'sg_ln_g': 1.0 + nrm((O, SG_WIDTH), 0.02),
        'sg_ln_b': nrm((O, SG_WIDTH), 0.02),
        'sg_w': nrm((O, SG_GROUPS, SG_CHUNK, SG_CHUNK), 0.5 * SG_CHUNK ** -0.5),
        'sg_b': 1.0 + nrm((O, SG_GROUPS, SG_CHUNK), 0.02),
    }


def reference(x, c, ada_w, ada_b, ln_g, ln_b, ffn_w_up, ffn_conv_w, ffn_conv_b, ffn_w_down,
              ev_w_in, ev_w_out, rw_mu, rw_w0, rw_w2, rw_a0, rw_a2, rw_g2, rw_k_k, rw_k_a, rw_r_k,
              rw_gn_g, rw_gn_b, ik_ln_g, ik_ln_b,
              od_w_in, od_w_out, pool_w, pool_scale, sg_ln_g, sg_ln_b, sg_w, sg_b):
    seq = x.shape[1]
    cos, sin = _rope_tables(seq)
    c_act = jax.nn.silu(c)
    for layer in range(DEPTH):
        mod = c_act @ ada_w[layer] + ada_b[layer]
        sh1, sc1, g1, sh2, sc2, g2 = [m[:, None, :] for m in jnp.split(mod, 6, axis=-1)]
        h = x * (1 + sc1) + sh1
        if layer % 2 == 0:
            e = layer // 2
            p = h @ ev_w_in[e]
            ya = _rwkv7_mix(p[..., :RWKV_COLS], rw_mu[e], rw_w0[e], rw_w2[e], rw_a0[e], rw_a2[e],
                            rw_g2[e], rw_k_k[e], rw_k_a[e], rw_r_k[e], rw_gn_g[e], rw_gn_b[e])
            yb = _dsa_mix(p[..., RWKV_COLS:], cos, sin, ik_ln_g[e], ik_ln_b[e])
            y = jnp.concatenate([ya, yb], axis=-1) @ ev_w_out[e]
        else:
            o = layer // 2
            p = h @ od_w_in[o]
            yc = _pool_mix(p[..., :POOL_WIDTH], pool_w[o], pool_scale[o])
            yd = _sgu_mix(p[..., POOL_WIDTH:POOL_WIDTH + SG_WIDTH], p[..., POOL_WIDTH + SG_WIDTH:],
                          sg_ln_g[o], sg_ln_b[o], sg_w[o], sg_b[o])
            y = jnp.concatenate([yc, yd], axis=-1) @ od_w_out[o]
        x = _layer_norm(ALPHA * x + g1 * y, ln_g[layer, 0], ln_b[layer, 0])
        h = x * (1 + sc2) + sh2
        y = _conv_ffn(h, ffn_w_up[layer], ffn_conv_w[layer], ffn_conv_b[layer], ffn_w_down[layer])
        x = _layer_norm(ALPHA * x + g2 * y, ln_g[layer, 1], ln_b[layer, 1])
    return x
```

```python
import functools

import numpy as np
import jax
import jax.numpy as jnp
from jax import lax
from jax.experimental import pallas as pl
from jax.experimental.pallas import tpu as pltpu

F32 = jnp.float32
BF16 = jnp.bfloat16
I32 = jnp.int32

D_MODEL = 1024
DEPTH = 4
CHUNK = 64
HEAD_DIM = 64
RWKV_WIDTH = D_MODEL // 2
RWKV_HEADS = RWKV_WIDTH // HEAD_DIM
RWKV_LORA_W = 64
RWKV_LORA_A = 64
RWKV_LORA_G = 128
RWKV_COLS = 3 * RWKV_WIDTH + RWKV_LORA_W + RWKV_LORA_A + RWKV_LORA_G
DSA_WIDTH = D_MODEL - RWKV_WIDTH
DSA_HEADS = DSA_WIDTH // HEAD_DIM
IDX_HEADS = 4
IDX_DIM = 64
INDEX_TOPK = 256
Q_BLOCK = 128
ROPE_THETA = 500000.0
ROPE_DIM = HEAD_DIM // 4
POOL_WINDOWS = (2, 4, 8, 16)
POOL_WIDTH = D_MODEL // 2
POOL_GROUP_DIM = POOL_WIDTH // len(POOL_WINDOWS)
SG_WIDTH = D_MODEL - POOL_WIDTH
SG_GROUPS = 4
SG_GROUP_DIM = SG_WIDTH // SG_GROUPS
SG_CHUNK = 128
D_FF = 2816
ALPHA = (2.0 * DEPTH) ** 0.25
LN_EPS = 1e-5
GN_EPS = 64e-5
NEG_INF = -1e30
INT_MIN = -(2 ** 31)

LANES = 128
SUBLANES = 8
VMEM_LIMIT = 56 * 1024 * 1024

RW_TILE = 128
RW_CHUNK = 16
KEY_TILE = 512
PAD_IDX = 384


def _dot(a, b):
    return jnp.dot(a.astype(BF16), b.astype(BF16), preferred_element_type=F32)


def _dot_nt(a, b):
    return lax.dot_general(a.astype(BF16), b.astype(BF16), (((1,), (1,)), ((), ())),
                           preferred_element_type=F32)


def _dot_split(x, m01, terms):
    acc = None
    rem = x
    for _ in range(terms):
        piece = rem.astype(BF16)
        rem = rem - piece.astype(F32)
        part = jnp.dot(piece, m01, preferred_element_type=F32)
        acc = part if acc is None else acc + part
    return acc


def _dot_split_left(m01, x, terms):
    acc = None
    rem = x
    for _ in range(terms):
        piece = rem.astype(BF16)
        rem = rem - piece.astype(F32)
        part = jnp.dot(m01, piece, preferred_element_type=F32)
        acc = part if acc is None else acc + part
    return acc


def _ln(x, g, b, eps):
    mu = jnp.mean(x, axis=-1, keepdims=True)
    xc = x - mu
    var = jnp.mean(xc * xc, axis=-1, keepdims=True)
    return xc * lax.rsqrt(var + eps) * g + b


def _gelu(x):
    return 0.5 * x * (1.0 + lax.erf(x * 0.7071067811865476))


def _params(sem):
    return pltpu.CompilerParams(dimension_semantics=sem, vmem_limit_bytes=VMEM_LIMIT)


def _mod_kernel(c_ref, w_ref, b_ref, o_ref):
    c = c_ref[...]
    ca = c * jax.nn.sigmoid(c)
    o_ref[0] = jnp.dot(ca, w_ref[0], preferred_element_type=F32,
                       precision=lax.Precision.HIGHEST) + b_ref[0]


def _modulation(c, ada_w, ada_b):
    bsz, d = c.shape
    depth = ada_w.shape[0]
    n = ada_w.shape[2]
    tn = 1536
    rows = -(-bsz // SUBLANES) * SUBLANES
    c8 = jnp.pad(c, ((0, rows - bsz), (0, 0)))
    out = pl.pallas_call(
        _mod_kernel,
        grid=(depth, n // tn),
        in_specs=[pl.BlockSpec((rows, d), lambda l, j: (0, 0)),
                  pl.BlockSpec((1, d, tn), lambda l, j: (l, 0, j)),
                  pl.BlockSpec((1, 1, tn), lambda l, j: (l, 0, j))],
        out_specs=pl.BlockSpec((1, rows, tn), lambda l, j: (l, 0, j)),
        out_shape=jax.ShapeDtypeStruct((depth, rows, n), F32),
        compiler_params=_params(("arbitrary", "arbitrary")),
        name="adaln_mod",
    )(c8, ada_w, ada_b.reshape(depth, 1, n))
    return out[:, :bsz].reshape(depth, bsz, 6, d)


def _rope(x, cos_t, sin_a, sin_b):
    n = x.shape[1] // LANES
    rep = (lambda t: jnp.concatenate([t] * n, axis=1)) if n > 1 else (lambda t: t)
    width = x.shape[1]
    half = ROPE_DIM // 2
    return (x * rep(cos_t) + pltpu.roll(x, width - half, 1) * rep(sin_a)
            + pltpu.roll(x, half, 1) * rep(sin_b))


def _even_in_kernel(x_ref, mod_ref, wr_ref, wqkv_ref, widx_ref, cos_ref, sa_ref, sb_ref,
                    ikg_ref, ikb_ref,
                    pr_ref, q_ref, kt_ref, v_ref, qi_ref, kit_ref, wi_ref):
    m = mod_ref[0]
    h = (x_ref[0] * (1.0 + m[1:2]) + m[0:1]).astype(BF16)
    pr_ref[0] = jnp.dot(h, wr_ref[...], preferred_element_type=F32)
    qkv = jnp.dot(h, wqkv_ref[...], preferred_element_type=F32)
    cos_t = cos_ref[...]
    sin_a = sa_ref[...]
    sin_b = sb_ref[...]
    w = DSA_WIDTH
    q = _rope(qkv[:, :w], cos_t, sin_a, sin_b) * (HEAD_DIM ** -0.5)
    k = _rope(qkv[:, w:2 * w], cos_t, sin_a, sin_b)
    q_ref[0] = q.astype(BF16)
    kt_ref[0] = k.T.astype(BF16)
    v_ref[0] = qkv[:, 2 * w:].astype(BF16)
    idx = jnp.dot(h, widx_ref[...], preferred_element_type=F32)
    nq = IDX_HEADS * IDX_DIM
    qi_ref[0] = _rope(idx[:, :nq], cos_t, sin_a, sin_b).astype(BF16)
    blk = idx[:, nq:nq + LANES]
    lane = lax.broadcasted_iota(I32, blk.shape, 1)
    is_k = lane < IDX_DIM
    mu = jnp.sum(jnp.where(is_k, blk, 0.0), axis=1, keepdims=True) * (1.0 / IDX_DIM)
    xc = jnp.where(is_k, blk - mu, 0.0)
    var = jnp.sum(xc * xc, axis=1, keepdims=True) * (1.0 / IDX_DIM)
    kin = xc * lax.rsqrt(var + LN_EPS) * ikg_ref[...] + ikb_ref[...]
    kir = _rope(kin, cos_t, sin_a, sin_b)
    kit_ref[0] = kir.T[:IDX_DIM].astype(BF16)
    wi_ref[0] = blk * (IDX_HEADS ** -0.5 * IDX_DIM ** -0.5)


def _even_in_proj(x, mod, w_in, ik_g, ik_b, tables):
    bsz, seq, d = x.shape
    tm = 256
    w_r = w_in[:, :RWKV_COLS].astype(BF16)
    w_qkv = w_in[:, RWKV_COLS:RWKV_COLS + 3 * DSA_WIDTH].astype(BF16)
    w_idx = w_in[:, RWKV_COLS + 3 * DSA_WIDTH:]
    w_idx = jnp.pad(w_idx, ((0, 0), (0, PAD_IDX - w_idx.shape[1]))).astype(BF16)
    pad = LANES - IDX_DIM
    ikg = jnp.pad(ik_g, (0, pad)).reshape(1, LANES)
    ikb = jnp.pad(ik_b, (0, pad)).reshape(1, LANES)
    cos_t, sin_a, sin_b = tables
    full = lambda shape: pl.BlockSpec(shape, lambda b, i: (0,) * len(shape))
    tab = pl.BlockSpec((tm, LANES), lambda b, i: (i, 0))
    nq = IDX_HEADS * IDX_DIM
    return pl.pallas_call(
        _even_in_kernel,
        grid=(bsz, seq // tm),
        in_specs=[pl.BlockSpec((1, tm, d), lambda b, i: (b, i, 0)),
                  pl.BlockSpec((1, 6, d), lambda b, i: (b, 0, 0)),
                  full(w_r.shape), full(w_qkv.shape), full(w_idx.shape),
                  tab, tab, tab, full((1, LANES)), full((1, LANES))],
        out_specs=[pl.BlockSpec((1, tm, RWKV_COLS), lambda b, i: (b, i, 0)),
                   pl.BlockSpec((1, tm, DSA_WIDTH), lambda b, i: (b, i, 0)),
                   pl.BlockSpec((1, DSA_WIDTH, tm), lambda b, i: (b, 0, i)),
                   pl.BlockSpec((1, tm, DSA_WIDTH), lambda b, i: (b, i, 0)),
                   pl.BlockSpec((1, tm, nq), lambda b, i: (b, i, 0)),
                   pl.BlockSpec((1, IDX_DIM, tm), lambda b, i: (b, 0, i)),
                   pl.BlockSpec((1, tm, LANES), lambda b, i: (b, i, 0))],
        out_shape=[jax.ShapeDtypeStruct((bsz, seq, RWKV_COLS), F32),
                   jax.ShapeDtypeStruct((bsz, seq, DSA_WIDTH), BF16),
                   jax.ShapeDtypeStruct((bsz, DSA_WIDTH, seq), BF16),
                   jax.ShapeDtypeStruct((bsz, seq, DSA_WIDTH), BF16),
                   jax.ShapeDtypeStruct((bsz, seq, nq), BF16),
                   jax.ShapeDtypeStruct((bsz, IDX_DIM, seq), BF16),
                   jax.ShapeDtypeStruct((bsz, seq, LANES), F32)],
        compiler_params=_params(("parallel", "parallel")),
        name="even_in_proj",
    )(x, mod, w_r, w_qkv, w_idx, cos_t, sin_a, sin_b, ikg, ikb)


def _rwkv_kernel(p_ref, pp_ref, mu_ref, vec_ref, w2_ref, a2_ref, g2_ref, ltri_ref, ustr_ref,
                 seg_ref, o_ref, s_ref, obuf_ref):
    i = pl.program_id(1)

    @pl.when(i == 0)
    def _():
        s_ref[...] = jnp.zeros_like(s_ref)

    tt = RW_TILE
    w = RWKV_WIDTH
    hd = HEAD_DIM
    nchunk = tt // RW_CHUNK
    seg = seg_ref[...]
    segsum = lambda t: _dot_split(t, seg, 2)

    p = p_ref[0]
    prow = pp_ref[0][SUBLANES - 1:SUBLANES] * (i > 0).astype(F32)
    rowi = lax.broadcasted_iota(I32, (tt, 1), 0)
    xprev = jnp.where(rowi == 0, prow, pltpu.roll(p, 1, 0))
    ps = p + (xprev - p) * mu_ref[...]
    r = ps[:, :w]
    k = ps[:, w:2 * w]
    v = ps[:, 2 * w:3 * w]
    o1 = 3 * w
    wd = ps[:, o1:o1 + RWKV_LORA_W]
    ad = ps[:, o1 + RWKV_LORA_W:o1 + RWKV_LORA_W + RWKV_LORA_A]
    gd = ps[:, o1 + RWKV_LORA_W + RWKV_LORA_A:]
    vec = vec_ref[...]
    w0, a0, k_k, k_a, r_k, gn_g, gn_b = (vec[j:j + 1] for j in range(7))

    y = -(w0 + _dot(jnp.tanh(wd), w2_ref[...]))
    softplus = jnp.maximum(y, 0.0) + jnp.log1p(jnp.exp(-jnp.abs(y)))
    logw = -jnp.exp(-softplus - 0.5)
    a = jax.nn.sigmoid(a0 + _dot(ad, a2_ref[...]))
    g = _dot(jax.nn.sigmoid(gd), g2_ref[...])
    kk = k * k_k
    kk = kk / jnp.maximum(jnp.sqrt(segsum(kk * kk)), 1e-12)
    k2 = k * (1.0 + (a - 1.0) * k_a)
    bonus = segsum(r * k2 * r_k) * v

    cum = _dot_split_left(ltri_ref[...], logw, 3)
    rem = _dot_split_left(ustr_ref[...], logw, 3)
    pt = jnp.exp(cum)
    ipt = jnp.exp(-cum)
    erem = jnp.exp(rem)
    kka = kk * a
    at = -kk * jnp.exp(cum - logw)
    rt = r * pt
    bt = kka * ipt
    kt = k2 * ipt
    bp = kka * erem
    kp = k2 * erem

    ti = lax.broadcasted_iota(I32, (tt, tt), 0)
    si = lax.broadcasted_iota(I32, (tt, tt), 1)
    same = (ti // RW_CHUNK) == (si // RW_CHUNK)
    strict = same & (si < ti)
    incl = same & (si <= ti)
    tb = lax.broadcasted_iota(I32, (tt, nchunk * hd), 0)
    cb = lax.broadcasted_iota(I32, (tt, nchunk * hd), 1)
    blkmask = (tb // RW_CHUNK) == (cb // hd)
    tile_chunks = lambda t: jnp.where(blkmask, jnp.concatenate([t] * nchunk, axis=1), 0.0)

    for h in range(RWKV_HEADS):
        sl = slice(h * hd, (h + 1) * hd)
        at_h, rt_h, vh = at[:, sl], rt[:, sl], v[:, sl]
        x = _dot_nt(jnp.concatenate([at_h, rt_h], axis=0),
                    jnp.concatenate([bt[:, sl], kt[:, sl]], axis=0))
        a_ab = jnp.where(strict, x[:tt, :tt], 0.0)
        a_ak = jnp.where(strict, x[:tt, tt:], 0.0)
        a_rb = jnp.where(incl, x[tt:, :tt], 0.0)
        a_rk = jnp.where(incl, x[tt:, tt:], 0.0)
        yv = jnp.concatenate([at_h, _dot(a_ak, vh)], axis=1)
        apow = a_ab
        span = 1
        while True:
            yv = yv + _dot(apow, yv)
            span *= 2
            if span >= RW_CHUNK:
                break
            apow = _dot(apow, apow)
        arb_y = _dot(a_rb, yv)
        qt = rt_h + arb_y[:, :hd]
        o0 = arb_y[:, hd:] + _dot(a_rk, vh)
        yt = yv.T
        bpb = tile_chunks(bp[:, sl])
        kpb = tile_chunks(kp[:, sl])
        g_all = _dot(yt[:hd], bpb)
        h_all = _dot(jnp.concatenate([yt[hd:], vh.T], axis=1),
                     jnp.concatenate([bpb, kpb], axis=0))
        s = s_ref[h]
        for n in range(nchunk):
            rows = slice(n * RW_CHUNK, (n + 1) * RW_CHUNK)
            cols = slice(n * hd, (n + 1) * hd)
            obuf_ref[rows, sl] = _dot_nt(qt[rows], s) + o0[rows]
            pc = pt[(n + 1) * RW_CHUNK - 1:(n + 1) * RW_CHUNK, sl]
            s = s * pc + _dot(s, g_all[:, cols]) + h_all[:, cols]
        s_ref[h] = s

    o = obuf_ref[...]
    mean = segsum(o) * (1.0 / hd)
    oc = o - mean
    var = segsum(oc * oc) * (1.0 / hd)
    on = oc * lax.rsqrt(var + GN_EPS) * gn_g + gn_b
    o_ref[0] = ((on + bonus) * g).astype(BF16)


def _rwkv(p_r, mu, w0, w2, a0, a2, g2, k_k, k_a, r_k, gn_g, gn_b):
    bsz, seq, _ = p_r.shape
    tt = RW_TILE
    w = RWKV_WIDTH
    vec = jnp.stack([w0, a0, k_k, k_a, r_k.reshape(w), gn_g, gn_b, jnp.zeros_like(w0)])
    t_idx = np.arange(tt)
    same = (t_idx[:, None] // RW_CHUNK) == (t_idx[None, :] // RW_CHUNK)
    ltri = jnp.asarray(same & (t_idx[None, :] <= t_idx[:, None]), BF16)
    ustr = jnp.asarray(same & (t_idx[None, :] > t_idx[:, None]), BF16)
    c_idx = np.arange(w)
    seg = jnp.asarray((c_idx[:, None] // HEAD_DIM) == (c_idx[None, :] // HEAD_DIM), BF16)
    full = lambda shape: pl.BlockSpec(shape, lambda b, i: (0,) * len(shape))
    per_tile = tt // SUBLANES
    return pl.pallas_call(
        _rwkv_kernel,
        grid=(bsz, seq // tt),
        in_specs=[pl.BlockSpec((1, tt, RWKV_COLS), lambda b, i: (b, i, 0)),
                  pl.BlockSpec((1, SUBLANES, RWKV_COLS),
                               lambda b, i: (b, jnp.maximum(i * per_tile - 1, 0), 0)),
                  full((1, RWKV_COLS)), full((SUBLANES, w)),
                  full(w2.shape), full(a2.shape), full(g2.shape),
                  full((tt, tt)), full((tt, tt)), full((w, w))],
        out_specs=pl.BlockSpec((1, tt, w), lambda b, i: (b, i, 0)),
        out_shape=jax.ShapeDtypeStruct((bsz, seq, w), BF16),
        scratch_shapes=[pltpu.VMEM((RWKV_HEADS, HEAD_DIM, HEAD_DIM), F32),
                        pltpu.VMEM((tt, w), F32)],
        compiler_params=_params(("parallel", "arbitrary")),
        name="rwkv7_mix",
    )(p_r, p_r, mu.reshape(1, RWKV_COLS), vec, w2.astype(BF16), a2.astype(BF16),
      g2.astype(BF16), ltri, ustr, seg)


def _dsa_kernel(q_ref, kt_ref, v_ref, qi_ref, kit_ref, wi_ref, tri_ref, o_ref,
                key_ref, acc_ref, m_ref, l_ref, *, topk):
    qb = pl.program_id(1)
    nq = Q_BLOCK
    kt_sz = KEY_TILE
    start = qb * nq
    nkt = (start + nq + kt_sz - 1) // kt_sz
    row = lax.broadcasted_iota(I32, (nq, 1), 0)
    lim = start + (row // CHUNK + 1) * CHUNK
    wi = wi_ref[0]
    qi = qi_ref[0]

    def score_tile(j, carry):
        off = pl.multiple_of(j * kt_sz, kt_sz)
        kit = kit_ref[0, :, pl.ds(off, kt_sz)]
        s = jnp.zeros((nq, kt_sz), F32)
        for h in range(IDX_HEADS):
            d = jnp.dot(qi[:, h * IDX_DIM:(h + 1) * IDX_DIM], kit, preferred_element_type=F32)
            s = s + wi[:, IDX_DIM + h:IDX_DIM + h + 1] * jnp.maximum(d, 0.0)
        s = s + 0.0
        bits = pltpu.bitcast(s, I32)
        key = bits ^ ((bits >> 31) & 0x7FFFFFFF)
        sidx = off + lax.broadcasted_iota(I32, (nq, kt_sz), 1)
        key_ref[:, pl.ds(off, kt_sz)] = jnp.where(sidx < lim, key, INT_MIN)
        return carry

    lax.fori_loop(0, nkt, score_tile, 0)

    def count(pred):
        def body(j, c):
            off = pl.multiple_of(j * kt_sz, kt_sz)
            hit = jnp.where(pred(key_ref[:, pl.ds(off, kt_sz)]), 1, 0)
            part = hit[:, :LANES]
            for t in range(1, kt_sz // LANES):
                part = part + hit[:, t * LANES:(t + 1) * LANES]
            return c + part
        c = lax.fori_loop(0, nkt, body, jnp.zeros((nq, LANES), I32))
        return jnp.sum(c, axis=1, keepdims=True)

    zero = jnp.zeros((nq, 1), I32)
    lo = jnp.where(count(lambda kv: kv >= zero) >= topk, zero, zero + INT_MIN)

    def bit_step(b, lo):
        cand = lo + lax.shift_left(jnp.int32(1), 30 - b)
        return jnp.where(count(lambda kv: kv >= cand) >= topk, cand, lo)

    th = lax.fori_loop(0, 31, bit_step, lo)
    need = jnp.where(th == INT_MIN, 0, topk - count(lambda kv: kv > th)).astype(F32)

    acc_ref[...] = jnp.zeros_like(acc_ref)
    m_ref[...] = jnp.full_like(m_ref, NEG_INF)
    l_ref[...] = jnp.zeros_like(l_ref)
    q = q_ref[0]
    tri = tri_ref[...]

    def attn_tile(j, eq_before):
        off = pl.multiple_of(j * kt_sz, kt_sz)
        kv = key_ref[:, pl.ds(off, kt_sz)]
        eq = kv == th
        pre = eq_before + jnp.dot(jnp.where(eq, 1.0, 0.0).astype(BF16), tri,
                                  preferred_element_type=F32)
        sel = (kv > th) | (eq & (pre <= need))
        vt = v_ref[0, pl.ds(off, kt_sz), :]
        for h in range(DSA_HEADS):
            sl = slice(h * HEAD_DIM, (h + 1) * HEAD_DIM)
            kh = kt_ref[0, sl, pl.ds(off, kt_sz)]
            s = jnp.dot(q[:, sl], kh, preferred_element_type=F32)
            s = jnp.where(sel, s, NEG_INF)
            m_old = m_ref[h]
            m_new = jnp.maximum(m_old, jnp.max(s, axis=1, keepdims=True))
            scale = jnp.exp(m_old - m_new)
            pr = jnp.where(sel, jnp.exp(s - m_new), 0.0)
            l_ref[h] = scale * l_ref[h] + jnp.sum(pr, axis=1, keepdims=True)
            m_ref[h] = m_new
            acc_ref[:, sl] = scale * acc_ref[:, sl] + jnp.dot(
                pr.astype(BF16), vt[:, sl], preferred_element_type=F32)
        return pre[:, kt_sz - 1:kt_sz]

    lax.fori_loop(0, nkt, attn_tile, jnp.zeros((nq, 1), F32))
    inv = jnp.concatenate(
        [jnp.broadcast_to(1.0 / l_ref[h], (nq, HEAD_DIM)) for h in range(DSA_HEADS)], axis=1)
    o_ref[0] = (acc_ref[...] * inv).astype(BF16)


def _dsa(q, k_t, v, qi, ki_t, wi):
    bsz, seq, w = q.shape
    nq = Q_BLOCK
    topk = min(INDEX_TOPK, seq // 4)
    t_idx = np.arange(KEY_TILE)
    tri = jnp.asarray(t_idx[:, None] <= t_idx[None, :], BF16)
    return pl.pallas_call(
        functools.partial(_dsa_kernel, topk=topk),
        grid=(bsz, seq // nq),
        in_specs=[pl.BlockSpec((1, nq, w), lambda b, i: (b, i, 0)),
                  pl.BlockSpec((1, w, seq), lambda b, i: (b, 0, 0)),
                  pl.BlockSpec((1, seq, w), lambda b, i: (b, 0, 0)),
                  pl.BlockSpec((1, nq, IDX_HEADS * IDX_DIM), lambda b, i: (b, i, 0)),
                  pl.BlockSpec((1, IDX_DIM, seq), lambda b, i: (b, 0, 0)),
                  pl.BlockSpec((1, nq, LANES), lambda b, i: (b, i, 0)),
                  pl.BlockSpec((KEY_TILE, KEY_TILE), lambda b, i: (0, 0))],
        out_specs=pl.BlockSpec((1, nq, w), lambda b, i: (b, i, 0)),
        out_shape=jax.ShapeDtypeStruct((bsz, seq, w), BF16),
        scratch_shapes=[pltpu.VMEM((nq, seq), I32),
                        pltpu.VMEM((nq, w), F32),
                        pltpu.VMEM((DSA_HEADS, nq, 1), F32),
                        pltpu.VMEM((DSA_HEADS, nq, 1), F32)],
        compiler_params=_params(("parallel", "arbitrary")),
        name="dsa_mix",
    )(q, k_t, v, qi, ki_t, wi, tri)


def _odd_in_kernel(x_ref, mod_ref, w_ref, o_ref):
    m = mod_ref[0]
    h = (x_ref[0] * (1.0 + m[1:2]) + m[0:1]).astype(BF16)
    o_ref[0] = jnp.dot(h, w_ref[...], preferred_element_type=F32)


def _odd_in_proj(x, mod, w_in):
    bsz, seq, d = x.shape
    tm = 512
    n = w_in.shape[1]
    return pl.pallas_call(
        _odd_in_kernel,
        grid=(bsz, seq // tm),
        in_specs=[pl.BlockSpec((1, tm, d), lambda b, i: (b, i, 0)),
                  pl.BlockSpec((1, 6, d), lambda b, i: (b, 0, 0)),
                  pl.BlockSpec((d, n), lambda b, i: (0, 0))],
        out_specs=pl.BlockSpec((1, tm, n), lambda b, i: (b, i, 0)),
        out_shape=jax.ShapeDtypeStruct((bsz, seq, n), F32),
        compiler_params=_params(("parallel", "parallel")),
        name="odd_in_proj",
    )(x, mod, w_in.astype(BF16))


POOL_HALO = 16


def _odd_mix_kernel(p_ref, prev_ref, pw_ref, ps_ref, lng_ref, lnb_ref, ws_ref, bs_ref, o_ref):
    i = pl.program_id(1)
    tm = p_ref.shape[1]
    gd = POOL_GROUP_DIM
    p = p_ref[0]
    prev = prev_ref[0] * (i > 0).astype(F32)
    t_glob = (i * tm + lax.broadcasted_iota(I32, (tm, 1), 0)).astype(F32)
    scale = ps_ref[...]
    for gi, win in enumerate(POOL_WINDOWS):
        cols = slice(gi * gd, (gi + 1) * gd)
        xg = p[:, cols]
        s = jnp.concatenate([prev[:, cols], xg], axis=0)
        span = 1
        while span < win:
            s = s[span:] + s[:-span]
            span *= 2
        first = POOL_HALO + 1 - win
        pooled = s[first:first + tm] / jnp.minimum(t_glob + 1.0, float(win)) - xg
        o_ref[0, :, cols] = (_dot(pooled, pw_ref[gi]) * scale[:, cols]).astype(BF16)

    u = _gelu(p[:, POOL_WIDTH:POOL_WIDTH + SG_WIDTH])
    v = _ln(_gelu(p[:, POOL_WIDTH + SG_WIDTH:]), lng_ref[...], lnb_ref[...], LN_EPS)
    ti = lax.broadcasted_iota(I32, (SG_CHUNK, SG_CHUNK), 0)
    si = lax.broadcasted_iota(I32, (SG_CHUNK, SG_CHUNK), 1)
    bs = bs_ref[...]
    for gi in range(SG_GROUPS):
        cols = slice(gi * SG_GROUP_DIM, (gi + 1) * SG_GROUP_DIM)
        ws = jnp.where(si <= ti, ws_ref[gi], 0.0)
        for n in range(tm // SG_CHUNK):
            rows = slice(n * SG_CHUNK, (n + 1) * SG_CHUNK)
            z = _dot(ws, v[rows, cols]) + bs[:, gi:gi + 1]
            o_ref[0, rows, POOL_WIDTH + gi * SG_GROUP_DIM:POOL_WIDTH + (gi + 1) * SG_GROUP_DIM] = (
                u[rows, cols] * z).astype(BF16)


def _odd_mix(p, pool_w, pool_scale, sg_ln_g, sg_ln_b, sg_w, sg_b):
    bsz, seq, n = p.shape
    tm = 256
    per_tile = tm // POOL_HALO
    full = lambda shape: pl.BlockSpec(shape, lambda b, i: (0,) * len(shape))
    return pl.pallas_call(
        _odd_mix_kernel,
        grid=(bsz, seq // tm),
        in_specs=[pl.BlockSpec((1, tm, n), lambda b, i: (b, i, 0)),
                  pl.BlockSpec((1, POOL_HALO, POOL_WIDTH),
                               lambda b, i: (b, jnp.maximum(i * per_tile - 1, 0), 0)),
                  full(pool_w.shape), full((1, POOL_WIDTH)), full((1, SG_WIDTH)),
                  full((1, SG_WIDTH)), full(sg_w.shape), full((SG_CHUNK, SG_GROUPS))],
        out_specs=pl.BlockSpec((1, tm, D_MODEL), lambda b, i: (b, i, 0)),
        out_shape=jax.ShapeDtypeStruct((bsz, seq, D_MODEL), BF16),
        compiler_params=_params(("parallel", "parallel")),
        name="pool_sgu_mix",
    )(p, p, pool_w.astype(BF16), pool_scale.reshape(1, -1), sg_ln_g.reshape(1, -1),
      sg_ln_b.reshape(1, -1), sg_w, sg_b.T)


def _proj_ln_kernel(*refs, n_in, gate_row):
    a_refs = refs[:n_in]
    w_refs = refs[n_in:2 * n_in]
    x_ref, mod_ref, g_ref, b_ref, o_ref = refs[2 * n_in:]
    y = None
    for a_ref, w_ref in zip(a_refs, w_refs):
        part = jnp.dot(a_ref[0], w_ref[...], preferred_element_type=F32)
        y = part if y is None else y + part
    gate = mod_ref[0][gate_row:gate_row + 1]
    o_ref[0] = _ln(ALPHA * x_ref[0] + gate * y, g_ref[...], b_ref[...], LN_EPS)


def _proj_ln(acts, weights, x, mod, gate_row, ln_g, ln_b, name):
    bsz, seq, d = x.shape
    tm = 512
    n_in = len(acts)
    in_specs = [pl.BlockSpec((1, tm, a.shape[2]), lambda b, i: (b, i, 0)) for a in acts]
    in_specs += [pl.BlockSpec(w.shape, lambda b, i: (0, 0)) for w in weights]
    in_specs += [pl.BlockSpec((1, tm, d), lambda b, i: (b, i, 0)),
                 pl.BlockSpec((1, 6, d), lambda b, i: (b, 0, 0)),
                 pl.BlockSpec((1, d), lambda b, i: (0, 0)),
                 pl.BlockSpec((1, d), lambda b, i: (0, 0))]
    return pl.pallas_call(
        functools.partial(_proj_ln_kernel, n_in=n_in, gate_row=gate_row),
        grid=(bsz, seq // tm),
        in_specs=in_specs,
        out_specs=pl.BlockSpec((1, tm, d), lambda b, i: (b, i, 0)),
        out_shape=jax.ShapeDtypeStruct((bsz, seq, d), F32),
        compiler_params=_params(("parallel", "parallel")),
        name=name,
    )(*acts, *[w.astype(BF16) for w in weights], x, mod, ln_g.reshape(1, d), ln_b.reshape(1, d))


def _ffn_up_kernel(x_ref, xp_ref, mod_ref, wg_ref, wv_ref, cwg_ref, cwv_ref, cbg_ref, cbv_ref,
                   o_ref):
    i = pl.program_id(2)
    m = mod_ref[0]
    sc = 1.0 + m[4:5]
    sh = m[3:4]
    h = (x_ref[0] * sc + sh).astype(BF16)
    hp = (xp_ref[0] * sc + sh).astype(BF16)
    live = (i > 0).astype(F32)
    tm = h.shape[0]
    rowi = lax.broadcasted_iota(I32, (tm, 1), 0)

    def conv(w_ref, cw_ref, cb_ref):
        u = jnp.dot(h, w_ref[...], preferred_element_type=F32)
        up = jnp.dot(hp, w_ref[...], preferred_element_type=F32) * live
        last = up[SUBLANES - 1:SUBLANES]
        last2 = up[SUBLANES - 2:SUBLANES - 1]
        u1 = jnp.where(rowi == 0, last, pltpu.roll(u, 1, 0))
        u2 = jnp.where(rowi == 0, last2, jnp.where(rowi == 1, last, pltpu.roll(u, 2, 0)))
        cw = cw_ref[...]
        return cb_ref[...] + u2 * cw[0:1] + u1 * cw[1:2] + u * cw[2:3]

    gate = conv(wg_ref, cwg_ref, cbg_ref)
    val = conv(wv_ref, cwv_ref, cbv_ref)
    o_ref[0] = (gate * jax.nn.sigmoid(gate) * val).astype(BF16)


def _ffn_up(x, mod, w_up, conv_w, conv_b):
    bsz, seq, d = x.shape
    tm = 512
    tn = D_FF // 2
    nt = D_FF // tn
    per_tile = tm // SUBLANES
    wb = w_up.astype(BF16)
    cb = conv_b.reshape(1, -1)
    return pl.pallas_call(
        _ffn_up_kernel,
        grid=(bsz, nt, seq // tm),
        in_specs=[pl.BlockSpec((1, tm, d), lambda b, n, i: (b, i, 0)),
                  pl.BlockSpec((1, SUBLANES, d),
                               lambda b, n, i: (b, jnp.maximum(i * per_tile - 1, 0), 0)),
                  pl.BlockSpec((1, 6, d), lambda b, n, i: (b, 0, 0)),
                  pl.BlockSpec((d, tn), lambda b, n, i: (0, n)),
                  pl.BlockSpec((d, tn), lambda b, n, i: (0, n + nt)),
                  pl.BlockSpec((3, tn), lambda b, n, i: (0, n)),
                  pl.BlockSpec((3, tn), lambda b, n, i: (0, n + nt)),
                  pl.BlockSpec((1, tn), lambda b, n, i: (0, n)),
                  pl.BlockSpec((1, tn), lambda b, n, i: (0, n + nt))],
        out_specs=pl.BlockSpec((1, tm, tn), lambda b, n, i: (b, i, n)),
        out_shape=jax.ShapeDtypeStruct((bsz, seq, D_FF), BF16),
        compiler_params=_params(("parallel", "parallel", "parallel")),
        name="ffn_up_conv_gate",
    )(x, x, mod, wb, wb, conv_w, conv_w, cb, cb)


def _rope_tables(seq):
    inv = ROPE_THETA ** (-jnp.arange(0, ROPE_DIM, 2, dtype=F32) / ROPE_DIM)
    ang = jnp.arange(seq, dtype=F32)[:, None] * inv[None, :]
    cos, sin = jnp.cos(ang), jnp.sin(ang)
    half = ROPE_DIM // 2
    rest = HEAD_DIM - ROPE_DIM
    one = jnp.ones((seq, rest), F32)
    zero = jnp.zeros((seq, rest), F32)
    zh = jnp.zeros((seq, half), F32)
    head = lambda *parts: jnp.concatenate(parts * (LANES // HEAD_DIM), axis=1)
    return head(cos, cos, one), head(-sin, zh, zero), head(zh, sin, zero)


def kernel(x, c, ada_w, ada_b, ln_g, ln_b, ffn_w_up, ffn_conv_w, ffn_conv_b, ffn_w_down, ev_w_in, ev_w_out, rw_mu, rw_w0, rw_w2, rw_a0, rw_a2, rw_g2, rw_k_k, rw_k_a, rw_r_k, rw_gn_g, rw_gn_b, ik_ln_g, ik_ln_b, od_w_in, od_w_out, pool_w, pool_scale, sg_ln_g, sg_ln_b, sg_w, sg_b):
    seq = x.shape[1]
    tables = _rope_tables(seq)
    mods = _modulation(c, ada_w, ada_b)
    for layer in range(DEPTH):
        mod = mods[layer]
        if layer % 2 == 0:
            e = layer // 2
            p_r, q, k_t, v, qi, ki_t, wi = _even_in_proj(x, mod, ev_w_in[e], ik_ln_g[e],
                                                         ik_ln_b[e], tables)
            ya = _rwkv(p_r, rw_mu[e], rw_w0[e], rw_w2[e], rw_a0[e], rw_a2[e], rw_g2[e],
                       rw_k_k[e], rw_k_a[e], rw_r_k[e], rw_gn_g[e], rw_gn_b[e])
            yb = _dsa(q, k_t, v, qi, ki_t, wi)
            w_out = ev_w_out[e]
            x = _proj_ln([ya, yb], [w_out[:RWKV_WIDTH], w_out[RWKV_WIDTH:]], x, mod, 2,
                         ln_g[layer, 0], ln_b[layer, 0], "even_out_proj_ln")
        else:
            o = layer // 2
            p = _odd_in_proj(x, mod, od_w_in[o])
            yc = _odd_mix(p, pool_w[o], pool_scale[o], sg_ln_g[o], sg_ln_b[o], sg_w[o], sg_b[o])
            x = _proj_ln([yc], [od_w_out[o]], x, mod, 2, ln_g[layer, 0], ln_b[layer, 0],
                         "odd_out_proj_ln")
        act = _ffn_up(x, mod, ffn_w_up[layer], ffn_conv_w[layer], ffn_conv_b[layer])
        x = _proj_ln([act], [ffn_w_down[layer]], x, mod, 5, ln_g[layer, 1], ln_b[layer, 1],
                     "ffn_down_ln")
    return x
```

```python
import functools

import numpy as np
import jax
import jax.numpy as jnp
from jax import lax
from jax.experimental import pallas as pl
from jax.experimental.pallas import tpu as pltpu

F32 = jnp.float32
BF16 = jnp.bfloat16
I32 = jnp.int32

D_MODEL = 1024
DEPTH = 4
CHUNK = 64
HEAD_DIM = 64
RWKV_WIDTH = D_MODEL // 2
RWKV_HEADS = RWKV_WIDTH // HEAD_DIM
RWKV_LORA_W = 64
RWKV_LORA_A = 64
RWKV_LORA_G = 128
RWKV_COLS = 3 * RWKV_WIDTH + RWKV_LORA_W + RWKV_LORA_A + RWKV_LORA_G
DSA_WIDTH = D_MODEL - RWKV_WIDTH
DSA_HEADS = DSA_WIDTH // HEAD_DIM
IDX_HEADS = 4
IDX_DIM = 64
INDEX_TOPK = 256
Q_BLOCK = 128
ROPE_THETA = 500000.0
ROPE_DIM = HEAD_DIM // 4
POOL_WINDOWS = (2, 4, 8, 16)
POOL_WIDTH = D_MODEL // 2
POOL_GROUP_DIM = POOL_WIDTH // len(POOL_WINDOWS)
SG_WIDTH = D_MODEL - POOL_WIDTH
SG_GROUPS = 4
SG_GROUP_DIM = SG_WIDTH // SG_GROUPS
SG_CHUNK = 128
D_FF = 2816
ALPHA = (2.0 * DEPTH) ** 0.25
LN_EPS = 1e-5
GN_EPS = 64e-5
NEG_INF = -1e30
INT_MIN = -(2 ** 31)
LOG2E = 1.4426950408889634

LANES = 128
SUBLANES = 8
VMEM_LIMIT = 56 * 1024 * 1024

RW_TILE = 128
RW_CHUNK = 16
KEY_TILE = 2048
TRI_TILE = 512
DSA_GROUP = 2
PAD_IDX = 384


def _dot(a, b):
    return jnp.dot(a.astype(BF16), b.astype(BF16), preferred_element_type=F32)


def _dot_nt(a, b):
    return lax.dot_general(a.astype(BF16), b.astype(BF16), (((1,), (1,)), ((), ())),
                           preferred_element_type=F32)


def _dot_split(x, m01, terms):
    acc = None
    rem = x
    for _ in range(terms):
        piece = rem.astype(BF16)
        rem = rem - piece.astype(F32)
        part = jnp.dot(piece, m01, preferred_element_type=F32)
        acc = part if acc is None else acc + part
    return acc


def _dot_split_left(m01, x, terms):
    acc = None
    rem = x
    for _ in range(terms):
        piece = rem.astype(BF16)
        rem = rem - piece.astype(F32)
        part = jnp.dot(m01, piece, preferred_element_type=F32)
        acc = part if acc is None else acc + part
    return acc


def _ln(x, g, b, eps):
    mu = jnp.mean(x, axis=-1, keepdims=True)
    xc = x - mu
    var = jnp.mean(xc * xc, axis=-1, keepdims=True)
    return xc * lax.rsqrt(var + eps) * g + b


def _gelu(x):
    return 0.5 * x * (1.0 + lax.erf(x * 0.7071067811865476))


def _params(sem):
    return pltpu.CompilerParams(dimension_semantics=sem, vmem_limit_bytes=VMEM_LIMIT)


def _mod_kernel(c_ref, w_ref, b_ref, o_ref):
    c = c_ref[...]
    ca = c * jax.nn.sigmoid(c)
    o_ref[0] = jnp.dot(ca, w_ref[0], preferred_element_type=F32,
                       precision=lax.Precision.HIGHEST) + b_ref[0]


def _modulation(c, ada_w, ada_b):
    bsz, d = c.shape
    depth = ada_w.shape[0]
    n = ada_w.shape[2]
    tn = 1536
    rows = -(-bsz // SUBLANES) * SUBLANES
    c8 = jnp.pad(c, ((0, rows - bsz), (0, 0)))
    out = pl.pallas_call(
        _mod_kernel,
        grid=(depth, n // tn),
        in_specs=[pl.BlockSpec((rows, d), lambda l, j: (0, 0)),
                  pl.BlockSpec((1, d, tn), lambda l, j: (l, 0, j)),
                  pl.BlockSpec((1, 1, tn), lambda l, j: (l, 0, j))],
        out_specs=pl.BlockSpec((1, rows, tn), lambda l, j: (l, 0, j)),
        out_shape=jax.ShapeDtypeStruct((depth, rows, n), F32),
        compiler_params=_params(("arbitrary", "arbitrary")),
        name="adaln_mod",
    )(c8, ada_w, ada_b.reshape(depth, 1, n))
    return out[:, :bsz].reshape(depth, bsz, 6, d)


def _rope(x, cos_t, sin_a, sin_b):
    n = x.shape[1] // LANES
    rep = (lambda t: jnp.concatenate([t] * n, axis=1)) if n > 1 else (lambda t: t)
    width = x.shape[1]
    half = ROPE_DIM // 2
    return (x * rep(cos_t) + pltpu.roll(x, width - half, 1) * rep(sin_a)
            + pltpu.roll(x, half, 1) * rep(sin_b))


def _even_in_kernel(x_ref, mod_ref, wr_ref, wqkv_ref, widx_ref, cos_ref, sa_ref, sb_ref,
                    ikg_ref, ikb_ref,
                    pr_ref, q_ref, kt_ref, v_ref, qi_ref, kit_ref, wi_ref):
    m = mod_ref[0]
    h = (x_ref[0] * (1.0 + m[1:2]) + m[0:1]).astype(BF16)
    pr_ref[0] = jnp.dot(h, wr_ref[...], preferred_element_type=F32)
    qkv = jnp.dot(h, wqkv_ref[...], preferred_element_type=F32)
    cos_t = cos_ref[...]
    sin_a = sa_ref[...]
    sin_b = sb_ref[...]
    w = DSA_WIDTH
    q = _rope(qkv[:, :w], cos_t, sin_a, sin_b) * (HEAD_DIM ** -0.5 * LOG2E)
    k = _rope(qkv[:, w:2 * w], cos_t, sin_a, sin_b)
    for hh in range(DSA_HEADS):
        q_ref[0, hh] = q[:, hh * HEAD_DIM:(hh + 1) * HEAD_DIM].astype(BF16)
    kt_ref[0] = k.T.astype(BF16)
    v_ref[0] = qkv[:, 2 * w:].astype(BF16)
    idx = jnp.dot(h, widx_ref[...], preferred_element_type=F32)
    nq = IDX_HEADS * IDX_DIM
    qi_ref[0] = _rope(idx[:, :nq], cos_t, sin_a, sin_b).astype(BF16)
    blk = idx[:, nq:nq + LANES]
    lane = lax.broadcasted_iota(I32, blk.shape, 1)
    is_k = lane < IDX_DIM
    mu = jnp.sum(jnp.where(is_k, blk, 0.0), axis=1, keepdims=True) * (1.0 / IDX_DIM)
    xc = jnp.where(is_k, blk - mu, 0.0)
    var = jnp.sum(xc * xc, axis=1, keepdims=True) * (1.0 / IDX_DIM)
    kin = xc * lax.rsqrt(var + LN_EPS) * ikg_ref[...] + ikb_ref[...]
    kir = _rope(kin, cos_t, sin_a, sin_b)
    kit_ref[0] = kir.T[:IDX_DIM].astype(BF16)
    wi_ref[0] = blk * (IDX_HEADS ** -0.5 * IDX_DIM ** -0.5)


def _even_in_proj(x, mod, w_in, ik_g, ik_b, tables):
    bsz, seq, d = x.shape
    tm = 256
    w_r = w_in[:, :RWKV_COLS].astype(BF16)
    w_qkv = w_in[:, RWKV_COLS:RWKV_COLS + 3 * DSA_WIDTH].astype(BF16)
    w_idx = w_in[:, RWKV_COLS + 3 * DSA_WIDTH:]
    w_idx = jnp.pad(w_idx, ((0, 0), (0, PAD_IDX - w_idx.shape[1]))).astype(BF16)
    pad = LANES - IDX_DIM
    ikg = jnp.pad(ik_g, (0, pad)).reshape(1, LANES)
    ikb = jnp.pad(ik_b, (0, pad)).reshape(1, LANES)
    cos_t, sin_a, sin_b = tables
    full = lambda shape: pl.BlockSpec(shape, lambda b, i: (0,) * len(shape))
    tab = pl.BlockSpec((tm, LANES), lambda b, i: (i, 0))
    nq = IDX_HEADS * IDX_DIM
    return pl.pallas_call(
        _even_in_kernel,
        grid=(bsz, seq // tm),
        in_specs=[pl.BlockSpec((1, tm, d), lambda b, i: (b, i, 0)),
                  pl.BlockSpec((1, 6, d), lambda b, i: (b, 0, 0)),
                  full(w_r.shape), full(w_qkv.shape), full(w_idx.shape),
                  tab, tab, tab, full((1, LANES)), full((1, LANES))],
        out_specs=[pl.BlockSpec((1, tm, RWKV_COLS), lambda b, i: (b, i, 0)),
                   pl.BlockSpec((1, DSA_HEADS, tm, HEAD_DIM), lambda b, i: (b, 0, i, 0)),
                   pl.BlockSpec((1, DSA_WIDTH, tm), lambda b, i: (b, 0, i)),
                   pl.BlockSpec((1, tm, DSA_WIDTH), lambda b, i: (b, i, 0)),
                   pl.BlockSpec((1, tm, nq), lambda b, i: (b, i, 0)),
                   pl.BlockSpec((1, IDX_DIM, tm), lambda b, i: (b, 0, i)),
                   pl.BlockSpec((1, tm, LANES), lambda b, i: (b, i, 0))],
        out_shape=[jax.ShapeDtypeStruct((bsz, seq, RWKV_COLS), F32),
                   jax.ShapeDtypeStruct((bsz, DSA_HEADS, seq, HEAD_DIM), BF16),
                   jax.ShapeDtypeStruct((bsz, DSA_WIDTH, seq), BF16),
                   jax.ShapeDtypeStruct((bsz, seq, DSA_WIDTH), BF16),
                   jax.ShapeDtypeStruct((bsz, seq, nq), BF16),
                   jax.ShapeDtypeStruct((bsz, IDX_DIM, seq), BF16),
                   jax.ShapeDtypeStruct((bsz, seq, LANES), F32)],
        compiler_params=_params(("parallel", "parallel")),
        name="even_in_proj",
    )(x, mod, w_r, w_qkv, w_idx, cos_t, sin_a, sin_b, ikg, ikb)


def _rwkv_kernel(p_ref, pp_ref, mu_ref, vec_ref, w2_ref, a2_ref, g2_ref, ltri_ref, ustr_ref,
                 seg_ref, o_ref, s_ref, obuf_ref):
    i = pl.program_id(1)

    @pl.when(i == 0)
    def _():
        s_ref[...] = jnp.zeros_like(s_ref)

    tt = RW_TILE
    w = RWKV_WIDTH
    hd = HEAD_DIM
    nchunk = tt // RW_CHUNK
    seg = seg_ref[...]
    segsum = lambda t: _dot_split(t, seg, 2)

    p = p_ref[0]
    prow = pp_ref[0][SUBLANES - 1:SUBLANES] * (i > 0).astype(F32)
    rowi = lax.broadcasted_iota(I32, (tt, 1), 0)
    xprev = jnp.where(rowi == 0, prow, pltpu.roll(p, 1, 0))
    ps = p + (xprev - p) * mu_ref[...]
    r = ps[:, :w]
    k = ps[:, w:2 * w]
    v = ps[:, 2 * w:3 * w]
    o1 = 3 * w
    wd = ps[:, o1:o1 + RWKV_LORA_W]
    ad = ps[:, o1 + RWKV_LORA_W:o1 + RWKV_LORA_W + RWKV_LORA_A]
    gd = ps[:, o1 + RWKV_LORA_W + RWKV_LORA_A:]
    vec = vec_ref[...]
    w0, a0, k_k, k_a, r_k, gn_g, gn_b = (vec[j:j + 1] for j in range(7))

    y = -(w0 + _dot(jnp.tanh(wd), w2_ref[...]))
    softplus = jnp.maximum(y, 0.0) + jnp.log1p(jnp.exp(-jnp.abs(y)))
    logw = -jnp.exp(-softplus - 0.5)
    a = jax.nn.sigmoid(a0 + _dot(ad, a2_ref[...]))
    g = _dot(jax.nn.sigmoid(gd), g2_ref[...])
    kk = k * k_k
    kk = kk / jnp.maximum(jnp.sqrt(segsum(kk * kk)), 1e-12)
    k2 = k * (1.0 + (a - 1.0) * k_a)
    bonus = segsum(r * k2 * r_k) * v

    cum = _dot_split_left(ltri_ref[...], logw, 3)
    rem = _dot_split_left(ustr_ref[...], logw, 3)
    pt = jnp.exp(cum)
    ipt = jnp.exp(-cum)
    erem = jnp.exp(rem)
    kka = kk * a
    at = -kk * jnp.exp(cum - logw)
    rt = r * pt
    bt = kka * ipt
    kt = k2 * ipt
    bp = kka * erem
    kp = k2 * erem

    ti = lax.broadcasted_iota(I32, (tt, tt), 0)
    si = lax.broadcasted_iota(I32, (tt, tt), 1)
    same = (ti // RW_CHUNK) == (si // RW_CHUNK)
    strict = same & (si < ti)
    incl = same & (si <= ti)
    tb = lax.broadcasted_iota(I32, (tt, nchunk * hd), 0)
    cb = lax.broadcasted_iota(I32, (tt, nchunk * hd), 1)
    blkmask = (tb // RW_CHUNK) == (cb // hd)
    tile_chunks = lambda t: jnp.where(blkmask, jnp.concatenate([t] * nchunk, axis=1), 0.0)

    heads = range(RWKV_HEADS)
    sls = [slice(h * hd, (h + 1) * hd) for h in heads]
    at_h = [at[:, sl] for sl in sls]
    rt_h = [rt[:, sl] for sl in sls]
    vh = [v[:, sl] for sl in sls]
    x = [_dot_nt(jnp.concatenate([at_h[h], rt_h[h]], axis=0),
                 jnp.concatenate([bt[:, sls[h]], kt[:, sls[h]]], axis=0)) for h in heads]
    a_ab = [jnp.where(strict, x[h][:tt, :tt], 0.0) for h in heads]
    a_ak = [jnp.where(strict, x[h][:tt, tt:], 0.0) for h in heads]
    a_rb = [jnp.where(incl, x[h][tt:, :tt], 0.0) for h in heads]
    a_rk = [jnp.where(incl, x[h][tt:, tt:], 0.0) for h in heads]
    yv = [jnp.concatenate([at_h[h], _dot(a_ak[h], vh[h])], axis=1) for h in heads]
    apow = a_ab
    span = 1
    while True:
        yv = [yv[h] + _dot(apow[h], yv[h]) for h in heads]
        span *= 2
        if span >= RW_CHUNK:
            break
        apow = [_dot(apow[h], apow[h]) for h in heads]
    arb_y = [_dot(a_rb[h], yv[h]) for h in heads]
    ark_v = [_dot(a_rk[h], vh[h]) for h in heads]
    qt = [rt_h[h] + arb_y[h][:, :hd] for h in heads]
    o0 = [arb_y[h][:, hd:] + ark_v[h] for h in heads]
    yt = [yv[h].T for h in heads]
    bpb = [tile_chunks(bp[:, sl]) for sl in sls]
    kpb = [tile_chunks(kp[:, sl]) for sl in sls]
    g_all = [_dot(yt[h][:hd], bpb[h]) for h in heads]
    h_all = [_dot(jnp.concatenate([yt[h][hd:], vh[h].T], axis=1),
                  jnp.concatenate([bpb[h], kpb[h]], axis=0)) for h in heads]
    s = [s_ref[h] for h in heads]
    for n in range(nchunk):
        rows = slice(n * RW_CHUNK, (n + 1) * RW_CHUNK)
        cols = slice(n * hd, (n + 1) * hd)
        for h in heads:
            obuf_ref[rows, sls[h]] = _dot_nt(qt[h][rows], s[h]) + o0[h][rows]
        last = (n + 1) * RW_CHUNK - 1
        s = [s[h] * pt[last:last + 1, sls[h]] + _dot(s[h], g_all[h][:, cols]) + h_all[h][:, cols]
             for h in heads]
    for h in heads:
        s_ref[h] = s[h]

    o = obuf_ref[...]
    mean = segsum(o) * (1.0 / hd)
    oc = o - mean
    var = segsum(oc * oc) * (1.0 / hd)
    on = oc * lax.rsqrt(var + GN_EPS) * gn_g + gn_b
    o_ref[0] = ((on + bonus) * g).astype(BF16)


def _rwkv(p_r, mu, w0, w2, a0, a2, g2, k_k, k_a, r_k, gn_g, gn_b):
    bsz, seq, _ = p_r.shape
    tt = RW_TILE
    w = RWKV_WIDTH
    vec = jnp.stack([w0, a0, k_k, k_a, r_k.reshape(w), gn_g, gn_b, jnp.zeros_like(w0)])
    t_idx = np.arange(tt)
    same = (t_idx[:, None] // RW_CHUNK) == (t_idx[None, :] // RW_CHUNK)
    ltri = jnp.asarray(same & (t_idx[None, :] <= t_idx[:, None]), BF16)
    ustr = jnp.asarray(same & (t_idx[None, :] > t_idx[:, None]), BF16)
    c_idx = np.arange(w)
    seg = jnp.asarray((c_idx[:, None] // HEAD_DIM) == (c_idx[None, :] // HEAD_DIM), BF16)
    full = lambda shape: pl.BlockSpec(shape, lambda b, i: (0,) * len(shape))
    per_tile = tt // SUBLANES
    return pl.pallas_call(
        _rwkv_kernel,
        grid=(bsz, seq // tt),
        in_specs=[pl.BlockSpec((1, tt, RWKV_COLS), lambda b, i: (b, i, 0)),
                  pl.BlockSpec((1, SUBLANES, RWKV_COLS),
                               lambda b, i: (b, jnp.maximum(i * per_tile - 1, 0), 0)),
                  full((1, RWKV_COLS)), full((SUBLANES, w)),
                  full(w2.shape), full(a2.shape), full(g2.shape),
                  full((tt, tt)), full((tt, tt)), full((w, w))],
        out_specs=pl.BlockSpec((1, tt, w), lambda b, i: (b, i, 0)),
        out_shape=jax.ShapeDtypeStruct((bsz, seq, w), BF16),
        scratch_shapes=[pltpu.VMEM((RWKV_HEADS, HEAD_DIM, HEAD_DIM), F32),
                        pltpu.VMEM((tt, w), F32)],
        compiler_params=_params(("parallel", "arbitrary")),
        name="rwkv7_mix",
    )(p_r, p_r, mu.reshape(1, RWKV_COLS), vec, w2.astype(BF16), a2.astype(BF16),
      g2.astype(BF16), ltri, ustr, seg)


def _dsa_kernel(q_ref, kt_ref, v_ref, qi_ref, kit_ref, wi_ref, tri_ref, o_ref,
                key_ref, bias_ref, acc_ref, m_ref, l_ref, *, topk):
    qb = pl.program_id(1)
    nq = Q_BLOCK
    kt_sz = bias_ref.shape[1]
    start = qb * nq
    nkt = (start + nq + kt_sz - 1) // kt_sz
    row = lax.broadcasted_iota(I32, (nq, 1), 0)
    lim = start + (row // CHUNK + 1) * CHUNK
    wi = wi_ref[0]
    qi = qi_ref[0]

    def score_tile(j, carry):
        off = pl.multiple_of(j * kt_sz, kt_sz)
        kit = kit_ref[0, :, pl.ds(off, kt_sz)]
        s = jnp.zeros((nq, kt_sz), F32)
        for h in range(IDX_HEADS):
            d = jnp.dot(qi[:, h * IDX_DIM:(h + 1) * IDX_DIM], kit, preferred_element_type=F32)
            s = s + wi[:, IDX_DIM + h:IDX_DIM + h + 1] * jnp.maximum(d, 0.0)
        s = s + 0.0
        bits = pltpu.bitcast(s, I32)
        key = bits ^ ((bits >> 31) & 0x7FFFFFFF)
        sidx = off + lax.broadcasted_iota(I32, (nq, kt_sz), 1)
        key_ref[:, pl.ds(off, kt_sz)] = jnp.where(sidx < lim, key, INT_MIN)
        return carry

    lax.fori_loop(0, nkt, score_tile, 0)

    def count(pred):
        def body(j, c):
            off = pl.multiple_of(j * kt_sz, kt_sz)
            hit = jnp.where(pred(key_ref[:, pl.ds(off, kt_sz)]), 1, 0)
            part = hit[:, :LANES]
            for t in range(1, kt_sz // LANES):
                part = part + hit[:, t * LANES:(t + 1) * LANES]
            return c + part
        c = lax.fori_loop(0, nkt, body, jnp.zeros((nq, LANES), I32))
        return jnp.sum(c, axis=1, keepdims=True)

    zero = jnp.zeros((nq, 1), I32)
    lo = jnp.where(count(lambda kv: kv >= zero) >= topk, zero, zero + INT_MIN)

    def bit_step(b, lo):
        cand = lo + lax.shift_left(jnp.int32(1), 30 - b)
        return jnp.where(count(lambda kv: kv >= cand) >= topk, cand, lo)

    th = lax.fori_loop(0, 31, bit_step, lo)
    need = jnp.where(th == INT_MIN, 0, topk - count(lambda kv: kv > th)).astype(F32)

    acc_ref[...] = jnp.zeros_like(acc_ref)
    m_ref[...] = jnp.full_like(m_ref, NEG_INF)
    l_ref[...] = jnp.zeros_like(l_ref)
    tri = tri_ref[...]
    low_half = lax.broadcasted_iota(I32, (nq, LANES), 1) < HEAD_DIM

    def attn_tile(j, eq_before):
        off = pl.multiple_of(j * kt_sz, kt_sz)
        eq_run = eq_before
        for t in range(kt_sz // TRI_TILE):
            kv = key_ref[:, pl.ds(off + t * TRI_TILE, TRI_TILE)]
            eq = kv == th
            pre = eq_run + jnp.dot(jnp.where(eq, 1.0, 0.0).astype(BF16), tri,
                                   preferred_element_type=F32)
            sel = (kv > th) | (eq & (pre <= need))
            bias_ref[:, t * TRI_TILE:(t + 1) * TRI_TILE] = jnp.where(sel, 0.0, NEG_INF)
            eq_run = pre[:, TRI_TILE - 1:TRI_TILE]

        def logits(h):
            kh = kt_ref[0, h * HEAD_DIM:(h + 1) * HEAD_DIM, pl.ds(off, kt_sz)]
            return jnp.dot(q_ref[0, h], kh, preferred_element_type=F32) + bias_ref[...]

        groups = [list(range(g, g + DSA_GROUP)) for g in range(0, DSA_HEADS, DSA_GROUP)]
        s_next = [logits(h) for h in groups[0]]
        for gi, group in enumerate(groups):
            s = s_next
            if gi + 1 < len(groups):
                s_next = [logits(h) for h in groups[gi + 1]]
            m_old = [m_ref[h] for h in group]
            m_new = [jnp.maximum(m_old[i], jnp.max(s[i], axis=1, keepdims=True))
                     for i in range(DSA_GROUP)]
            pr = [jnp.exp2(s[i] - m_new[i]) for i in range(DSA_GROUP)]
            scale = [jnp.exp2(m_old[i] - m_new[i]) for i in range(DSA_GROUP)]
            rows = [jnp.sum(pr[i], axis=1, keepdims=True) for i in range(DSA_GROUP)]
            pv = [jnp.dot(pr[i].astype(BF16),
                          v_ref[0, pl.ds(off, kt_sz), (h // 2) * LANES:(h // 2 + 1) * LANES],
                          preferred_element_type=F32) for i, h in enumerate(group)]
            for i, h in enumerate(group):
                l_ref[h] = scale[i] * l_ref[h] + rows[i]
                m_ref[h] = m_new[i]
            for i in range(0, DSA_GROUP, 2):
                pair = group[i] // 2
                cols = slice(pair * LANES, (pair + 1) * LANES)
                acc_ref[:, cols] = (jnp.where(low_half, scale[i], scale[i + 1]) * acc_ref[:, cols]
                                    + jnp.where(low_half, pv[i], pv[i + 1]))
        return eq_run

    lax.fori_loop(0, nkt, attn_tile, jnp.zeros((nq, 1), F32))
    for pair in range(DSA_HEADS // 2):
        cols = slice(pair * LANES, (pair + 1) * LANES)
        inv = jnp.where(low_half, 1.0 / l_ref[2 * pair], 1.0 / l_ref[2 * pair + 1])
        o_ref[0, :, cols] = (acc_ref[:, cols] * inv).astype(BF16)


def _dsa(q, k_t, v, qi, ki_t, wi):
    bsz, seq, w = v.shape
    nq = Q_BLOCK
    topk = min(INDEX_TOPK, seq // 4)
    t_idx = np.arange(TRI_TILE)
    tri = jnp.asarray(t_idx[:, None] <= t_idx[None, :], BF16)
    return pl.pallas_call(
        functools.partial(_dsa_kernel, topk=topk),
        grid=(bsz, seq // nq),
        in_specs=[pl.BlockSpec((1, DSA_HEADS, nq, HEAD_DIM), lambda b, i: (b, 0, i, 0)),
                  pl.BlockSpec((1, w, seq), lambda b, i: (b, 0, 0)),
                  pl.BlockSpec((1, seq, w), lambda b, i: (b, 0, 0)),
                  pl.BlockSpec((1, nq, IDX_HEADS * IDX_DIM), lambda b, i: (b, i, 0)),
                  pl.BlockSpec((1, IDX_DIM, seq), lambda b, i: (b, 0, 0)),
                  pl.BlockSpec((1, nq, LANES), lambda b, i: (b, i, 0)),
                  pl.BlockSpec((TRI_TILE, TRI_TILE), lambda b, i: (0, 0))],
        out_specs=pl.BlockSpec((1, nq, w), lambda b, i: (b, i, 0)),
        out_shape=jax.ShapeDtypeStruct((bsz, seq, w), BF16),
        scratch_shapes=[pltpu.VMEM((nq, seq), I32),
                        pltpu.VMEM((nq, min(KEY_TILE, seq)), F32),
                        pltpu.VMEM((nq, w), F32),
                        pltpu.VMEM((DSA_HEADS, nq, 1), F32),
                        pltpu.VMEM((DSA_HEADS, nq, 1), F32)],
        compiler_params=_params(("parallel", "arbitrary")),
        name="dsa_mix",
    )(q, k_t, v, qi, ki_t, wi, tri)


def _odd_in_kernel(x_ref, mod_ref, w_ref, o_ref):
    m = mod_ref[0]
    h = (x_ref[0] * (1.0 + m[1:2]) + m[0:1]).astype(BF16)
    o_ref[0] = jnp.dot(h, w_ref[...], preferred_element_type=F32)


def _odd_in_proj(x, mod, w_in):
    bsz, seq, d = x.shape
    tm = 512
    n = w_in.shape[1]
    return pl.pallas_call(
        _odd_in_kernel,
        grid=(bsz, seq // tm),
        in_specs=[pl.BlockSpec((1, tm, d), lambda b, i: (b, i, 0)),
                  pl.BlockSpec((1, 6, d), lambda b, i: (b, 0, 0)),
                  pl.BlockSpec((d, n), lambda b, i: (0, 0))],
        out_specs=pl.BlockSpec((1, tm, n), lambda b, i: (b, i, 0)),
        out_shape=jax.ShapeDtypeStruct((bsz, seq, n), F32),
        compiler_params=_params(("parallel", "parallel")),
        name="odd_in_proj",
    )(x, mod, w_in.astype(BF16))


POOL_HALO = 16


def _odd_mix_kernel(p_ref, prev_ref, pw_ref, ps_ref, lng_ref, lnb_ref, ws_ref, bs_ref, o_ref):
    i = pl.program_id(1)
    tm = p_ref.shape[1]
    gd = POOL_GROUP_DIM
    p = p_ref[0]
    prev = prev_ref[0] * (i > 0).astype(F32)
    t_glob = (i * tm + lax.broadcasted_iota(I32, (tm, 1), 0)).astype(F32)
    scale = ps_ref[...]
    for gi, win in enumerate(POOL_WINDOWS):
        cols = slice(gi * gd, (gi + 1) * gd)
        xg = p[:, cols]
        s = jnp.concatenate([prev[:, cols], xg], axis=0)
        span = 1
        while span < win:
            s = s[span:] + s[:-span]
            span *= 2
        first = POOL_HALO + 1 - win
        pooled = s[first:first + tm] / jnp.minimum(t_glob + 1.0, float(win)) - xg
        o_ref[0, :, cols] = (_dot(pooled, pw_ref[gi]) * scale[:, cols]).astype(BF16)

    u = _gelu(p[:, POOL_WIDTH:POOL_WIDTH + SG_WIDTH])
    v = _ln(_gelu(p[:, POOL_WIDTH + SG_WIDTH:]), lng_ref[...], lnb_ref[...], LN_EPS)
    ti = lax.broadcasted_iota(I32, (SG_CHUNK, SG_CHUNK), 0)
    si = lax.broadcasted_iota(I32, (SG_CHUNK, SG_CHUNK), 1)
    bs = bs_ref[...]
    for gi in range(SG_GROUPS):
        cols = slice(gi * SG_GROUP_DIM, (gi + 1) * SG_GROUP_DIM)
        ws = jnp.where(si <= ti, ws_ref[gi], 0.0)
        for n in range(tm // SG_CHUNK):
            rows = slice(n * SG_CHUNK, (n + 1) * SG_CHUNK)
            z = _dot(ws, v[rows, cols]) + bs[:, gi:gi + 1]
            o_ref[0, rows, POOL_WIDTH + gi * SG_GROUP_DIM:POOL_WIDTH + (gi + 1) * SG_GROUP_DIM] = (
                u[rows, cols] * z).astype(BF16)


def _odd_mix(p, pool_w, pool_scale, sg_ln_g, sg_ln_b, sg_w, sg_b):
    bsz, seq, n = p.shape
    tm = 256
    per_tile = tm // POOL_HALO
    full = lambda shape: pl.BlockSpec(shape, lambda b, i: (0,) * len(shape))
    return pl.pallas_call(
        _odd_mix_kernel,
        grid=(bsz, seq // tm),
        in_specs=[pl.BlockSpec((1, tm, n), lambda b, i: (b, i, 0)),
                  pl.BlockSpec((1, POOL_HALO, POOL_WIDTH),
                               lambda b, i: (b, jnp.maximum(i * per_tile - 1, 0), 0)),
                  full(pool_w.shape), full((1, POOL_WIDTH)), full((1, SG_WIDTH)),
                  full((1, SG_WIDTH)), full(sg_w.shape), full((SG_CHUNK, SG_GROUPS))],
        out_specs=pl.BlockSpec((1, tm, D_MODEL), lambda b, i: (b, i, 0)),
        out_shape=jax.ShapeDtypeStruct((bsz, seq, D_MODEL), BF16),
        compiler_params=_params(("parallel", "parallel")),
        name="pool_sgu_mix",
    )(p, p, pool_w.astype(BF16), pool_scale.reshape(1, -1), sg_ln_g.reshape(1, -1),
      sg_ln_b.reshape(1, -1), sg_w, sg_b.T)


def _proj_ln_kernel(*refs, n_in, gate_row):
    a_refs = refs[:n_in]
    w_refs = refs[n_in:2 * n_in]
    x_ref, mod_ref, g_ref, b_ref, o_ref = refs[2 * n_in:]
    y = None
    for a_ref, w_ref in zip(a_refs, w_refs):
        part = jnp.dot(a_ref[0], w_ref[...], preferred_element_type=F32)
        y = part if y is None else y + part
    gate = mod_ref[0][gate_row:gate_row + 1]
    o_ref[0] = _ln(ALPHA * x_ref[0] + gate * y, g_ref[...], b_ref[...], LN_EPS)


def _proj_ln(acts, weights, x, mod, gate_row, ln_g, ln_b, name):
    bsz, seq, d = x.shape
    tm = 512
    n_in = len(acts)
    in_specs = [pl.BlockSpec((1, tm, a.shape[2]), lambda b, i: (b, i, 0)) for a in acts]
    in_specs += [pl.BlockSpec(w.shape, lambda b, i: (0, 0)) for w in weights]
    in_specs += [pl.BlockSpec((1, tm, d), lambda b, i: (b, i, 0)),
                 pl.BlockSpec((1, 6, d), lambda b, i: (b, 0, 0)),
                 pl.BlockSpec((1, d), lambda b, i: (0, 0)),
                 pl.BlockSpec((1, d), lambda b, i: (0, 0))]
    return pl.pallas_call(
        functools.partial(_proj_ln_kernel, n_in=n_in, gate_row=gate_row),
        grid=(bsz, seq // tm),
        in_specs=in_specs,
        out_specs=pl.BlockSpec((1, tm, d), lambda b, i: (b, i, 0)),
        out_shape=jax.ShapeDtypeStruct((bsz, seq, d), F32),
        compiler_params=_params(("parallel", "parallel")),
        name=name,
    )(*acts, *[w.astype(BF16) for w in weights], x, mod, ln_g.reshape(1, d), ln_b.reshape(1, d))


def _ffn_up_kernel(x_ref, xp_ref, mod_ref, wg_ref, wv_ref, cwg_ref, cwv_ref, cbg_ref, cbv_ref,
                   o_ref):
    i = pl.program_id(2)
    m = mod_ref[0]
    sc = 1.0 + m[4:5]
    sh = m[3:4]
    h = (x_ref[0] * sc + sh).astype(BF16)
    hp = (xp_ref[0] * sc + sh).astype(BF16)
    live = (i > 0).astype(F32)
    tm = h.shape[0]
    rowi = lax.broadcasted_iota(I32, (tm, 1), 0)

    def conv(w_ref, cw_ref, cb_ref):
        u = jnp.dot(h, w_ref[...], preferred_element_type=F32)
        up = jnp.dot(hp, w_ref[...], preferred_element_type=F32) * live
        last = up[SUBLANES - 1:SUBLANES]
        last2 = up[SUBLANES - 2:SUBLANES - 1]
        u1 = jnp.where(rowi == 0, last, pltpu.roll(u, 1, 0))
        u2 = jnp.where(rowi == 0, last2, jnp.where(rowi == 1, last, pltpu.roll(u, 2, 0)))
        cw = cw_ref[...]
        return cb_ref[...] + u2 * cw[0:1] + u1 * cw[1:2] + u * cw[2:3]

    gate = conv(wg_ref, cwg_ref, cbg_ref)
    val = conv(wv_ref, cwv_ref, cbv_ref)
    o_ref[0] = (gate * jax.nn.sigmoid(gate) * val).astype(BF16)


def _ffn_up(x, mod, w_up, conv_w, conv_b):
    bsz, seq, d = x.shape
    tm = 512
    tn = D_FF // 2
    nt = D_FF // tn
    per_tile = tm // SUBLANES
    wb = w_up.astype(BF16)
    cb = conv_b.reshape(1, -1)
    return pl.pallas_call(
        _ffn_up_kernel,
        grid=(bsz, nt, seq // tm),
        in_specs=[pl.BlockSpec((1, tm, d), lambda b, n, i: (b, i, 0)),
                  pl.BlockSpec((1, SUBLANES, d),
                               lambda b, n, i: (b, jnp.maximum(i * per_tile - 1, 0), 0)),
                  pl.BlockSpec((1, 6, d), lambda b, n, i: (b, 0, 0)),
                  pl.BlockSpec((d, tn), lambda b, n, i: (0, n)),
                  pl.BlockSpec((d, tn), lambda b, n, i: (0, n + nt)),
                  pl.BlockSpec((3, tn), lambda b, n, i: (0, n)),
                  pl.BlockSpec((3, tn), lambda b, n, i: (0, n + nt)),
                  pl.BlockSpec((1, tn), lambda b, n, i: (0, n)),
                  pl.BlockSpec((1, tn), lambda b, n, i: (0, n + nt))],
        out_specs=pl.BlockSpec((1, tm, tn), lambda b, n, i: (b, i, n)),
        out_shape=jax.ShapeDtypeStruct((bsz, seq, D_FF), BF16),
        compiler_params=_params(("parallel", "parallel", "parallel")),
        name="ffn_up_conv_gate",
    )(x, x, mod, wb, wb, conv_w, conv_w, cb, cb)


def _rope_tables(seq):
    inv = ROPE_THETA ** (-jnp.arange(0, ROPE_DIM, 2, dtype=F32) / ROPE_DIM)
    ang = jnp.arange(seq, dtype=F32)[:, None] * inv[None, :]
    cos, sin = jnp.cos(ang), jnp.sin(ang)
    half = ROPE_DIM // 2
    rest = HEAD_DIM - ROPE_DIM
    one = jnp.ones((seq, rest), F32)
    zero = jnp.zeros((seq, rest), F32)
    zh = jnp.zeros((seq, half), F32)
    head = lambda *parts: jnp.concatenate(parts * (LANES // HEAD_DIM), axis=1)
    return head(cos, cos, one), head(-sin, zh, zero), head(zh, sin, zero)


def kernel(x, c, ada_w, ada_b, ln_g, ln_b, ffn_w_up, ffn_conv_w, ffn_conv_b, ffn_w_down, ev_w_in, ev_w_out, rw_mu, rw_w0, rw_w2, rw_a0, rw_a2, rw_g2, rw_k_k, rw_k_a, rw_r_k, rw_gn_g, rw_gn_b, ik_ln_g, ik_ln_b, od_w_in, od_w_out, pool_w, pool_scale, sg_ln_g, sg_ln_b, sg_w, sg_b):
    seq = x.shape[1]
    tables = _rope_tables(seq)
    mods = _modulation(c, ada_w, ada_b)
    for layer in range(DEPTH):
        mod = mods[layer]
        if layer % 2 == 0:
            e = layer // 2
            p_r, q, k_t, v, qi, ki_t, wi = _even_in_proj(x, mod, ev_w_in[e], ik_ln_g[e],
                                                         ik_ln_b[e], tables)
            ya = _rwkv(p_r, rw_mu[e], rw_w0[e], rw_w2[e], rw_a0[e], rw_a2[e], rw_g2[e],
                       rw_k_k[e], rw_k_a[e], rw_r_k[e], rw_gn_g[e], rw_gn_b[e])
            yb = _dsa(q, k_t, v, qi, ki_t, wi)
            w_out = ev_w_out[e]
            x = _proj_ln([ya, yb], [w_out[:RWKV_WIDTH], w_out[RWKV_WIDTH:]], x, mod, 2,
                         ln_g[layer, 0], ln_b[layer, 0], "even_out_proj_ln")
        else:
            o = layer // 2
            p = _odd_in_proj(x, mod, od_w_in[o])
            yc = _odd_mix(p, pool_w[o], pool_scale[o], sg_ln_g[o], sg_ln_b[o], sg_w[o], sg_b[o])
            x = _proj_ln([yc], [od_w_out[o]], x, mod, 2, ln_g[layer, 0], ln_b[layer, 0],
                         "odd_out_proj_ln")
        act = _ffn_up(x, mod, ffn_w_up[layer], ffn_conv_w[layer], ffn_conv_b[layer])
        x = _proj_ln([act], [ffn_w_down[layer]], x, mod, 5, ln_g[layer, 1], ln_b[layer, 1],
                     "ffn_down_ln")
    return x
```

```python
import functools

import numpy as np
import jax
import jax.numpy as jnp
from jax import lax
from jax.experimental import pallas as pl
from jax.experimental.pallas import tpu as pltpu

F32 = jnp.float32
BF16 = jnp.bfloat16
I32 = jnp.int32

D_MODEL = 1024
DEPTH = 4
CHUNK = 64
HEAD_DIM = 64
RWKV_WIDTH = D_MODEL // 2
RWKV_HEADS = RWKV_WIDTH // HEAD_DIM
RWKV_LORA_W = 64
RWKV_LORA_A = 64
RWKV_LORA_G = 128
RWKV_COLS = 3 * RWKV_WIDTH + RWKV_LORA_W + RWKV_LORA_A + RWKV_LORA_G
DSA_WIDTH = D_MODEL - RWKV_WIDTH
DSA_HEADS = DSA_WIDTH // HEAD_DIM
IDX_HEADS = 4
IDX_DIM = 64
INDEX_TOPK = 256
Q_BLOCK = 128
ROPE_THETA = 500000.0
ROPE_DIM = HEAD_DIM // 4
POOL_WINDOWS = (2, 4, 8, 16)
POOL_WIDTH = D_MODEL // 2
POOL_GROUP_DIM = POOL_WIDTH // len(POOL_WINDOWS)
SG_WIDTH = D_MODEL - POOL_WIDTH
SG_GROUPS = 4
SG_GROUP_DIM = SG_WIDTH // SG_GROUPS
SG_CHUNK = 128
D_FF = 2816
ALPHA = (2.0 * DEPTH) ** 0.25
LN_EPS = 1e-5
GN_EPS = 64e-5
NEG_INF = -1e30
INT_MIN = -(2 ** 31)
LOG2E = 1.4426950408889634

LANES = 128
SUBLANES = 8
VMEM_LIMIT = 56 * 1024 * 1024

RW_TILE = 128
RW_CHUNK = 16
KEY_TILE = 1024
TRI_TILE = 256
PAD_IDX = 384


def _dot(a, b):
    return jnp.dot(a.astype(BF16), b.astype(BF16), preferred_element_type=F32)


def _dot_nt(a, b):
    return lax.dot_general(a.astype(BF16), b.astype(BF16), (((1,), (1,)), ((), ())),
                           preferred_element_type=F32)


def _dot_split(x, m01, terms):
    acc = None
    rem = x
    for _ in range(terms):
        piece = rem.astype(BF16)
        rem = rem - piece.astype(F32)
        part = jnp.dot(piece, m01, preferred_element_type=F32)
        acc = part if acc is None else acc + part
    return acc


def _dot_split_left(m01, x, terms):
    acc = None
    rem = x
    for _ in range(terms):
        piece = rem.astype(BF16)
        rem = rem - piece.astype(F32)
        part = jnp.dot(m01, piece, preferred_element_type=F32)
        acc = part if acc is None else acc + part
    return acc


def _ln(x, g, b, eps):
    mu = jnp.mean(x, axis=-1, keepdims=True)
    xc = x - mu
    var = jnp.mean(xc * xc, axis=-1, keepdims=True)
    return xc * lax.rsqrt(var + eps) * g + b


def _gelu(x):
    return 0.5 * x * (1.0 + lax.erf(x * 0.7071067811865476))


RED_ROWS = 64


def _fold_rows(x, op):
    rows, n = x.shape
    part = op(x.reshape(rows // RED_ROWS, RED_ROWS, n), axis=0)
    return op(part, axis=0, keepdims=True)


def _params(sem):
    return pltpu.CompilerParams(dimension_semantics=sem, vmem_limit_bytes=VMEM_LIMIT)


def _mod_kernel(c_ref, w_ref, b_ref, o_ref):
    c = c_ref[...]
    ca = c * jax.nn.sigmoid(c)
    o_ref[0] = jnp.dot(ca, w_ref[0], preferred_element_type=F32,
                       precision=lax.Precision.HIGHEST) + b_ref[0]


def _modulation(c, ada_w, ada_b):
    bsz, d = c.shape
    depth = ada_w.shape[0]
    n = ada_w.shape[2]
    tn = 1536
    rows = -(-bsz // SUBLANES) * SUBLANES
    c8 = jnp.pad(c, ((0, rows - bsz), (0, 0)))
    out = pl.pallas_call(
        _mod_kernel,
        grid=(depth, n // tn),
        in_specs=[pl.BlockSpec((rows, d), lambda l, j: (0, 0)),
                  pl.BlockSpec((1, d, tn), lambda l, j: (l, 0, j)),
                  pl.BlockSpec((1, 1, tn), lambda l, j: (l, 0, j))],
        out_specs=pl.BlockSpec((1, rows, tn), lambda l, j: (l, 0, j)),
        out_shape=jax.ShapeDtypeStruct((depth, rows, n), F32),
        compiler_params=_params(("arbitrary", "arbitrary")),
        name="adaln_mod",
    )(c8, ada_w, ada_b.reshape(depth, 1, n))
    return out[:, :bsz].reshape(depth, bsz, 6, d)


def _rope(x, cos_t, sin_a, sin_b):
    n = x.shape[1] // LANES
    rep = (lambda t: jnp.concatenate([t] * n, axis=1)) if n > 1 else (lambda t: t)
    width = x.shape[1]
    half = ROPE_DIM // 2
    return (x * rep(cos_t) + pltpu.roll(x, width - half, 1) * rep(sin_a)
            + pltpu.roll(x, half, 1) * rep(sin_b))


def _even_in_kernel(x_ref, mod_ref, wr_ref, wqkv_ref, widx_ref, cos_ref, sa_ref, sb_ref,
                    ikg_ref, ikb_ref,
                    pr_ref, qt_ref, k_ref, vt_ref, qit_ref, ki_ref, wit_ref):
    m = mod_ref[0]
    h = (x_ref[0] * (1.0 + m[1:2]) + m[0:1]).astype(BF16)
    pr_ref[0] = jnp.dot(h, wr_ref[...], preferred_element_type=F32)
    qkv = jnp.dot(h, wqkv_ref[...], preferred_element_type=F32)
    cos_t = cos_ref[...]
    sin_a = sa_ref[...]
    sin_b = sb_ref[...]
    w = DSA_WIDTH
    q = _rope(qkv[:, :w], cos_t, sin_a, sin_b) * (HEAD_DIM ** -0.5 * LOG2E)
    qt_ref[0] = q.T.astype(BF16)
    k_ref[0] = _rope(qkv[:, w:2 * w], cos_t, sin_a, sin_b).astype(BF16)
    vt_ref[0] = qkv[:, 2 * w:].T.astype(BF16)
    idx = jnp.dot(h, widx_ref[...], preferred_element_type=F32)
    nq = IDX_HEADS * IDX_DIM
    qit_ref[0] = _rope(idx[:, :nq], cos_t, sin_a, sin_b).T.astype(BF16)
    blk = idx[:, nq:nq + LANES]
    lane = lax.broadcasted_iota(I32, blk.shape, 1)
    is_k = lane < IDX_DIM
    mu = jnp.sum(jnp.where(is_k, blk, 0.0), axis=1, keepdims=True) * (1.0 / IDX_DIM)
    xc = jnp.where(is_k, blk - mu, 0.0)
    var = jnp.sum(xc * xc, axis=1, keepdims=True) * (1.0 / IDX_DIM)
    kin = xc * lax.rsqrt(var + LN_EPS) * ikg_ref[...] + ikb_ref[...]
    ki_ref[0] = _rope(kin, cos_t, sin_a, sin_b)[:, :IDX_DIM].astype(BF16)
    wit = (blk * (IDX_HEADS ** -0.5 * IDX_DIM ** -0.5)).T
    wit_ref[0] = wit[IDX_DIM:IDX_DIM + SUBLANES]


def _even_in_proj(x, mod, w_in, ik_g, ik_b, tables):
    bsz, seq, d = x.shape
    tm = 256
    w_r = w_in[:, :RWKV_COLS].astype(BF16)
    w_qkv = w_in[:, RWKV_COLS:RWKV_COLS + 3 * DSA_WIDTH].astype(BF16)
    w_idx = w_in[:, RWKV_COLS + 3 * DSA_WIDTH:]
    w_idx = jnp.pad(w_idx, ((0, 0), (0, PAD_IDX - w_idx.shape[1]))).astype(BF16)
    pad = LANES - IDX_DIM
    ikg = jnp.pad(ik_g, (0, pad)).reshape(1, LANES)
    ikb = jnp.pad(ik_b, (0, pad)).reshape(1, LANES)
    cos_t, sin_a, sin_b = tables
    full = lambda shape: pl.BlockSpec(shape, lambda b, i: (0,) * len(shape))
    tab = pl.BlockSpec((tm, LANES), lambda b, i: (i, 0))
    nq = IDX_HEADS * IDX_DIM
    rows = lambda n: pl.BlockSpec((1, tm, n), lambda b, i: (b, i, 0))
    cols = lambda n: pl.BlockSpec((1, n, tm), lambda b, i: (b, 0, i))
    return pl.pallas_call(
        _even_in_kernel,
        grid=(bsz, seq // tm),
        in_specs=[pl.BlockSpec((1, tm, d), lambda b, i: (b, i, 0)),
                  pl.BlockSpec((1, 6, d), lambda b, i: (b, 0, 0)),
                  full(w_r.shape), full(w_qkv.shape), full(w_idx.shape),
                  tab, tab, tab, full((1, LANES)), full((1, LANES))],
        out_specs=[rows(RWKV_COLS), cols(DSA_WIDTH), rows(DSA_WIDTH), cols(DSA_WIDTH),
                   cols(nq), rows(IDX_DIM), cols(SUBLANES)],
        out_shape=[jax.ShapeDtypeStruct((bsz, seq, RWKV_COLS), F32),
                   jax.ShapeDtypeStruct((bsz, DSA_WIDTH, seq), BF16),
                   jax.ShapeDtypeStruct((bsz, seq, DSA_WIDTH), BF16),
                   jax.ShapeDtypeStruct((bsz, DSA_WIDTH, seq), BF16),
                   jax.ShapeDtypeStruct((bsz, nq, seq), BF16),
                   jax.ShapeDtypeStruct((bsz, seq, IDX_DIM), BF16),
                   jax.ShapeDtypeStruct((bsz, SUBLANES, seq), F32)],
        compiler_params=_params(("parallel", "parallel")),
        name="even_in_proj",
    )(x, mod, w_r, w_qkv, w_idx, cos_t, sin_a, sin_b, ikg, ikb)


def _rwkv_kernel(p_ref, pp_ref, mu_ref, vec_ref, w2_ref, a2_ref, g2_ref, ltri_ref, ustr_ref,
                 seg_ref, o_ref, s_ref, obuf_ref):
    i = pl.program_id(1)

    @pl.when(i == 0)
    def _():
        s_ref[...] = jnp.zeros_like(s_ref)

    tt = RW_TILE
    w = RWKV_WIDTH
    hd = HEAD_DIM
    nchunk = tt // RW_CHUNK
    seg = seg_ref[...]
    segsum = lambda t: _dot_split(t, seg, 2)

    p = p_ref[0]
    prow = pp_ref[0][SUBLANES - 1:SUBLANES] * (i > 0).astype(F32)
    rowi = lax.broadcasted_iota(I32, (tt, 1), 0)
    xprev = jnp.where(rowi == 0, prow, pltpu.roll(p, 1, 0))
    ps = p + (xprev - p) * mu_ref[...]
    r = ps[:, :w]
    k = ps[:, w:2 * w]
    v = ps[:, 2 * w:3 * w]
    o1 = 3 * w
    wd = ps[:, o1:o1 + RWKV_LORA_W]
    ad = ps[:, o1 + RWKV_LORA_W:o1 + RWKV_LORA_W + RWKV_LORA_A]
    gd = ps[:, o1 + RWKV_LORA_W + RWKV_LORA_A:]
    vec = vec_ref[...]
    w0, a0, k_k, k_a, r_k, gn_g, gn_b = (vec[j:j + 1] for j in range(7))

    y = -(w0 + _dot(jnp.tanh(wd), w2_ref[...]))
    softplus = jnp.maximum(y, 0.0) + jnp.log1p(jnp.exp(-jnp.abs(y)))
    logw = -jnp.exp(-softplus - 0.5)
    a = jax.nn.sigmoid(a0 + _dot(ad, a2_ref[...]))
    g = _dot(jax.nn.sigmoid(gd), g2_ref[...])
    kk = k * k_k
    kk = kk / jnp.maximum(jnp.sqrt(segsum(kk * kk)), 1e-12)
    k2 = k * (1.0 + (a - 1.0) * k_a)
    bonus = segsum(r * k2 * r_k) * v

    cum = _dot_split_left(ltri_ref[...], logw, 3)
    rem = _dot_split_left(ustr_ref[...], logw, 3)
    pt = jnp.exp(cum)
    ipt = jnp.exp(-cum)
    erem = jnp.exp(rem)
    kka = kk * a
    at = -kk * jnp.exp(cum - logw)
    rt = r * pt
    bt = kka * ipt
    kt = k2 * ipt
    bp = kka * erem
    kp = k2 * erem

    ti = lax.broadcasted_iota(I32, (tt, tt), 0)
    si = lax.broadcasted_iota(I32, (tt, tt), 1)
    same = (ti // RW_CHUNK) == (si // RW_CHUNK)
    strict = same & (si < ti)
    incl = same & (si <= ti)
    tb = lax.broadcasted_iota(I32, (tt, nchunk * hd), 0)
    cb = lax.broadcasted_iota(I32, (tt, nchunk * hd), 1)
    blkmask = (tb // RW_CHUNK) == (cb // hd)
    tile_chunks = lambda t: jnp.where(blkmask, jnp.concatenate([t] * nchunk, axis=1), 0.0)

    heads = range(RWKV_HEADS)
    sls = [slice(h * hd, (h + 1) * hd) for h in heads]
    at_h = [at[:, sl] for sl in sls]
    rt_h = [rt[:, sl] for sl in sls]
    vh = [v[:, sl] for sl in sls]
    x = [_dot_nt(jnp.concatenate([at_h[h], rt_h[h]], axis=0),
                 jnp.concatenate([bt[:, sls[h]], kt[:, sls[h]]], axis=0)) for h in heads]
    a_ab = [jnp.where(strict, x[h][:tt, :tt], 0.0) for h in heads]
    a_ak = [jnp.where(strict, x[h][:tt, tt:], 0.0) for h in heads]
    a_rb = [jnp.where(incl, x[h][tt:, :tt], 0.0) for h in heads]
    a_rk = [jnp.where(incl, x[h][tt:, tt:], 0.0) for h in heads]
    yv = [jnp.concatenate([at_h[h], _dot(a_ak[h], vh[h])], axis=1) for h in heads]
    apow = a_ab
    span = 1
    while True:
        yv = [yv[h] + _dot(apow[h], yv[h]) for h in heads]
        span *= 2
        if span >= RW_CHUNK:
            break
        apow = [_dot(apow[h], apow[h]) for h in heads]
    arb_y = [_dot(a_rb[h], yv[h]) for h in heads]
    ark_v = [_dot(a_rk[h], vh[h]) for h in heads]
    qt = [rt_h[h] + arb_y[h][:, :hd] for h in heads]
    o0 = [arb_y[h][:, hd:] + ark_v[h] for h in heads]
    yt = [yv[h].T for h in heads]
    bpb = [tile_chunks(bp[:, sl]) for sl in sls]
    kpb = [tile_chunks(kp[:, sl]) for sl in sls]
    g_all = [_dot(yt[h][:hd], bpb[h]) for h in heads]
    h_all = [_dot(jnp.concatenate([yt[h][hd:], vh[h].T], axis=1),
                  jnp.concatenate([bpb[h], kpb[h]], axis=0)) for h in heads]
    s = [s_ref[h] for h in heads]
    for n in range(nchunk):
        rows = slice(n * RW_CHUNK, (n + 1) * RW_CHUNK)
        cols = slice(n * hd, (n + 1) * hd)
        for h in heads:
            obuf_ref[rows, sls[h]] = _dot_nt(qt[h][rows], s[h]) + o0[h][rows]
        last = (n + 1) * RW_CHUNK - 1
        s = [s[h] * pt[last:last + 1, sls[h]] + _dot(s[h], g_all[h][:, cols]) + h_all[h][:, cols]
             for h in heads]
    for h in heads:
        s_ref[h] = s[h]

    o = obuf_ref[...]
    mean = segsum(o) * (1.0 / hd)
    oc = o - mean
    var = segsum(oc * oc) * (1.0 / hd)
    on = oc * lax.rsqrt(var + GN_EPS) * gn_g + gn_b
    o_ref[0] = ((on + bonus) * g).astype(BF16)


def _rwkv(p_r, mu, w0, w2, a0, a2, g2, k_k, k_a, r_k, gn_g, gn_b):
    bsz, seq, _ = p_r.shape
    tt = RW_TILE
    w = RWKV_WIDTH
    vec = jnp.stack([w0, a0, k_k, k_a, r_k.reshape(w), gn_g, gn_b, jnp.zeros_like(w0)])
    t_idx = np.arange(tt)
    same = (t_idx[:, None] // RW_CHUNK) == (t_idx[None, :] // RW_CHUNK)
    ltri = jnp.asarray(same & (t_idx[None, :] <= t_idx[:, None]), BF16)
    ustr = jnp.asarray(same & (t_idx[None, :] > t_idx[:, None]), BF16)
    c_idx = np.arange(w)
    seg = jnp.asarray((c_idx[:, None] // HEAD_DIM) == (c_idx[None, :] // HEAD_DIM), BF16)
    full = lambda shape: pl.BlockSpec(shape, lambda b, i: (0,) * len(shape))
    per_tile = tt // SUBLANES
    return pl.pallas_call(
        _rwkv_kernel,
        grid=(bsz, seq // tt),
        in_specs=[pl.BlockSpec((1, tt, RWKV_COLS), lambda b, i: (b, i, 0)),
                  pl.BlockSpec((1, SUBLANES, RWKV_COLS),
                               lambda b, i: (b, jnp.maximum(i * per_tile - 1, 0), 0)),
                  full((1, RWKV_COLS)), full((SUBLANES, w)),
                  full(w2.shape), full(a2.shape), full(g2.shape),
                  full((tt, tt)), full((tt, tt)), full((w, w))],
        out_specs=pl.BlockSpec((1, tt, w), lambda b, i: (b, i, 0)),
        out_shape=jax.ShapeDtypeStruct((bsz, seq, w), BF16),
        scratch_shapes=[pltpu.VMEM((RWKV_HEADS, HEAD_DIM, HEAD_DIM), F32),
                        pltpu.VMEM((tt, w), F32)],
        compiler_params=_params(("parallel", "arbitrary")),
        name="rwkv7_mix",
    )(p_r, p_r, mu.reshape(1, RWKV_COLS), vec, w2.astype(BF16), a2.astype(BF16),
      g2.astype(BF16), ltri, ustr, seg)


def _dsa_kernel(qt_ref, k_ref, vt_ref, qit_ref, ki_ref, wit_ref, low_ref, o_ref,
                key_ref, bias_ref, acc_ref, m_ref, l_ref, *, topk):
    qb = pl.program_id(1)
    nq = Q_BLOCK
    kt_sz = bias_ref.shape[0]
    hd = HEAD_DIM
    start = qb * nq
    nkt = (start + nq + kt_sz - 1) // kt_sz
    col = lax.broadcasted_iota(I32, (1, nq), 1)
    lim = start + (col // CHUNK + 1) * CHUNK
    wit = wit_ref[0]
    qit = qit_ref[0]
    qi_cat = jnp.concatenate([qit[h * IDX_DIM:(h + 1) * IDX_DIM] for h in range(IDX_HEADS)],
                             axis=1)

    def score_tile(j, carry):
        off = pl.multiple_of(j * kt_sz, kt_sz)
        d = jnp.dot(ki_ref[0, pl.ds(off, kt_sz), :], qi_cat, preferred_element_type=F32)
        s = jnp.zeros((kt_sz, nq), F32)
        for h in range(IDX_HEADS):
            s = s + wit[h:h + 1] * jnp.maximum(d[:, h * nq:(h + 1) * nq], 0.0)
        s = s + 0.0
        bits = pltpu.bitcast(s, I32)
        key = bits ^ ((bits >> 31) & 0x7FFFFFFF)
        sidx = off + lax.broadcasted_iota(I32, (kt_sz, nq), 0)
        key_ref[pl.ds(off, kt_sz), :] = jnp.where(sidx < lim, key, INT_MIN)
        return carry

    lax.fori_loop(0, nkt, score_tile, 0)

    def count(pred):
        def body(j, c):
            off = pl.multiple_of(j * kt_sz, kt_sz)
            hit = jnp.where(pred(key_ref[pl.ds(off, kt_sz), :]), 1, 0)
            return c + hit.reshape(kt_sz // RED_ROWS, RED_ROWS, nq).sum(axis=0)
        c = lax.fori_loop(0, nkt, body, jnp.zeros((RED_ROWS, nq), I32))
        return jnp.sum(c, axis=0, keepdims=True)

    zero = jnp.zeros((1, nq), I32)
    lo = jnp.where(count(lambda kv: kv >= zero) >= topk, zero, zero + INT_MIN)

    def bit_step(b, lo):
        cand = lo + lax.shift_left(jnp.int32(1), 30 - b)
        return jnp.where(count(lambda kv: kv >= cand) >= topk, cand, lo)

    th = lax.fori_loop(0, 31, bit_step, lo)
    need = jnp.where(th == INT_MIN, 0, topk - count(lambda kv: kv > th)).astype(F32)

    acc_ref[...] = jnp.zeros_like(acc_ref)
    m_ref[...] = jnp.full_like(m_ref, NEG_INF)
    l_ref[...] = jnp.zeros_like(l_ref)
    low = low_ref[...]
    npair = DSA_HEADS // 2
    top_rows = lax.broadcasted_iota(I32, (2 * hd, nq), 0) < hd
    qt = qt_ref[0].astype(F32)
    q_bd = []
    for p in range(npair):
        qp = qt[p * 2 * hd:(p + 1) * 2 * hd]
        q_bd.append(jnp.concatenate([jnp.where(top_rows, qp, 0.0), jnp.where(top_rows, 0.0, qp)],
                                    axis=1).astype(BF16))

    def attn_tile(j, eq_before):
        off = pl.multiple_of(j * kt_sz, kt_sz)
        eq_run = eq_before
        for t in range(kt_sz // TRI_TILE):
            kv = key_ref[pl.ds(off + t * TRI_TILE, TRI_TILE), :]
            eq = kv == th
            pre = eq_run + jnp.dot(low, jnp.where(eq, 1.0, 0.0).astype(BF16),
                                   preferred_element_type=F32)
            sel = (kv > th) | (eq & (pre <= need))
            bias_ref[t * TRI_TILE:(t + 1) * TRI_TILE, :] = jnp.where(sel, 0.0, NEG_INF)
            eq_run = pre[TRI_TILE - 1:TRI_TILE]
        bias = bias_ref[...]

        def logits(p):
            half = kt_sz // 2
            return jnp.concatenate(
                [jnp.dot(k_ref[0, pl.ds(off + r * half, half), p * 2 * hd:(p + 1) * 2 * hd],
                         q_bd[p], preferred_element_type=F32) for r in range(2)], axis=0)

        st_next = logits(0)
        for p in range(npair):
            cols = slice(p * 2 * hd, (p + 1) * 2 * hd)
            st = st_next
            if p + 1 < npair:
                st_next = logits(p + 1)
            prs, scales = [], []
            for i in range(2):
                h = 2 * p + i
                s = st[:, i * nq:(i + 1) * nq] + bias
                m_old = m_ref[h:h + 1]
                m_new = jnp.maximum(m_old, _fold_rows(s, jnp.max))
                pr = jnp.exp2(s - m_new)
                scale = jnp.exp2(m_old - m_new)
                l_ref[h:h + 1] = scale * l_ref[h:h + 1] + _fold_rows(pr, jnp.sum)
                m_ref[h:h + 1] = m_new
                prs.append(pr.astype(BF16))
                scales.append(scale)
            pv = jnp.dot(vt_ref[0, cols, pl.ds(off, kt_sz)], jnp.concatenate(prs, axis=1),
                         preferred_element_type=F32)
            new = jnp.where(top_rows, pv[:, :nq], pv[:, nq:])
            acc_ref[p] = jnp.where(top_rows, scales[0], scales[1]) * acc_ref[p] + new
        return eq_run

    lax.fori_loop(0, nkt, attn_tile, jnp.zeros((1, nq), F32))
    for p in range(npair):
        inv = jnp.where(top_rows, 1.0 / l_ref[2 * p:2 * p + 1], 1.0 / l_ref[2 * p + 1:2 * p + 2])
        o_ref[0, :, p * 2 * hd:(p + 1) * 2 * hd] = (acc_ref[p] * inv).T.astype(BF16)


def _dsa(q_t, k, v_t, qi_t, ki, wi_t):
    bsz, seq, w = k.shape
    nq = Q_BLOCK
    topk = min(INDEX_TOPK, seq // 4)
    kt_sz = min(KEY_TILE, seq)
    t_idx = np.arange(TRI_TILE)
    low = jnp.asarray(t_idx[None, :] <= t_idx[:, None], BF16)
    cols = lambda n: pl.BlockSpec((1, n, nq), lambda b, i: (b, 0, i))
    return pl.pallas_call(
        functools.partial(_dsa_kernel, topk=topk),
        grid=(bsz, seq // nq),
        in_specs=[cols(w),
                  pl.BlockSpec((1, seq, w), lambda b, i: (b, 0, 0)),
                  pl.BlockSpec((1, w, seq), lambda b, i: (b, 0, 0)),
                  cols(IDX_HEADS * IDX_DIM),
                  pl.BlockSpec((1, seq, IDX_DIM), lambda b, i: (b, 0, 0)),
                  cols(SUBLANES),
                  pl.BlockSpec((TRI_TILE, TRI_TILE), lambda b, i: (0, 0))],
        out_specs=pl.BlockSpec((1, nq, w), lambda b, i: (b, i, 0)),
        out_shape=jax.ShapeDtypeStruct((bsz, seq, w), BF16),
        scratch_shapes=[pltpu.VMEM((seq, nq), I32),
                        pltpu.VMEM((kt_sz, nq), F32),
                        pltpu.VMEM((DSA_HEADS // 2, 2 * HEAD_DIM, nq), F32),
                        pltpu.VMEM((DSA_HEADS, nq), F32),
                        pltpu.VMEM((DSA_HEADS, nq), F32)],
        compiler_params=_params(("parallel", "arbitrary")),
        name="dsa_mix",
    )(q_t, k, v_t, qi_t, ki, wi_t, low)


def _odd_in_kernel(x_ref, mod_ref, w_ref, o_ref):
    m = mod_ref[0]
    h = (x_ref[0] * (1.0 + m[1:2]) + m[0:1]).astype(BF16)
    o_ref[0] = jnp.dot(h, w_ref[...], preferred_element_type=F32)


def _odd_in_proj(x, mod, w_in):
    bsz, seq, d = x.shape
    tm = 512
    n = w_in.shape[1]
    return pl.pallas_call(
        _odd_in_kernel,
        grid=(bsz, seq // tm),
        in_specs=[pl.BlockSpec((1, tm, d), lambda b, i: (b, i, 0)),
                  pl.BlockSpec((1, 6, d), lambda b, i: (b, 0, 0)),
                  pl.BlockSpec((d, n), lambda b, i: (0, 0))],
        out_specs=pl.BlockSpec((1, tm, n), lambda b, i: (b, i, 0)),
        out_shape=jax.ShapeDtypeStruct((bsz, seq, n), F32),
        compiler_params=_params(("parallel", "parallel")),
        name="odd_in_proj",
    )(x, mod, w_in.astype(BF16))


POOL_HALO = 16


def _odd_mix_kernel(p_ref, prev_ref, pw_ref, ps_ref, lng_ref, lnb_ref, ws_ref, bs_ref, o_ref):
    i = pl.program_id(1)
    tm = p_ref.shape[1]
    gd = POOL_GROUP_DIM
    p = p_ref[0]
    prev = prev_ref[0] * (i > 0).astype(F32)
    t_glob = (i * tm + lax.broadcasted_iota(I32, (tm, 1), 0)).astype(F32)
    scale = ps_ref[...]
    for gi, win in enumerate(POOL_WINDOWS):
        cols = slice(gi * gd, (gi + 1) * gd)
        xg = p[:, cols]
        s = jnp.concatenate([prev[:, cols], xg], axis=0)
        span = 1
        while span < win:
            s = s[span:] + s[:-span]
            span *= 2
        first = POOL_HALO + 1 - win
        pooled = s[first:first + tm] / jnp.minimum(t_glob + 1.0, float(win)) - xg
        o_ref[0, :, cols] = (_dot(pooled, pw_ref[gi]) * scale[:, cols]).astype(BF16)

    u = _gelu(p[:, POOL_WIDTH:POOL_WIDTH + SG_WIDTH])
    v = _ln(_gelu(p[:, POOL_WIDTH + SG_WIDTH:]), lng_ref[...], lnb_ref[...], LN_EPS)
    ti = lax.broadcasted_iota(I32, (SG_CHUNK, SG_CHUNK), 0)
    si = lax.broadcasted_iota(I32, (SG_CHUNK, SG_CHUNK), 1)
    bs = bs_ref[...]
    for gi in range(SG_GROUPS):
        cols = slice(gi * SG_GROUP_DIM, (gi + 1) * SG_GROUP_DIM)
        ws = jnp.where(si <= ti, ws_ref[gi], 0.0)
        for n in range(tm // SG_CHUNK):
            rows = slice(n * SG_CHUNK, (n + 1) * SG_CHUNK)
            z = _dot(ws, v[rows, cols]) + bs[:, gi:gi + 1]
            o_ref[0, rows, POOL_WIDTH + gi * SG_GROUP_DIM:POOL_WIDTH + (gi + 1) * SG_GROUP_DIM] = (
                u[rows, cols] * z).astype(BF16)


def _odd_mix(p, pool_w, pool_scale, sg_ln_g, sg_ln_b, sg_w, sg_b):
    bsz, seq, n = p.shape
    tm = 256
    per_tile = tm // POOL_HALO
    full = lambda shape: pl.BlockSpec(shape, lambda b, i: (0,) * len(shape))
    return pl.pallas_call(
        _odd_mix_kernel,
        grid=(bsz, seq // tm),
        in_specs=[pl.BlockSpec((1, tm, n), lambda b, i: (b, i, 0)),
                  pl.BlockSpec((1, POOL_HALO, POOL_WIDTH),
                               lambda b, i: (b, jnp.maximum(i * per_tile - 1, 0), 0)),
                  full(pool_w.shape), full((1, POOL_WIDTH)), full((1, SG_WIDTH)),
                  full((1, SG_WIDTH)), full(sg_w.shape), full((SG_CHUNK, SG_GROUPS))],
        out_specs=pl.BlockSpec((1, tm, D_MODEL), lambda b, i: (b, i, 0)),
        out_shape=jax.ShapeDtypeStruct((bsz, seq, D_MODEL), BF16),
        compiler_params=_params(("parallel", "parallel")),
        name="pool_sgu_mix",
    )(p, p, pool_w.astype(BF16), pool_scale.reshape(1, -1), sg_ln_g.reshape(1, -1),
      sg_ln_b.reshape(1, -1), sg_w, sg_b.T)


def _proj_ln_kernel(*refs, n_in, gate_row):
    a_refs = refs[:n_in]
    w_refs = refs[n_in:2 * n_in]
    x_ref, mod_ref, g_ref, b_ref, o_ref = refs[2 * n_in:]
    y = None
    for a_ref, w_ref in zip(a_refs, w_refs):
        part = jnp.dot(a_ref[0], w_ref[...], preferred_element_type=F32)
        y = part if y is None else y + part
    gate = mod_ref[0][gate_row:gate_row + 1]
    o_ref[0] = _ln(ALPHA * x_ref[0] + gate * y, g_ref[...], b_ref[...], LN_EPS)


def _proj_ln(acts, weights, x, mod, gate_row, ln_g, ln_b, name):
    bsz, seq, d = x.shape
    tm = 512
    n_in = len(acts)
    in_specs = [pl.BlockSpec((1, tm, a.shape[2]), lambda b, i: (b, i, 0)) for a in acts]
    in_specs += [pl.BlockSpec(w.shape, lambda b, i: (0, 0)) for w in weights]
    in_specs += [pl.BlockSpec((1, tm, d), lambda b, i: (b, i, 0)),
                 pl.BlockSpec((1, 6, d), lambda b, i: (b, 0, 0)),
                 pl.BlockSpec((1, d), lambda b, i: (0, 0)),
                 pl.BlockSpec((1, d), lambda b, i: (0, 0))]
    return pl.pallas_call(
        functools.partial(_proj_ln_kernel, n_in=n_in, gate_row=gate_row),
        grid=(bsz, seq // tm),
        in_specs=in_specs,
        out_specs=pl.BlockSpec((1, tm, d), lambda b, i: (b, i, 0)),
        out_shape=jax.ShapeDtypeStruct((bsz, seq, d), F32),
        compiler_params=_params(("parallel", "parallel")),
        name=name,
    )(*acts, *[w.astype(BF16) for w in weights], x, mod, ln_g.reshape(1, d), ln_b.reshape(1, d))


def _ffn_up_kernel(x_ref, xp_ref, mod_ref, wg_ref, wv_ref, cwg_ref, cwv_ref, cbg_ref, cbv_ref,
                   o_ref):
    i = pl.program_id(2)
    m = mod_ref[0]
    sc = 1.0 + m[4:5]
    sh = m[3:4]
    h = (x_ref[0] * sc + sh).astype(BF16)
    hp = (xp_ref[0] * sc + sh).astype(BF16)
    live = (i > 0).astype(F32)
    tm = h.shape[0]
    rowi = lax.broadcasted_iota(I32, (tm, 1), 0)

    def conv(w_ref, cw_ref, cb_ref):
        u = jnp.dot(h, w_ref[...], preferred_element_type=F32)
        up = jnp.dot(hp, w_ref[...], preferred_element_type=F32) * live
        last = up[SUBLANES - 1:SUBLANES]
        last2 = up[SUBLANES - 2:SUBLANES - 1]
        u1 = jnp.where(rowi == 0, last, pltpu.roll(u, 1, 0))
        u2 = jnp.where(rowi == 0, last2, jnp.where(rowi == 1, last, pltpu.roll(u, 2, 0)))
        cw = cw_ref[...]
        return cb_ref[...] + u2 * cw[0:1] + u1 * cw[1:2] + u * cw[2:3]

    gate = conv(wg_ref, cwg_ref, cbg_ref)
    val = conv(wv_ref, cwv_ref, cbv_ref)
    o_ref[0] = (gate * jax.nn.sigmoid(gate) * val).astype(BF16)


def _ffn_up(x, mod, w_up, conv_w, conv_b):
    bsz, seq, d = x.shape
    tm = 512
    tn = D_FF // 2
    nt = D_FF // tn
    per_tile = tm // SUBLANES
    wb = w_up.astype(BF16)
    cb = conv_b.reshape(1, -1)
    return pl.pallas_call(
        _ffn_up_kernel,
        grid=(bsz, nt, seq // tm),
        in_specs=[pl.BlockSpec((1, tm, d), lambda b, n, i: (b, i, 0)),
                  pl.BlockSpec((1, SUBLANES, d),
                               lambda b, n, i: (b, jnp.maximum(i * per_tile - 1, 0), 0)),
                  pl.BlockSpec((1, 6, d), lambda b, n, i: (b, 0, 0)),
                  pl.BlockSpec((d, tn), lambda b, n, i: (0, n)),
                  pl.BlockSpec((d, tn), lambda b, n, i: (0, n + nt)),
                  pl.BlockSpec((3, tn), lambda b, n, i: (0, n)),
                  pl.BlockSpec((3, tn), lambda b, n, i: (0, n + nt)),
                  pl.BlockSpec((1, tn), lambda b, n, i: (0, n)),
                  pl.BlockSpec((1, tn), lambda b, n, i: (0, n + nt))],
        out_specs=pl.BlockSpec((1, tm, tn), lambda b, n, i: (b, i, n)),
        out_shape=jax.ShapeDtypeStruct((bsz, seq, D_FF), BF16),
        compiler_params=_params(("parallel", "parallel", "parallel")),
        name="ffn_up_conv_gate",
    )(x, x, mod, wb, wb, conv_w, conv_w, cb, cb)


def _rope_tables(seq):
    inv = ROPE_THETA ** (-jnp.arange(0, ROPE_DIM, 2, dtype=F32) / ROPE_DIM)
    ang = jnp.arange(seq, dtype=F32)[:, None] * inv[None, :]
    cos, sin = jnp.cos(ang), jnp.sin(ang)
    half = ROPE_DIM // 2
    rest = HEAD_DIM - ROPE_DIM
    one = jnp.ones((seq, rest), F32)
    zero = jnp.zeros((seq, rest), F32)
    zh = jnp.zeros((seq, half), F32)
    head = lambda *parts: jnp.concatenate(parts * (LANES // HEAD_DIM), axis=1)
    return head(cos, cos, one), head(-sin, zh, zero), head(zh, sin, zero)


def kernel(x, c, ada_w, ada_b, ln_g, ln_b, ffn_w_up, ffn_conv_w, ffn_conv_b, ffn_w_down, ev_w_in, ev_w_out, rw_mu, rw_w0, rw_w2, rw_a0, rw_a2, rw_g2, rw_k_k, rw_k_a, rw_r_k, rw_gn_g, rw_gn_b, ik_ln_g, ik_ln_b, od_w_in, od_w_out, pool_w, pool_scale, sg_ln_g, sg_ln_b, sg_w, sg_b):
    seq = x.shape[1]
    tables = _rope_tables(seq)
    mods = _modulation(c, ada_w, ada_b)
    for layer in range(DEPTH):
        mod = mods[layer]
        if layer % 2 == 0:
            e = layer // 2
            p_r, q_t, k, v_t, qi_t, ki, wi_t = _even_in_proj(x, mod, ev_w_in[e], ik_ln_g[e],
                                                             ik_ln_b[e], tables)
            ya = _rwkv(p_r, rw_mu[e], rw_w0[e], rw_w2[e], rw_a0[e], rw_a2[e], rw_g2[e],
                       rw_k_k[e], rw_k_a[e], rw_r_k[e], rw_gn_g[e], rw_gn_b[e])
            yb = _dsa(q_t, k, v_t, qi_t, ki, wi_t)
            w_out = ev_w_out[e]
            x = _proj_ln([ya, yb], [w_out[:RWKV_WIDTH], w_out[RWKV_WIDTH:]], x, mod, 2,
                         ln_g[layer, 0], ln_b[layer, 0], "even_out_proj_ln")
        else:
            o = layer // 2
            p = _odd_in_proj(x, mod, od_w_in[o])
            yc = _odd_mix(p, pool_w[o], pool_scale[o], sg_ln_g[o], sg_ln_b[o], sg_w[o], sg_b[o])
            x = _proj_ln([yc], [od_w_out[o]], x, mod, 2, ln_g[layer, 0], ln_b[layer, 0],
                         "odd_out_proj_ln")
        act = _ffn_up(x, mod, ffn_w_up[layer], ffn_conv_w[layer], ffn_conv_b[layer])
        x = _proj_ln([act], [ffn_w_down[layer]], x, mod, 5, ln_g[layer, 1], ln_b[layer, 1],
                     "ffn_down_ln")
    return x
```

```python
import functools

import numpy as np
import jax
import jax.numpy as jnp
from jax import lax
from jax.experimental import pallas as pl
from jax.experimental.pallas import tpu as pltpu

F32 = jnp.float32
BF16 = jnp.bfloat16
I32 = jnp.int32

D_MODEL = 1024
DEPTH = 4
CHUNK = 64
HEAD_DIM = 64
RWKV_WIDTH = D_MODEL // 2
RWKV_HEADS = RWKV_WIDTH // HEAD_DIM
RWKV_LORA_W = 64
RWKV_LORA_A = 64
RWKV_LORA_G = 128
RWKV_COLS = 3 * RWKV_WIDTH + RWKV_LORA_W + RWKV_LORA_A + RWKV_LORA_G
DSA_WIDTH = D_MODEL - RWKV_WIDTH
DSA_HEADS = DSA_WIDTH // HEAD_DIM
IDX_HEADS = 4
IDX_DIM = 64
INDEX_TOPK = 256
Q_BLOCK = 128
ROPE_THETA = 500000.0
ROPE_DIM = HEAD_DIM // 4
POOL_WINDOWS = (2, 4, 8, 16)
POOL_WIDTH = D_MODEL // 2
POOL_GROUP_DIM = POOL_WIDTH // len(POOL_WINDOWS)
SG_WIDTH = D_MODEL - POOL_WIDTH
SG_GROUPS = 4
SG_GROUP_DIM = SG_WIDTH // SG_GROUPS
SG_CHUNK = 128
D_FF = 2816
ALPHA = (2.0 * DEPTH) ** 0.25
LN_EPS = 1e-5
GN_EPS = 64e-5
NEG_INF = -1e30
INT_MIN = -(2 ** 31)
LOG2E = 1.4426950408889634

LANES = 128
SUBLANES = 8
VMEM_LIMIT = 56 * 1024 * 1024

RW_TILE = 128
RW_CHUNK = 16
RW_ROWS = 2
KEY_TILE = 1024
TRI_TILE = 256
PAD_IDX = 384

def _dot(a, b):
    return jnp.dot(a.astype(BF16), b.astype(BF16), preferred_element_type=F32)


def _dot_nt(a, b):
    return lax.dot_general(a.astype(BF16), b.astype(BF16), (((1,), (1,)), ((), ())),
                           preferred_element_type=F32)


def _dot_split(x, m01, terms):
    acc = None
    rem = x
    for _ in range(terms):
        piece = rem.astype(BF16)
        rem = rem - piece.astype(F32)
        part = jnp.dot(piece, m01, preferred_element_type=F32)
        acc = part if acc is None else acc + part
    return acc


def _dot_split_left(m01, x, terms):
    acc = None
    rem = x
    for _ in range(terms):
        piece = rem.astype(BF16)
        rem = rem - piece.astype(F32)
        part = jnp.dot(m01, piece, preferred_element_type=F32)
        acc = part if acc is None else acc + part
    return acc


def _ln(x, g, b, eps):
    mu = jnp.mean(x, axis=-1, keepdims=True)
    xc = x - mu
    var = jnp.mean(xc * xc, axis=-1, keepdims=True)
    return xc * lax.rsqrt(var + eps) * g + b


def _gelu(x):
    return 0.5 * x * (1.0 + lax.erf(x * 0.7071067811865476))


RED_ROWS = 64


def _fold_rows(x, op):
    rows, n = x.shape
    part = op(x.reshape(rows // RED_ROWS, RED_ROWS, n), axis=0)
    return op(part, axis=0, keepdims=True)


def _params(sem):
    return pltpu.CompilerParams(dimension_semantics=sem, vmem_limit_bytes=VMEM_LIMIT)


def _mod_kernel(c_ref, w_ref, b_ref, o_ref):
    c = c_ref[...]
    ca = c * jax.nn.sigmoid(c)
    o_ref[0] = jnp.dot(ca, w_ref[0], preferred_element_type=F32,
                       precision=lax.Precision.HIGHEST) + b_ref[0]


def _modulation(c, ada_w, ada_b):
    bsz, d = c.shape
    depth = ada_w.shape[0]
    n = ada_w.shape[2]
    tn = 1536
    rows = -(-bsz // SUBLANES) * SUBLANES
    c8 = jnp.pad(c, ((0, rows - bsz), (0, 0)))
    out = pl.pallas_call(
        _mod_kernel,
        grid=(depth, n // tn),
        in_specs=[pl.BlockSpec((rows, d), lambda l, j: (0, 0)),
                  pl.BlockSpec((1, d, tn), lambda l, j: (l, 0, j)),
                  pl.BlockSpec((1, 1, tn), lambda l, j: (l, 0, j))],
        out_specs=pl.BlockSpec((1, rows, tn), lambda l, j: (l, 0, j)),
        out_shape=jax.ShapeDtypeStruct((depth, rows, n), F32),
        compiler_params=_params(("arbitrary", "arbitrary")),
        name="adaln_mod",
    )(c8, ada_w, ada_b.reshape(depth, 1, n))
    return out[:, :bsz].reshape(depth, bsz, 6, d)


def _rope(x, cos_t, sin_a, sin_b):
    n = x.shape[1] // LANES
    rep = (lambda t: jnp.concatenate([t] * n, axis=1)) if n > 1 else (lambda t: t)
    width = x.shape[1]
    half = ROPE_DIM // 2
    return (x * rep(cos_t) + pltpu.roll(x, width - half, 1) * rep(sin_a)
            + pltpu.roll(x, half, 1) * rep(sin_b))


def _even_in_kernel(x_ref, mod_ref, wr_ref, wqkv_ref, widx_ref, cos_ref, sa_ref, sb_ref,
                    ikg_ref, ikb_ref,
                    pr_ref, qt_ref, k_ref, vt_ref, qit_ref, ki_ref, wit_ref):
    m = mod_ref[0]
    h = (x_ref[0] * (1.0 + m[1:2]) + m[0:1]).astype(BF16)
    pr_ref[0] = jnp.dot(h, wr_ref[...], preferred_element_type=F32)
    qkv = jnp.dot(h, wqkv_ref[...], preferred_element_type=F32)
    cos_t = cos_ref[...]
    sin_a = sa_ref[...]
    sin_b = sb_ref[...]
    w = DSA_WIDTH
    q = _rope(qkv[:, :w], cos_t, sin_a, sin_b) * (HEAD_DIM ** -0.5 * LOG2E)
    qt_ref[0] = q.T.astype(BF16)
    k_ref[0] = _rope(qkv[:, w:2 * w], cos_t, sin_a, sin_b).astype(BF16)
    vt_ref[0] = qkv[:, 2 * w:].T.astype(BF16)
    idx = jnp.dot(h, widx_ref[...], preferred_element_type=F32)
    nq = IDX_HEADS * IDX_DIM
    qit_ref[0] = _rope(idx[:, :nq], cos_t, sin_a, sin_b).T.astype(BF16)
    blk = idx[:, nq:nq + LANES]
    lane = lax.broadcasted_iota(I32, blk.shape, 1)
    is_k = lane < IDX_DIM
    mu = jnp.sum(jnp.where(is_k, blk, 0.0), axis=1, keepdims=True) * (1.0 / IDX_DIM)
    xc = jnp.where(is_k, blk - mu, 0.0)
    var = jnp.sum(xc * xc, axis=1, keepdims=True) * (1.0 / IDX_DIM)
    kin = xc * lax.rsqrt(var + LN_EPS) * ikg_ref[...] + ikb_ref[...]
    ki_ref[0] = _rope(kin, cos_t, sin_a, sin_b)[:, :IDX_DIM].astype(BF16)
    wit = (blk * (IDX_HEADS ** -0.5 * IDX_DIM ** -0.5)).T
    wit_ref[0] = wit[IDX_DIM:IDX_DIM + SUBLANES]


def _split_w_kernel(w_ref, wr_ref, wqkv_ref, widx_ref):
    w = w_ref[0]
    c1 = RWKV_COLS
    c2 = RWKV_COLS + 3 * DSA_WIDTH
    wr_ref[0] = w[:, :c1].astype(BF16)
    wqkv_ref[0] = w[:, c1:c2].astype(BF16)
    tail = w[:, c2:]
    zeros = jnp.zeros((w.shape[0], PAD_IDX - tail.shape[1]), F32)
    widx_ref[0] = jnp.concatenate([tail, zeros], axis=1).astype(BF16)


def _split_even_weights(ev_w_in):
    n_even, d, n = ev_w_in.shape
    tr = 256
    widths = (RWKV_COLS, 3 * DSA_WIDTH, PAD_IDX)
    return pl.pallas_call(
        _split_w_kernel,
        grid=(n_even, d // tr),
        in_specs=[pl.BlockSpec((1, tr, n), lambda e, i: (e, i, 0))],
        out_specs=[pl.BlockSpec((1, tr, wd), lambda e, i: (e, i, 0)) for wd in widths],
        out_shape=[jax.ShapeDtypeStruct((n_even, d, wd), BF16) for wd in widths],
        compiler_params=_params(("parallel", "parallel")),
        name="split_even_weights",
    )(ev_w_in)


def _even_in_proj(x, mod, weights, ik_g, ik_b, tables):
    bsz, seq, d = x.shape
    tm = 256
    w_r, w_qkv, w_idx = weights
    pad = LANES - IDX_DIM
    ikg = jnp.pad(ik_g, (0, pad)).reshape(1, LANES)
    ikb = jnp.pad(ik_b, (0, pad)).reshape(1, LANES)
    cos_t, sin_a, sin_b = tables
    full = lambda shape: pl.BlockSpec(shape, lambda b, i: (0,) * len(shape))
    tab = pl.BlockSpec((tm, LANES), lambda b, i: (i, 0))
    nq = IDX_HEADS * IDX_DIM
    rows = lambda n: pl.BlockSpec((1, tm, n), lambda b, i: (b, i, 0))
    cols = lambda n: pl.BlockSpec((1, n, tm), lambda b, i: (b, 0, i))
    return pl.pallas_call(
        _even_in_kernel,
        grid=(bsz, seq // tm),
        in_specs=[pl.BlockSpec((1, tm, d), lambda b, i: (b, i, 0)),
                  pl.BlockSpec((1, 6, d), lambda b, i: (b, 0, 0)),
                  full(w_r.shape), full(w_qkv.shape), full(w_idx.shape),
                  tab, tab, tab, full((1, LANES)), full((1, LANES))],
        out_specs=[rows(RWKV_COLS), cols(DSA_WIDTH), rows(DSA_WIDTH), cols(DSA_WIDTH),
                   cols(nq), rows(IDX_DIM), cols(SUBLANES)],
        out_shape=[jax.ShapeDtypeStruct((bsz, seq, RWKV_COLS), F32),
                   jax.ShapeDtypeStruct((bsz, DSA_WIDTH, seq), BF16),
                   jax.ShapeDtypeStruct((bsz, seq, DSA_WIDTH), BF16),
                   jax.ShapeDtypeStruct((bsz, DSA_WIDTH, seq), BF16),
                   jax.ShapeDtypeStruct((bsz, nq, seq), BF16),
                   jax.ShapeDtypeStruct((bsz, seq, IDX_DIM), BF16),
                   jax.ShapeDtypeStruct((bsz, SUBLANES, seq), F32)],
        compiler_params=_params(("parallel", "parallel")),
        name="even_in_proj",
    )(x, mod, w_r, w_qkv, w_idx, cos_t, sin_a, sin_b, ikg, ikb)


def _rwkv_kernel(p_ref, pp_ref, mu_ref, vec_ref, w2_ref, a2_ref, g2_ref, ltri_ref, ustr_ref,
                 seg_ref, o_ref, s_ref, obuf_ref):
    i = pl.program_id(1)

    @pl.when(i == 0)
    def _():
        s_ref[...] = jnp.zeros_like(s_ref)

    tt = RW_TILE
    w = RWKV_WIDTH
    hd = HEAD_DIM
    nb = p_ref.shape[0]
    nchunk = tt // RW_CHUNK
    seg = seg_ref[...]
    segsum = lambda t: _dot_split(t, seg, 2)

    vec = vec_ref[...]
    w0, a0, k_k, k_a, r_k, gn_g, gn_b = (vec[j:j + 1] for j in range(7))
    rowi = lax.broadcasted_iota(I32, (tt, 1), 0)
    live = (i > 0).astype(F32)

    def prepare(b):
        p = p_ref[b]
        prow = pp_ref[b][SUBLANES - 1:SUBLANES] * live
        xprev = jnp.where(rowi == 0, prow, pltpu.roll(p, 1, 0))
        ps = p + (xprev - p) * mu_ref[...]
        r = ps[:, :w]
        k = ps[:, w:2 * w]
        v = ps[:, 2 * w:3 * w]
        o1 = 3 * w
        wd = ps[:, o1:o1 + RWKV_LORA_W]
        ad = ps[:, o1 + RWKV_LORA_W:o1 + RWKV_LORA_W + RWKV_LORA_A]
        gd = ps[:, o1 + RWKV_LORA_W + RWKV_LORA_A:]
        y = -(w0 + _dot(jnp.tanh(wd), w2_ref[...]))
        softplus = jnp.maximum(y, 0.0) + jnp.log1p(jnp.exp(-jnp.abs(y)))
        logw = -jnp.exp(-softplus - 0.5)
        a = jax.nn.sigmoid(a0 + _dot(ad, a2_ref[...]))
        g = _dot(jax.nn.sigmoid(gd), g2_ref[...])
        kk = k * k_k
        kk = kk / jnp.maximum(jnp.sqrt(segsum(kk * kk)), 1e-12)
        k2 = k * (1.0 + (a - 1.0) * k_a)
        bonus = segsum(r * k2 * r_k) * v
        cum = _dot_split_left(ltri_ref[...], logw, 3)
        rem = _dot_split_left(ustr_ref[...], logw, 3)
        pt = jnp.exp(cum)
        ipt = jnp.exp(-cum)
        erem = jnp.exp(rem)
        kka = kk * a
        return dict(at=-kk * jnp.exp(cum - logw), rt=r * pt, bt=kka * ipt, kt=k2 * ipt,
                    bp=kka * erem, kp=k2 * erem, v=v, pt=pt, bonus=bonus, g=g)

    rows_in = [prepare(b) for b in range(nb)]

    ti = lax.broadcasted_iota(I32, (tt, tt), 0)
    si = lax.broadcasted_iota(I32, (tt, tt), 1)
    same = (ti // RW_CHUNK) == (si // RW_CHUNK)
    strict = same & (si < ti)
    incl = same & (si <= ti)
    tb = lax.broadcasted_iota(I32, (tt, nchunk * hd), 0)
    cb = lax.broadcasted_iota(I32, (tt, nchunk * hd), 1)
    blkmask = (tb // RW_CHUNK) == (cb // hd)
    tile_chunks = lambda t: jnp.where(blkmask, jnp.concatenate([t] * nchunk, axis=1), 0.0)

    units = [(b, slice(h * hd, (h + 1) * hd)) for b in range(nb) for h in range(RWKV_HEADS)]
    idx = range(len(units))
    pick = lambda name: [rows_in[b][name][:, sl] for b, sl in units]
    at_h, rt_h, vh, bt_h, kt_h = pick("at"), pick("rt"), pick("v"), pick("bt"), pick("kt")
    x = [_dot_nt(jnp.concatenate([at_h[u], rt_h[u]], axis=0),
                 jnp.concatenate([bt_h[u], kt_h[u]], axis=0)) for u in idx]
    a_ab = [jnp.where(strict, x[u][:tt, :tt], 0.0) for u in idx]
    a_ak = [jnp.where(strict, x[u][:tt, tt:], 0.0) for u in idx]
    a_rb = [jnp.where(incl, x[u][tt:, :tt], 0.0) for u in idx]
    a_rk = [jnp.where(incl, x[u][tt:, tt:], 0.0) for u in idx]
    yv = [jnp.concatenate([at_h[u], _dot(a_ak[u], vh[u])], axis=1) for u in idx]
    apow = a_ab
    span = 1
    while True:
        yv = [yv[u] + _dot(apow[u], yv[u]) for u in idx]
        span *= 2
        if span >= RW_CHUNK:
            break
        apow = [_dot(apow[u], apow[u]) for u in idx]
    arb_y = [_dot(a_rb[u], yv[u]) for u in idx]
    ark_v = [_dot(a_rk[u], vh[u]) for u in idx]
    qt = [rt_h[u] + arb_y[u][:, :hd] for u in idx]
    o0 = [arb_y[u][:, hd:] + ark_v[u] for u in idx]
    yt = [yv[u].T for u in idx]
    bpb = [tile_chunks(t) for t in pick("bp")]
    kpb = [tile_chunks(t) for t in pick("kp")]
    g_all = [_dot(yt[u][:hd], bpb[u]) for u in idx]
    h_all = [_dot(jnp.concatenate([yt[u][hd:], vh[u].T], axis=1),
                  jnp.concatenate([bpb[u], kpb[u]], axis=0)) for u in idx]
    pt_h = pick("pt")
    s = [s_ref[u] for u in idx]
    for n in range(nchunk):
        rows = slice(n * RW_CHUNK, (n + 1) * RW_CHUNK)
        cols = slice(n * hd, (n + 1) * hd)
        for u in idx:
            b, sl = units[u]
            obuf_ref[b, rows, sl] = _dot_nt(qt[u][rows], s[u]) + o0[u][rows]
        last = (n + 1) * RW_CHUNK - 1
        s = [s[u] * pt_h[u][last:last + 1] + _dot(s[u], g_all[u][:, cols]) + h_all[u][:, cols]
             for u in idx]
    for u in idx:
        s_ref[u] = s[u]

    for b in range(nb):
        o = obuf_ref[b]
        mean = segsum(o) * (1.0 / hd)
        oc = o - mean
        var = segsum(oc * oc) * (1.0 / hd)
        on = oc * lax.rsqrt(var + GN_EPS) * gn_g + gn_b
        o_ref[b] = ((on + rows_in[b]["bonus"]) * rows_in[b]["g"]).astype(BF16)


def _rwkv(p_r, mu, w0, w2, a0, a2, g2, k_k, k_a, r_k, gn_g, gn_b):
    bsz, seq, _ = p_r.shape
    tt = RW_TILE
    nb = RW_ROWS if bsz % RW_ROWS == 0 else 1
    w = RWKV_WIDTH
    vec = jnp.stack([w0, a0, k_k, k_a, r_k.reshape(w), gn_g, gn_b, jnp.zeros_like(w0)])
    t_idx = np.arange(tt)
    same = (t_idx[:, None] // RW_CHUNK) == (t_idx[None, :] // RW_CHUNK)
    ltri = jnp.asarray(same & (t_idx[None, :] <= t_idx[:, None]), BF16)
    ustr = jnp.asarray(same & (t_idx[None, :] > t_idx[:, None]), BF16)
    c_idx = np.arange(w)
    seg = jnp.asarray((c_idx[:, None] // HEAD_DIM) == (c_idx[None, :] // HEAD_DIM), BF16)
    full = lambda shape: pl.BlockSpec(shape, lambda b, i: (0,) * len(shape))
    per_tile = tt // SUBLANES
    return pl.pallas_call(
        _rwkv_kernel,
        grid=(bsz // nb, seq // tt),
        in_specs=[pl.BlockSpec((nb, tt, RWKV_COLS), lambda b, i: (b, i, 0)),
                  pl.BlockSpec((nb, SUBLANES, RWKV_COLS),
                               lambda b, i: (b, jnp.maximum(i * per_tile - 1, 0), 0)),
                  full((1, RWKV_COLS)), full((SUBLANES, w)),
                  full(w2.shape), full(a2.shape), full(g2.shape),
                  full((tt, tt)), full((tt, tt)), full((w, w))],
        out_specs=pl.BlockSpec((nb, tt, w), lambda b, i: (b, i, 0)),
        out_shape=jax.ShapeDtypeStruct((bsz, seq, w), BF16),
        scratch_shapes=[pltpu.VMEM((nb * RWKV_HEADS, HEAD_DIM, HEAD_DIM), F32),
                        pltpu.VMEM((nb, tt, w), F32)],
        compiler_params=_params(("parallel", "arbitrary")),
        name="rwkv7_mix",
    )(p_r, p_r, mu.reshape(1, RWKV_COLS), vec, w2.astype(BF16), a2.astype(BF16),
      g2.astype(BF16), ltri, ustr, seg)


def _dsa_kernel(qt_ref, k_ref, vt_ref, qit_ref, ki_ref, wit_ref, low_ref, o_ref,
                key_ref, bias_ref, acc_ref, m_ref, l_ref, *, topk):
    qb = pl.program_id(1)
    nq = Q_BLOCK
    kt_sz = bias_ref.shape[0]
    hd = HEAD_DIM
    start = qb * nq
    nkt = (start + nq + kt_sz - 1) // kt_sz
    col = lax.broadcasted_iota(I32, (1, nq), 1)
    lim = start + (col // CHUNK + 1) * CHUNK
    wit = wit_ref[0]
    qit = qit_ref[0]
    qi_cat = jnp.concatenate([qit[h * IDX_DIM:(h + 1) * IDX_DIM] for h in range(IDX_HEADS)],
                             axis=1)

    def score_tile(j, carry):
        off = pl.multiple_of(j * kt_sz, kt_sz)
        d = jnp.dot(ki_ref[0, pl.ds(off, kt_sz), :], qi_cat, preferred_element_type=F32)
        s = jnp.zeros((kt_sz, nq), F32)
        for h in range(IDX_HEADS):
            s = s + wit[h:h + 1] * jnp.maximum(d[:, h * nq:(h + 1) * nq], 0.0)
        s = s + 0.0
        bits = pltpu.bitcast(s, I32)
        key = bits ^ ((bits >> 31) & 0x7FFFFFFF)
        sidx = off + lax.broadcasted_iota(I32, (kt_sz, nq), 0)
        key_ref[pl.ds(off, kt_sz), :] = jnp.where(sidx < lim, key, INT_MIN)
        return carry

    lax.fori_loop(0, nkt, score_tile, 0)

    def count(pred):
        def body(j, c):
            off = pl.multiple_of(j * kt_sz, kt_sz)
            hit = jnp.where(pred(key_ref[pl.ds(off, kt_sz), :]), 1, 0)
            return c + hit.reshape(kt_sz // RED_ROWS, RED_ROWS, nq).sum(axis=0)
        c = lax.fori_loop(0, nkt, body, jnp.zeros((RED_ROWS, nq), I32))
        return jnp.sum(c, axis=0, keepdims=True)

    zero = jnp.zeros((1, nq), I32)
    lo = jnp.where(count(lambda kv: kv >= zero) >= topk, zero, zero + INT_MIN)

    def bit_step(b, lo):
        cand = lo + lax.shift_left(jnp.int32(1), 30 - b)
        return jnp.where(count(lambda kv: kv >= cand) >= topk, cand, lo)

    th = lax.fori_loop(0, 31, bit_step, lo)
    need = jnp.where(th == INT_MIN, 0, topk - count(lambda kv: kv > th)).astype(F32)

    acc_ref[...] = jnp.zeros_like(acc_ref)
    m_ref[...] = jnp.full_like(m_ref, NEG_INF)
    l_ref[...] = jnp.zeros_like(l_ref)
    low = low_ref[...]
    npair = DSA_HEADS // 2
    top_rows = lax.broadcasted_iota(I32, (2 * hd, nq), 0) < hd
    qt = qt_ref[0].astype(F32)
    q_bd = []
    for p in range(npair):
        qp = qt[p * 2 * hd:(p + 1) * 2 * hd]
        q_bd.append(jnp.concatenate([jnp.where(top_rows, qp, 0.0), jnp.where(top_rows, 0.0, qp)],
                                    axis=1).astype(BF16))

    def attn_tile(j, eq_before):
        off = pl.multiple_of(j * kt_sz, kt_sz)
        eq_run = eq_before
        for t in range(kt_sz // TRI_TILE):
            kv = key_ref[pl.ds(off + t * TRI_TILE, TRI_TILE), :]
            eq = kv == th
            pre = eq_run + jnp.dot(low, jnp.where(eq, 1.0, 0.0).astype(BF16),
                                   preferred_element_type=F32)
            sel = (kv > th) | (eq & (pre <= need))
            bias_ref[t * TRI_TILE:(t + 1) * TRI_TILE, :] = jnp.where(sel, 0.0, NEG_INF)
            eq_run = pre[TRI_TILE - 1:TRI_TILE]
        bias = bias_ref[...]

        def logits(p):
            half = kt_sz // 2
            return jnp.concatenate(
                [jnp.dot(k_ref[0, pl.ds(off + r * half, half), p * 2 * hd:(p + 1) * 2 * hd],
                         q_bd[p], preferred_element_type=F32) for r in range(2)], axis=0)

        st_next = logits(0)
        for p in range(npair):
            cols = slice(p * 2 * hd, (p + 1) * 2 * hd)
            st = st_next
            if p + 1 < npair:
                st_next = logits(p + 1)
            prs, scales = [], []
            for i in range(2):
                h = 2 * p + i
                s = st[:, i * nq:(i + 1) * nq] + bias
                m_old = m_ref[h:h + 1]
                m_new = jnp.maximum(m_old, _fold_rows(s, jnp.max))
                pr = jnp.exp2(s - m_new)
                scale = jnp.exp2(m_old - m_new)
                l_ref[h:h + 1] = scale * l_ref[h:h + 1] + _fold_rows(pr, jnp.sum)
                m_ref[h:h + 1] = m_new
                prs.append(pr.astype(BF16))
                scales.append(scale)
            pv = jnp.dot(vt_ref[0, cols, pl.ds(off, kt_sz)], jnp.concatenate(prs, axis=1),
                         preferred_element_type=F32)
            new = jnp.where(top_rows, pv[:, :nq], pv[:, nq:])
            acc_ref[p] = jnp.where(top_rows, scales[0], scales[1]) * acc_ref[p] + new
        return eq_run

    lax.fori_loop(0, nkt, attn_tile, jnp.zeros((1, nq), F32))
    for p in range(npair):
        inv = jnp.where(top_rows, 1.0 / l_ref[2 * p:2 * p + 1], 1.0 / l_ref[2 * p + 1:2 * p + 2])
        o_ref[0, :, p * 2 * hd:(p + 1) * 2 * hd] = (acc_ref[p] * inv).T.astype(BF16)


def _dsa(q_t, k, v_t, qi_t, ki, wi_t):
    bsz, seq, w = k.shape
    nq = Q_BLOCK
    topk = min(INDEX_TOPK, seq // 4)
    kt_sz = min(KEY_TILE, seq)
    t_idx = np.arange(TRI_TILE)
    low = jnp.asarray(t_idx[None, :] <= t_idx[:, None], BF16)
    cols = lambda n: pl.BlockSpec((1, n, nq), lambda b, i: (b, 0, i))
    return pl.pallas_call(
        functools.partial(_dsa_kernel, topk=topk),
        grid=(bsz, seq // nq),
        in_specs=[cols(w),
                  pl.BlockSpec((1, seq, w), lambda b, i: (b, 0, 0)),
                  pl.BlockSpec((1, w, seq), lambda b, i: (b, 0, 0)),
                  cols(IDX_HEADS * IDX_DIM),
                  pl.BlockSpec((1, seq, IDX_DIM), lambda b, i: (b, 0, 0)),
                  cols(SUBLANES),
                  pl.BlockSpec((TRI_TILE, TRI_TILE), lambda b, i: (0, 0))],
        out_specs=pl.BlockSpec((1, nq, w), lambda b, i: (b, i, 0)),
        out_shape=jax.ShapeDtypeStruct((bsz, seq, w), BF16),
        scratch_shapes=[pltpu.VMEM((seq, nq), I32),
                        pltpu.VMEM((kt_sz, nq), F32),
                        pltpu.VMEM((DSA_HEADS // 2, 2 * HEAD_DIM, nq), F32),
                        pltpu.VMEM((DSA_HEADS, nq), F32),
                        pltpu.VMEM((DSA_HEADS, nq), F32)],
        compiler_params=_params(("parallel", "arbitrary")),
        name="dsa_mix",
    )(q_t, k, v_t, qi_t, ki, wi_t, low)


def _odd_in_kernel(x_ref, mod_ref, w_ref, o_ref):
    m = mod_ref[0]
    h = (x_ref[0] * (1.0 + m[1:2]) + m[0:1]).astype(BF16)
    o_ref[0] = jnp.dot(h, w_ref[...], preferred_element_type=F32)


def _odd_in_proj(x, mod, w_in):
    bsz, seq, d = x.shape
    tm = 512
    n = w_in.shape[1]
    return pl.pallas_call(
        _odd_in_kernel,
        grid=(bsz, seq // tm),
        in_specs=[pl.BlockSpec((1, tm, d), lambda b, i: (b, i, 0)),
                  pl.BlockSpec((1, 6, d), lambda b, i: (b, 0, 0)),
                  pl.BlockSpec((d, n), lambda b, i: (0, 0))],
        out_specs=pl.BlockSpec((1, tm, n), lambda b, i: (b, i, 0)),
        out_shape=jax.ShapeDtypeStruct((bsz, seq, n), F32),
        compiler_params=_params(("parallel", "parallel")),
        name="odd_in_proj",
    )(x, mod, w_in.astype(BF16))


POOL_HALO = 16


def _odd_mix_kernel(p_ref, prev_ref, pw_ref, ps_ref, lng_ref, lnb_ref, ws_ref, bs_ref, o_ref):
    i = pl.program_id(1)
    tm = p_ref.shape[1]
    gd = POOL_GROUP_DIM
    p = p_ref[0]
    prev = prev_ref[0] * (i > 0).astype(F32)
    t_glob = (i * tm + lax.broadcasted_iota(I32, (tm, 1), 0)).astype(F32)
    scale = ps_ref[...]
    for gi, win in enumerate(POOL_WINDOWS):
        cols = slice(gi * gd, (gi + 1) * gd)
        xg = p[:, cols]
        s = jnp.concatenate([prev[:, cols], xg], axis=0)
        span = 1
        while span < win:
            s = s[span:] + s[:-span]
            span *= 2
        first = POOL_HALO + 1 - win
        pooled = s[first:first + tm] / jnp.minimum(t_glob + 1.0, float(win)) - xg
        o_ref[0, :, cols] = (_dot(pooled, pw_ref[gi]) * scale[:, cols]).astype(BF16)

    u = _gelu(p[:, POOL_WIDTH:POOL_WIDTH + SG_WIDTH])
    v = _ln(_gelu(p[:, POOL_WIDTH + SG_WIDTH:]), lng_ref[...], lnb_ref[...], LN_EPS)
    ti = lax.broadcasted_iota(I32, (SG_CHUNK, SG_CHUNK), 0)
    si = lax.broadcasted_iota(I32, (SG_CHUNK, SG_CHUNK), 1)
    bs = bs_ref[...]
    for gi in range(SG_GROUPS):
        cols = slice(gi * SG_GROUP_DIM, (gi + 1) * SG_GROUP_DIM)
        ws = jnp.where(si <= ti, ws_ref[gi], 0.0)
        for n in range(tm // SG_CHUNK):
            rows = slice(n * SG_CHUNK, (n + 1) * SG_CHUNK)
            z = _dot(ws, v[rows, cols]) + bs[:, gi:gi + 1]
            o_ref[0, rows, POOL_WIDTH + gi * SG_GROUP_DIM:POOL_WIDTH + (gi + 1) * SG_GROUP_DIM] = (
                u[rows, cols] * z).astype(BF16)


def _odd_mix(p, pool_w, pool_scale, sg_ln_g, sg_ln_b, sg_w, sg_b):
    bsz, seq, n = p.shape
    tm = 256
    per_tile = tm // POOL_HALO
    full = lambda shape: pl.BlockSpec(shape, lambda b, i: (0,) * len(shape))
    return pl.pallas_call(
        _odd_mix_kernel,
        grid=(bsz, seq // tm),
        in_specs=[pl.BlockSpec((1, tm, n), lambda b, i: (b, i, 0)),
                  pl.BlockSpec((1, POOL_HALO, POOL_WIDTH),
                               lambda b, i: (b, jnp.maximum(i * per_tile - 1, 0), 0)),
                  full(pool_w.shape), full((1, POOL_WIDTH)), full((1, SG_WIDTH)),
                  full((1, SG_WIDTH)), full(sg_w.shape), full((SG_CHUNK, SG_GROUPS))],
        out_specs=pl.BlockSpec((1, tm, D_MODEL), lambda b, i: (b, i, 0)),
        out_shape=jax.ShapeDtypeStruct((bsz, seq, D_MODEL), BF16),
        compiler_params=_params(("parallel", "parallel")),
        name="pool_sgu_mix",
    )(p, p, pool_w.astype(BF16), pool_scale.reshape(1, -1), sg_ln_g.reshape(1, -1),
      sg_ln_b.reshape(1, -1), sg_w, sg_b.T)


def _proj_ln_kernel(*refs, n_in, gate_row):
    a_refs = refs[:n_in]
    w_refs = refs[n_in:2 * n_in]
    x_ref, mod_ref, g_ref, b_ref, o_ref = refs[2 * n_in:]
    y = None
    for a_ref, w_ref in zip(a_refs, w_refs):
        part = jnp.dot(a_ref[0], w_ref[...], preferred_element_type=F32)
        y = part if y is None else y + part
    gate = mod_ref[0][gate_row:gate_row + 1]
    o_ref[0] = _ln(ALPHA * x_ref[0] + gate * y, g_ref[...], b_ref[...], LN_EPS)


def _proj_ln(acts, weights, x, mod, gate_row, ln_g, ln_b, name):
    bsz, seq, d = x.shape
    tm = 512
    n_in = len(acts)
    in_specs = [pl.BlockSpec((1, tm, a.shape[2]), lambda b, i: (b, i, 0)) for a in acts]
    in_specs += [pl.BlockSpec(w.shape, lambda b, i: (0, 0)) for w in weights]
    in_specs += [pl.BlockSpec((1, tm, d), lambda b, i: (b, i, 0)),
                 pl.BlockSpec((1, 6, d), lambda b, i: (b, 0, 0)),
                 pl.BlockSpec((1, d), lambda b, i: (0, 0)),
                 pl.BlockSpec((1, d), lambda b, i: (0, 0))]
    return pl.pallas_call(
        functools.partial(_proj_ln_kernel, n_in=n_in, gate_row=gate_row),
        grid=(bsz, seq // tm),
        in_specs=in_specs,
        out_specs=pl.BlockSpec((1, tm, d), lambda b, i: (b, i, 0)),
        out_shape=jax.ShapeDtypeStruct((bsz, seq, d), F32),
        compiler_params=_params(("parallel", "parallel")),
        name=name,
    )(*acts, *[w.astype(BF16) for w in weights], x, mod, ln_g.reshape(1, d), ln_b.reshape(1, d))


def _ffn_up_kernel(x_ref, xp_ref, mod_ref, wg_ref, wv_ref, cwg_ref, cwv_ref, cbg_ref, cbv_ref,
                   o_ref):
    i = pl.program_id(2)
    m = mod_ref[0]
    sc = 1.0 + m[4:5]
    sh = m[3:4]
    tm = x_ref.shape[1]
    live = (i > 0).astype(F32)
    h = jnp.concatenate([(xp_ref[0] * sc + sh) * live, x_ref[0] * sc + sh], axis=0).astype(BF16)
    halo = SUBLANES

    def conv(w_ref, cw_ref, cb_ref):
        u = jnp.dot(h, w_ref[...], preferred_element_type=F32)
        cw = cw_ref[...]
        return (cb_ref[...] + u[halo - 2:halo - 2 + tm] * cw[0:1]
                + u[halo - 1:halo - 1 + tm] * cw[1:2] + u[halo:halo + tm] * cw[2:3])

    gate = conv(wg_ref, cwg_ref, cbg_ref)
    val = conv(wv_ref, cwv_ref, cbv_ref)
    o_ref[0] = (gate * jax.nn.sigmoid(gate) * val).astype(BF16)


def _ffn_up(x, mod, w_up, conv_w, conv_b):
    bsz, seq, d = x.shape
    tm = 512
    tn = D_FF // 2
    nt = D_FF // tn
    per_tile = tm // SUBLANES
    wb = w_up.astype(BF16)
    cb = conv_b.reshape(1, -1)
    return pl.pallas_call(
        _ffn_up_kernel,
        grid=(bsz, nt, seq // tm),
        in_specs=[pl.BlockSpec((1, tm, d), lambda b, n, i: (b, i, 0)),
                  pl.BlockSpec((1, SUBLANES, d),
                               lambda b, n, i: (b, jnp.maximum(i * per_tile - 1, 0), 0)),
                  pl.BlockSpec((1, 6, d), lambda b, n, i: (b, 0, 0)),
                  pl.BlockSpec((d, tn), lambda b, n, i: (0, n)),
                  pl.BlockSpec((d, tn), lambda b, n, i: (0, n + nt)),
                  pl.BlockSpec((3, tn), lambda b, n, i: (0, n)),
                  pl.BlockSpec((3, tn), lambda b, n, i: (0, n + nt)),
                  pl.BlockSpec((1, tn), lambda b, n, i: (0, n)),
                  pl.BlockSpec((1, tn), lambda b, n, i: (0, n + nt))],
        out_specs=pl.BlockSpec((1, tm, tn), lambda b, n, i: (b, i, n)),
        out_shape=jax.ShapeDtypeStruct((bsz, seq, D_FF), BF16),
        compiler_params=_params(("parallel", "parallel", "parallel")),
        name="ffn_up_conv_gate",
    )(x, x, mod, wb, wb, conv_w, conv_w, cb, cb)


def _rope_tables(seq):
    inv = ROPE_THETA ** (-jnp.arange(0, ROPE_DIM, 2, dtype=F32) / ROPE_DIM)
    ang = jnp.arange(seq, dtype=F32)[:, None] * inv[None, :]
    cos, sin = jnp.cos(ang), jnp.sin(ang)
    half = ROPE_DIM // 2
    rest = HEAD_DIM - ROPE_DIM
    one = jnp.ones((seq, rest), F32)
    zero = jnp.zeros((seq, rest), F32)
    zh = jnp.zeros((seq, half), F32)
    head = lambda *parts: jnp.concatenate(parts * (LANES // HEAD_DIM), axis=1)
    return head(cos, cos, one), head(-sin, zh, zero), head(zh, sin, zero)


def kernel(x, c, ada_w, ada_b, ln_g, ln_b, ffn_w_up, ffn_conv_w, ffn_conv_b, ffn_w_down, ev_w_in, ev_w_out, rw_mu, rw_w0, rw_w2, rw_a0, rw_a2, rw_g2, rw_k_k, rw_k_a, rw_r_k, rw_gn_g, rw_gn_b, ik_ln_g, ik_ln_b, od_w_in, od_w_out, pool_w, pool_scale, sg_ln_g, sg_ln_b, sg_w, sg_b):
    seq = x.shape[1]
    tables = _rope_tables(seq)
    mods = _modulation(c, ada_w, ada_b)
    ev_w = _split_even_weights(ev_w_in)
    for layer in range(DEPTH):
        mod = mods[layer]
        if layer % 2 == 0:
            e = layer // 2
            p_r, q_t, k, v_t, qi_t, ki, wi_t = _even_in_proj(x, mod, [w[e] for w in ev_w],
                                                             ik_ln_g[e], ik_ln_b[e], tables)
            ya = _rwkv(p_r, rw_mu[e], rw_w0[e], rw_w2[e], rw_a0[e], rw_a2[e], rw_g2[e],
                       rw_k_k[e], rw_k_a[e], rw_r_k[e], rw_gn_g[e], rw_gn_b[e])
            yb = _dsa(q_t, k, v_t, qi_t, ki, wi_t)
            w_out = ev_w_out[e]
            x = _proj_ln([ya, yb], [w_out[:RWKV_WIDTH], w_out[RWKV_WIDTH:]], x, mod, 2,
                         ln_g[layer, 0], ln_b[layer, 0], "even_out_proj_ln")
        else:
            o = layer // 2
            p = _odd_in_proj(x, mod, od_w_in[o])
            yc = _odd_mix(p, pool_w[o], pool_scale[o], sg_ln_g[o], sg_ln_b[o], sg_w[o], sg_b[o])
            x = _proj_ln([yc], [od_w_out[o]], x, mod, 2, ln_g[layer, 0], ln_b[layer, 0],
                         "odd_out_proj_ln")
        act = _ffn_up(x, mod, ffn_w_up[layer], ffn_conv_w[layer], ffn_conv_b[layer])
        x = _proj_ln([act], [ffn_w_down[layer]], x, mod, 5, ln_g[layer, 1], ln_b[layer, 1],
                     "ffn_down_ln")
    return x
```

```python
import functools

import numpy as np
import jax
import jax.numpy as jnp
from jax import lax
from jax.experimental import pallas as pl
from jax.experimental.pallas import tpu as pltpu

F32 = jnp.float32
BF16 = jnp.bfloat16
I32 = jnp.int32

D_MODEL = 1024
DEPTH = 4
CHUNK = 64
HEAD_DIM = 64
RWKV_WIDTH = D_MODEL // 2
RWKV_HEADS = RWKV_WIDTH // HEAD_DIM
RWKV_LORA_W = 64
RWKV_LORA_A = 64
RWKV_LORA_G = 128
RWKV_COLS = 3 * RWKV_WIDTH + RWKV_LORA_W + RWKV_LORA_A + RWKV_LORA_G
DSA_WIDTH = D_MODEL - RWKV_WIDTH
DSA_HEADS = DSA_WIDTH // HEAD_DIM
IDX_HEADS = 4
IDX_DIM = 64
INDEX_TOPK = 256
Q_BLOCK = 128
ROPE_THETA = 500000.0
ROPE_DIM = HEAD_DIM // 4
POOL_WINDOWS = (2, 4, 8, 16)
POOL_WIDTH = D_MODEL // 2
POOL_GROUP_DIM = POOL_WIDTH // len(POOL_WINDOWS)
SG_WIDTH = D_MODEL - POOL_WIDTH
SG_GROUPS = 4
SG_GROUP_DIM = SG_WIDTH // SG_GROUPS
SG_CHUNK = 128
D_FF = 2816
ALPHA = (2.0 * DEPTH) ** 0.25
LN_EPS = 1e-5
GN_EPS = 64e-5
NEG_INF = -1e30
INT_MIN = -(2 ** 31)
LOG2E = 1.4426950408889634

LANES = 128
SUBLANES = 8
VMEM_LIMIT = 56 * 1024 * 1024

RW_TILE = 128
RW_CHUNK = 16
RW_ROWS = 2
KEY_TILE = 1024
TRI_TILE = 256
PAD_IDX = 384

def _dot(a, b):
    return jnp.dot(a.astype(BF16), b.astype(BF16), preferred_element_type=F32)


def _dot_nt(a, b):
    return lax.dot_general(a.astype(BF16), b.astype(BF16), (((1,), (1,)), ((), ())),
                           preferred_element_type=F32)


def _dot_split(x, m01, terms):
    acc = None
    rem = x
    for _ in range(terms):
        piece = rem.astype(BF16)
        rem = rem - piece.astype(F32)
        part = jnp.dot(piece, m01, preferred_element_type=F32)
        acc = part if acc is None else acc + part
    return acc


def _dot_split_left(m01, x, terms):
    acc = None
    rem = x
    for _ in range(terms):
        piece = rem.astype(BF16)
        rem = rem - piece.astype(F32)
        part = jnp.dot(m01, piece, preferred_element_type=F32)
        acc = part if acc is None else acc + part
    return acc


def _ln(x, g, b, eps):
    mu = jnp.mean(x, axis=-1, keepdims=True)
    xc = x - mu
    var = jnp.mean(xc * xc, axis=-1, keepdims=True)
    return xc * lax.rsqrt(var + eps) * g + b


def _gelu(x):
    return 0.5 * x * (1.0 + lax.erf(x * 0.7071067811865476))


RED_ROWS = 64


def _fold_rows(x, op):
    rows, n = x.shape
    part = op(x.reshape(rows // RED_ROWS, RED_ROWS, n), axis=0)
    return op(part, axis=0, keepdims=True)


def _params(sem):
    return pltpu.CompilerParams(dimension_semantics=sem, vmem_limit_bytes=VMEM_LIMIT)


def _mod_kernel(c_ref, w_ref, b_ref, o_ref):
    c = c_ref[...]
    ca = c * jax.nn.sigmoid(c)
    o_ref[0] = jnp.dot(ca, w_ref[0], preferred_element_type=F32,
                       precision=lax.Precision.HIGHEST) + b_ref[0]


def _modulation(c, ada_w, ada_b):
    bsz, d = c.shape
    depth = ada_w.shape[0]
    n = ada_w.shape[2]
    tn = 1536
    rows = -(-bsz // SUBLANES) * SUBLANES
    c8 = jnp.pad(c, ((0, rows - bsz), (0, 0)))
    out = pl.pallas_call(
        _mod_kernel,
        grid=(depth, n // tn),
        in_specs=[pl.BlockSpec((rows, d), lambda l, j: (0, 0)),
                  pl.BlockSpec((1, d, tn), lambda l, j: (l, 0, j)),
                  pl.BlockSpec((1, 1, tn), lambda l, j: (l, 0, j))],
        out_specs=pl.BlockSpec((1, rows, tn), lambda l, j: (l, 0, j)),
        out_shape=jax.ShapeDtypeStruct((depth, rows, n), F32),
        compiler_params=_params(("arbitrary", "arbitrary")),
        name="adaln_mod",
    )(c8, ada_w, ada_b.reshape(depth, 1, n))
    return out[:, :bsz].reshape(depth, bsz, 6, d)


def _rope(x, cos_t, sin_a, sin_b):
    n = x.shape[1] // LANES
    rep = (lambda t: jnp.concatenate([t] * n, axis=1)) if n > 1 else (lambda t: t)
    width = x.shape[1]
    half = ROPE_DIM // 2
    return (x * rep(cos_t) + pltpu.roll(x, width - half, 1) * rep(sin_a)
            + pltpu.roll(x, half, 1) * rep(sin_b))


def _even_in_kernel(x_ref, mod_ref, wr_ref, wqkv_ref, widx_ref, cos_ref, sa_ref, sb_ref,
                    ikg_ref, ikb_ref,
                    pr_ref, qt_ref, k_ref, vt_ref, qit_ref, ki_ref, wit_ref):
    m = mod_ref[0]
    h = (x_ref[0] * (1.0 + m[1:2]) + m[0:1]).astype(BF16)
    pr_ref[0] = jnp.dot(h, wr_ref[...], preferred_element_type=F32)
    qkv = jnp.dot(h, wqkv_ref[...], preferred_element_type=F32)
    cos_t = cos_ref[...]
    sin_a = sa_ref[...]
    sin_b = sb_ref[...]
    w = DSA_WIDTH
    q = _rope(qkv[:, :w], cos_t, sin_a, sin_b) * (HEAD_DIM ** -0.5 * LOG2E)
    qt_ref[0] = q.T.astype(BF16)
    k_ref[0] = _rope(qkv[:, w:2 * w], cos_t, sin_a, sin_b).astype(BF16)
    vt_ref[0] = qkv[:, 2 * w:].T.astype(BF16)
    idx = jnp.dot(h, widx_ref[...], preferred_element_type=F32)
    nq = IDX_HEADS * IDX_DIM
    qit_ref[0] = _rope(idx[:, :nq], cos_t, sin_a, sin_b).T.astype(BF16)
    blk = idx[:, nq:nq + LANES]
    lane = lax.broadcasted_iota(I32, blk.shape, 1)
    is_k = lane < IDX_DIM
    mu = jnp.sum(jnp.where(is_k, blk, 0.0), axis=1, keepdims=True) * (1.0 / IDX_DIM)
    xc = jnp.where(is_k, blk - mu, 0.0)
    var = jnp.sum(xc * xc, axis=1, keepdims=True) * (1.0 / IDX_DIM)
    kin = xc * lax.rsqrt(var + LN_EPS) * ikg_ref[...] + ikb_ref[...]
    ki_ref[0] = _rope(kin, cos_t, sin_a, sin_b)[:, :IDX_DIM].astype(BF16)
    wit = (blk * (IDX_HEADS ** -0.5 * IDX_DIM ** -0.5)).T
    wit_ref[0] = wit[IDX_DIM:IDX_DIM + SUBLANES]


def _split_w_kernel(w_ref, wr_ref, wqkv_ref, widx_ref):
    w = w_ref[0]
    c1 = RWKV_COLS
    c2 = RWKV_COLS + 3 * DSA_WIDTH
    wr_ref[0] = w[:, :c1].astype(BF16)
    wqkv_ref[0] = w[:, c1:c2].astype(BF16)
    tail = w[:, c2:]
    zeros = jnp.zeros((w.shape[0], PAD_IDX - tail.shape[1]), F32)
    widx_ref[0] = jnp.concatenate([tail, zeros], axis=1).astype(BF16)


def _split_even_weights(ev_w_in):
    n_even, d, n = ev_w_in.shape
    tr = 256
    widths = (RWKV_COLS, 3 * DSA_WIDTH, PAD_IDX)
    return pl.pallas_call(
        _split_w_kernel,
        grid=(n_even, d // tr),
        in_specs=[pl.BlockSpec((1, tr, n), lambda e, i: (e, i, 0))],
        out_specs=[pl.BlockSpec((1, tr, wd), lambda e, i: (e, i, 0)) for wd in widths],
        out_shape=[jax.ShapeDtypeStruct((n_even, d, wd), BF16) for wd in widths],
        compiler_params=_params(("parallel", "parallel")),
        name="split_even_weights",
    )(ev_w_in)


def _even_in_proj(x, mod, weights, ik_g, ik_b, tables):
    bsz, seq, d = x.shape
    tm = 256
    w_r, w_qkv, w_idx = weights
    pad = LANES - IDX_DIM
    ikg = jnp.pad(ik_g, (0, pad)).reshape(1, LANES)
    ikb = jnp.pad(ik_b, (0, pad)).reshape(1, LANES)
    cos_t, sin_a, sin_b = tables
    full = lambda shape: pl.BlockSpec(shape, lambda b, i: (0,) * len(shape))
    tab = pl.BlockSpec((tm, LANES), lambda b, i: (i, 0))
    nq = IDX_HEADS * IDX_DIM
    rows = lambda n: pl.BlockSpec((1, tm, n), lambda b, i: (b, i, 0))
    cols = lambda n: pl.BlockSpec((1, n, tm), lambda b, i: (b, 0, i))
    return pl.pallas_call(
        _even_in_kernel,
        grid=(bsz, seq // tm),
        in_specs=[pl.BlockSpec((1, tm, d), lambda b, i: (b, i, 0)),
                  pl.BlockSpec((1, 6, d), lambda b, i: (b, 0, 0)),
                  full(w_r.shape), full(w_qkv.shape), full(w_idx.shape),
                  tab, tab, tab, full((1, LANES)), full((1, LANES))],
        out_specs=[rows(RWKV_COLS), cols(DSA_WIDTH), rows(DSA_WIDTH), cols(DSA_WIDTH),
                   cols(nq), rows(IDX_DIM), cols(SUBLANES)],
        out_shape=[jax.ShapeDtypeStruct((bsz, seq, RWKV_COLS), F32),
                   jax.ShapeDtypeStruct((bsz, DSA_WIDTH, seq), BF16),
                   jax.ShapeDtypeStruct((bsz, seq, DSA_WIDTH), BF16),
                   jax.ShapeDtypeStruct((bsz, DSA_WIDTH, seq), BF16),
                   jax.ShapeDtypeStruct((bsz, nq, seq), BF16),
                   jax.ShapeDtypeStruct((bsz, seq, IDX_DIM), BF16),
                   jax.ShapeDtypeStruct((bsz, SUBLANES, seq), F32)],
        compiler_params=_params(("parallel", "parallel")),
        name="even_in_proj",
    )(x, mod, w_r, w_qkv, w_idx, cos_t, sin_a, sin_b, ikg, ikb)


def _rwkv_kernel(p_ref, pp_ref, mu_ref, vec_ref, w2_ref, a2_ref, g2_ref, ltri_ref, ustr_ref,
                 seg_ref, o_ref, s_ref, obuf_ref):
    i = pl.program_id(1)

    @pl.when(i == 0)
    def _():
        s_ref[...] = jnp.zeros_like(s_ref)

    tt = RW_TILE
    w = RWKV_WIDTH
    hd = HEAD_DIM
    nb = p_ref.shape[0]
    nchunk = tt // RW_CHUNK
    seg = seg_ref[...]
    segsum = lambda t: _dot_split(t, seg, 2)

    vec = vec_ref[...]
    w0, a0, k_k, k_a, r_k, gn_g, gn_b = (vec[j:j + 1] for j in range(7))
    rowi = lax.broadcasted_iota(I32, (tt, 1), 0)
    live = (i > 0).astype(F32)

    def prepare(b):
        p = p_ref[b]
        prow = pp_ref[b][SUBLANES - 1:SUBLANES] * live
        xprev = jnp.where(rowi == 0, prow, pltpu.roll(p, 1, 0))
        ps = p + (xprev - p) * mu_ref[...]
        r = ps[:, :w]
        k = ps[:, w:2 * w]
        v = ps[:, 2 * w:3 * w]
        o1 = 3 * w
        wd = ps[:, o1:o1 + RWKV_LORA_W]
        ad = ps[:, o1 + RWKV_LORA_W:o1 + RWKV_LORA_W + RWKV_LORA_A]
        gd = ps[:, o1 + RWKV_LORA_W + RWKV_LORA_A:]
        y = -(w0 + _dot(jnp.tanh(wd), w2_ref[...]))
        softplus = jnp.maximum(y, 0.0) + jnp.log1p(jnp.exp(-jnp.abs(y)))
        logw = -jnp.exp(-softplus - 0.5)
        a = jax.nn.sigmoid(a0 + _dot(ad, a2_ref[...]))
        g = _dot(jax.nn.sigmoid(gd), g2_ref[...])
        kk = k * k_k
        kk = kk / jnp.maximum(jnp.sqrt(segsum(kk * kk)), 1e-12)
        k2 = k * (1.0 + (a - 1.0) * k_a)
        bonus = segsum(r * k2 * r_k) * v
        cum = _dot_split_left(ltri_ref[...], logw, 3)
        rem = _dot_split_left(ustr_ref[...], logw, 3)
        pt = jnp.exp(cum)
        ipt = jnp.exp(-cum)
        erem = jnp.exp(rem)
        kka = kk * a
        return dict(at=-kk * jnp.exp(cum - logw), rt=r * pt, bt=kka * ipt, kt=k2 * ipt,
                    bp=kka * erem, kp=k2 * erem, v=v, pt=pt, bonus=bonus, g=g)

    rows_in = [prepare(b) for b in range(nb)]

    ti = lax.broadcasted_iota(I32, (tt, tt), 0)
    si = lax.broadcasted_iota(I32, (tt, tt), 1)
    same = (ti // RW_CHUNK) == (si // RW_CHUNK)
    strict = same & (si < ti)
    incl = same & (si <= ti)
    tb = lax.broadcasted_iota(I32, (tt, nchunk * hd), 0)
    cb = lax.broadcasted_iota(I32, (tt, nchunk * hd), 1)
    blkmask = (tb // RW_CHUNK) == (cb // hd)
    tile_chunks = lambda t: jnp.where(blkmask, jnp.concatenate([t] * nchunk, axis=1), 0.0)

    units = [(b, slice(h * hd, (h + 1) * hd)) for b in range(nb) for h in range(RWKV_HEADS)]
    idx = range(len(units))
    pick = lambda name: [rows_in[b][name][:, sl] for b, sl in units]
    at_h, rt_h, vh, bt_h, kt_h = pick("at"), pick("rt"), pick("v"), pick("bt"), pick("kt")
    x = [_dot_nt(jnp.concatenate([at_h[u], rt_h[u]], axis=0),
                 jnp.concatenate([bt_h[u], kt_h[u]], axis=0)) for u in idx]
    a_ab = [jnp.where(strict, x[u][:tt, :tt], 0.0) for u in idx]
    a_ak = [jnp.where(strict, x[u][:tt, tt:], 0.0) for u in idx]
    a_rb = [jnp.where(incl, x[u][tt:, :tt], 0.0) for u in idx]
    a_rk = [jnp.where(incl, x[u][tt:, tt:], 0.0) for u in idx]
    yv = [jnp.concatenate([at_h[u], _dot(a_ak[u], vh[u])], axis=1) for u in idx]
    apow = a_ab
    span = 1
    while True:
        yv = [yv[u] + _dot(apow[u], yv[u]) for u in idx]
        span *= 2
        if span >= RW_CHUNK:
            break
        apow = [_dot(apow[u], apow[u]) for u in idx]
    arb_y = [_dot(a_rb[u], yv[u]) for u in idx]
    ark_v = [_dot(a_rk[u], vh[u]) for u in idx]
    qt = [rt_h[u] + arb_y[u][:, :hd] for u in idx]
    o0 = [arb_y[u][:, hd:] + ark_v[u] for u in idx]
    yt = [yv[u].T for u in idx]
    bpb = [tile_chunks(t) for t in pick("bp")]
    kpb = [tile_chunks(t) for t in pick("kp")]
    g_all = [_dot(yt[u][:hd], bpb[u]) for u in idx]
    h_all = [_dot(jnp.concatenate([yt[u][hd:], vh[u].T], axis=1),
                  jnp.concatenate([bpb[u], kpb[u]], axis=0)) for u in idx]
    pt_h = pick("pt")
    s = [s_ref[u] for u in idx]
    for n in range(nchunk):
        rows = slice(n * RW_CHUNK, (n + 1) * RW_CHUNK)
        cols = slice(n * hd, (n + 1) * hd)
        for u in idx:
            b, sl = units[u]
            obuf_ref[b, rows, sl] = _dot_nt(qt[u][rows], s[u]) + o0[u][rows]
        last = (n + 1) * RW_CHUNK - 1
        s = [s[u] * pt_h[u][last:last + 1] + _dot(s[u], g_all[u][:, cols]) + h_all[u][:, cols]
             for u in idx]
    for u in idx:
        s_ref[u] = s[u]

    for b in range(nb):
        o = obuf_ref[b]
        mean = segsum(o) * (1.0 / hd)
        oc = o - mean
        var = segsum(oc * oc) * (1.0 / hd)
        on = oc * lax.rsqrt(var + GN_EPS) * gn_g + gn_b
        o_ref[b] = ((on + rows_in[b]["bonus"]) * rows_in[b]["g"]).astype(BF16)


def _rwkv(p_r, mu, w0, w2, a0, a2, g2, k_k, k_a, r_k, gn_g, gn_b):
    bsz, seq, _ = p_r.shape
    tt = RW_TILE
    nb = RW_ROWS if bsz % RW_ROWS == 0 else 1
    w = RWKV_WIDTH
    vec = jnp.stack([w0, a0, k_k, k_a, r_k.reshape(w), gn_g, gn_b, jnp.zeros_like(w0)])
    t_idx = np.arange(tt)
    same = (t_idx[:, None] // RW_CHUNK) == (t_idx[None, :] // RW_CHUNK)
    ltri = jnp.asarray(same & (t_idx[None, :] <= t_idx[:, None]), BF16)
    ustr = jnp.asarray(same & (t_idx[None, :] > t_idx[:, None]), BF16)
    c_idx = np.arange(w)
    seg = jnp.asarray((c_idx[:, None] // HEAD_DIM) == (c_idx[None, :] // HEAD_DIM), BF16)
    full = lambda shape: pl.BlockSpec(shape, lambda b, i: (0,) * len(shape))
    per_tile = tt // SUBLANES
    return pl.pallas_call(
        _rwkv_kernel,
        grid=(bsz // nb, seq // tt),
        in_specs=[pl.BlockSpec((nb, tt, RWKV_COLS), lambda b, i: (b, i, 0)),
                  pl.BlockSpec((nb, SUBLANES, RWKV_COLS),
                               lambda b, i: (b, jnp.maximum(i * per_tile - 1, 0), 0)),
                  full((1, RWKV_COLS)), full((SUBLANES, w)),
                  full(w2.shape), full(a2.shape), full(g2.shape),
                  full((tt, tt)), full((tt, tt)), full((w, w))],
        out_specs=pl.BlockSpec((nb, tt, w), lambda b, i: (b, i, 0)),
        out_shape=jax.ShapeDtypeStruct((bsz, seq, w), BF16),
        scratch_shapes=[pltpu.VMEM((nb * RWKV_HEADS, HEAD_DIM, HEAD_DIM), F32),
                        pltpu.VMEM((nb, tt, w), F32)],
        compiler_params=_params(("parallel", "arbitrary")),
        name="rwkv7_mix",
    )(p_r, p_r, mu.reshape(1, RWKV_COLS), vec, w2.astype(BF16), a2.astype(BF16),
      g2.astype(BF16), ltri, ustr, seg)


def _dsa_kernel(qt_ref, k_ref, vt_ref, qit_ref, ki_ref, wit_ref, low_ref, o_ref,
                key_ref, bias_ref, acc_ref, m_ref, l_ref, *, topk):
    qb = pl.program_id(1)
    nq = Q_BLOCK
    kt_sz = bias_ref.shape[0]
    hd = HEAD_DIM
    start = qb * nq
    nkt = (start + nq + kt_sz - 1) // kt_sz
    col = lax.broadcasted_iota(I32, (1, nq), 1)
    lim = start + (col // CHUNK + 1) * CHUNK
    wit = wit_ref[0]
    qit = qit_ref[0]
    qi_cat = jnp.concatenate([qit[h * IDX_DIM:(h + 1) * IDX_DIM] for h in range(IDX_HEADS)],
                             axis=1)

    def score_tile(j, carry):
        off = pl.multiple_of(j * kt_sz, kt_sz)
        d = jnp.dot(ki_ref[0, pl.ds(off, kt_sz), :], qi_cat, preferred_element_type=F32)
        s = jnp.zeros((kt_sz, nq), F32)
        for h in range(IDX_HEADS):
            s = s + wit[h:h + 1] * jnp.maximum(d[:, h * nq:(h + 1) * nq], 0.0)
        s = s + 0.0
        bits = pltpu.bitcast(s, I32)
        key = bits ^ ((bits >> 31) & 0x7FFFFFFF)
        sidx = off + lax.broadcasted_iota(I32, (kt_sz, nq), 0)
        key_ref[pl.ds(off, kt_sz), :] = jnp.where(sidx < lim, key, INT_MIN)
        return carry

    lax.fori_loop(0, nkt, score_tile, 0)

    def count(pred):
        def body(j, c):
            off = pl.multiple_of(j * kt_sz, kt_sz)
            hit = jnp.where(pred(key_ref[pl.ds(off, kt_sz), :]), 1, 0)
            return c + hit.reshape(kt_sz // RED_ROWS, RED_ROWS, nq).sum(axis=0)
        c = lax.fori_loop(0, nkt, body, jnp.zeros((RED_ROWS, nq), I32))
        return jnp.sum(c, axis=0, keepdims=True)

    zero = jnp.zeros((1, nq), I32)
    lo = jnp.where(count(lambda kv: kv >= zero) >= topk, zero, zero + INT_MIN)

    def bit_step(b, lo):
        cand = lo + lax.shift_left(jnp.int32(1), 30 - b)
        return jnp.where(count(lambda kv: kv >= cand) >= topk, cand, lo)

    th = lax.fori_loop(0, 31, bit_step, lo)
    need = jnp.where(th == INT_MIN, 0, topk - count(lambda kv: kv > th)).astype(F32)

    acc_ref[...] = jnp.zeros_like(acc_ref)
    m_ref[...] = jnp.full_like(m_ref, NEG_INF)
    l_ref[...] = jnp.zeros_like(l_ref)
    low = low_ref[...]
    npair = DSA_HEADS // 2
    top_rows = lax.broadcasted_iota(I32, (2 * hd, nq), 0) < hd
    qt = qt_ref[0].astype(F32)
    q_bd = []
    for p in range(npair):
        qp = qt[p * 2 * hd:(p + 1) * 2 * hd]
        q_bd.append(jnp.concatenate([jnp.where(top_rows, qp, 0.0), jnp.where(top_rows, 0.0, qp)],
                                    axis=1).astype(BF16))

    def attn_tile(j, eq_before):
        off = pl.multiple_of(j * kt_sz, kt_sz)
        eq_run = eq_before
        for t in range(kt_sz // TRI_TILE):
            kv = key_ref[pl.ds(off + t * TRI_TILE, TRI_TILE), :]
            eq = kv == th
            pre = eq_run + jnp.dot(low, jnp.where(eq, 1.0, 0.0).astype(BF16),
                                   preferred_element_type=F32)
            sel = (kv > th) | (eq & (pre <= need))
            bias_ref[t * TRI_TILE:(t + 1) * TRI_TILE, :] = jnp.where(sel, 0.0, NEG_INF)
            eq_run = pre[TRI_TILE - 1:TRI_TILE]
        bias = bias_ref[...]

        def logits(p):
            half = kt_sz // 2
            return jnp.concatenate(
                [jnp.dot(k_ref[0, pl.ds(off + r * half, half), p * 2 * hd:(p + 1) * 2 * hd],
                         q_bd[p], preferred_element_type=F32) for r in range(2)], axis=0)

        st_next = logits(0)
        for p in range(npair):
            cols = slice(p * 2 * hd, (p + 1) * 2 * hd)
            st = st_next
            if p + 1 < npair:
                st_next = logits(p + 1)
            prs, scales = [], []
            for i in range(2):
                h = 2 * p + i
                s = st[:, i * nq:(i + 1) * nq] + bias
                m_old = m_ref[h:h + 1]
                m_new = jnp.maximum(m_old, _fold_rows(s, jnp.max))
                pr = jnp.exp2(s - m_new)
                scale = jnp.exp2(m_old - m_new)
                l_ref[h:h + 1] = scale * l_ref[h:h + 1] + _fold_rows(pr, jnp.sum)
                m_ref[h:h + 1] = m_new
                prs.append(pr.astype(BF16))
                scales.append(scale)
            pv = jnp.dot(vt_ref[0, cols, pl.ds(off, kt_sz)], jnp.concatenate(prs, axis=1),
                         preferred_element_type=F32)
            new = jnp.where(top_rows, pv[:, :nq], pv[:, nq:])
            acc_ref[p] = jnp.where(top_rows, scales[0], scales[1]) * acc_ref[p] + new
        return eq_run

    lax.fori_loop(0, nkt, attn_tile, jnp.zeros((1, nq), F32))
    for p in range(npair):
        inv = jnp.where(top_rows, 1.0 / l_ref[2 * p:2 * p + 1], 1.0 / l_ref[2 * p + 1:2 * p + 2])
        o_ref[0, :, p * 2 * hd:(p + 1) * 2 * hd] = (acc_ref[p] * inv).T.astype(BF16)


def _dsa(q_t, k, v_t, qi_t, ki, wi_t):
    bsz, seq, w = k.shape
    nq = Q_BLOCK
    topk = min(INDEX_TOPK, seq // 4)
    kt_sz = min(KEY_TILE, seq)
    t_idx = np.arange(TRI_TILE)
    low = jnp.asarray(t_idx[None, :] <= t_idx[:, None], BF16)
    cols = lambda n: pl.BlockSpec((1, n, nq), lambda b, i: (b, 0, i))
    return pl.pallas_call(
        functools.partial(_dsa_kernel, topk=topk),
        grid=(bsz, seq // nq),
        in_specs=[cols(w),
                  pl.BlockSpec((1, seq, w), lambda b, i: (b, 0, 0)),
                  pl.BlockSpec((1, w, seq), lambda b, i: (b, 0, 0)),
                  cols(IDX_HEADS * IDX_DIM),
                  pl.BlockSpec((1, seq, IDX_DIM), lambda b, i: (b, 0, 0)),
                  cols(SUBLANES),
                  pl.BlockSpec((TRI_TILE, TRI_TILE), lambda b, i: (0, 0))],
        out_specs=pl.BlockSpec((1, nq, w), lambda b, i: (b, i, 0)),
        out_shape=jax.ShapeDtypeStruct((bsz, seq, w), BF16),
        scratch_shapes=[pltpu.VMEM((seq, nq), I32),
                        pltpu.VMEM((kt_sz, nq), F32),
                        pltpu.VMEM((DSA_HEADS // 2, 2 * HEAD_DIM, nq), F32),
                        pltpu.VMEM((DSA_HEADS, nq), F32),
                        pltpu.VMEM((DSA_HEADS, nq), F32)],
        compiler_params=_params(("parallel", "arbitrary")),
        name="dsa_mix",
    )(q_t, k, v_t, qi_t, ki, wi_t, low)


def _odd_in_kernel(x_ref, mod_ref, w_ref, o_ref):
    m = mod_ref[0]
    h = (x_ref[0] * (1.0 + m[1:2]) + m[0:1]).astype(BF16)
    o_ref[0] = jnp.dot(h, w_ref[...], preferred_element_type=F32)


def _odd_in_proj(x, mod, w_in):
    bsz, seq, d = x.shape
    tm = 512
    n = w_in.shape[1]
    return pl.pallas_call(
        _odd_in_kernel,
        grid=(bsz, seq // tm),
        in_specs=[pl.BlockSpec((1, tm, d), lambda b, i: (b, i, 0)),
                  pl.BlockSpec((1, 6, d), lambda b, i: (b, 0, 0)),
                  pl.BlockSpec((d, n), lambda b, i: (0, 0))],
        out_specs=pl.BlockSpec((1, tm, n), lambda b, i: (b, i, 0)),
        out_shape=jax.ShapeDtypeStruct((bsz, seq, n), F32),
        compiler_params=_params(("parallel", "parallel")),
        name="odd_in_proj",
    )(x, mod, w_in.astype(BF16))


POOL_HALO = 16


def _odd_mix_kernel(p_ref, prev_ref, pw_ref, ps_ref, lng_ref, lnb_ref, ws_ref, bs_ref, o_ref):
    i = pl.program_id(1)
    tm = p_ref.shape[1]
    gd = POOL_GROUP_DIM
    p = p_ref[0]
    prev = prev_ref[0] * (i > 0).astype(F32)
    t_glob = (i * tm + lax.broadcasted_iota(I32, (tm, 1), 0)).astype(F32)
    scale = ps_ref[...]
    for gi, win in enumerate(POOL_WINDOWS):
        cols = slice(gi * gd, (gi + 1) * gd)
        xg = p[:, cols]
        s = jnp.concatenate([prev[:, cols], xg], axis=0)
        span = 1
        while span < win:
            s = s[span:] + s[:-span]
            span *= 2
        first = POOL_HALO + 1 - win
        pooled = s[first:first + tm] / jnp.minimum(t_glob + 1.0, float(win)) - xg
        o_ref[0, :, cols] = (_dot(pooled, pw_ref[gi]) * scale[:, cols]).astype(BF16)

    u = _gelu(p[:, POOL_WIDTH:POOL_WIDTH + SG_WIDTH])
    v = _ln(_gelu(p[:, POOL_WIDTH + SG_WIDTH:]), lng_ref[...], lnb_ref[...], LN_EPS)
    ti = lax.broadcasted_iota(I32, (SG_CHUNK, SG_CHUNK), 0)
    si = lax.broadcasted_iota(I32, (SG_CHUNK, SG_CHUNK), 1)
    bs = bs_ref[...]
    for gi in range(SG_GROUPS):
        cols = slice(gi * SG_GROUP_DIM, (gi + 1) * SG_GROUP_DIM)
        ws = jnp.where(si <= ti, ws_ref[gi], 0.0)
        for n in range(tm // SG_CHUNK):
            rows = slice(n * SG_CHUNK, (n + 1) * SG_CHUNK)
            z = _dot(ws, v[rows, cols]) + bs[:, gi:gi + 1]
            o_ref[0, rows, POOL_WIDTH + gi * SG_GROUP_DIM:POOL_WIDTH + (gi + 1) * SG_GROUP_DIM] = (
                u[rows, cols] * z).astype(BF16)


def _odd_mix(p, pool_w, pool_scale, sg_ln_g, sg_ln_b, sg_w, sg_b):
    bsz, seq, n = p.shape
    tm = 256
    per_tile = tm // POOL_HALO
    full = lambda shape: pl.BlockSpec(shape, lambda b, i: (0,) * len(shape))
    return pl.pallas_call(
        _odd_mix_kernel,
        grid=(bsz, seq // tm),
        in_specs=[pl.BlockSpec((1, tm, n), lambda b, i: (b, i, 0)),
                  pl.BlockSpec((1, POOL_HALO, POOL_WIDTH),
                               lambda b, i: (b, jnp.maximum(i * per_tile - 1, 0), 0)),
                  full(pool_w.shape), full((1, POOL_WIDTH)), full((1, SG_WIDTH)),
                  full((1, SG_WIDTH)), full(sg_w.shape), full((SG_CHUNK, SG_GROUPS))],
        out_specs=pl.BlockSpec((1, tm, D_MODEL), lambda b, i: (b, i, 0)),
        out_shape=jax.ShapeDtypeStruct((bsz, seq, D_MODEL), BF16),
        compiler_params=_params(("parallel", "parallel")),
        name="pool_sgu_mix",
    )(p, p, pool_w.astype(BF16), pool_scale.reshape(1, -1), sg_ln_g.reshape(1, -1),
      sg_ln_b.reshape(1, -1), sg_w, sg_b.T)


def _proj_ln_kernel(*refs, n_in, gate_row):
    a_refs = refs[:n_in]
    w_refs = refs[n_in:2 * n_in]
    x_ref, mod_ref, g_ref, b_ref, o_ref = refs[2 * n_in:]
    y = None
    for a_ref, w_ref in zip(a_refs, w_refs):
        part = jnp.dot(a_ref[0], w_ref[...], preferred_element_type=F32)
        y = part if y is None else y + part
    gate = mod_ref[0][gate_row:gate_row + 1]
    o_ref[0] = _ln(ALPHA * x_ref[0] + gate * y, g_ref[...], b_ref[...], LN_EPS)


def _proj_ln(acts, weights, x, mod, gate_row, ln_g, ln_b, name):
    bsz, seq, d = x.shape
    tm = 512
    n_in = len(acts)
    in_specs = [pl.BlockSpec((1, tm, a.shape[2]), lambda b, i: (b, i, 0)) for a in acts]
    in_specs += [pl.BlockSpec(w.shape, lambda b, i: (0, 0)) for w in weights]
    in_specs += [pl.BlockSpec((1, tm, d), lambda b, i: (b, i, 0)),
                 pl.BlockSpec((1, 6, d), lambda b, i: (b, 0, 0)),
                 pl.BlockSpec((1, d), lambda b, i: (0, 0)),
                 pl.BlockSpec((1, d), lambda b, i: (0, 0))]
    return pl.pallas_call(
        functools.partial(_proj_ln_kernel, n_in=n_in, gate_row=gate_row),
        grid=(bsz, seq // tm),
        in_specs=in_specs,
        out_specs=pl.BlockSpec((1, tm, d), lambda b, i: (b, i, 0)),
        out_shape=jax.ShapeDtypeStruct((bsz, seq, d), F32),
        compiler_params=_params(("parallel", "parallel")),
        name=name,
    )(*acts, *[w.astype(BF16) for w in weights], x, mod, ln_g.reshape(1, d), ln_b.reshape(1, d))


def _ffn_up_kernel(x_ref, xp_ref, mod_ref, wg_ref, wv_ref, cwg_ref, cwv_ref, cbg_ref, cbv_ref,
                   o_ref):
    i = pl.program_id(2)
    m = mod_ref[0]
    sc = 1.0 + m[4:5]
    sh = m[3:4]
    tm = x_ref.shape[1]
    live = (i > 0).astype(F32)
    h = jnp.concatenate([(xp_ref[0] * sc + sh) * live, x_ref[0] * sc + sh], axis=0).astype(BF16)
    halo = SUBLANES

    def conv(w_ref, cw_ref, cb_ref):
        u = jnp.dot(h, w_ref[...], preferred_element_type=F32)
        cw = cw_ref[...]
        u1 = pltpu.roll(u, 1, 0)
        u2 = pltpu.roll(u, 2, 0)
        return (cb_ref[...] + u2[halo:] * cw[0:1] + u1[halo:] * cw[1:2] + u[halo:] * cw[2:3])

    gate = conv(wg_ref, cwg_ref, cbg_ref)
    val = conv(wv_ref, cwv_ref, cbv_ref)
    o_ref[0] = (gate * jax.nn.sigmoid(gate) * val).astype(BF16)


def _ffn_up(x, mod, w_up, conv_w, conv_b):
    bsz, seq, d = x.shape
    tm = 512
    tn = D_FF
    nt = D_FF // tn
    per_tile = tm // SUBLANES
    wb = w_up.astype(BF16)
    cb = conv_b.reshape(1, -1)
    return pl.pallas_call(
        _ffn_up_kernel,
        grid=(bsz, nt, seq // tm),
        in_specs=[pl.BlockSpec((1, tm, d), lambda b, n, i: (b, i, 0)),
                  pl.BlockSpec((1, SUBLANES, d),
                               lambda b, n, i: (b, jnp.maximum(i * per_tile - 1, 0), 0)),
                  pl.BlockSpec((1, 6, d), lambda b, n, i: (b, 0, 0)),
                  pl.BlockSpec((d, tn), lambda b, n, i: (0, n)),
                  pl.BlockSpec((d, tn), lambda b, n, i: (0, n + nt)),
                  pl.BlockSpec((3, tn), lambda b, n, i: (0, n)),
                  pl.BlockSpec((3, tn), lambda b, n, i: (0, n + nt)),
                  pl.BlockSpec((1, tn), lambda b, n, i: (0, n)),
                  pl.BlockSpec((1, tn), lambda b, n, i: (0, n + nt))],
        out_specs=pl.BlockSpec((1, tm, tn), lambda b, n, i: (b, i, n)),
        out_shape=jax.ShapeDtypeStruct((bsz, seq, D_FF), BF16),
        compiler_params=_params(("parallel", "parallel", "parallel")),
        name="ffn_up_conv_gate",
    )(x, x, mod, wb, wb, conv_w, conv_w, cb, cb)


def _rope_tables(seq):
    inv = ROPE_THETA ** (-jnp.arange(0, ROPE_DIM, 2, dtype=F32) / ROPE_DIM)
    ang = jnp.arange(seq, dtype=F32)[:, None] * inv[None, :]
    cos, sin = jnp.cos(ang), jnp.sin(ang)
    half = ROPE_DIM // 2
    rest = HEAD_DIM - ROPE_DIM
    one = jnp.ones((seq, rest), F32)
    zero = jnp.zeros((seq, rest), F32)
    zh = jnp.zeros((seq, half), F32)
    head = lambda *parts: jnp.concatenate(parts * (LANES // HEAD_DIM), axis=1)
    return head(cos, cos, one), head(-sin, zh, zero), head(zh, sin, zero)


def kernel(x, c, ada_w, ada_b, ln_g, ln_b, ffn_w_up, ffn_conv_w, ffn_conv_b, ffn_w_down, ev_w_in, ev_w_out, rw_mu, rw_w0, rw_w2, rw_a0, rw_a2, rw_g2, rw_k_k, rw_k_a, rw_r_k, rw_gn_g, rw_gn_b, ik_ln_g, ik_ln_b, od_w_in, od_w_out, pool_w, pool_scale, sg_ln_g, sg_ln_b, sg_w, sg_b):
    seq = x.shape[1]
    tables = _rope_tables(seq)
    mods = _modulation(c, ada_w, ada_b)
    ev_w = _split_even_weights(ev_w_in)
    for layer in range(DEPTH):
        mod = mods[layer]
        if layer % 2 == 0:
            e = layer // 2
            p_r, q_t, k, v_t, qi_t, ki, wi_t = _even_in_proj(x, mod, [w[e] for w in ev_w],
                                                             ik_ln_g[e], ik_ln_b[e], tables)
            ya = _rwkv(p_r, rw_mu[e], rw_w0[e], rw_w2[e], rw_a0[e], rw_a2[e], rw_g2[e],
                       rw_k_k[e], rw_k_a[e], rw_r_k[e], rw_gn_g[e], rw_gn_b[e])
            yb = _dsa(q_t, k, v_t, qi_t, ki, wi_t)
            w_out = ev_w_out[e]
            x = _proj_ln([ya, yb], [w_out[:RWKV_WIDTH], w_out[RWKV_WIDTH:]], x, mod, 2,
                         ln_g[layer, 0], ln_b[layer, 0], "even_out_proj_ln")
        else:
            o = layer // 2
            p = _odd_in_proj(x, mod, od_w_in[o])
            yc = _odd_mix(p, pool_w[o], pool_scale[o], sg_ln_g[o], sg_ln_b[o], sg_w[o], sg_b[o])
            x = _proj_ln([yc], [od_w_out[o]], x, mod, 2, ln_g[layer, 0], ln_b[layer, 0],
                         "odd_out_proj_ln")
        act = _ffn_up(x, mod, ffn_w_up[layer], ffn_conv_w[layer], ffn_conv_b[layer])
        x = _proj_ln([act], [ffn_w_down[layer]], x, mod, 5, ln_g[layer, 1], ln_b[layer, 1],
                     "ffn_down_ln")
    return x
```

```python
import functools

import numpy as np
import jax
import jax.numpy as jnp
from jax import lax
from jax.experimental import pallas as pl
from jax.experimental.pallas import tpu as pltpu

F32 = jnp.float32
BF16 = jnp.bfloat16
I32 = jnp.int32

D_MODEL = 1024
DEPTH = 4
CHUNK = 64
HEAD_DIM = 64
RWKV_WIDTH = D_MODEL // 2
RWKV_HEADS = RWKV_WIDTH // HEAD_DIM
RWKV_LORA_W = 64
RWKV_LORA_A = 64
RWKV_LORA_G = 128
RWKV_COLS = 3 * RWKV_WIDTH + RWKV_LORA_W + RWKV_LORA_A + RWKV_LORA_G
DSA_WIDTH = D_MODEL - RWKV_WIDTH
DSA_HEADS = DSA_WIDTH // HEAD_DIM
IDX_HEADS = 4
IDX_DIM = 64
INDEX_TOPK = 256
Q_BLOCK = 128
ROPE_THETA = 500000.0
ROPE_DIM = HEAD_DIM // 4
POOL_WINDOWS = (2, 4, 8, 16)
POOL_WIDTH = D_MODEL // 2
POOL_GROUP_DIM = POOL_WIDTH // len(POOL_WINDOWS)
SG_WIDTH = D_MODEL - POOL_WIDTH
SG_GROUPS = 4
SG_GROUP_DIM = SG_WIDTH // SG_GROUPS
SG_CHUNK = 128
D_FF = 2816
ALPHA = (2.0 * DEPTH) ** 0.25
LN_EPS = 1e-5
GN_EPS = 64e-5
NEG_INF = -1e30
INT_MIN = -(2 ** 31)
LOG2E = 1.4426950408889634

LANES = 128
SUBLANES = 8
VMEM_LIMIT = 56 * 1024 * 1024

RW_TILE = 128
RW_CHUNK = 16
RW_ROWS = 4
KEY_TILE = 1024
TRI_TILE = 256
PAD_IDX = 384

def _dot(a, b):
    return jnp.dot(a.astype(BF16), b.astype(BF16), preferred_element_type=F32)


def _dot_nt(a, b):
    return lax.dot_general(a.astype(BF16), b.astype(BF16), (((1,), (1,)), ((), ())),
                           preferred_element_type=F32)


def _dot_split(x, m01, terms):
    acc = None
    rem = x
    for _ in range(terms):
        piece = rem.astype(BF16)
        rem = rem - piece.astype(F32)
        part = jnp.dot(piece, m01, preferred_element_type=F32)
        acc = part if acc is None else acc + part
    return acc


def _dot_split_left(m01, x, terms):
    acc = None
    rem = x
    for _ in range(terms):
        piece = rem.astype(BF16)
        rem = rem - piece.astype(F32)
        part = jnp.dot(m01, piece, preferred_element_type=F32)
        acc = part if acc is None else acc + part
    return acc


def _ln(x, g, b, eps):
    mu = jnp.mean(x, axis=-1, keepdims=True)
    xc = x - mu
    var = jnp.mean(xc * xc, axis=-1, keepdims=True)
    return xc * lax.rsqrt(var + eps) * g + b


def _gelu(x):
    return 0.5 * x * (1.0 + lax.erf(x * 0.7071067811865476))


RED_ROWS = 64


def _fold_rows(x, op):
    rows, n = x.shape
    part = op(x.reshape(rows // RED_ROWS, RED_ROWS, n), axis=0)
    return op(part, axis=0, keepdims=True)


def _params(sem):
    return pltpu.CompilerParams(dimension_semantics=sem, vmem_limit_bytes=VMEM_LIMIT)


def _mod_kernel(c_ref, w_ref, b_ref, o_ref):
    c = c_ref[...]
    ca = c * jax.nn.sigmoid(c)
    o_ref[0] = jnp.dot(ca, w_ref[0], preferred_element_type=F32,
                       precision=lax.Precision.HIGHEST) + b_ref[0]


def _modulation(c, ada_w, ada_b):
    bsz, d = c.shape
    depth = ada_w.shape[0]
    n = ada_w.shape[2]
    tn = 1536
    rows = -(-bsz // SUBLANES) * SUBLANES
    c8 = jnp.pad(c, ((0, rows - bsz), (0, 0)))
    out = pl.pallas_call(
        _mod_kernel,
        grid=(depth, n // tn),
        in_specs=[pl.BlockSpec((rows, d), lambda l, j: (0, 0)),
                  pl.BlockSpec((1, d, tn), lambda l, j: (l, 0, j)),
                  pl.BlockSpec((1, 1, tn), lambda l, j: (l, 0, j))],
        out_specs=pl.BlockSpec((1, rows, tn), lambda l, j: (l, 0, j)),
        out_shape=jax.ShapeDtypeStruct((depth, rows, n), F32),
        compiler_params=_params(("arbitrary", "arbitrary")),
        name="adaln_mod",
    )(c8, ada_w, ada_b.reshape(depth, 1, n))
    return out[:, :bsz].reshape(depth, bsz, 6, d)


def _rope(x, cos_t, sin_a, sin_b):
    n = x.shape[1] // LANES
    rep = (lambda t: jnp.concatenate([t] * n, axis=1)) if n > 1 else (lambda t: t)
    width = x.shape[1]
    half = ROPE_DIM // 2
    return (x * rep(cos_t) + pltpu.roll(x, width - half, 1) * rep(sin_a)
            + pltpu.roll(x, half, 1) * rep(sin_b))


def _even_in_kernel(x_ref, mod_ref, wr_ref, wqkv_ref, widx_ref, cos_ref, sa_ref, sb_ref,
                    ikg_ref, ikb_ref,
                    pr_ref, qt_ref, k_ref, vt_ref, qit_ref, ki_ref, wit_ref):
    m = mod_ref[0]
    h = (x_ref[0] * (1.0 + m[1:2]) + m[0:1]).astype(BF16)
    pr_ref[0] = jnp.dot(h, wr_ref[...], preferred_element_type=F32)
    qkv = jnp.dot(h, wqkv_ref[...], preferred_element_type=F32)
    cos_t = cos_ref[...]
    sin_a = sa_ref[...]
    sin_b = sb_ref[...]
    w = DSA_WIDTH
    q = _rope(qkv[:, :w], cos_t, sin_a, sin_b) * (HEAD_DIM ** -0.5 * LOG2E)
    qt_ref[0] = q.T.astype(BF16)
    k_ref[0] = _rope(qkv[:, w:2 * w], cos_t, sin_a, sin_b).astype(BF16)
    vt_ref[0] = qkv[:, 2 * w:].T.astype(BF16)
    idx = jnp.dot(h, widx_ref[...], preferred_element_type=F32)
    nq = IDX_HEADS * IDX_DIM
    qit_ref[0] = _rope(idx[:, :nq], cos_t, sin_a, sin_b).T.astype(BF16)
    blk = idx[:, nq:nq + LANES]
    lane = lax.broadcasted_iota(I32, blk.shape, 1)
    is_k = lane < IDX_DIM
    mu = jnp.sum(jnp.where(is_k, blk, 0.0), axis=1, keepdims=True) * (1.0 / IDX_DIM)
    xc = jnp.where(is_k, blk - mu, 0.0)
    var = jnp.sum(xc * xc, axis=1, keepdims=True) * (1.0 / IDX_DIM)
    kin = xc * lax.rsqrt(var + LN_EPS) * ikg_ref[...] + ikb_ref[...]
    ki_ref[0] = _rope(kin, cos_t, sin_a, sin_b)[:, :IDX_DIM].astype(BF16)
    wit = (blk * (IDX_HEADS ** -0.5 * IDX_DIM ** -0.5)).T
    wit_ref[0] = wit[IDX_DIM:IDX_DIM + SUBLANES]


def _split_w_kernel(w_ref, wr_ref, wqkv_ref, widx_ref):
    w = w_ref[0]
    c1 = RWKV_COLS
    c2 = RWKV_COLS + 3 * DSA_WIDTH
    wr_ref[0] = w[:, :c1].astype(BF16)
    wqkv_ref[0] = w[:, c1:c2].astype(BF16)
    tail = w[:, c2:]
    zeros = jnp.zeros((w.shape[0], PAD_IDX - tail.shape[1]), F32)
    widx_ref[0] = jnp.concatenate([tail, zeros], axis=1).astype(BF16)


def _split_even_weights(ev_w_in):
    n_even, d, n = ev_w_in.shape
    tr = 256
    widths = (RWKV_COLS, 3 * DSA_WIDTH, PAD_IDX)
    return pl.pallas_call(
        _split_w_kernel,
        grid=(n_even, d // tr),
        in_specs=[pl.BlockSpec((1, tr, n), lambda e, i: (e, i, 0))],
        out_specs=[pl.BlockSpec((1, tr, wd), lambda e, i: (e, i, 0)) for wd in widths],
        out_shape=[jax.ShapeDtypeStruct((n_even, d, wd), BF16) for wd in widths],
        compiler_params=_params(("parallel", "parallel")),
        name="split_even_weights",
    )(ev_w_in)


def _even_in_proj(x, mod, weights, ik_g, ik_b, tables):
    bsz, seq, d = x.shape
    tm = 256
    w_r, w_qkv, w_idx = weights
    pad = LANES - IDX_DIM
    ikg = jnp.pad(ik_g, (0, pad)).reshape(1, LANES)
    ikb = jnp.pad(ik_b, (0, pad)).reshape(1, LANES)
    cos_t, sin_a, sin_b = tables
    full = lambda shape: pl.BlockSpec(shape, lambda b, i: (0,) * len(shape))
    tab = pl.BlockSpec((tm, LANES), lambda b, i: (i, 0))
    nq = IDX_HEADS * IDX_DIM
    rows = lambda n: pl.BlockSpec((1, tm, n), lambda b, i: (b, i, 0))
    cols = lambda n: pl.BlockSpec((1, n, tm), lambda b, i: (b, 0, i))
    return pl.pallas_call(
        _even_in_kernel,
        grid=(bsz, seq // tm),
        in_specs=[pl.BlockSpec((1, tm, d), lambda b, i: (b, i, 0)),
                  pl.BlockSpec((1, 6, d), lambda b, i: (b, 0, 0)),
                  full(w_r.shape), full(w_qkv.shape), full(w_idx.shape),
                  tab, tab, tab, full((1, LANES)), full((1, LANES))],
        out_specs=[rows(RWKV_COLS), cols(DSA_WIDTH), rows(DSA_WIDTH), cols(DSA_WIDTH),
                   cols(nq), rows(IDX_DIM), cols(SUBLANES)],
        out_shape=[jax.ShapeDtypeStruct((bsz, seq, RWKV_COLS), F32),
                   jax.ShapeDtypeStruct((bsz, DSA_WIDTH, seq), BF16),
                   jax.ShapeDtypeStruct((bsz, seq, DSA_WIDTH), BF16),
                   jax.ShapeDtypeStruct((bsz, DSA_WIDTH, seq), BF16),
                   jax.ShapeDtypeStruct((bsz, nq, seq), BF16),
                   jax.ShapeDtypeStruct((bsz, seq, IDX_DIM), BF16),
                   jax.ShapeDtypeStruct((bsz, SUBLANES, seq), F32)],
        compiler_params=_params(("parallel", "parallel")),
        name="even_in_proj",
    )(x, mod, w_r, w_qkv, w_idx, cos_t, sin_a, sin_b, ikg, ikb)


def _rwkv_kernel(p_ref, pp_ref, mu_ref, vec_ref, w2_ref, a2_ref, g2_ref, ltri_ref, ustr_ref,
                 seg_ref, o_ref, s_ref, obuf_ref):
    i = pl.program_id(1)

    @pl.when(i == 0)
    def _():
        s_ref[...] = jnp.zeros_like(s_ref)

    tt = RW_TILE
    w = RWKV_WIDTH
    hd = HEAD_DIM
    nb = p_ref.shape[0]
    nchunk = tt // RW_CHUNK
    seg = seg_ref[...]
    segsum = lambda t: _dot_split(t, seg, 2)

    vec = vec_ref[...]
    w0, a0, k_k, k_a, r_k, gn_g, gn_b = (vec[j:j + 1] for j in range(7))
    rowi = lax.broadcasted_iota(I32, (tt, 1), 0)
    live = (i > 0).astype(F32)

    def prepare(b):
        p = p_ref[b]
        prow = pp_ref[b][SUBLANES - 1:SUBLANES] * live
        xprev = jnp.where(rowi == 0, prow, pltpu.roll(p, 1, 0))
        ps = p + (xprev - p) * mu_ref[...]
        r = ps[:, :w]
        k = ps[:, w:2 * w]
        v = ps[:, 2 * w:3 * w]
        o1 = 3 * w
        wd = ps[:, o1:o1 + RWKV_LORA_W]
        ad = ps[:, o1 + RWKV_LORA_W:o1 + RWKV_LORA_W + RWKV_LORA_A]
        gd = ps[:, o1 + RWKV_LORA_W + RWKV_LORA_A:]
        y = -(w0 + _dot(jnp.tanh(wd), w2_ref[...]))
        softplus = jnp.maximum(y, 0.0) + jnp.log1p(jnp.exp(-jnp.abs(y)))
        logw = -jnp.exp(-softplus - 0.5)
        a = jax.nn.sigmoid(a0 + _dot(ad, a2_ref[...]))
        g = _dot(jax.nn.sigmoid(gd), g2_ref[...])
        kk = k * k_k
        kk = kk / jnp.maximum(jnp.sqrt(segsum(kk * kk)), 1e-12)
        k2 = k * (1.0 + (a - 1.0) * k_a)
        bonus = segsum(r * k2 * r_k) * v
        cum = _dot_split_left(ltri_ref[...], logw, 3)
        rem = _dot_split_left(ustr_ref[...], logw, 3)
        pt = jnp.exp(cum)
        ipt = jnp.exp(-cum)
        erem = jnp.exp(rem)
        kka = kk * a
        return dict(at=-kk * jnp.exp(cum - logw), rt=r * pt, bt=kka * ipt, kt=k2 * ipt,
                    bp=kka * erem, kp=k2 * erem, v=v, pt=pt, bonus=bonus, g=g)

    rows_in = [prepare(b) for b in range(nb)]

    ti = lax.broadcasted_iota(I32, (tt, tt), 0)
    si = lax.broadcasted_iota(I32, (tt, tt), 1)
    same = (ti // RW_CHUNK) == (si // RW_CHUNK)
    strict = same & (si < ti)
    incl = same & (si <= ti)
    tb = lax.broadcasted_iota(I32, (tt, nchunk * hd), 0)
    cb = lax.broadcasted_iota(I32, (tt, nchunk * hd), 1)
    blkmask = (tb // RW_CHUNK) == (cb // hd)
    tile_chunks = lambda t: jnp.where(blkmask, jnp.concatenate([t] * nchunk, axis=1), 0.0)

    units = [(b, slice(h * hd, (h + 1) * hd)) for b in range(nb) for h in range(RWKV_HEADS)]
    idx = range(len(units))
    pick = lambda name: [rows_in[b][name][:, sl] for b, sl in units]
    at_h, rt_h, vh, bt_h, kt_h = pick("at"), pick("rt"), pick("v"), pick("bt"), pick("kt")
    x = [_dot_nt(jnp.concatenate([at_h[u], rt_h[u]], axis=0),
                 jnp.concatenate([bt_h[u], kt_h[u]], axis=0)) for u in idx]
    a_ab = [jnp.where(strict, x[u][:tt, :tt], 0.0) for u in idx]
    a_ak = [jnp.where(strict, x[u][:tt, tt:], 0.0) for u in idx]
    a_rb = [jnp.where(incl, x[u][tt:, :tt], 0.0) for u in idx]
    a_rk = [jnp.where(incl, x[u][tt:, tt:], 0.0) for u in idx]
    yv = [jnp.concatenate([at_h[u], _dot(a_ak[u], vh[u])], axis=1) for u in idx]
    apow = a_ab
    span = 1
    while True:
        yv = [yv[u] + _dot(apow[u], yv[u]) for u in idx]
        span *= 2
        if span >= RW_CHUNK:
            break
        apow = [_dot(apow[u], apow[u]) for u in idx]
    arb_y = [_dot(a_rb[u], yv[u]) for u in idx]
    ark_v = [_dot(a_rk[u], vh[u]) for u in idx]
    qt = [rt_h[u] + arb_y[u][:, :hd] for u in idx]
    o0 = [arb_y[u][:, hd:] + ark_v[u] for u in idx]
    yt = [yv[u].T for u in idx]
    bpb = [tile_chunks(t) for t in pick("bp")]
    kpb = [tile_chunks(t) for t in pick("kp")]
    g_all = [_dot(yt[u][:hd], bpb[u]) for u in idx]
    h_all = [_dot(jnp.concatenate([yt[u][hd:], vh[u].T], axis=1),
                  jnp.concatenate([bpb[u], kpb[u]], axis=0)) for u in idx]
    pt_h = pick("pt")
    s = [s_ref[u] for u in idx]
    for n in range(nchunk):
        rows = slice(n * RW_CHUNK, (n + 1) * RW_CHUNK)
        cols = slice(n * hd, (n + 1) * hd)
        for u in idx:
            b, sl = units[u]
            obuf_ref[b, rows, sl] = _dot_nt(qt[u][rows], s[u]) + o0[u][rows]
        last = (n + 1) * RW_CHUNK - 1
        s = [s[u] * pt_h[u][last:last + 1] + _dot(s[u], g_all[u][:, cols]) + h_all[u][:, cols]
             for u in idx]
    for u in idx:
        s_ref[u] = s[u]

    for b in range(nb):
        o = obuf_ref[b]
        mean = segsum(o) * (1.0 / hd)
        oc = o - mean
        var = segsum(oc * oc) * (1.0 / hd)
        on = oc * lax.rsqrt(var + GN_EPS) * gn_g + gn_b
        o_ref[b] = ((on + rows_in[b]["bonus"]) * rows_in[b]["g"]).astype(BF16)


def _rwkv(p_r, mu, w0, w2, a0, a2, g2, k_k, k_a, r_k, gn_g, gn_b):
    bsz, seq, _ = p_r.shape
    tt = RW_TILE
    nb = RW_ROWS if bsz % RW_ROWS == 0 else 1
    w = RWKV_WIDTH
    vec = jnp.stack([w0, a0, k_k, k_a, r_k.reshape(w), gn_g, gn_b, jnp.zeros_like(w0)])
    t_idx = np.arange(tt)
    same = (t_idx[:, None] // RW_CHUNK) == (t_idx[None, :] // RW_CHUNK)
    ltri = jnp.asarray(same & (t_idx[None, :] <= t_idx[:, None]), BF16)
    ustr = jnp.asarray(same & (t_idx[None, :] > t_idx[:, None]), BF16)
    c_idx = np.arange(w)
    seg = jnp.asarray((c_idx[:, None] // HEAD_DIM) == (c_idx[None, :] // HEAD_DIM), BF16)
    full = lambda shape: pl.BlockSpec(shape, lambda b, i: (0,) * len(shape))
    per_tile = tt // SUBLANES
    return pl.pallas_call(
        _rwkv_kernel,
        grid=(bsz // nb, seq // tt),
        in_specs=[pl.BlockSpec((nb, tt, RWKV_COLS), lambda b, i: (b, i, 0)),
                  pl.BlockSpec((nb, SUBLANES, RWKV_COLS),
                               lambda b, i: (b, jnp.maximum(i * per_tile - 1, 0), 0)),
                  full((1, RWKV_COLS)), full((SUBLANES, w)),
                  full(w2.shape), full(a2.shape), full(g2.shape),
                  full((tt, tt)), full((tt, tt)), full((w, w))],
        out_specs=pl.BlockSpec((nb, tt, w), lambda b, i: (b, i, 0)),
        out_shape=jax.ShapeDtypeStruct((bsz, seq, w), BF16),
        scratch_shapes=[pltpu.VMEM((nb * RWKV_HEADS, HEAD_DIM, HEAD_DIM), F32),
                        pltpu.VMEM((nb, tt, w), F32)],
        compiler_params=_params(("parallel", "arbitrary")),
        name="rwkv7_mix",
    )(p_r, p_r, mu.reshape(1, RWKV_COLS), vec, w2.astype(BF16), a2.astype(BF16),
      g2.astype(BF16), ltri, ustr, seg)


def _dsa_kernel(qt_ref, k_ref, vt_ref, qit_ref, ki_ref, wit_ref, low_ref, o_ref,
                key_ref, bias_ref, acc_ref, m_ref, l_ref, *, topk):
    qb = pl.program_id(1)
    nq = Q_BLOCK
    kt_sz = bias_ref.shape[0]
    hd = HEAD_DIM
    start = qb * nq
    nkt = (start + nq + kt_sz - 1) // kt_sz
    col = lax.broadcasted_iota(I32, (1, nq), 1)
    lim = start + (col // CHUNK + 1) * CHUNK
    wit = wit_ref[0]
    qit = qit_ref[0]
    qi_cat = jnp.concatenate([qit[h * IDX_DIM:(h + 1) * IDX_DIM] for h in range(IDX_HEADS)],
                             axis=1)

    def score_tile(j, carry):
        off = pl.multiple_of(j * kt_sz, kt_sz)
        d = jnp.dot(ki_ref[0, pl.ds(off, kt_sz), :], qi_cat, preferred_element_type=F32)
        s = jnp.zeros((kt_sz, nq), F32)
        for h in range(IDX_HEADS):
            s = s + wit[h:h + 1] * jnp.maximum(d[:, h * nq:(h + 1) * nq], 0.0)
        s = s + 0.0
        bits = pltpu.bitcast(s, I32)
        key = bits ^ ((bits >> 31) & 0x7FFFFFFF)
        sidx = off + lax.broadcasted_iota(I32, (kt_sz, nq), 0)
        key_ref[pl.ds(off, kt_sz), :] = jnp.where(sidx < lim, key, INT_MIN)
        return carry

    lax.fori_loop(0, nkt, score_tile, 0)

    def count(pred):
        def body(j, c):
            off = pl.multiple_of(j * kt_sz, kt_sz)
            hit = jnp.where(pred(key_ref[pl.ds(off, kt_sz), :]), 1, 0)
            return c + hit.reshape(kt_sz // RED_ROWS, RED_ROWS, nq).sum(axis=0)
        c = lax.fori_loop(0, nkt, body, jnp.zeros((RED_ROWS, nq), I32))
        return jnp.sum(c, axis=0, keepdims=True)

    zero = jnp.zeros((1, nq), I32)
    lo = jnp.where(count(lambda kv: kv >= zero) >= topk, zero, zero + INT_MIN)

    def bit_step(b, lo):
        cand = lo + lax.shift_left(jnp.int32(1), 30 - b)
        return jnp.where(count(lambda kv: kv >= cand) >= topk, cand, lo)

    th = lax.fori_loop(0, 31, bit_step, lo)
    need = jnp.where(th == INT_MIN, 0, topk - count(lambda kv: kv > th)).astype(F32)

    acc_ref[...] = jnp.zeros_like(acc_ref)
    m_ref[...] = jnp.full_like(m_ref, NEG_INF)
    l_ref[...] = jnp.zeros_like(l_ref)
    low = low_ref[...]
    npair = DSA_HEADS // 2
    top_rows = lax.broadcasted_iota(I32, (2 * hd, nq), 0) < hd
    qt = qt_ref[0].astype(F32)
    q_bd = []
    for p in range(npair):
        qp = qt[p * 2 * hd:(p + 1) * 2 * hd]
        q_bd.append(jnp.concatenate([jnp.where(top_rows, qp, 0.0), jnp.where(top_rows, 0.0, qp)],
                                    axis=1).astype(BF16))

    def attn_tile(j, eq_before):
        off = pl.multiple_of(j * kt_sz, kt_sz)
        eq_run = eq_before
        for t in range(kt_sz // TRI_TILE):
            kv = key_ref[pl.ds(off + t * TRI_TILE, TRI_TILE), :]
            eq = kv == th
            pre = eq_run + jnp.dot(low, jnp.where(eq, 1.0, 0.0).astype(BF16),
                                   preferred_element_type=F32)
            sel = (kv > th) | (eq & (pre <= need))
            bias_ref[t * TRI_TILE:(t + 1) * TRI_TILE, :] = jnp.where(sel, 0.0, NEG_INF)
            eq_run = pre[TRI_TILE - 1:TRI_TILE]
        bias = bias_ref[...]

        def logits(p):
            half = kt_sz // 2
            return jnp.concatenate(
                [jnp.dot(k_ref[0, pl.ds(off + r * half, half), p * 2 * hd:(p + 1) * 2 * hd],
                         q_bd[p], preferred_element_type=F32) for r in range(2)], axis=0)

        st_next = logits(0)
        for p in range(npair):
            cols = slice(p * 2 * hd, (p + 1) * 2 * hd)
            st = st_next
            if p + 1 < npair:
                st_next = logits(p + 1)
            prs, scales = [], []
            for i in range(2):
                h = 2 * p + i
                s = st[:, i * nq:(i + 1) * nq] + bias
                m_old = m_ref[h:h + 1]
                m_new = jnp.maximum(m_old, _fold_rows(s, jnp.max))
                pr = jnp.exp2(s - m_new)
                scale = jnp.exp2(m_old - m_new)
                l_ref[h:h + 1] = scale * l_ref[h:h + 1] + _fold_rows(pr, jnp.sum)
                m_ref[h:h + 1] = m_new
                prs.append(pr.astype(BF16))
                scales.append(scale)
            pv = jnp.dot(vt_ref[0, cols, pl.ds(off, kt_sz)], jnp.concatenate(prs, axis=1),
                         preferred_element_type=F32)
            new = jnp.where(top_rows, pv[:, :nq], pv[:, nq:])
            acc_ref[p] = jnp.where(top_rows, scales[0], scales[1]) * acc_ref[p] + new
        return eq_run

    lax.fori_loop(0, nkt, attn_tile, jnp.zeros((1, nq), F32))
    for p in range(npair):
        inv = jnp.where(top_rows, 1.0 / l_ref[2 * p:2 * p + 1], 1.0 / l_ref[2 * p + 1:2 * p + 2])
        o_ref[0, :, p * 2 * hd:(p + 1) * 2 * hd] = (acc_ref[p] * inv).T.astype(BF16)


def _dsa(q_t, k, v_t, qi_t, ki, wi_t):
    bsz, seq, w = k.shape
    nq = Q_BLOCK
    topk = min(INDEX_TOPK, seq // 4)
    kt_sz = min(KEY_TILE, seq)
    t_idx = np.arange(TRI_TILE)
    low = jnp.asarray(t_idx[None, :] <= t_idx[:, None], BF16)
    cols = lambda n: pl.BlockSpec((1, n, nq), lambda b, i: (b, 0, i))
    return pl.pallas_call(
        functools.partial(_dsa_kernel, topk=topk),
        grid=(bsz, seq // nq),
        in_specs=[cols(w),
                  pl.BlockSpec((1, seq, w), lambda b, i: (b, 0, 0)),
                  pl.BlockSpec((1, w, seq), lambda b, i: (b, 0, 0)),
                  cols(IDX_HEADS * IDX_DIM),
                  pl.BlockSpec((1, seq, IDX_DIM), lambda b, i: (b, 0, 0)),
                  cols(SUBLANES),
                  pl.BlockSpec((TRI_TILE, TRI_TILE), lambda b, i: (0, 0))],
        out_specs=pl.BlockSpec((1, nq, w), lambda b, i: (b, i, 0)),
        out_shape=jax.ShapeDtypeStruct((bsz, seq, w), BF16),
        scratch_shapes=[pltpu.VMEM((seq, nq), I32),
                        pltpu.VMEM((kt_sz, nq), F32),
                        pltpu.VMEM((DSA_HEADS // 2, 2 * HEAD_DIM, nq), F32),
                        pltpu.VMEM((DSA_HEADS, nq), F32),
                        pltpu.VMEM((DSA_HEADS, nq), F32)],
        compiler_params=_params(("parallel", "arbitrary")),
        name="dsa_mix",
    )(q_t, k, v_t, qi_t, ki, wi_t, low)


POOL_HALO = 16


def _odd_layer_kernel(x_ref, xp_ref, mod_ref, win_ref, pw_ref, ps_ref, lng_ref, lnb_ref, ws_ref,
                      bs_ref, wout_ref, g_ref, b_ref, o_ref, y_ref):
    i = pl.program_id(1)
    m = mod_ref[0]
    sc = 1.0 + m[1:2]
    sh = m[0:1]
    x = x_ref[0]
    tm = x.shape[0]
    gd = POOL_GROUP_DIM
    p = jnp.dot((x * sc + sh).astype(BF16), win_ref[...], preferred_element_type=F32)
    prev = jnp.dot((xp_ref[0] * sc + sh).astype(BF16), win_ref[:, :POOL_WIDTH],
                   preferred_element_type=F32) * (i > 0).astype(F32)
    t_glob = (i * tm + lax.broadcasted_iota(I32, (tm, 1), 0)).astype(F32)
    scale = ps_ref[...]
    for gi, win in enumerate(POOL_WINDOWS):
        cols = slice(gi * gd, (gi + 1) * gd)
        xg = p[:, cols]
        s = jnp.concatenate([prev[:, cols], xg], axis=0)
        span = 1
        while span < win:
            s = s[span:] + s[:-span]
            span *= 2
        first = POOL_HALO + 1 - win
        pooled = s[first:first + tm] / jnp.minimum(t_glob + 1.0, float(win)) - xg
        y_ref[:, cols] = (_dot(pooled, pw_ref[gi]) * scale[:, cols]).astype(BF16)

    u = _gelu(p[:, POOL_WIDTH:POOL_WIDTH + SG_WIDTH])
    v = _ln(_gelu(p[:, POOL_WIDTH + SG_WIDTH:]), lng_ref[...], lnb_ref[...], LN_EPS)
    ti = lax.broadcasted_iota(I32, (SG_CHUNK, SG_CHUNK), 0)
    si = lax.broadcasted_iota(I32, (SG_CHUNK, SG_CHUNK), 1)
    bs = bs_ref[...]
    for gi in range(SG_GROUPS):
        cols = slice(gi * SG_GROUP_DIM, (gi + 1) * SG_GROUP_DIM)
        ws = jnp.where(si <= ti, ws_ref[gi], 0.0)
        for n in range(tm // SG_CHUNK):
            rows = slice(n * SG_CHUNK, (n + 1) * SG_CHUNK)
            z = _dot(ws, v[rows, cols]) + bs[:, gi:gi + 1]
            y_ref[rows, POOL_WIDTH + gi * SG_GROUP_DIM:POOL_WIDTH + (gi + 1) * SG_GROUP_DIM] = (
                u[rows, cols] * z).astype(BF16)

    y = jnp.dot(y_ref[...], wout_ref[...], preferred_element_type=F32)
    o_ref[0] = _ln(ALPHA * x + m[2:3] * y, g_ref[...], b_ref[...], LN_EPS)


def _odd_layer(x, mod, w_in, pool_w, pool_scale, sg_ln_g, sg_ln_b, sg_w, sg_b, w_out, ln_g, ln_b):
    bsz, seq, d = x.shape
    tm = 256
    per_tile = tm // POOL_HALO
    full = lambda shape: pl.BlockSpec(shape, lambda b, i: (0,) * len(shape))
    once = lambda shape: pl.BlockSpec(shape, lambda b, i: (0,) * len(shape),
                                      pipeline_mode=pl.Buffered(1))
    return pl.pallas_call(
        _odd_layer_kernel,
        grid=(bsz, seq // tm),
        in_specs=[pl.BlockSpec((1, tm, d), lambda b, i: (b, i, 0)),
                  pl.BlockSpec((1, POOL_HALO, d),
                               lambda b, i: (b, jnp.maximum(i * per_tile - 1, 0), 0)),
                  pl.BlockSpec((1, 6, d), lambda b, i: (b, 0, 0)),
                  once(w_in.shape), full(pool_w.shape), full((1, POOL_WIDTH)),
                  full((1, SG_WIDTH)), full((1, SG_WIDTH)), full(sg_w.shape),
                  full((SG_CHUNK, SG_GROUPS)), once(w_out.shape), full((1, d)), full((1, d))],
        out_specs=pl.BlockSpec((1, tm, d), lambda b, i: (b, i, 0)),
        out_shape=jax.ShapeDtypeStruct((bsz, seq, d), F32),
        scratch_shapes=[pltpu.VMEM((tm, d), BF16)],
        compiler_params=_params(("parallel", "parallel")),
        name="odd_layer_mix",
    )(x, x, mod, w_in.astype(BF16), pool_w.astype(BF16), pool_scale.reshape(1, -1),
      sg_ln_g.reshape(1, -1), sg_ln_b.reshape(1, -1), sg_w, sg_b.T, w_out.astype(BF16),
      ln_g.reshape(1, d), ln_b.reshape(1, d))


def _proj_ln_kernel(*refs, n_in, gate_row):
    a_refs = refs[:n_in]
    w_refs = refs[n_in:2 * n_in]
    x_ref, mod_ref, g_ref, b_ref, o_ref = refs[2 * n_in:]
    y = None
    for a_ref, w_ref in zip(a_refs, w_refs):
        part = jnp.dot(a_ref[0], w_ref[...], preferred_element_type=F32)
        y = part if y is None else y + part
    gate = mod_ref[0][gate_row:gate_row + 1]
    o_ref[0] = _ln(ALPHA * x_ref[0] + gate * y, g_ref[...], b_ref[...], LN_EPS)


def _proj_ln(acts, weights, x, mod, gate_row, ln_g, ln_b, name):
    bsz, seq, d = x.shape
    tm = 512
    n_in = len(acts)
    in_specs = [pl.BlockSpec((1, tm, a.shape[2]), lambda b, i: (b, i, 0)) for a in acts]
    in_specs += [pl.BlockSpec(w.shape, lambda b, i: (0, 0)) for w in weights]
    in_specs += [pl.BlockSpec((1, tm, d), lambda b, i: (b, i, 0)),
                 pl.BlockSpec((1, 6, d), lambda b, i: (b, 0, 0)),
                 pl.BlockSpec((1, d), lambda b, i: (0, 0)),
                 pl.BlockSpec((1, d), lambda b, i: (0, 0))]
    return pl.pallas_call(
        functools.partial(_proj_ln_kernel, n_in=n_in, gate_row=gate_row),
        grid=(bsz, seq // tm),
        in_specs=in_specs,
        out_specs=pl.BlockSpec((1, tm, d), lambda b, i: (b, i, 0)),
        out_shape=jax.ShapeDtypeStruct((bsz, seq, d), F32),
        compiler_params=_params(("parallel", "parallel")),
        name=name,
    )(*acts, *[w.astype(BF16) for w in weights], x, mod, ln_g.reshape(1, d), ln_b.reshape(1, d))


def _ffn_kernel(x_ref, xp_ref, mod_ref, wg_ref, wv_ref, cwg_ref, cwv_ref, cbg_ref, cbv_ref,
                wd_ref, g_ref, b_ref, o_ref):
    i = pl.program_id(1)
    m = mod_ref[0]
    sc = 1.0 + m[4:5]
    sh = m[3:4]
    x = x_ref[0]
    live = (i > 0).astype(F32)
    h = jnp.concatenate([(xp_ref[0] * sc + sh) * live, x * sc + sh], axis=0).astype(BF16)
    halo = SUBLANES

    def conv(w_ref, cw_ref, cb_ref):
        u = jnp.dot(h, w_ref[...], preferred_element_type=F32)
        cw = cw_ref[...]
        u1 = pltpu.roll(u, 1, 0)
        u2 = pltpu.roll(u, 2, 0)
        return (cb_ref[...] + u2[halo:] * cw[0:1] + u1[halo:] * cw[1:2] + u[halo:] * cw[2:3])

    gate = conv(wg_ref, cwg_ref, cbg_ref)
    val = conv(wv_ref, cwv_ref, cbv_ref)
    act = (gate * jax.nn.sigmoid(gate) * val).astype(BF16)
    y = jnp.dot(act, wd_ref[...], preferred_element_type=F32)
    o_ref[0] = _ln(ALPHA * x + m[5:6] * y, g_ref[...], b_ref[...], LN_EPS)


def _conv_ffn(x, mod, w_up, conv_w, conv_b, w_down, ln_g, ln_b):
    bsz, seq, d = x.shape
    tm = 512
    per_tile = tm // SUBLANES
    wb = w_up.astype(BF16)
    cb = conv_b.reshape(1, -1)
    once = dict(pipeline_mode=pl.Buffered(1))
    return pl.pallas_call(
        _ffn_kernel,
        grid=(bsz, seq // tm),
        in_specs=[pl.BlockSpec((1, tm, d), lambda b, i: (b, i, 0)),
                  pl.BlockSpec((1, SUBLANES, d),
                               lambda b, i: (b, jnp.maximum(i * per_tile - 1, 0), 0)),
                  pl.BlockSpec((1, 6, d), lambda b, i: (b, 0, 0)),
                  pl.BlockSpec((d, D_FF), lambda b, i: (0, 0), **once),
                  pl.BlockSpec((d, D_FF), lambda b, i: (0, 1), **once),
                  pl.BlockSpec((3, D_FF), lambda b, i: (0, 0), **once),
                  pl.BlockSpec((3, D_FF), lambda b, i: (0, 1), **once),
                  pl.BlockSpec((1, D_FF), lambda b, i: (0, 0), **once),
                  pl.BlockSpec((1, D_FF), lambda b, i: (0, 1), **once),
                  pl.BlockSpec((D_FF, d), lambda b, i: (0, 0), **once),
                  pl.BlockSpec((1, d), lambda b, i: (0, 0)),
                  pl.BlockSpec((1, d), lambda b, i: (0, 0))],
        out_specs=pl.BlockSpec((1, tm, d), lambda b, i: (b, i, 0)),
        out_shape=jax.ShapeDtypeStruct((bsz, seq, d), F32),
        compiler_params=_params(("parallel", "parallel")),
        name="conv_ffn_ln",
    )(x, x, mod, wb, wb, conv_w, conv_w, cb, cb, w_down.astype(BF16),
      ln_g.reshape(1, d), ln_b.reshape(1, d))


def _rope_tables(seq):
    inv = ROPE_THETA ** (-jnp.arange(0, ROPE_DIM, 2, dtype=F32) / ROPE_DIM)
    ang = jnp.arange(seq, dtype=F32)[:, None] * inv[None, :]
    cos, sin = jnp.cos(ang), jnp.sin(ang)
    half = ROPE_DIM // 2
    rest = HEAD_DIM - ROPE_DIM
    one = jnp.ones((seq, rest), F32)
    zero = jnp.zeros((seq, rest), F32)
    zh = jnp.zeros((seq, half), F32)
    head = lambda *parts: jnp.concatenate(parts * (LANES // HEAD_DIM), axis=1)
    return head(cos, cos, one), head(-sin, zh, zero), head(zh, sin, zero)


def kernel(x, c, ada_w, ada_b, ln_g, ln_b, ffn_w_up, ffn_conv_w, ffn_conv_b, ffn_w_down, ev_w_in, ev_w_out, rw_mu, rw_w0, rw_w2, rw_a0, rw_a2, rw_g2, rw_k_k, rw_k_a, rw_r_k, rw_gn_g, rw_gn_b, ik_ln_g, ik_ln_b, od_w_in, od_w_out, pool_w, pool_scale, sg_ln_g, sg_ln_b, sg_w, sg_b):
    seq = x.shape[1]
    tables = _rope_tables(seq)
    mods = _modulation(c, ada_w, ada_b)
    ev_w = _split_even_weights(ev_w_in)
    for layer in range(DEPTH):
        mod = mods[layer]
        if layer % 2 == 0:
            e = layer // 2
            p_r, q_t, k, v_t, qi_t, ki, wi_t = _even_in_proj(x, mod, [w[e] for w in ev_w],
                                                             ik_ln_g[e], ik_ln_b[e], tables)
            ya = _rwkv(p_r, rw_mu[e], rw_w0[e], rw_w2[e], rw_a0[e], rw_a2[e], rw_g2[e],
                       rw_k_k[e], rw_k_a[e], rw_r_k[e], rw_gn_g[e], rw_gn_b[e])
            yb = _dsa(q_t, k, v_t, qi_t, ki, wi_t)
            w_out = ev_w_out[e]
            x = _proj_ln([ya, yb], [w_out[:RWKV_WIDTH], w_out[RWKV_WIDTH:]], x, mod, 2,
                         ln_g[layer, 0], ln_b[layer, 0], "even_out_proj_ln")
        else:
            o = layer // 2
            x = _odd_layer(x, mod, od_w_in[o], pool_w[o], pool_scale[o], sg_ln_g[o], sg_ln_b[o],
                           sg_w[o], sg_b[o], od_w_out[o], ln_g[layer, 0], ln_b[layer, 0])
        x = _conv_ffn(x, mod, ffn_w_up[layer], ffn_conv_w[layer], ffn_conv_b[layer],
                      ffn_w_down[layer], ln_g[layer, 1], ln_b[layer, 1])
    return x
```

```python
import functools

import numpy as np
import jax
import jax.numpy as jnp
from jax import lax
from jax.experimental import pallas as pl
from jax.experimental.pallas import tpu as pltpu

F32 = jnp.float32
BF16 = jnp.bfloat16
I32 = jnp.int32

D_MODEL = 1024
DEPTH = 4
CHUNK = 64
HEAD_DIM = 64
RWKV_WIDTH = D_MODEL // 2
RWKV_HEADS = RWKV_WIDTH // HEAD_DIM
RWKV_LORA_W = 64
RWKV_LORA_A = 64
RWKV_LORA_G = 128
RWKV_COLS = 3 * RWKV_WIDTH + RWKV_LORA_W + RWKV_LORA_A + RWKV_LORA_G
DSA_WIDTH = D_MODEL - RWKV_WIDTH
DSA_HEADS = DSA_WIDTH // HEAD_DIM
IDX_HEADS = 4
IDX_DIM = 64
INDEX_TOPK = 256
Q_BLOCK = 128
ROPE_THETA = 500000.0
ROPE_DIM = HEAD_DIM // 4
POOL_WINDOWS = (2, 4, 8, 16)
POOL_WIDTH = D_MODEL // 2
POOL_GROUP_DIM = POOL_WIDTH // len(POOL_WINDOWS)
SG_WIDTH = D_MODEL - POOL_WIDTH
SG_GROUPS = 4
SG_GROUP_DIM = SG_WIDTH // SG_GROUPS
SG_CHUNK = 128
D_FF = 2816
ALPHA = (2.0 * DEPTH) ** 0.25
LN_EPS = 1e-5
GN_EPS = 64e-5
NEG_INF = -1e30
INT_MIN = -(2 ** 31)
LOG2E = 1.4426950408889634

LANES = 128
SUBLANES = 8
MXU_TILE = 256
VMEM_LIMIT = 56 * 1024 * 1024

RW_TILE = 128
RW_CHUNK = 16
RW_ROWS = 4
KEY_TILE = 1024
TRI_TILE = 256
PAD_IDX = 384

def _dot(a, b):
    return jnp.dot(a.astype(BF16), b.astype(BF16), preferred_element_type=F32)


def _dot_nt(a, b):
    return lax.dot_general(a.astype(BF16), b.astype(BF16), (((1,), (1,)), ((), ())),
                           preferred_element_type=F32)


def _dot_split(x, m01, terms):
    acc = None
    rem = x
    for _ in range(terms):
        piece = rem.astype(BF16)
        rem = rem - piece.astype(F32)
        part = jnp.dot(piece, m01, preferred_element_type=F32)
        acc = part if acc is None else acc + part
    return acc


def _dot_split_left(m01, x, terms):
    acc = None
    rem = x
    for _ in range(terms):
        piece = rem.astype(BF16)
        rem = rem - piece.astype(F32)
        part = jnp.dot(m01, piece, preferred_element_type=F32)
        acc = part if acc is None else acc + part
    return acc


def _ln(x, g, b, eps):
    mu = jnp.mean(x, axis=-1, keepdims=True)
    xc = x - mu
    var = jnp.mean(xc * xc, axis=-1, keepdims=True)
    return xc * lax.rsqrt(var + eps) * g + b


def _gelu(x):
    return 0.5 * x * (1.0 + lax.erf(x * 0.7071067811865476))


RED_ROWS = 64


def _fold_rows(x, op):
    rows, n = x.shape
    part = op(x.reshape(rows // RED_ROWS, RED_ROWS, n), axis=0)
    return op(part, axis=0, keepdims=True)


def _params(sem):
    return pltpu.CompilerParams(dimension_semantics=sem, vmem_limit_bytes=VMEM_LIMIT)


def _mod_kernel(c_ref, w_ref, b_ref, o_ref):
    c = c_ref[...]
    ca = c * jax.nn.sigmoid(c)
    o_ref[0] = jnp.dot(ca, w_ref[0], preferred_element_type=F32,
                       precision=lax.Precision.HIGHEST) + b_ref[0]


def _modulation(c, ada_w, ada_b):
    bsz, d = c.shape
    depth = ada_w.shape[0]
    n = ada_w.shape[2]
    tn = 1536
    rows = -(-bsz // SUBLANES) * SUBLANES
    c8 = jnp.pad(c, ((0, rows - bsz), (0, 0)))
    out = pl.pallas_call(
        _mod_kernel,
        grid=(depth, n // tn),
        in_specs=[pl.BlockSpec((rows, d), lambda l, j: (0, 0)),
                  pl.BlockSpec((1, d, tn), lambda l, j: (l, 0, j)),
                  pl.BlockSpec((1, 1, tn), lambda l, j: (l, 0, j))],
        out_specs=pl.BlockSpec((1, rows, tn), lambda l, j: (l, 0, j)),
        out_shape=jax.ShapeDtypeStruct((depth, rows, n), F32),
        compiler_params=_params(("arbitrary", "arbitrary")),
        name="adaln_mod",
    )(c8, ada_w, ada_b.reshape(depth, 1, n))
    return out[:, :bsz].reshape(depth, bsz, 6, d)


def _rope(x, cos_t, sin_a, sin_b):
    n = x.shape[1] // LANES
    rep = (lambda t: jnp.concatenate([t] * n, axis=1)) if n > 1 else (lambda t: t)
    width = x.shape[1]
    half = ROPE_DIM // 2
    return (x * rep(cos_t) + pltpu.roll(x, width - half, 1) * rep(sin_a)
            + pltpu.roll(x, half, 1) * rep(sin_b))


def _even_in_kernel(x_ref, mod_ref, wr_ref, wqkv_ref, widx_ref, cos_ref, sa_ref, sb_ref,
                    ikg_ref, ikb_ref,
                    pr_ref, qt_ref, k_ref, vt_ref, qit_ref, ki_ref, wit_ref):
    m = mod_ref[0]
    h = (x_ref[0] * (1.0 + m[1:2]) + m[0:1]).astype(BF16)
    pr_ref[0] = jnp.dot(h, wr_ref[...], preferred_element_type=F32)
    qkv = jnp.dot(h, wqkv_ref[...], preferred_element_type=F32)
    cos_t = cos_ref[...]
    sin_a = sa_ref[...]
    sin_b = sb_ref[...]
    w = DSA_WIDTH
    q = _rope(qkv[:, :w], cos_t, sin_a, sin_b) * (HEAD_DIM ** -0.5 * LOG2E)
    qt_ref[0] = q.T.astype(BF16)
    k_ref[0] = _rope(qkv[:, w:2 * w], cos_t, sin_a, sin_b).astype(BF16)
    vt_ref[0] = qkv[:, 2 * w:].T.astype(BF16)
    idx = jnp.dot(h, widx_ref[...], preferred_element_type=F32)
    nq = IDX_HEADS * IDX_DIM
    qit_ref[0] = _rope(idx[:, :nq], cos_t, sin_a, sin_b).T.astype(BF16)
    blk = idx[:, nq:nq + LANES]
    lane = lax.broadcasted_iota(I32, blk.shape, 1)
    is_k = lane < IDX_DIM
    mu = jnp.sum(jnp.where(is_k, blk, 0.0), axis=1, keepdims=True) * (1.0 / IDX_DIM)
    xc = jnp.where(is_k, blk - mu, 0.0)
    var = jnp.sum(xc * xc, axis=1, keepdims=True) * (1.0 / IDX_DIM)
    kin = xc * lax.rsqrt(var + LN_EPS) * ikg_ref[...] + ikb_ref[...]
    ki_ref[0] = _rope(kin, cos_t, sin_a, sin_b)[:, :IDX_DIM].astype(BF16)
    wit = (blk * (IDX_HEADS ** -0.5 * IDX_DIM ** -0.5)).T
    wit_ref[0] = wit[IDX_DIM:IDX_DIM + SUBLANES]


def _split_w_kernel(w_ref, wr_ref, wqkv_ref, widx_ref):
    w = w_ref[0]
    c1 = RWKV_COLS
    c2 = RWKV_COLS + 3 * DSA_WIDTH
    wr_ref[0] = w[:, :c1].astype(BF16)
    wqkv_ref[0] = w[:, c1:c2].astype(BF16)
    tail = w[:, c2:]
    zeros = jnp.zeros((w.shape[0], PAD_IDX - tail.shape[1]), F32)
    widx_ref[0] = jnp.concatenate([tail, zeros], axis=1).astype(BF16)


def _split_even_weights(ev_w_in):
    n_even, d, n = ev_w_in.shape
    tr = 256
    widths = (RWKV_COLS, 3 * DSA_WIDTH, PAD_IDX)
    return pl.pallas_call(
        _split_w_kernel,
        grid=(n_even, d // tr),
        in_specs=[pl.BlockSpec((1, tr, n), lambda e, i: (e, i, 0))],
        out_specs=[pl.BlockSpec((1, tr, wd), lambda e, i: (e, i, 0)) for wd in widths],
        out_shape=[jax.ShapeDtypeStruct((n_even, d, wd), BF16) for wd in widths],
        compiler_params=_params(("parallel", "parallel")),
        name="split_even_weights",
    )(ev_w_in)


def _even_in_proj(x, mod, weights, ik_g, ik_b, tables):
    bsz, seq, d = x.shape
    tm = 256
    w_r, w_qkv, w_idx = weights
    pad = LANES - IDX_DIM
    ikg = jnp.pad(ik_g, (0, pad)).reshape(1, LANES)
    ikb = jnp.pad(ik_b, (0, pad)).reshape(1, LANES)
    cos_t, sin_a, sin_b = tables
    full = lambda shape: pl.BlockSpec(shape, lambda b, i: (0,) * len(shape))
    tab = pl.BlockSpec((tm, LANES), lambda b, i: (i, 0))
    nq = IDX_HEADS * IDX_DIM
    rows = lambda n: pl.BlockSpec((1, tm, n), lambda b, i: (b, i, 0))
    cols = lambda n: pl.BlockSpec((1, n, tm), lambda b, i: (b, 0, i))
    return pl.pallas_call(
        _even_in_kernel,
        grid=(bsz, seq // tm),
        in_specs=[pl.BlockSpec((1, tm, d), lambda b, i: (b, i, 0)),
                  pl.BlockSpec((1, 6, d), lambda b, i: (b, 0, 0)),
                  full(w_r.shape), full(w_qkv.shape), full(w_idx.shape),
                  tab, tab, tab, full((1, LANES)), full((1, LANES))],
        out_specs=[rows(RWKV_COLS), cols(DSA_WIDTH), rows(DSA_WIDTH), cols(DSA_WIDTH),
                   cols(nq), rows(IDX_DIM), cols(SUBLANES)],
        out_shape=[jax.ShapeDtypeStruct((bsz, seq, RWKV_COLS), F32),
                   jax.ShapeDtypeStruct((bsz, DSA_WIDTH, seq), BF16),
                   jax.ShapeDtypeStruct((bsz, seq, DSA_WIDTH), BF16),
                   jax.ShapeDtypeStruct((bsz, DSA_WIDTH, seq), BF16),
                   jax.ShapeDtypeStruct((bsz, nq, seq), BF16),
                   jax.ShapeDtypeStruct((bsz, seq, IDX_DIM), BF16),
                   jax.ShapeDtypeStruct((bsz, SUBLANES, seq), F32)],
        compiler_params=_params(("parallel", "parallel")),
        name="even_in_proj",
    )(x, mod, w_r, w_qkv, w_idx, cos_t, sin_a, sin_b, ikg, ikb)


def _rwkv_kernel(p_ref, pp_ref, mu_ref, vec_ref, w2_ref, a2_ref, g2_ref, ltri_ref, ustr_ref,
                 seg_ref, o_ref, s_ref, obuf_ref):
    i = pl.program_id(1)

    @pl.when(i == 0)
    def _():
        s_ref[...] = jnp.zeros_like(s_ref)

    tt = RW_TILE
    w = RWKV_WIDTH
    hd = HEAD_DIM
    nb = p_ref.shape[0]
    nchunk = tt // RW_CHUNK
    seg = seg_ref[...]
    half_w = seg.shape[0]

    def segsum(t):
        return jnp.concatenate([_dot_split(t[:, c:c + half_w], seg, 2)
                                for c in range(0, w, half_w)], axis=1)

    vec = vec_ref[...]
    w0, a0, k_k, k_a, r_k, gn_g, gn_b = (vec[j:j + 1] for j in range(7))
    rowi = lax.broadcasted_iota(I32, (tt, 1), 0)
    live = (i > 0).astype(F32)

    def prepare(b):
        p = p_ref[b]
        prow = pp_ref[b][SUBLANES - 1:SUBLANES] * live
        xprev = jnp.where(rowi == 0, prow, pltpu.roll(p, 1, 0))
        ps = p + (xprev - p) * mu_ref[...]
        r = ps[:, :w]
        k = ps[:, w:2 * w]
        v = ps[:, 2 * w:3 * w]
        o1 = 3 * w
        wd = ps[:, o1:o1 + RWKV_LORA_W]
        ad = ps[:, o1 + RWKV_LORA_W:o1 + RWKV_LORA_W + RWKV_LORA_A]
        gd = ps[:, o1 + RWKV_LORA_W + RWKV_LORA_A:]
        y = -(w0 + _dot(jnp.tanh(wd), w2_ref[...]))
        softplus = jnp.maximum(y, 0.0) + jnp.log1p(jnp.exp(-jnp.abs(y)))
        logw = -jnp.exp(-softplus - 0.5)
        a = jax.nn.sigmoid(a0 + _dot(ad, a2_ref[...]))
        g = _dot(jax.nn.sigmoid(gd), g2_ref[...])
        kk = k * k_k
        kk = kk / jnp.maximum(jnp.sqrt(segsum(kk * kk)), 1e-12)
        k2 = k * (1.0 + (a - 1.0) * k_a)
        bonus = segsum(r * k2 * r_k) * v
        cum = _dot_split_left(ltri_ref[...], logw, 3)
        rem = _dot_split_left(ustr_ref[...], logw, 3)
        pt = jnp.exp(cum)
        ipt = jnp.exp(-cum)
        erem = jnp.exp(rem)
        kka = kk * a
        return dict(at=-kk * jnp.exp(cum - logw), rt=r * pt, bt=kka * ipt, kt=k2 * ipt,
                    bp=kka * erem, kp=k2 * erem, v=v, pt=pt, bonus=bonus, g=g)

    rows_in = [prepare(b) for b in range(nb)]

    ti = lax.broadcasted_iota(I32, (tt, tt), 0)
    si = lax.broadcasted_iota(I32, (tt, tt), 1)
    same = (ti // RW_CHUNK) == (si // RW_CHUNK)
    strict = same & (si < ti)
    incl = same & (si <= ti)
    tb = lax.broadcasted_iota(I32, (tt, nchunk * hd), 0)
    cb = lax.broadcasted_iota(I32, (tt, nchunk * hd), 1)
    blkmask = (tb // RW_CHUNK) == (cb // hd)
    tile_chunks = lambda t: jnp.where(blkmask, jnp.concatenate([t] * nchunk, axis=1), 0.0)

    units = [(b, slice(h * hd, (h + 1) * hd)) for b in range(nb) for h in range(RWKV_HEADS)]
    idx = range(len(units))
    pick = lambda name: [rows_in[b][name][:, sl] for b, sl in units]
    at_h, rt_h, vh, bt_h, kt_h = pick("at"), pick("rt"), pick("v"), pick("bt"), pick("kt")
    x = [_dot_nt(jnp.concatenate([at_h[u], rt_h[u]], axis=0),
                 jnp.concatenate([bt_h[u], kt_h[u]], axis=0)) for u in idx]
    a_ab = [jnp.where(strict, x[u][:tt, :tt], 0.0) for u in idx]
    a_ak = [jnp.where(strict, x[u][:tt, tt:], 0.0) for u in idx]
    a_rb = [jnp.where(incl, x[u][tt:, :tt], 0.0) for u in idx]
    a_rk = [jnp.where(incl, x[u][tt:, tt:], 0.0) for u in idx]
    def bd(m0, m1):
        z0 = jnp.zeros((m0.shape[0], m1.shape[1]), m0.dtype)
        z1 = jnp.zeros((m1.shape[0], m0.shape[1]), m1.dtype)
        return jnp.concatenate([jnp.concatenate([m0, z0], axis=1),
                                jnp.concatenate([z1, m1], axis=1)], axis=0)

    def pair_dot(lhs, rhs):
        out = []
        for u in range(0, len(lhs), 2):
            res = _dot(jnp.concatenate([lhs[u], lhs[u + 1]], axis=1), bd(rhs[u], rhs[u + 1]))
            cut = rhs[u].shape[1]
            out += [res[:, :cut], res[:, cut:]]
        return out

    aak_v = pair_dot(a_ak, vh)
    yv = [jnp.concatenate([at_h[u], aak_v[u]], axis=1) for u in idx]
    apow = a_ab
    span = 1
    while True:
        step = pair_dot(apow, yv)
        yv = [yv[u] + step[u] for u in idx]
        span *= 2
        if span >= RW_CHUNK:
            break
        apow = pair_dot(apow, apow)
    arb_y = pair_dot(a_rb, yv)
    ark_v = pair_dot(a_rk, vh)
    qt = [rt_h[u] + arb_y[u][:, :hd] for u in idx]
    o0 = [arb_y[u][:, hd:] + ark_v[u] for u in idx]
    yt = [yv[u].T for u in idx]
    bpb = [tile_chunks(t) for t in pick("bp")]
    kpb = [tile_chunks(t) for t in pick("kp")]
    g_all = [_dot(yt[u][:hd], bpb[u]) for u in idx]
    h_all = [_dot(jnp.concatenate([yt[u][hd:], vh[u].T], axis=1),
                  jnp.concatenate([bpb[u], kpb[u]], axis=0)) for u in idx]
    pt_h = pick("pt")
    pairs = range(0, len(units), 2)
    low_lanes = lax.broadcasted_iota(I32, (hd, 2 * hd), 1) < hd
    s = [jnp.concatenate([s_ref[u], s_ref[u + 1]], axis=1) for u in pairs]
    for n in range(nchunk):
        rows = slice(n * RW_CHUNK, (n + 1) * RW_CHUNK)
        cols = slice(n * hd, (n + 1) * hd)
        last = (n + 1) * RW_CHUNK - 1
        for j, u in enumerate(pairs):
            s_bd = jnp.concatenate([jnp.where(low_lanes, s[j], 0.0),
                                    jnp.where(low_lanes, 0.0, s[j])], axis=0)
            o_pair = _dot_nt(jnp.concatenate([qt[u][rows], qt[u + 1][rows]], axis=1), s_bd)
            for i in range(2):
                b, sl = units[u + i]
                obuf_ref[b, rows, sl] = o_pair[:, i * hd:(i + 1) * hd] + o0[u + i][rows]
        s = [s[j] * jnp.concatenate([pt_h[u][last:last + 1], pt_h[u + 1][last:last + 1]], axis=1)
             + _dot(s[j], bd(g_all[u][:, cols], g_all[u + 1][:, cols]))
             + jnp.concatenate([h_all[u][:, cols], h_all[u + 1][:, cols]], axis=1)
             for j, u in enumerate(pairs)]
    for j, u in enumerate(pairs):
        s_ref[u] = s[j][:, :hd]
        s_ref[u + 1] = s[j][:, hd:]

    for b in range(nb):
        o = obuf_ref[b]
        mean = segsum(o) * (1.0 / hd)
        oc = o - mean
        var = segsum(oc * oc) * (1.0 / hd)
        on = oc * lax.rsqrt(var + GN_EPS) * gn_g + gn_b
        o_ref[b] = ((on + rows_in[b]["bonus"]) * rows_in[b]["g"]).astype(BF16)


def _rwkv(p_r, mu, w0, w2, a0, a2, g2, k_k, k_a, r_k, gn_g, gn_b):
    bsz, seq, _ = p_r.shape
    tt = RW_TILE
    nb = RW_ROWS if bsz % RW_ROWS == 0 else 1
    w = RWKV_WIDTH
    vec = jnp.stack([w0, a0, k_k, k_a, r_k.reshape(w), gn_g, gn_b, jnp.zeros_like(w0)])
    t_idx = np.arange(tt)
    same = (t_idx[:, None] // RW_CHUNK) == (t_idx[None, :] // RW_CHUNK)
    ltri = jnp.asarray(same & (t_idx[None, :] <= t_idx[:, None]), BF16)
    ustr = jnp.asarray(same & (t_idx[None, :] > t_idx[:, None]), BF16)
    c_idx = np.arange(MXU_TILE)
    seg = jnp.asarray((c_idx[:, None] // HEAD_DIM) == (c_idx[None, :] // HEAD_DIM), BF16)
    full = lambda shape: pl.BlockSpec(shape, lambda b, i: (0,) * len(shape))
    per_tile = tt // SUBLANES
    return pl.pallas_call(
        _rwkv_kernel,
        grid=(bsz // nb, seq // tt),
        in_specs=[pl.BlockSpec((nb, tt, RWKV_COLS), lambda b, i: (b, i, 0)),
                  pl.BlockSpec((nb, SUBLANES, RWKV_COLS),
                               lambda b, i: (b, jnp.maximum(i * per_tile - 1, 0), 0)),
                  full((1, RWKV_COLS)), full((SUBLANES, w)),
                  full(w2.shape), full(a2.shape), full(g2.shape),
                  full((tt, tt)), full((tt, tt)), full((MXU_TILE, MXU_TILE))],
        out_specs=pl.BlockSpec((nb, tt, w), lambda b, i: (b, i, 0)),
        out_shape=jax.ShapeDtypeStruct((bsz, seq, w), BF16),
        scratch_shapes=[pltpu.VMEM((nb * RWKV_HEADS, HEAD_DIM, HEAD_DIM), F32),
                        pltpu.VMEM((nb, tt, w), F32)],
        compiler_params=_params(("parallel", "arbitrary")),
        name="rwkv7_mix",
    )(p_r, p_r, mu.reshape(1, RWKV_COLS), vec, w2.astype(BF16), a2.astype(BF16),
      g2.astype(BF16), ltri, ustr, seg)


def _dsa_kernel(qt_ref, k_ref, vt_ref, qit_ref, ki_ref, wit_ref, low_ref, o_ref,
                key_ref, bias_ref, acc_ref, m_ref, l_ref, *, topk):
    qb = pl.program_id(1)
    nq = Q_BLOCK
    kt_sz = bias_ref.shape[0]
    hd = HEAD_DIM
    start = qb * nq
    nkt = (start + nq + kt_sz - 1) // kt_sz
    col = lax.broadcasted_iota(I32, (1, nq), 1)
    lim = start + (col // CHUNK + 1) * CHUNK
    wit = wit_ref[0]
    qit = qit_ref[0]
    qi_cat = jnp.concatenate([qit[h * IDX_DIM:(h + 1) * IDX_DIM] for h in range(IDX_HEADS)],
                             axis=1)

    def score_tile(j, carry):
        off = pl.multiple_of(j * kt_sz, kt_sz)
        d = jnp.dot(ki_ref[0, pl.ds(off, kt_sz), :], qi_cat, preferred_element_type=F32)
        s = jnp.zeros((kt_sz, nq), F32)
        for h in range(IDX_HEADS):
            s = s + wit[h:h + 1] * jnp.maximum(d[:, h * nq:(h + 1) * nq], 0.0)
        s = s + 0.0
        bits = pltpu.bitcast(s, I32)
        key = bits ^ ((bits >> 31) & 0x7FFFFFFF)
        sidx = off + lax.broadcasted_iota(I32, (kt_sz, nq), 0)
        key_ref[pl.ds(off, kt_sz), :] = jnp.where(sidx < lim, key, INT_MIN)
        return carry

    lax.fori_loop(0, nkt, score_tile, 0)

    def count(pred):
        def body(j, c):
            off = pl.multiple_of(j * kt_sz, kt_sz)
            hit = jnp.where(pred(key_ref[pl.ds(off, kt_sz), :]), 1, 0)
            return c + hit.reshape(kt_sz // RED_ROWS, RED_ROWS, nq).sum(axis=0)
        c = lax.fori_loop(0, nkt, body, jnp.zeros((RED_ROWS, nq), I32))
        return jnp.sum(c, axis=0, keepdims=True)

    zero = jnp.zeros((1, nq), I32)
    lo = jnp.where(count(lambda kv: kv >= zero) >= topk, zero, zero + INT_MIN)

    def bit_step(b, lo):
        cand = lo + lax.shift_left(jnp.int32(1), 30 - b)
        return jnp.where(count(lambda kv: kv >= cand) >= topk, cand, lo)

    th = lax.fori_loop(0, 31, bit_step, lo)
    need = jnp.where(th == INT_MIN, 0, topk - count(lambda kv: kv > th)).astype(F32)

    acc_ref[...] = jnp.zeros_like(acc_ref)
    m_ref[...] = jnp.full_like(m_ref, NEG_INF)
    l_ref[...] = jnp.zeros_like(l_ref)
    low = low_ref[...]
    npair = DSA_HEADS // 2
    top_rows = lax.broadcasted_iota(I32, (2 * hd, nq), 0) < hd
    qt = qt_ref[0].astype(F32)
    q_bd = []
    for p in range(npair):
        qp = qt[p * 2 * hd:(p + 1) * 2 * hd]
        q_bd.append(jnp.concatenate([jnp.where(top_rows, qp, 0.0), jnp.where(top_rows, 0.0, qp)],
                                    axis=1).astype(BF16))

    def attn_tile(j, eq_before):
        off = pl.multiple_of(j * kt_sz, kt_sz)
        eq_run = eq_before
        for t in range(kt_sz // TRI_TILE):
            kv = key_ref[pl.ds(off + t * TRI_TILE, TRI_TILE), :]
            eq = kv == th
            pre = eq_run + jnp.dot(low, jnp.where(eq, 1.0, 0.0).astype(BF16),
                                   preferred_element_type=F32)
            sel = (kv > th) | (eq & (pre <= need))
            bias_ref[t * TRI_TILE:(t + 1) * TRI_TILE, :] = jnp.where(sel, 0.0, NEG_INF)
            eq_run = pre[TRI_TILE - 1:TRI_TILE]
        bias = bias_ref[...]

        def logits(p):
            half = kt_sz // 2
            return jnp.concatenate(
                [jnp.dot(k_ref[0, pl.ds(off + r * half, half), p * 2 * hd:(p + 1) * 2 * hd],
                         q_bd[p], preferred_element_type=F32) for r in range(2)], axis=0)

        st_next = logits(0)
        for p in range(npair):
            cols = slice(p * 2 * hd, (p + 1) * 2 * hd)
            st = st_next
            if p + 1 < npair:
                st_next = logits(p + 1)
            prs, scales = [], []
            for i in range(2):
                h = 2 * p + i
                s = st[:, i * nq:(i + 1) * nq] + bias
                m_old = m_ref[h:h + 1]
                m_new = jnp.maximum(m_old, _fold_rows(s, jnp.max))
                pr = jnp.exp2(s - m_new)
                scale = jnp.exp2(m_old - m_new)
                l_ref[h:h + 1] = scale * l_ref[h:h + 1] + _fold_rows(pr, jnp.sum)
                m_ref[h:h + 1] = m_new
                prs.append(pr.astype(BF16))
                scales.append(scale)
            pv = jnp.dot(vt_ref[0, cols, pl.ds(off, kt_sz)], jnp.concatenate(prs, axis=1),
                         preferred_element_type=F32)
            new = jnp.where(top_rows, pv[:, :nq], pv[:, nq:])
            acc_ref[p] = jnp.where(top_rows, scales[0], scales[1]) * acc_ref[p] + new
        return eq_run

    lax.fori_loop(0, nkt, attn_tile, jnp.zeros((1, nq), F32))
    for p in range(npair):
        inv = jnp.where(top_rows, 1.0 / l_ref[2 * p:2 * p + 1], 1.0 / l_ref[2 * p + 1:2 * p + 2])
        o_ref[0, :, p * 2 * hd:(p + 1) * 2 * hd] = (acc_ref[p] * inv).T.astype(BF16)


def _dsa(q_t, k, v_t, qi_t, ki, wi_t):
    bsz, seq, w = k.shape
    nq = Q_BLOCK
    topk = min(INDEX_TOPK, seq // 4)
    kt_sz = min(KEY_TILE, seq)
    t_idx = np.arange(TRI_TILE)
    low = jnp.asarray(t_idx[None, :] <= t_idx[:, None], BF16)
    cols = lambda n: pl.BlockSpec((1, n, nq), lambda b, i: (b, 0, i))
    return pl.pallas_call(
        functools.partial(_dsa_kernel, topk=topk),
        grid=(bsz, seq // nq),
        in_specs=[cols(w),
                  pl.BlockSpec((1, seq, w), lambda b, i: (b, 0, 0)),
                  pl.BlockSpec((1, w, seq), lambda b, i: (b, 0, 0)),
                  cols(IDX_HEADS * IDX_DIM),
                  pl.BlockSpec((1, seq, IDX_DIM), lambda b, i: (b, 0, 0)),
                  cols(SUBLANES),
                  pl.BlockSpec((TRI_TILE, TRI_TILE), lambda b, i: (0, 0))],
        out_specs=pl.BlockSpec((1, nq, w), lambda b, i: (b, i, 0)),
        out_shape=jax.ShapeDtypeStruct((bsz, seq, w), BF16),
        scratch_shapes=[pltpu.VMEM((seq, nq), I32),
                        pltpu.VMEM((kt_sz, nq), F32),
                        pltpu.VMEM((DSA_HEADS // 2, 2 * HEAD_DIM, nq), F32),
                        pltpu.VMEM((DSA_HEADS, nq), F32),
                        pltpu.VMEM((DSA_HEADS, nq), F32)],
        compiler_params=_params(("parallel", "arbitrary")),
        name="dsa_mix",
    )(q_t, k, v_t, qi_t, ki, wi_t, low)


POOL_HALO = 16


def _odd_layer_kernel(x_ref, xp_ref, mod_ref, win_ref, pw_ref, ps_ref, lng_ref, lnb_ref, ws_ref,
                      bs_ref, wout_ref, g_ref, b_ref, o_ref, y_ref):
    i = pl.program_id(1)
    m = mod_ref[0]
    sc = 1.0 + m[1:2]
    sh = m[0:1]
    x = x_ref[0]
    tm = x.shape[0]
    gd = POOL_GROUP_DIM
    p = jnp.dot((x * sc + sh).astype(BF16), win_ref[...], preferred_element_type=F32)
    prev = jnp.dot((xp_ref[0] * sc + sh).astype(BF16), win_ref[:, :POOL_WIDTH],
                   preferred_element_type=F32) * (i > 0).astype(F32)
    t_glob = (i * tm + lax.broadcasted_iota(I32, (tm, 1), 0)).astype(F32)
    scale = ps_ref[...]
    for gi, win in enumerate(POOL_WINDOWS):
        cols = slice(gi * gd, (gi + 1) * gd)
        xg = p[:, cols]
        s = jnp.concatenate([prev[:, cols], xg], axis=0)
        span = 1
        while span < win:
            s = s[span:] + s[:-span]
            span *= 2
        first = POOL_HALO + 1 - win
        pooled = s[first:first + tm] / jnp.minimum(t_glob + 1.0, float(win)) - xg
        y_ref[:, cols] = (_dot(pooled, pw_ref[gi]) * scale[:, cols]).astype(BF16)

    u = _gelu(p[:, POOL_WIDTH:POOL_WIDTH + SG_WIDTH])
    v = _ln(_gelu(p[:, POOL_WIDTH + SG_WIDTH:]), lng_ref[...], lnb_ref[...], LN_EPS)
    ti = lax.broadcasted_iota(I32, (SG_CHUNK, SG_CHUNK), 0)
    si = lax.broadcasted_iota(I32, (SG_CHUNK, SG_CHUNK), 1)
    bs = bs_ref[...]
    for gi in range(SG_GROUPS):
        cols = slice(gi * SG_GROUP_DIM, (gi + 1) * SG_GROUP_DIM)
        ws = jnp.where(si <= ti, ws_ref[gi], 0.0)
        for n in range(tm // SG_CHUNK):
            rows = slice(n * SG_CHUNK, (n + 1) * SG_CHUNK)
            z = _dot(ws, v[rows, cols]) + bs[:, gi:gi + 1]
            y_ref[rows, POOL_WIDTH + gi * SG_GROUP_DIM:POOL_WIDTH + (gi + 1) * SG_GROUP_DIM] = (
                u[rows, cols] * z).astype(BF16)

    y = jnp.dot(y_ref[...], wout_ref[...], preferred_element_type=F32)
    o_ref[0] = _ln(ALPHA * x + m[2:3] * y, g_ref[...], b_ref[...], LN_EPS)


def _odd_layer(x, mod, w_in, pool_w, pool_scale, sg_ln_g, sg_ln_b, sg_w, sg_b, w_out, ln_g, ln_b):
    bsz, seq, d = x.shape
    tm = 256
    per_tile = tm // POOL_HALO
    full = lambda shape: pl.BlockSpec(shape, lambda b, i: (0,) * len(shape))
    once = lambda shape: pl.BlockSpec(shape, lambda b, i: (0,) * len(shape),
                                      pipeline_mode=pl.Buffered(1))
    return pl.pallas_call(
        _odd_layer_kernel,
        grid=(bsz, seq // tm),
        in_specs=[pl.BlockSpec((1, tm, d), lambda b, i: (b, i, 0)),
                  pl.BlockSpec((1, POOL_HALO, d),
                               lambda b, i: (b, jnp.maximum(i * per_tile - 1, 0), 0)),
                  pl.BlockSpec((1, 6, d), lambda b, i: (b, 0, 0)),
                  once(w_in.shape), full(pool_w.shape), full((1, POOL_WIDTH)),
                  full((1, SG_WIDTH)), full((1, SG_WIDTH)), full(sg_w.shape),
                  full((SG_CHUNK, SG_GROUPS)), once(w_out.shape), full((1, d)), full((1, d))],
        out_specs=pl.BlockSpec((1, tm, d), lambda b, i: (b, i, 0)),
        out_shape=jax.ShapeDtypeStruct((bsz, seq, d), F32),
        scratch_shapes=[pltpu.VMEM((tm, d), BF16)],
        compiler_params=_params(("parallel", "parallel")),
        name="odd_layer_mix",
    )(x, x, mod, w_in.astype(BF16), pool_w.astype(BF16), pool_scale.reshape(1, -1),
      sg_ln_g.reshape(1, -1), sg_ln_b.reshape(1, -1), sg_w, sg_b.T, w_out.astype(BF16),
      ln_g.reshape(1, d), ln_b.reshape(1, d))


def _proj_ln_kernel(*refs, n_in, gate_row):
    a_refs = refs[:n_in]
    w_refs = refs[n_in:2 * n_in]
    x_ref, mod_ref, g_ref, b_ref, o_ref = refs[2 * n_in:]
    y = None
    for a_ref, w_ref in zip(a_refs, w_refs):
        part = jnp.dot(a_ref[0], w_ref[...], preferred_element_type=F32)
        y = part if y is None else y + part
    gate = mod_ref[0][gate_row:gate_row + 1]
    o_ref[0] = _ln(ALPHA * x_ref[0] + gate * y, g_ref[...], b_ref[...], LN_EPS)


def _proj_ln(acts, weights, x, mod, gate_row, ln_g, ln_b, name):
    bsz, seq, d = x.shape
    tm = 512
    n_in = len(acts)
    in_specs = [pl.BlockSpec((1, tm, a.shape[2]), lambda b, i: (b, i, 0)) for a in acts]
    in_specs += [pl.BlockSpec(w.shape, lambda b, i: (0, 0)) for w in weights]
    in_specs += [pl.BlockSpec((1, tm, d), lambda b, i: (b, i, 0)),
                 pl.BlockSpec((1, 6, d), lambda b, i: (b, 0, 0)),
                 pl.BlockSpec((1, d), lambda b, i: (0, 0)),
                 pl.BlockSpec((1, d), lambda b, i: (0, 0))]
    return pl.pallas_call(
        functools.partial(_proj_ln_kernel, n_in=n_in, gate_row=gate_row),
        grid=(bsz, seq // tm),
        in_specs=in_specs,
        out_specs=pl.BlockSpec((1, tm, d), lambda b, i: (b, i, 0)),
        out_shape=jax.ShapeDtypeStruct((bsz, seq, d), F32),
        compiler_params=_params(("parallel", "parallel")),
        name=name,
    )(*acts, *[w.astype(BF16) for w in weights], x, mod, ln_g.reshape(1, d), ln_b.reshape(1, d))


def _ffn_kernel(x_ref, xp_ref, mod_ref, wg_ref, wv_ref, cwg_ref, cwv_ref, cbg_ref, cbv_ref,
                wd_ref, g_ref, b_ref, o_ref):
    i = pl.program_id(1)
    m = mod_ref[0]
    sc = 1.0 + m[4:5]
    sh = m[3:4]
    x = x_ref[0]
    live = (i > 0).astype(F32)
    h = jnp.concatenate([(xp_ref[0] * sc + sh) * live, x * sc + sh], axis=0).astype(BF16)
    halo = SUBLANES

    def conv(w_ref, cw_ref, cb_ref):
        u = jnp.dot(h, w_ref[...], preferred_element_type=F32)
        cw = cw_ref[...]
        u1 = pltpu.roll(u, 1, 0)
        u2 = pltpu.roll(u, 2, 0)
        return (cb_ref[...] + u2[halo:] * cw[0:1] + u1[halo:] * cw[1:2] + u[halo:] * cw[2:3])

    gate = conv(wg_ref, cwg_ref, cbg_ref)
    val = conv(wv_ref, cwv_ref, cbv_ref)
    act = (gate * jax.nn.sigmoid(gate) * val).astype(BF16)
    y = jnp.dot(act, wd_ref[...], preferred_element_type=F32)
    o_ref[0] = _ln(ALPHA * x + m[5:6] * y, g_ref[...], b_ref[...], LN_EPS)


def _conv_ffn(x, mod, w_up, conv_w, conv_b, w_down, ln_g, ln_b):
    bsz, seq, d = x.shape
    tm = 512
    per_tile = tm // SUBLANES
    wb = w_up.astype(BF16)
    cb = conv_b.reshape(1, -1)
    once = dict(pipeline_mode=pl.Buffered(1))
    return pl.pallas_call(
        _ffn_kernel,
        grid=(bsz, seq // tm),
        in_specs=[pl.BlockSpec((1, tm, d), lambda b, i: (b, i, 0)),
                  pl.BlockSpec((1, SUBLANES, d),
                               lambda b, i: (b, jnp.maximum(i * per_tile - 1, 0), 0)),
                  pl.BlockSpec((1, 6, d), lambda b, i: (b, 0, 0)),
                  pl.BlockSpec((d, D_FF), lambda b, i: (0, 0), **once),
                  pl.BlockSpec((d, D_FF), lambda b, i: (0, 1), **once),
                  pl.BlockSpec((3, D_FF), lambda b, i: (0, 0), **once),
                  pl.BlockSpec((3, D_FF), lambda b, i: (0, 1), **once),
                  pl.BlockSpec((1, D_FF), lambda b, i: (0, 0), **once),
                  pl.BlockSpec((1, D_FF), lambda b, i: (0, 1), **once),
                  pl.BlockSpec((D_FF, d), lambda b, i: (0, 0), **once),
                  pl.BlockSpec((1, d), lambda b, i: (0, 0)),
                  pl.BlockSpec((1, d), lambda b, i: (0, 0))],
        out_specs=pl.BlockSpec((1, tm, d), lambda b, i: (b, i, 0)),
        out_shape=jax.ShapeDtypeStruct((bsz, seq, d), F32),
        compiler_params=_params(("parallel", "parallel")),
        name="conv_ffn_ln",
    )(x, x, mod, wb, wb, conv_w, conv_w, cb, cb, w_down.astype(BF16),
      ln_g.reshape(1, d), ln_b.reshape(1, d))


def _rope_tables(seq):
    inv = ROPE_THETA ** (-jnp.arange(0, ROPE_DIM, 2, dtype=F32) / ROPE_DIM)
    ang = jnp.arange(seq, dtype=F32)[:, None] * inv[None, :]
    cos, sin = jnp.cos(ang), jnp.sin(ang)
    half = ROPE_DIM // 2
    rest = HEAD_DIM - ROPE_DIM
    one = jnp.ones((seq, rest), F32)
    zero = jnp.zeros((seq, rest), F32)
    zh = jnp.zeros((seq, half), F32)
    head = lambda *parts: jnp.concatenate(parts * (LANES // HEAD_DIM), axis=1)
    return head(cos, cos, one), head(-sin, zh, zero), head(zh, sin, zero)


def kernel(x, c, ada_w, ada_b, ln_g, ln_b, ffn_w_up, ffn_conv_w, ffn_conv_b, ffn_w_down, ev_w_in, ev_w_out, rw_mu, rw_w0, rw_w2, rw_a0, rw_a2, rw_g2, rw_k_k, rw_k_a, rw_r_k, rw_gn_g, rw_gn_b, ik_ln_g, ik_ln_b, od_w_in, od_w_out, pool_w, pool_scale, sg_ln_g, sg_ln_b, sg_w, sg_b):
    seq = x.shape[1]
    tables = _rope_tables(seq)
    mods = _modulation(c, ada_w, ada_b)
    ev_w = _split_even_weights(ev_w_in)
    for layer in range(DEPTH):
        mod = mods[layer]
        if layer % 2 == 0:
            e = layer // 2
            p_r, q_t, k, v_t, qi_t, ki, wi_t = _even_in_proj(x, mod, [w[e] for w in ev_w],
                                                             ik_ln_g[e], ik_ln_b[e], tables)
            ya = _rwkv(p_r, rw_mu[e], rw_w0[e], rw_w2[e], rw_a0[e], rw_a2[e], rw_g2[e],
                       rw_k_k[e], rw_k_a[e], rw_r_k[e], rw_gn_g[e], rw_gn_b[e])
            yb = _dsa(q_t, k, v_t, qi_t, ki, wi_t)
            w_out = ev_w_out[e]
            x = _proj_ln([ya, yb], [w_out[:RWKV_WIDTH], w_out[RWKV_WIDTH:]], x, mod, 2,
                         ln_g[layer, 0], ln_b[layer, 0], "even_out_proj_ln")
        else:
            o = layer // 2
            x = _odd_layer(x, mod, od_w_in[o], pool_w[o], pool_scale[o], sg_ln_g[o], sg_ln_b[o],
                           sg_w[o], sg_b[o], od_w_out[o], ln_g[layer, 0], ln_b[layer, 0])
        x = _conv_ffn(x, mod, ffn_w_up[layer], ffn_conv_w[layer], ffn_conv_b[layer],
                      ffn_w_down[layer], ln_g[layer, 1], ln_b[layer, 1])
    return x
```

```python
import functools

import numpy as np
import jax
import jax.numpy as jnp
from jax import lax
from jax.experimental import pallas as pl
from jax.experimental.pallas import tpu as pltpu

F32 = jnp.float32
BF16 = jnp.bfloat16
I32 = jnp.int32

D_MODEL = 1024
DEPTH = 4
CHUNK = 64
HEAD_DIM = 64
RWKV_WIDTH = D_MODEL // 2
RWKV_HEADS = RWKV_WIDTH // HEAD_DIM
RWKV_LORA_W = 64
RWKV_LORA_A = 64
RWKV_LORA_G = 128
RWKV_COLS = 3 * RWKV_WIDTH + RWKV_LORA_W + RWKV_LORA_A + RWKV_LORA_G
DSA_WIDTH = D_MODEL - RWKV_WIDTH
DSA_HEADS = DSA_WIDTH // HEAD_DIM
IDX_HEADS = 4
IDX_DIM = 64
INDEX_TOPK = 256
Q_BLOCK = 128
ROPE_THETA = 500000.0
ROPE_DIM = HEAD_DIM // 4
POOL_WINDOWS = (2, 4, 8, 16)
POOL_WIDTH = D_MODEL // 2
POOL_GROUP_DIM = POOL_WIDTH // len(POOL_WINDOWS)
SG_WIDTH = D_MODEL - POOL_WIDTH
SG_GROUPS = 4
SG_GROUP_DIM = SG_WIDTH // SG_GROUPS
SG_CHUNK = 128
D_FF = 2816
ALPHA = (2.0 * DEPTH) ** 0.25
LN_EPS = 1e-5
GN_EPS = 64e-5
NEG_INF = -1e30
INT_MIN = -(2 ** 31)
LOG2E = 1.4426950408889634

LANES = 128
SUBLANES = 8
MXU_TILE = 256
VMEM_LIMIT = 56 * 1024 * 1024

RW_TILE = 128
RW_CHUNK = 16
RW_ROWS = 4
KEY_TILE = 1024
TRI_TILE = 256
SLOTS = 256
PAD_IDX = 384

def _dot(a, b):
    return jnp.dot(a.astype(BF16), b.astype(BF16), preferred_element_type=F32)


def _dot_nt(a, b):
    return lax.dot_general(a.astype(BF16), b.astype(BF16), (((1,), (1,)), ((), ())),
                           preferred_element_type=F32)


def _dot_split(x, m01, terms):
    acc = None
    rem = x
    for _ in range(terms):
        piece = rem.astype(BF16)
        rem = rem - piece.astype(F32)
        part = jnp.dot(piece, m01, preferred_element_type=F32)
        acc = part if acc is None else acc + part
    return acc


def _dot_split_left(m01, x, terms):
    acc = None
    rem = x
    for _ in range(terms):
        piece = rem.astype(BF16)
        rem = rem - piece.astype(F32)
        part = jnp.dot(m01, piece, preferred_element_type=F32)
        acc = part if acc is None else acc + part
    return acc


def _ln(x, g, b, eps):
    mu = jnp.mean(x, axis=-1, keepdims=True)
    xc = x - mu
    var = jnp.mean(xc * xc, axis=-1, keepdims=True)
    return xc * lax.rsqrt(var + eps) * g + b


def _gelu(x):
    return 0.5 * x * (1.0 + lax.erf(x * 0.7071067811865476))


RED_ROWS = 64


def _fold_rows(x, op):
    rows, n = x.shape
    part = op(x.reshape(rows // RED_ROWS, RED_ROWS, n), axis=0)
    return op(part, axis=0, keepdims=True)


def _params(sem):
    return pltpu.CompilerParams(dimension_semantics=sem, vmem_limit_bytes=VMEM_LIMIT)


def _mod_kernel(c_ref, w_ref, b_ref, o_ref):
    c = c_ref[...]
    ca = c * jax.nn.sigmoid(c)
    o_ref[0] = jnp.dot(ca, w_ref[0], preferred_element_type=F32,
                       precision=lax.Precision.HIGHEST) + b_ref[0]


def _modulation(c, ada_w, ada_b):
    bsz, d = c.shape
    depth = ada_w.shape[0]
    n = ada_w.shape[2]
    tn = 1536
    rows = -(-bsz // SUBLANES) * SUBLANES
    c8 = jnp.pad(c, ((0, rows - bsz), (0, 0)))
    out = pl.pallas_call(
        _mod_kernel,
        grid=(depth, n // tn),
        in_specs=[pl.BlockSpec((rows, d), lambda l, j: (0, 0)),
                  pl.BlockSpec((1, d, tn), lambda l, j: (l, 0, j)),
                  pl.BlockSpec((1, 1, tn), lambda l, j: (l, 0, j))],
        out_specs=pl.BlockSpec((1, rows, tn), lambda l, j: (l, 0, j)),
        out_shape=jax.ShapeDtypeStruct((depth, rows, n), F32),
        compiler_params=_params(("arbitrary", "arbitrary")),
        name="adaln_mod",
    )(c8, ada_w, ada_b.reshape(depth, 1, n))
    return out[:, :bsz].reshape(depth, bsz, 6, d)


def _rope(x, cos_t, sin_a, sin_b):
    n = x.shape[1] // LANES
    rep = (lambda t: jnp.concatenate([t] * n, axis=1)) if n > 1 else (lambda t: t)
    width = x.shape[1]
    half = ROPE_DIM // 2
    return (x * rep(cos_t) + pltpu.roll(x, width - half, 1) * rep(sin_a)
            + pltpu.roll(x, half, 1) * rep(sin_b))


def _even_in_kernel(x_ref, mod_ref, wr_ref, wqkv_ref, widx_ref, cos_ref, sa_ref, sb_ref,
                    ikg_ref, ikb_ref,
                    pr_ref, qt_ref, k_ref, vt_ref, qit_ref, ki_ref, wit_ref):
    m = mod_ref[0]
    h = (x_ref[0] * (1.0 + m[1:2]) + m[0:1]).astype(BF16)
    pr_ref[0] = jnp.dot(h, wr_ref[...], preferred_element_type=F32)
    qkv = jnp.dot(h, wqkv_ref[...], preferred_element_type=F32)
    cos_t = cos_ref[...]
    sin_a = sa_ref[...]
    sin_b = sb_ref[...]
    w = DSA_WIDTH
    q = _rope(qkv[:, :w], cos_t, sin_a, sin_b) * (HEAD_DIM ** -0.5 * LOG2E)
    qt_ref[0] = q.T.astype(BF16)
    k_ref[0] = _rope(qkv[:, w:2 * w], cos_t, sin_a, sin_b).astype(BF16)
    vt_ref[0] = qkv[:, 2 * w:].T.astype(BF16)
    idx = jnp.dot(h, widx_ref[...], preferred_element_type=F32)
    nq = IDX_HEADS * IDX_DIM
    qit_ref[0] = _rope(idx[:, :nq], cos_t, sin_a, sin_b).T.astype(BF16)
    blk = idx[:, nq:nq + LANES]
    lane = lax.broadcasted_iota(I32, blk.shape, 1)
    is_k = lane < IDX_DIM
    mu = jnp.sum(jnp.where(is_k, blk, 0.0), axis=1, keepdims=True) * (1.0 / IDX_DIM)
    xc = jnp.where(is_k, blk - mu, 0.0)
    var = jnp.sum(xc * xc, axis=1, keepdims=True) * (1.0 / IDX_DIM)
    kin = xc * lax.rsqrt(var + LN_EPS) * ikg_ref[...] + ikb_ref[...]
    ki_ref[0] = _rope(kin, cos_t, sin_a, sin_b)[:, :IDX_DIM].astype(BF16)
    wit = (blk * (IDX_HEADS ** -0.5 * IDX_DIM ** -0.5)).T
    wit_ref[0] = wit[IDX_DIM:IDX_DIM + SUBLANES]


def _split_w_kernel(w_ref, wr_ref, wqkv_ref, widx_ref):
    w = w_ref[0]
    c1 = RWKV_COLS
    c2 = RWKV_COLS + 3 * DSA_WIDTH
    wr_ref[0] = w[:, :c1].astype(BF16)
    wqkv_ref[0] = w[:, c1:c2].astype(BF16)
    tail = w[:, c2:]
    zeros = jnp.zeros((w.shape[0], PAD_IDX - tail.shape[1]), F32)
    widx_ref[0] = jnp.concatenate([tail, zeros], axis=1).astype(BF16)


def _split_even_weights(ev_w_in):
    n_even, d, n = ev_w_in.shape
    tr = 256
    widths = (RWKV_COLS, 3 * DSA_WIDTH, PAD_IDX)
    return pl.pallas_call(
        _split_w_kernel,
        grid=(n_even, d // tr),
        in_specs=[pl.BlockSpec((1, tr, n), lambda e, i: (e, i, 0))],
        out_specs=[pl.BlockSpec((1, tr, wd), lambda e, i: (e, i, 0)) for wd in widths],
        out_shape=[jax.ShapeDtypeStruct((n_even, d, wd), BF16) for wd in widths],
        compiler_params=_params(("parallel", "parallel")),
        name="split_even_weights",
    )(ev_w_in)


def _even_in_proj(x, mod, weights, ik_g, ik_b, tables):
    bsz, seq, d = x.shape
    tm = 256
    w_r, w_qkv, w_idx = weights
    pad = LANES - IDX_DIM
    ikg = jnp.pad(ik_g, (0, pad)).reshape(1, LANES)
    ikb = jnp.pad(ik_b, (0, pad)).reshape(1, LANES)
    cos_t, sin_a, sin_b = tables
    full = lambda shape: pl.BlockSpec(shape, lambda b, i: (0,) * len(shape))
    tab = pl.BlockSpec((tm, LANES), lambda b, i: (i, 0))
    nq = IDX_HEADS * IDX_DIM
    rows = lambda n: pl.BlockSpec((1, tm, n), lambda b, i: (b, i, 0))
    cols = lambda n: pl.BlockSpec((1, n, tm), lambda b, i: (b, 0, i))
    return pl.pallas_call(
        _even_in_kernel,
        grid=(bsz, seq // tm),
        in_specs=[pl.BlockSpec((1, tm, d), lambda b, i: (b, i, 0)),
                  pl.BlockSpec((1, 6, d), lambda b, i: (b, 0, 0)),
                  full(w_r.shape), full(w_qkv.shape), full(w_idx.shape),
                  tab, tab, tab, full((1, LANES)), full((1, LANES))],
        out_specs=[rows(RWKV_COLS), cols(DSA_WIDTH), rows(DSA_WIDTH), cols(DSA_WIDTH),
                   cols(nq), rows(IDX_DIM), cols(SUBLANES)],
        out_shape=[jax.ShapeDtypeStruct((bsz, seq, RWKV_COLS), F32),
                   jax.ShapeDtypeStruct((bsz, DSA_WIDTH, seq), BF16),
                   jax.ShapeDtypeStruct((bsz, seq, DSA_WIDTH), BF16),
                   jax.ShapeDtypeStruct((bsz, DSA_WIDTH, seq), BF16),
                   jax.ShapeDtypeStruct((bsz, nq, seq), BF16),
                   jax.ShapeDtypeStruct((bsz, seq, IDX_DIM), BF16),
                   jax.ShapeDtypeStruct((bsz, SUBLANES, seq), F32)],
        compiler_params=_params(("parallel", "parallel")),
        name="even_in_proj",
    )(x, mod, w_r, w_qkv, w_idx, cos_t, sin_a, sin_b, ikg, ikb)


def _rwkv_kernel(p_ref, pp_ref, mu_ref, vec_ref, w2_ref, a2_ref, g2_ref, ltri_ref, ustr_ref,
                 seg_ref, o_ref, s_ref, obuf_ref):
    i = pl.program_id(1)

    @pl.when(i == 0)
    def _():
        s_ref[...] = jnp.zeros_like(s_ref)

    tt = RW_TILE
    w = RWKV_WIDTH
    hd = HEAD_DIM
    nb = p_ref.shape[0]
    nchunk = tt // RW_CHUNK
    seg = seg_ref[...]
    half_w = seg.shape[0]

    def segsum(t):
        return jnp.concatenate([_dot_split(t[:, c:c + half_w], seg, 2)
                                for c in range(0, w, half_w)], axis=1)

    vec = vec_ref[...]
    w0, a0, k_k, k_a, r_k, gn_g, gn_b = (vec[j:j + 1] for j in range(7))
    rowi = lax.broadcasted_iota(I32, (tt, 1), 0)
    live = (i > 0).astype(F32)

    def prepare(b):
        p = p_ref[b]
        prow = pp_ref[b][SUBLANES - 1:SUBLANES] * live
        xprev = jnp.where(rowi == 0, prow, pltpu.roll(p, 1, 0))
        ps = p + (xprev - p) * mu_ref[...]
        r = ps[:, :w]
        k = ps[:, w:2 * w]
        v = ps[:, 2 * w:3 * w]
        o1 = 3 * w
        wd = ps[:, o1:o1 + RWKV_LORA_W]
        ad = ps[:, o1 + RWKV_LORA_W:o1 + RWKV_LORA_W + RWKV_LORA_A]
        gd = ps[:, o1 + RWKV_LORA_W + RWKV_LORA_A:]
        y = -(w0 + _dot(jnp.tanh(wd), w2_ref[...]))
        softplus = jnp.maximum(y, 0.0) + jnp.log1p(jnp.exp(-jnp.abs(y)))
        logw = -jnp.exp(-softplus - 0.5)
        a = jax.nn.sigmoid(a0 + _dot(ad, a2_ref[...]))
        g = _dot(jax.nn.sigmoid(gd), g2_ref[...])
        kk = k * k_k
        kk = kk / jnp.maximum(jnp.sqrt(segsum(kk * kk)), 1e-12)
        k2 = k * (1.0 + (a - 1.0) * k_a)
        bonus = segsum(r * k2 * r_k) * v
        cum = _dot_split_left(ltri_ref[...], logw, 3)
        rem = _dot_split_left(ustr_ref[...], logw, 3)
        pt = jnp.exp(cum)
        ipt = jnp.exp(-cum)
        erem = jnp.exp(rem)
        kka = kk * a
        return dict(at=-kk * jnp.exp(cum - logw), rt=r * pt, bt=kka * ipt, kt=k2 * ipt,
                    bp=kka * erem, kp=k2 * erem, v=v, pt=pt, bonus=bonus, g=g)

    rows_in = [prepare(b) for b in range(nb)]

    ti = lax.broadcasted_iota(I32, (tt, tt), 0)
    si = lax.broadcasted_iota(I32, (tt, tt), 1)
    same = (ti // RW_CHUNK) == (si // RW_CHUNK)
    strict = same & (si < ti)
    incl = same & (si <= ti)
    tb = lax.broadcasted_iota(I32, (tt, nchunk * hd), 0)
    cb = lax.broadcasted_iota(I32, (tt, nchunk * hd), 1)
    blkmask = (tb // RW_CHUNK) == (cb // hd)
    tile_chunks = lambda t: jnp.where(blkmask, jnp.concatenate([t] * nchunk, axis=1), 0.0)

    units = [(b, slice(h * hd, (h + 1) * hd)) for b in range(nb) for h in range(RWKV_HEADS)]
    idx = range(len(units))
    pick = lambda name: [rows_in[b][name][:, sl] for b, sl in units]
    at_h, rt_h, vh, bt_h, kt_h = pick("at"), pick("rt"), pick("v"), pick("bt"), pick("kt")
    x = [_dot_nt(jnp.concatenate([at_h[u], rt_h[u]], axis=0),
                 jnp.concatenate([bt_h[u], kt_h[u]], axis=0)) for u in idx]
    a_ab = [jnp.where(strict, x[u][:tt, :tt], 0.0) for u in idx]
    a_ak = [jnp.where(strict, x[u][:tt, tt:], 0.0) for u in idx]
    a_rb = [jnp.where(incl, x[u][tt:, :tt], 0.0) for u in idx]
    a_rk = [jnp.where(incl, x[u][tt:, tt:], 0.0) for u in idx]
    def bd(m0, m1):
        z0 = jnp.zeros((m0.shape[0], m1.shape[1]), m0.dtype)
        z1 = jnp.zeros((m1.shape[0], m0.shape[1]), m1.dtype)
        return jnp.concatenate([jnp.concatenate([m0, z0], axis=1),
                                jnp.concatenate([z1, m1], axis=1)], axis=0)

    def pair_dot(lhs, rhs):
        out = []
        for u in range(0, len(lhs), 2):
            res = _dot(jnp.concatenate([lhs[u], lhs[u + 1]], axis=1), bd(rhs[u], rhs[u + 1]))
            cut = rhs[u].shape[1]
            out += [res[:, :cut], res[:, cut:]]
        return out

    aak_v = pair_dot(a_ak, vh)
    yv = [jnp.concatenate([at_h[u], aak_v[u]], axis=1) for u in idx]
    apow = a_ab
    span = 1
    while True:
        step = pair_dot(apow, yv)
        yv = [yv[u] + step[u] for u in idx]
        span *= 2
        if span >= RW_CHUNK:
            break
        apow = pair_dot(apow, apow)
    arb_y = pair_dot(a_rb, yv)
    ark_v = pair_dot(a_rk, vh)
    qt = [rt_h[u] + arb_y[u][:, :hd] for u in idx]
    o0 = [arb_y[u][:, hd:] + ark_v[u] for u in idx]
    yt = [yv[u].T for u in idx]
    bpb = [tile_chunks(t) for t in pick("bp")]
    kpb = [tile_chunks(t) for t in pick("kp")]
    g_all = [_dot(yt[u][:hd], bpb[u]) for u in idx]
    h_all = [_dot(jnp.concatenate([yt[u][hd:], vh[u].T], axis=1),
                  jnp.concatenate([bpb[u], kpb[u]], axis=0)) for u in idx]
    pt_h = pick("pt")
    pairs = range(0, len(units), 2)
    low_lanes = lax.broadcasted_iota(I32, (hd, 2 * hd), 1) < hd
    s = [jnp.concatenate([s_ref[u], s_ref[u + 1]], axis=1) for u in pairs]
    for n in range(nchunk):
        rows = slice(n * RW_CHUNK, (n + 1) * RW_CHUNK)
        cols = slice(n * hd, (n + 1) * hd)
        last = (n + 1) * RW_CHUNK - 1
        for j, u in enumerate(pairs):
            s_bd = jnp.concatenate([jnp.where(low_lanes, s[j], 0.0),
                                    jnp.where(low_lanes, 0.0, s[j])], axis=0)
            o_pair = _dot_nt(jnp.concatenate([qt[u][rows], qt[u + 1][rows]], axis=1), s_bd)
            for i in range(2):
                b, sl = units[u + i]
                obuf_ref[b, rows, sl] = o_pair[:, i * hd:(i + 1) * hd] + o0[u + i][rows]
        s = [s[j] * jnp.concatenate([pt_h[u][last:last + 1], pt_h[u + 1][last:last + 1]], axis=1)
             + _dot(s[j], bd(g_all[u][:, cols], g_all[u + 1][:, cols]))
             + jnp.concatenate([h_all[u][:, cols], h_all[u + 1][:, cols]], axis=1)
             for j, u in enumerate(pairs)]
    for j, u in enumerate(pairs):
        s_ref[u] = s[j][:, :hd]
        s_ref[u + 1] = s[j][:, hd:]

    for b in range(nb):
        o = obuf_ref[b]
        mean = segsum(o) * (1.0 / hd)
        oc = o - mean
        var = segsum(oc * oc) * (1.0 / hd)
        on = oc * lax.rsqrt(var + GN_EPS) * gn_g + gn_b
        o_ref[b] = ((on + rows_in[b]["bonus"]) * rows_in[b]["g"]).astype(BF16)


def _rwkv(p_r, mu, w0, w2, a0, a2, g2, k_k, k_a, r_k, gn_g, gn_b):
    bsz, seq, _ = p_r.shape
    tt = RW_TILE
    nb = RW_ROWS if bsz % RW_ROWS == 0 else 1
    w = RWKV_WIDTH
    vec = jnp.stack([w0, a0, k_k, k_a, r_k.reshape(w), gn_g, gn_b, jnp.zeros_like(w0)])
    t_idx = np.arange(tt)
    same = (t_idx[:, None] // RW_CHUNK) == (t_idx[None, :] // RW_CHUNK)
    ltri = jnp.asarray(same & (t_idx[None, :] <= t_idx[:, None]), BF16)
    ustr = jnp.asarray(same & (t_idx[None, :] > t_idx[:, None]), BF16)
    c_idx = np.arange(MXU_TILE)
    seg = jnp.asarray((c_idx[:, None] // HEAD_DIM) == (c_idx[None, :] // HEAD_DIM), BF16)
    full = lambda shape: pl.BlockSpec(shape, lambda b, i: (0,) * len(shape))
    per_tile = tt // SUBLANES
    return pl.pallas_call(
        _rwkv_kernel,
        grid=(bsz // nb, seq // tt),
        in_specs=[pl.BlockSpec((nb, tt, RWKV_COLS), lambda b, i: (b, i, 0)),
                  pl.BlockSpec((nb, SUBLANES, RWKV_COLS),
                               lambda b, i: (b, jnp.maximum(i * per_tile - 1, 0), 0)),
                  full((1, RWKV_COLS)), full((SUBLANES, w)),
                  full(w2.shape), full(a2.shape), full(g2.shape),
                  full((tt, tt)), full((tt, tt)), full((MXU_TILE, MXU_TILE))],
        out_specs=pl.BlockSpec((nb, tt, w), lambda b, i: (b, i, 0)),
        out_shape=jax.ShapeDtypeStruct((bsz, seq, w), BF16),
        scratch_shapes=[pltpu.VMEM((nb * RWKV_HEADS, HEAD_DIM, HEAD_DIM), F32),
                        pltpu.VMEM((nb, tt, w), F32)],
        compiler_params=_params(("parallel", "arbitrary")),
        name="rwkv7_mix",
    )(p_r, p_r, mu.reshape(1, RWKV_COLS), vec, w2.astype(BF16), a2.astype(BF16),
      g2.astype(BF16), ltri, ustr, seg)


def _dsa_kernel(qt_ref, k_ref, vt_ref, qit_ref, ki_ref, wit_ref, low_ref, o_ref,
                key_ref, bias_ref, acc_ref, m_ref, l_ref, *, topk):
    qb = pl.program_id(1)
    nq = Q_BLOCK
    kt_sz = bias_ref.shape[0]
    hd = HEAD_DIM
    start = qb * nq
    nkt = (start + nq + kt_sz - 1) // kt_sz
    col = lax.broadcasted_iota(I32, (1, nq), 1)
    lim = start + (col // CHUNK + 1) * CHUNK
    wit = wit_ref[0]
    qit = qit_ref[0]
    qi_cat = jnp.concatenate([qit[h * IDX_DIM:(h + 1) * IDX_DIM] for h in range(IDX_HEADS)],
                             axis=1)

    def score_tile(j, carry):
        off = pl.multiple_of(j * kt_sz, kt_sz)
        d = jnp.dot(ki_ref[0, pl.ds(off, kt_sz), :], qi_cat, preferred_element_type=F32)
        s = jnp.zeros((kt_sz, nq), F32)
        for h in range(IDX_HEADS):
            s = s + wit[h:h + 1] * jnp.maximum(d[:, h * nq:(h + 1) * nq], 0.0)
        s = s + 0.0
        bits = pltpu.bitcast(s, I32)
        key = bits ^ ((bits >> 31) & 0x7FFFFFFF)
        sidx = off + lax.broadcasted_iota(I32, (kt_sz, nq), 0)
        key_ref[pl.ds(off, kt_sz), :] = jnp.where(sidx < lim, key, INT_MIN)
        return carry

    lax.fori_loop(0, nkt, score_tile, 0)

    def count(pred):
        def body(j, c):
            off = pl.multiple_of(j * kt_sz, kt_sz)
            hit = jnp.where(pred(key_ref[pl.ds(off, kt_sz), :]), 1, 0)
            return c + hit.reshape(kt_sz // RED_ROWS, RED_ROWS, nq).sum(axis=0)
        c = lax.fori_loop(0, nkt, body, jnp.zeros((RED_ROWS, nq), I32))
        return jnp.sum(c, axis=0, keepdims=True)

    def slot_max(j, c):
        off = pl.multiple_of(j * kt_sz, kt_sz)
        return jnp.maximum(c, key_ref[pl.ds(off, kt_sz), :].reshape(kt_sz // SLOTS, SLOTS, nq)
                           .max(axis=0))

    smax = lax.fori_loop(0, nkt, slot_max, jnp.full((SLOTS, nq), INT_MIN, I32))
    lo = jnp.min(smax, axis=0, keepdims=True)
    hi = jnp.max(smax, axis=0, keepdims=True) + 1
    zero = jnp.zeros((1, nq), I32)
    nonneg = count(lambda kv: kv >= zero) >= topk
    lo = jnp.where(nonneg, jnp.maximum(lo, 0), lo)
    hi = jnp.where(nonneg, hi, jnp.minimum(hi, 0))
    half_width = (hi >> 1) - (lo >> 1)
    steps = jnp.max(32 - lax.clz(half_width)) + 1

    def bisect(_, bounds):
        lo, hi = bounds
        mid = (lo >> 1) + (hi >> 1) + (lo & hi & 1)
        take = count(lambda kv: kv >= mid) >= topk
        return jnp.where(take, mid, lo), jnp.where(take, hi, mid)

    th, _ = lax.fori_loop(0, steps, bisect, (lo, hi))
    need = jnp.where(th == INT_MIN, 0, topk - count(lambda kv: kv > th)).astype(F32)

    acc_ref[...] = jnp.zeros_like(acc_ref)
    m_ref[...] = jnp.full_like(m_ref, NEG_INF)
    l_ref[...] = jnp.zeros_like(l_ref)
    low = low_ref[...]
    npair = DSA_HEADS // 2
    top_rows = lax.broadcasted_iota(I32, (2 * hd, nq), 0) < hd
    qt = qt_ref[0].astype(F32)
    q_bd = []
    for p in range(npair):
        qp = qt[p * 2 * hd:(p + 1) * 2 * hd]
        q_bd.append(jnp.concatenate([jnp.where(top_rows, qp, 0.0), jnp.where(top_rows, 0.0, qp)],
                                    axis=1).astype(BF16))

    pad_rows = 2 * SUBLANES
    ones_rows = (lax.broadcasted_iota(I32, (pad_rows, kt_sz), 0) == 0).astype(BF16)

    def attn_tile(j, eq_before):
        off = pl.multiple_of(j * kt_sz, kt_sz)
        eq_run = eq_before
        for t in range(kt_sz // TRI_TILE):
            kv = key_ref[pl.ds(off + t * TRI_TILE, TRI_TILE), :]
            eq = kv == th
            pre = eq_run + jnp.dot(low, jnp.where(eq, 1.0, 0.0).astype(BF16),
                                   preferred_element_type=F32)
            sel = (kv > th) | (eq & (pre <= need))
            bias_ref[t * TRI_TILE:(t + 1) * TRI_TILE, :] = jnp.where(sel, 0.0, NEG_INF)
            eq_run = pre[TRI_TILE - 1:TRI_TILE]
        bias = bias_ref[...]

        def logits(p):
            half = kt_sz // 2
            return jnp.concatenate(
                [jnp.dot(k_ref[0, pl.ds(off + r * half, half), p * 2 * hd:(p + 1) * 2 * hd],
                         q_bd[p], preferred_element_type=F32) for r in range(2)], axis=0)

        st_next = logits(0)
        for p in range(npair):
            cols = slice(p * 2 * hd, (p + 1) * 2 * hd)
            st = st_next
            if p + 1 < npair:
                st_next = logits(p + 1)
            prs, scales = [], []
            for i in range(2):
                h = 2 * p + i
                s = st[:, i * nq:(i + 1) * nq] + bias
                m_old = m_ref[h:h + 1]
                m_new = jnp.maximum(m_old, _fold_rows(s, jnp.max))
                prs.append(jnp.exp2(s - m_new).astype(BF16))
                scales.append(jnp.exp2(m_old - m_new))
                m_ref[h:h + 1] = m_new
            lhs = jnp.concatenate([vt_ref[0, cols, pl.ds(off, kt_sz)], ones_rows], axis=0)
            pv = jnp.dot(lhs, jnp.concatenate(prs, axis=1),
                         preferred_element_type=F32)
            for i in range(2):
                h = 2 * p + i
                l_ref[h:h + 1] = (scales[i] * l_ref[h:h + 1]
                                  + pv[2 * hd:2 * hd + 1, i * nq:(i + 1) * nq])
            new = jnp.where(top_rows, pv[:2 * hd, :nq], pv[:2 * hd, nq:])
            acc_ref[p] = jnp.where(top_rows, scales[0], scales[1]) * acc_ref[p] + new
        return eq_run

    lax.fori_loop(0, nkt, attn_tile, jnp.zeros((1, nq), F32))
    for p in range(npair):
        inv = jnp.where(top_rows, 1.0 / l_ref[2 * p:2 * p + 1], 1.0 / l_ref[2 * p + 1:2 * p + 2])
        o_ref[0, :, p * 2 * hd:(p + 1) * 2 * hd] = (acc_ref[p] * inv).T.astype(BF16)


def _dsa(q_t, k, v_t, qi_t, ki, wi_t):
    bsz, seq, w = k.shape
    nq = Q_BLOCK
    topk = min(INDEX_TOPK, seq // 4)
    kt_sz = min(KEY_TILE, seq)
    assert SLOTS >= topk and kt_sz % SLOTS == 0 and kt_sz % TRI_TILE == 0
    t_idx = np.arange(TRI_TILE)
    low = jnp.asarray(t_idx[None, :] <= t_idx[:, None], BF16)
    cols = lambda n: pl.BlockSpec((1, n, nq), lambda b, i: (b, 0, i))
    return pl.pallas_call(
        functools.partial(_dsa_kernel, topk=topk),
        grid=(bsz, seq // nq),
        in_specs=[cols(w),
                  pl.BlockSpec((1, seq, w), lambda b, i: (b, 0, 0)),
                  pl.BlockSpec((1, w, seq), lambda b, i: (b, 0, 0)),
                  cols(IDX_HEADS * IDX_DIM),
                  pl.BlockSpec((1, seq, IDX_DIM), lambda b, i: (b, 0, 0)),
                  cols(SUBLANES),
                  pl.BlockSpec((TRI_TILE, TRI_TILE), lambda b, i: (0, 0))],
        out_specs=pl.BlockSpec((1, nq, w), lambda b, i: (b, i, 0)),
        out_shape=jax.ShapeDtypeStruct((bsz, seq, w), BF16),
        scratch_shapes=[pltpu.VMEM((seq, nq), I32),
                        pltpu.VMEM((kt_sz, nq), F32),
                        pltpu.VMEM((DSA_HEADS // 2, 2 * HEAD_DIM, nq), F32),
                        pltpu.VMEM((DSA_HEADS, nq), F32),
                        pltpu.VMEM((DSA_HEADS, nq), F32)],
        compiler_params=_params(("parallel", "arbitrary")),
        name="dsa_mix",
    )(q_t, k, v_t, qi_t, ki, wi_t, low)


POOL_HALO = 16


def _odd_layer_kernel(x_ref, xp_ref, mod_ref, win_ref, pw_ref, ps_ref, lng_ref, lnb_ref, ws_ref,
                      bs_ref, wout_ref, g_ref, b_ref, o_ref, y_ref):
    i = pl.program_id(1)
    m = mod_ref[0]
    sc = 1.0 + m[1:2]
    sh = m[0:1]
    x = x_ref[0]
    tm = x.shape[0]
    gd = POOL_GROUP_DIM
    p = jnp.dot((x * sc + sh).astype(BF16), win_ref[...], preferred_element_type=F32)
    prev = jnp.dot((xp_ref[0] * sc + sh).astype(BF16), win_ref[:, :POOL_WIDTH],
                   preferred_element_type=F32) * (i > 0).astype(F32)
    t_glob = (i * tm + lax.broadcasted_iota(I32, (tm, 1), 0)).astype(F32)
    scale = ps_ref[...]
    for gi, win in enumerate(POOL_WINDOWS):
        cols = slice(gi * gd, (gi + 1) * gd)
        xg = p[:, cols]
        s = jnp.concatenate([prev[:, cols], xg], axis=0)
        span = 1
        while span < win:
            s = s[span:] + s[:-span]
            span *= 2
        first = POOL_HALO + 1 - win
        pooled = s[first:first + tm] / jnp.minimum(t_glob + 1.0, float(win)) - xg
        y_ref[:, cols] = (_dot(pooled, pw_ref[gi]) * scale[:, cols]).astype(BF16)

    u = _gelu(p[:, POOL_WIDTH:POOL_WIDTH + SG_WIDTH])
    v = _ln(_gelu(p[:, POOL_WIDTH + SG_WIDTH:]), lng_ref[...], lnb_ref[...], LN_EPS)
    ti = lax.broadcasted_iota(I32, (SG_CHUNK, SG_CHUNK), 0)
    si = lax.broadcasted_iota(I32, (SG_CHUNK, SG_CHUNK), 1)
    bs = bs_ref[...]
    for gi in range(SG_GROUPS):
        cols = slice(gi * SG_GROUP_DIM, (gi + 1) * SG_GROUP_DIM)
        ws = jnp.where(si <= ti, ws_ref[gi], 0.0)
        for n in range(tm // SG_CHUNK):
            rows = slice(n * SG_CHUNK, (n + 1) * SG_CHUNK)
            z = _dot(ws, v[rows, cols]) + bs[:, gi:gi + 1]
            y_ref[rows, POOL_WIDTH + gi * SG_GROUP_DIM:POOL_WIDTH + (gi + 1) * SG_GROUP_DIM] = (
                u[rows, cols] * z).astype(BF16)

    y = jnp.dot(y_ref[...], wout_ref[...], preferred_element_type=F32)
    o_ref[0] = _ln(ALPHA * x + m[2:3] * y, g_ref[...], b_ref[...], LN_EPS)


def _odd_layer(x, mod, w_in, pool_w, pool_scale, sg_ln_g, sg_ln_b, sg_w, sg_b, w_out, ln_g, ln_b):
    bsz, seq, d = x.shape
    tm = 256
    per_tile = tm // POOL_HALO
    full = lambda shape: pl.BlockSpec(shape, lambda b, i: (0,) * len(shape))
    once = lambda shape: pl.BlockSpec(shape, lambda b, i: (0,) * len(shape),
                                      pipeline_mode=pl.Buffered(1))
    return pl.pallas_call(
        _odd_layer_kernel,
        grid=(bsz, seq // tm),
        in_specs=[pl.BlockSpec((1, tm, d), lambda b, i: (b, i, 0)),
                  pl.BlockSpec((1, POOL_HALO, d),
                               lambda b, i: (b, jnp.maximum(i * per_tile - 1, 0), 0)),
                  pl.BlockSpec((1, 6, d), lambda b, i: (b, 0, 0)),
                  once(w_in.shape), full(pool_w.shape), full((1, POOL_WIDTH)),
                  full((1, SG_WIDTH)), full((1, SG_WIDTH)), full(sg_w.shape),
                  full((SG_CHUNK, SG_GROUPS)), once(w_out.shape), full((1, d)), full((1, d))],
        out_specs=pl.BlockSpec((1, tm, d), lambda b, i: (b, i, 0)),
        out_shape=jax.ShapeDtypeStruct((bsz, seq, d), F32),
        scratch_shapes=[pltpu.VMEM((tm, d), BF16)],
        compiler_params=_params(("parallel", "parallel")),
        name="odd_layer_mix",
    )(x, x, mod, w_in.astype(BF16), pool_w.astype(BF16), pool_scale.reshape(1, -1),
      sg_ln_g.reshape(1, -1), sg_ln_b.reshape(1, -1), sg_w, sg_b.T, w_out.astype(BF16),
      ln_g.reshape(1, d), ln_b.reshape(1, d))


def _proj_ln_kernel(*refs, n_in, gate_row):
    a_refs = refs[:n_in]
    w_refs = refs[n_in:2 * n_in]
    x_ref, mod_ref, g_ref, b_ref, o_ref = refs[2 * n_in:]
    y = None
    for a_ref, w_ref in zip(a_refs, w_refs):
        part = jnp.dot(a_ref[0], w_ref[...], preferred_element_type=F32)
        y = part if y is None else y + part
    gate = mod_ref[0][gate_row:gate_row + 1]
    o_ref[0] = _ln(ALPHA * x_ref[0] + gate * y, g_ref[...], b_ref[...], LN_EPS)


def _proj_ln(acts, weights, x, mod, gate_row, ln_g, ln_b, name):
    bsz, seq, d = x.shape
    tm = 512
    n_in = len(acts)
    in_specs = [pl.BlockSpec((1, tm, a.shape[2]), lambda b, i: (b, i, 0)) for a in acts]
    in_specs += [pl.BlockSpec(w.shape, lambda b, i: (0, 0)) for w in weights]
    in_specs += [pl.BlockSpec((1, tm, d), lambda b, i: (b, i, 0)),
                 pl.BlockSpec((1, 6, d), lambda b, i: (b, 0, 0)),
                 pl.BlockSpec((1, d), lambda b, i: (0, 0)),
                 pl.BlockSpec((1, d), lambda b, i: (0, 0))]
    return pl.pallas_call(
        functools.partial(_proj_ln_kernel, n_in=n_in, gate_row=gate_row),
        grid=(bsz, seq // tm),
        in_specs=in_specs,
        out_specs=pl.BlockSpec((1, tm, d), lambda b, i: (b, i, 0)),
        out_shape=jax.ShapeDtypeStruct((bsz, seq, d), F32),
        compiler_params=_params(("parallel", "parallel")),
        name=name,
    )(*acts, *[w.astype(BF16) for w in weights], x, mod, ln_g.reshape(1, d), ln_b.reshape(1, d))


def _ffn_kernel(x_ref, xp_ref, mod_ref, wg_ref, wv_ref, cwg_ref, cwv_ref, cbg_ref, cbv_ref,
                wd_ref, g_ref, b_ref, o_ref):
    i = pl.program_id(1)
    m = mod_ref[0]
    sc = 1.0 + m[4:5]
    sh = m[3:4]
    x = x_ref[0]
    live = (i > 0).astype(F32)
    h = jnp.concatenate([(xp_ref[0] * sc + sh) * live, x * sc + sh], axis=0).astype(BF16)
    halo = SUBLANES

    def conv(w_ref, cw_ref, cb_ref):
        u = jnp.dot(h, w_ref[...], preferred_element_type=F32)
        cw = cw_ref[...]
        u1 = pltpu.roll(u, 1, 0)
        u2 = pltpu.roll(u, 2, 0)
        return (cb_ref[...] + u2[halo:] * cw[0:1] + u1[halo:] * cw[1:2] + u[halo:] * cw[2:3])

    gate = conv(wg_ref, cwg_ref, cbg_ref)
    val = conv(wv_ref, cwv_ref, cbv_ref)
    act = (gate * jax.nn.sigmoid(gate) * val).astype(BF16)
    y = jnp.dot(act, wd_ref[...], preferred_element_type=F32)
    o_ref[0] = _ln(ALPHA * x + m[5:6] * y, g_ref[...], b_ref[...], LN_EPS)


def _conv_ffn(x, mod, w_up, conv_w, conv_b, w_down, ln_g, ln_b):
    bsz, seq, d = x.shape
    tm = 512
    per_tile = tm // SUBLANES
    wb = w_up.astype(BF16)
    cb = conv_b.reshape(1, -1)
    once = dict(pipeline_mode=pl.Buffered(1))
    return pl.pallas_call(
        _ffn_kernel,
        grid=(bsz, seq // tm),
        in_specs=[pl.BlockSpec((1, tm, d), lambda b, i: (b, i, 0)),
                  pl.BlockSpec((1, SUBLANES, d),
                               lambda b, i: (b, jnp.maximum(i * per_tile - 1, 0), 0)),
                  pl.BlockSpec((1, 6, d), lambda b, i: (b, 0, 0)),
                  pl.BlockSpec((d, D_FF), lambda b, i: (0, 0), **once),
                  pl.BlockSpec((d, D_FF), lambda b, i: (0, 1), **once),
                  pl.BlockSpec((3, D_FF), lambda b, i: (0, 0), **once),
                  pl.BlockSpec((3, D_FF), lambda b, i: (0, 1), **once),
                  pl.BlockSpec((1, D_FF), lambda b, i: (0, 0), **once),
                  pl.BlockSpec((1, D_FF), lambda b, i: (0, 1), **once),
                  pl.BlockSpec((D_FF, d), lambda b, i: (0, 0), **once),
                  pl.BlockSpec((1, d), lambda b, i: (0, 0)),
                  pl.BlockSpec((1, d), lambda b, i: (0, 0))],
        out_specs=pl.BlockSpec((1, tm, d), lambda b, i: (b, i, 0)),
        out_shape=jax.ShapeDtypeStruct((bsz, seq, d), F32),
        compiler_params=_params(("parallel", "parallel")),
        name="conv_ffn_ln",
    )(x, x, mod, wb, wb, conv_w, conv_w, cb, cb, w_down.astype(BF16),
      ln_g.reshape(1, d), ln_b.reshape(1, d))


def _rope_tables(seq):
    inv = ROPE_THETA ** (-jnp.arange(0, ROPE_DIM, 2, dtype=F32) / ROPE_DIM)
    ang = jnp.arange(seq, dtype=F32)[:, None] * inv[None, :]
    cos, sin = jnp.cos(ang), jnp.sin(ang)
    half = ROPE_DIM // 2
    rest = HEAD_DIM - ROPE_DIM
    one = jnp.ones((seq, rest), F32)
    zero = jnp.zeros((seq, rest), F32)
    zh = jnp.zeros((seq, half), F32)
    head = lambda *parts: jnp.concatenate(parts * (LANES // HEAD_DIM), axis=1)
    return head(cos, cos, one), head(-sin, zh, zero), head(zh, sin, zero)


def kernel(x, c, ada_w, ada_b, ln_g, ln_b, ffn_w_up, ffn_conv_w, ffn_conv_b, ffn_w_down, ev_w_in, ev_w_out, rw_mu, rw_w0, rw_w2, rw_a0, rw_a2, rw_g2, rw_k_k, rw_k_a, rw_r_k, rw_gn_g, rw_gn_b, ik_ln_g, ik_ln_b, od_w_in, od_w_out, pool_w, pool_scale, sg_ln_g, sg_ln_b, sg_w, sg_b):
    seq = x.shape[1]
    tables = _rope_tables(seq)
    mods = _modulation(c, ada_w, ada_b)
    ev_w = _split_even_weights(ev_w_in)
    for layer in range(DEPTH):
        mod = mods[layer]
        if layer % 2 == 0:
            e = layer // 2
            p_r, q_t, k, v_t, qi_t, ki, wi_t = _even_in_proj(x, mod, [w[e] for w in ev_w],
                                                             ik_ln_g[e], ik_ln_b[e], tables)
            ya = _rwkv(p_r, rw_mu[e], rw_w0[e], rw_w2[e], rw_a0[e], rw_a2[e], rw_g2[e],
                       rw_k_k[e], rw_k_a[e], rw_r_k[e], rw_gn_g[e], rw_gn_b[e])
            yb = _dsa(q_t, k, v_t, qi_t, ki, wi_t)
            w_out = ev_w_out[e]
            x = _proj_ln([ya, yb], [w_out[:RWKV_WIDTH], w_out[RWKV_WIDTH:]], x, mod, 2,
                         ln_g[layer, 0], ln_b[layer, 0], "even_out_proj_ln")
        else:
            o = layer // 2
            x = _odd_layer(x, mod, od_w_in[o], pool_w[o], pool_scale[o], sg_ln_g[o], sg_ln_b[o],
                           sg_w[o], sg_b[o], od_w_out[o], ln_g[layer, 0], ln_b[layer, 0])
        x = _conv_ffn(x, mod, ffn_w_up[layer], ffn_conv_w[layer], ffn_conv_b[layer],
                      ffn_w_down[layer], ln_g[layer, 1], ln_b[layer, 1])
    return x
```

```python
import functools

import numpy as np
import jax
import jax.numpy as jnp
from jax import lax
from jax.experimental import pallas as pl
from jax.experimental.pallas import tpu as pltpu

F32 = jnp.float32
BF16 = jnp.bfloat16
I32 = jnp.int32

D_MODEL = 1024
DEPTH = 4
CHUNK = 64
HEAD_DIM = 64
RWKV_WIDTH = D_MODEL // 2
RWKV_HEADS = RWKV_WIDTH // HEAD_DIM
RWKV_LORA_W = 64
RWKV_LORA_A = 64
RWKV_LORA_G = 128
RWKV_COLS = 3 * RWKV_WIDTH + RWKV_LORA_W + RWKV_LORA_A + RWKV_LORA_G
DSA_WIDTH = D_MODEL - RWKV_WIDTH
DSA_HEADS = DSA_WIDTH // HEAD_DIM
IDX_HEADS = 4
IDX_DIM = 64
INDEX_TOPK = 256
Q_BLOCK = 128
ROPE_THETA = 500000.0
ROPE_DIM = HEAD_DIM // 4
POOL_WINDOWS = (2, 4, 8, 16)
POOL_WIDTH = D_MODEL // 2
POOL_GROUP_DIM = POOL_WIDTH // len(POOL_WINDOWS)
SG_WIDTH = D_MODEL - POOL_WIDTH
SG_GROUPS = 4
SG_GROUP_DIM = SG_WIDTH // SG_GROUPS
SG_CHUNK = 128
D_FF = 2816
ALPHA = (2.0 * DEPTH) ** 0.25
LN_EPS = 1e-5
GN_EPS = 64e-5
NEG_INF = -1e30
INT_MIN = -(2 ** 31)
LOG2E = 1.4426950408889634

LANES = 128
SUBLANES = 8
MXU_TILE = 256
VMEM_LIMIT = 56 * 1024 * 1024

RW_TILE = 128
RW_CHUNK = 16
RW_ROWS = 4
KEY_TILE = 1024
TRI_TILE = 256
SLOTS = 256
PAD_IDX = 384

def _dot(a, b):
    return jnp.dot(a.astype(BF16), b.astype(BF16), preferred_element_type=F32)


def _dot_nt(a, b):
    return lax.dot_general(a.astype(BF16), b.astype(BF16), (((1,), (1,)), ((), ())),
                           preferred_element_type=F32)


def _dot_split(x, m01, terms):
    acc = None
    rem = x
    for _ in range(terms):
        piece = rem.astype(BF16)
        rem = rem - piece.astype(F32)
        part = jnp.dot(piece, m01, preferred_element_type=F32)
        acc = part if acc is None else acc + part
    return acc


def _dot_split_left(m01, x, terms):
    acc = None
    rem = x
    for _ in range(terms):
        piece = rem.astype(BF16)
        rem = rem - piece.astype(F32)
        part = jnp.dot(m01, piece, preferred_element_type=F32)
        acc = part if acc is None else acc + part
    return acc


def _ln(x, g, b, eps):
    mu = jnp.mean(x, axis=-1, keepdims=True)
    xc = x - mu
    var = jnp.mean(xc * xc, axis=-1, keepdims=True)
    return xc * lax.rsqrt(var + eps) * g + b


def _gelu(x):
    return 0.5 * x * (1.0 + lax.erf(x * 0.7071067811865476))


RED_ROWS = 64


def _fold_rows(x, op):
    rows, n = x.shape
    part = op(x.reshape(rows // RED_ROWS, RED_ROWS, n), axis=0)
    return op(part, axis=0, keepdims=True)


def _params(sem):
    return pltpu.CompilerParams(dimension_semantics=sem, vmem_limit_bytes=VMEM_LIMIT)


def _mod_kernel(c_ref, w_ref, b_ref, o_ref):
    c = c_ref[...]
    ca = c * jax.nn.sigmoid(c)
    o_ref[0] = jnp.dot(ca, w_ref[0], preferred_element_type=F32,
                       precision=lax.Precision.HIGHEST) + b_ref[0]


def _modulation(c, ada_w, ada_b):
    bsz, d = c.shape
    depth = ada_w.shape[0]
    n = ada_w.shape[2]
    tn = 1536
    rows = -(-bsz // SUBLANES) * SUBLANES
    c8 = jnp.pad(c, ((0, rows - bsz), (0, 0)))
    out = pl.pallas_call(
        _mod_kernel,
        grid=(depth, n // tn),
        in_specs=[pl.BlockSpec((rows, d), lambda l, j: (0, 0)),
                  pl.BlockSpec((1, d, tn), lambda l, j: (l, 0, j)),
                  pl.BlockSpec((1, 1, tn), lambda l, j: (l, 0, j))],
        out_specs=pl.BlockSpec((1, rows, tn), lambda l, j: (l, 0, j)),
        out_shape=jax.ShapeDtypeStruct((depth, rows, n), F32),
        compiler_params=_params(("arbitrary", "arbitrary")),
        name="adaln_mod",
    )(c8, ada_w, ada_b.reshape(depth, 1, n))
    return out[:, :bsz].reshape(depth, bsz, 6, d)


def _rope(x, cos_t, sin_a, sin_b):
    n = x.shape[1] // LANES
    rep = (lambda t: jnp.concatenate([t] * n, axis=1)) if n > 1 else (lambda t: t)
    width = x.shape[1]
    half = ROPE_DIM // 2
    return (x * rep(cos_t) + pltpu.roll(x, width - half, 1) * rep(sin_a)
            + pltpu.roll(x, half, 1) * rep(sin_b))


def _even_in_kernel(x_ref, mod_ref, wr_ref, wqkv_ref, widx_ref, cos_ref, sa_ref, sb_ref,
                    ikg_ref, ikb_ref,
                    pr_ref, qt_ref, k_ref, vt_ref, qit_ref, ki_ref, wit_ref):
    m = mod_ref[0]
    h = (x_ref[0] * (1.0 + m[1:2]) + m[0:1]).astype(BF16)
    pr_ref[0] = jnp.dot(h, wr_ref[...], preferred_element_type=F32)
    qkv = jnp.dot(h, wqkv_ref[...], preferred_element_type=F32)
    cos_t = cos_ref[...]
    sin_a = sa_ref[...]
    sin_b = sb_ref[...]
    w = DSA_WIDTH
    q = _rope(qkv[:, :w], cos_t, sin_a, sin_b) * (HEAD_DIM ** -0.5 * LOG2E)
    qt_ref[0] = q.T.astype(BF16)
    k_ref[0] = _rope(qkv[:, w:2 * w], cos_t, sin_a, sin_b).astype(BF16)
    vt_ref[0] = qkv[:, 2 * w:].T.astype(BF16)
    idx = jnp.dot(h, widx_ref[...], preferred_element_type=F32)
    nq = IDX_HEADS * IDX_DIM
    qit_ref[0] = _rope(idx[:, :nq], cos_t, sin_a, sin_b).T.astype(BF16)
    blk = idx[:, nq:nq + LANES]
    lane = lax.broadcasted_iota(I32, blk.shape, 1)
    is_k = lane < IDX_DIM
    mu = jnp.sum(jnp.where(is_k, blk, 0.0), axis=1, keepdims=True) * (1.0 / IDX_DIM)
    xc = jnp.where(is_k, blk - mu, 0.0)
    var = jnp.sum(xc * xc, axis=1, keepdims=True) * (1.0 / IDX_DIM)
    kin = xc * lax.rsqrt(var + LN_EPS) * ikg_ref[...] + ikb_ref[...]
    ki_ref[0] = _rope(kin, cos_t, sin_a, sin_b)[:, :IDX_DIM].astype(BF16)
    wit = (blk * (IDX_HEADS ** -0.5 * IDX_DIM ** -0.5)).T
    wit_ref[0] = wit[IDX_DIM:IDX_DIM + SUBLANES]


def _split_w_kernel(w_ref, wr_ref, wqkv_ref, widx_ref):
    w = w_ref[0]
    c1 = RWKV_COLS
    c2 = RWKV_COLS + 3 * DSA_WIDTH
    wr_ref[0] = w[:, :c1].astype(BF16)
    wqkv_ref[0] = w[:, c1:c2].astype(BF16)
    tail = w[:, c2:]
    zeros = jnp.zeros((w.shape[0], PAD_IDX - tail.shape[1]), F32)
    widx_ref[0] = jnp.concatenate([tail, zeros], axis=1).astype(BF16)


def _split_even_weights(ev_w_in):
    n_even, d, n = ev_w_in.shape
    tr = 256
    widths = (RWKV_COLS, 3 * DSA_WIDTH, PAD_IDX)
    return pl.pallas_call(
        _split_w_kernel,
        grid=(n_even, d // tr),
        in_specs=[pl.BlockSpec((1, tr, n), lambda e, i: (e, i, 0))],
        out_specs=[pl.BlockSpec((1, tr, wd), lambda e, i: (e, i, 0)) for wd in widths],
        out_shape=[jax.ShapeDtypeStruct((n_even, d, wd), BF16) for wd in widths],
        compiler_params=_params(("parallel", "parallel")),
        name="split_even_weights",
    )(ev_w_in)


def _even_in_proj(x, mod, weights, ik_g, ik_b, tables):
    bsz, seq, d = x.shape
    tm = 256
    w_r, w_qkv, w_idx = weights
    pad = LANES - IDX_DIM
    ikg = jnp.pad(ik_g, (0, pad)).reshape(1, LANES)
    ikb = jnp.pad(ik_b, (0, pad)).reshape(1, LANES)
    cos_t, sin_a, sin_b = tables
    full = lambda shape: pl.BlockSpec(shape, lambda b, i: (0,) * len(shape))
    tab = pl.BlockSpec((tm, LANES), lambda b, i: (i, 0))
    nq = IDX_HEADS * IDX_DIM
    rows = lambda n: pl.BlockSpec((1, tm, n), lambda b, i: (b, i, 0))
    cols = lambda n: pl.BlockSpec((1, n, tm), lambda b, i: (b, 0, i))
    return pl.pallas_call(
        _even_in_kernel,
        grid=(bsz, seq // tm),
        in_specs=[pl.BlockSpec((1, tm, d), lambda b, i: (b, i, 0)),
                  pl.BlockSpec((1, 6, d), lambda b, i: (b, 0, 0)),
                  full(w_r.shape), full(w_qkv.shape), full(w_idx.shape),
                  tab, tab, tab, full((1, LANES)), full((1, LANES))],
        out_specs=[rows(RWKV_COLS), cols(DSA_WIDTH), rows(DSA_WIDTH), cols(DSA_WIDTH),
                   cols(nq), rows(IDX_DIM), cols(SUBLANES)],
        out_shape=[jax.ShapeDtypeStruct((bsz, seq, RWKV_COLS), F32),
                   jax.ShapeDtypeStruct((bsz, DSA_WIDTH, seq), BF16),
                   jax.ShapeDtypeStruct((bsz, seq, DSA_WIDTH), BF16),
                   jax.ShapeDtypeStruct((bsz, DSA_WIDTH, seq), BF16),
                   jax.ShapeDtypeStruct((bsz, nq, seq), BF16),
                   jax.ShapeDtypeStruct((bsz, seq, IDX_DIM), BF16),
                   jax.ShapeDtypeStruct((bsz, SUBLANES, seq), F32)],
        compiler_params=_params(("parallel", "parallel")),
        name="even_in_proj",
    )(x, mod, w_r, w_qkv, w_idx, cos_t, sin_a, sin_b, ikg, ikb)


def _rwkv_kernel(p_ref, pp_ref, mu_ref, vec_ref, w2_ref, a2_ref, g2_ref, ltri_ref, ustr_ref,
                 seg_ref, o_ref, s_ref, obuf_ref):
    i = pl.program_id(1)

    @pl.when(i == 0)
    def _():
        s_ref[...] = jnp.zeros_like(s_ref)

    tt = RW_TILE
    w = RWKV_WIDTH
    hd = HEAD_DIM
    nb = p_ref.shape[0]
    nchunk = tt // RW_CHUNK
    seg = seg_ref[...]
    half_w = seg.shape[0]

    def segsum(t):
        return jnp.concatenate([_dot_split(t[:, c:c + half_w], seg, 2)
                                for c in range(0, w, half_w)], axis=1)

    vec = vec_ref[...]
    w0, a0, k_k, k_a, r_k, gn_g, gn_b = (vec[j:j + 1] for j in range(7))
    rowi = lax.broadcasted_iota(I32, (tt, 1), 0)
    live = (i > 0).astype(F32)

    def prepare(b):
        p = p_ref[b]
        prow = pp_ref[b][SUBLANES - 1:SUBLANES] * live
        xprev = jnp.where(rowi == 0, prow, pltpu.roll(p, 1, 0))
        ps = p + (xprev - p) * mu_ref[...]
        r = ps[:, :w]
        k = ps[:, w:2 * w]
        v = ps[:, 2 * w:3 * w]
        o1 = 3 * w
        wd = ps[:, o1:o1 + RWKV_LORA_W]
        ad = ps[:, o1 + RWKV_LORA_W:o1 + RWKV_LORA_W + RWKV_LORA_A]
        gd = ps[:, o1 + RWKV_LORA_W + RWKV_LORA_A:]
        y = -(w0 + _dot(jnp.tanh(wd), w2_ref[...]))
        softplus = jnp.maximum(y, 0.0) + jnp.log1p(jnp.exp(-jnp.abs(y)))
        logw = -jnp.exp(-softplus - 0.5)
        a = jax.nn.sigmoid(a0 + _dot(ad, a2_ref[...]))
        g = _dot(jax.nn.sigmoid(gd), g2_ref[...])
        kk = k * k_k
        kk = kk / jnp.maximum(jnp.sqrt(segsum(kk * kk)), 1e-12)
        k2 = k * (1.0 + (a - 1.0) * k_a)
        bonus = segsum(r * k2 * r_k) * v
        cum = _dot_split_left(ltri_ref[...], logw, 3)
        rem = _dot_split_left(ustr_ref[...], logw, 3)
        pt = jnp.exp(cum)
        ipt = jnp.exp(-cum)
        erem = jnp.exp(rem)
        kka = kk * a
        return dict(at=-kk * jnp.exp(cum - logw), rt=r * pt, bt=kka * ipt, kt=k2 * ipt,
                    bp=kka * erem, kp=k2 * erem, v=v, pt=pt, bonus=bonus, g=g)

    rows_in = [prepare(b) for b in range(nb)]

    ti = lax.broadcasted_iota(I32, (tt, tt), 0)
    si = lax.broadcasted_iota(I32, (tt, tt), 1)
    same = (ti // RW_CHUNK) == (si // RW_CHUNK)
    strict = same & (si < ti)
    incl = same & (si <= ti)
    tb = lax.broadcasted_iota(I32, (tt, nchunk * hd), 0)
    cb = lax.broadcasted_iota(I32, (tt, nchunk * hd), 1)
    blkmask = (tb // RW_CHUNK) == (cb // hd)
    tile_chunks = lambda t: jnp.where(blkmask, jnp.concatenate([t] * nchunk, axis=1), 0.0)

    units = [(b, slice(h * hd, (h + 1) * hd)) for b in range(nb) for h in range(RWKV_HEADS)]
    idx = range(len(units))
    pick = lambda name: [rows_in[b][name][:, sl] for b, sl in units]
    at_h, rt_h, vh, bt_h, kt_h = pick("at"), pick("rt"), pick("v"), pick("bt"), pick("kt")
    x = [_dot_nt(jnp.concatenate([at_h[u], rt_h[u]], axis=0),
                 jnp.concatenate([bt_h[u], kt_h[u]], axis=0)) for u in idx]
    a_ak = [jnp.where(strict, x[u][:tt, tt:], 0.0) for u in idx]
    a_rb = [jnp.where(incl, x[u][tt:, :tt], 0.0) for u in idx]
    a_rk = [jnp.where(incl, x[u][tt:, tt:], 0.0) for u in idx]
    def bd(m0, m1):
        z0 = jnp.zeros((m0.shape[0], m1.shape[1]), m0.dtype)
        z1 = jnp.zeros((m1.shape[0], m0.shape[1]), m1.dtype)
        return jnp.concatenate([jnp.concatenate([m0, z0], axis=1),
                                jnp.concatenate([z1, m1], axis=1)], axis=0)

    def pair_dot(lhs, rhs):
        out = []
        for u in range(0, len(lhs), 2):
            res = _dot(jnp.concatenate([lhs[u], lhs[u + 1]], axis=1), bd(rhs[u], rhs[u + 1]))
            cut = rhs[u].shape[1]
            out += [res[:, :cut], res[:, cut:]]
        return out

    aak_v = pair_dot(a_ak, vh)
    eye = (ti == si).astype(F32)
    inv = None
    blk = 1
    while blk < RW_CHUNK:
        below = same & ((ti // blk) % 2 == 1) & ((si // blk) % 2 == 0) & (
            (ti // (2 * blk)) == (si // (2 * blk)))
        a21 = [jnp.where(below, x[u][:tt, :tt], 0.0) for u in idx]
        if inv is None:
            inv = [eye + a21[u] for u in idx]
        else:
            left = pair_dot(a21, inv)
            grow = pair_dot(inv, left)
            inv = [inv[u] + grow[u] for u in idx]
        blk *= 2
    yv = pair_dot(inv, [jnp.concatenate([at_h[u], aak_v[u]], axis=1) for u in idx])
    arb_y = pair_dot(a_rb, yv)
    ark_v = pair_dot(a_rk, vh)
    qt = [rt_h[u] + arb_y[u][:, :hd] for u in idx]
    o0 = [arb_y[u][:, hd:] + ark_v[u] for u in idx]
    yt = [yv[u].T for u in idx]
    bpb = [tile_chunks(t) for t in pick("bp")]
    kpb = [tile_chunks(t) for t in pick("kp")]
    g_all = [_dot(yt[u][:hd], bpb[u]) for u in idx]
    h_all = [_dot(jnp.concatenate([yt[u][hd:], vh[u].T], axis=1),
                  jnp.concatenate([bpb[u], kpb[u]], axis=0)) for u in idx]
    pt_h = pick("pt")
    pairs = range(0, len(units), 2)
    low_lanes = lax.broadcasted_iota(I32, (hd, 2 * hd), 1) < hd
    s = [jnp.concatenate([s_ref[u], s_ref[u + 1]], axis=1) for u in pairs]
    for n in range(nchunk):
        rows = slice(n * RW_CHUNK, (n + 1) * RW_CHUNK)
        cols = slice(n * hd, (n + 1) * hd)
        last = (n + 1) * RW_CHUNK - 1
        for j, u in enumerate(pairs):
            s_bd = jnp.concatenate([jnp.where(low_lanes, s[j], 0.0),
                                    jnp.where(low_lanes, 0.0, s[j])], axis=0)
            o_pair = _dot_nt(jnp.concatenate([qt[u][rows], qt[u + 1][rows]], axis=1), s_bd)
            for i in range(2):
                b, sl = units[u + i]
                obuf_ref[b, rows, sl] = o_pair[:, i * hd:(i + 1) * hd] + o0[u + i][rows]
        s = [s[j] * jnp.concatenate([pt_h[u][last:last + 1], pt_h[u + 1][last:last + 1]], axis=1)
             + _dot(s[j], bd(g_all[u][:, cols], g_all[u + 1][:, cols]))
             + jnp.concatenate([h_all[u][:, cols], h_all[u + 1][:, cols]], axis=1)
             for j, u in enumerate(pairs)]
    for j, u in enumerate(pairs):
        s_ref[u] = s[j][:, :hd]
        s_ref[u + 1] = s[j][:, hd:]

    for b in range(nb):
        o = obuf_ref[b]
        mean = segsum(o) * (1.0 / hd)
        oc = o - mean
        var = segsum(oc * oc) * (1.0 / hd)
        on = oc * lax.rsqrt(var + GN_EPS) * gn_g + gn_b
        o_ref[b] = ((on + rows_in[b]["bonus"]) * rows_in[b]["g"]).astype(BF16)


def _rwkv(p_r, mu, w0, w2, a0, a2, g2, k_k, k_a, r_k, gn_g, gn_b):
    bsz, seq, _ = p_r.shape
    tt = RW_TILE
    nb = RW_ROWS if bsz % RW_ROWS == 0 else 1
    w = RWKV_WIDTH
    vec = jnp.stack([w0, a0, k_k, k_a, r_k.reshape(w), gn_g, gn_b, jnp.zeros_like(w0)])
    t_idx = np.arange(tt)
    same = (t_idx[:, None] // RW_CHUNK) == (t_idx[None, :] // RW_CHUNK)
    ltri = jnp.asarray(same & (t_idx[None, :] <= t_idx[:, None]), BF16)
    ustr = jnp.asarray(same & (t_idx[None, :] > t_idx[:, None]), BF16)
    c_idx = np.arange(MXU_TILE)
    seg = jnp.asarray((c_idx[:, None] // HEAD_DIM) == (c_idx[None, :] // HEAD_DIM), BF16)
    full = lambda shape: pl.BlockSpec(shape, lambda b, i: (0,) * len(shape))
    per_tile = tt // SUBLANES
    return pl.pallas_call(
        _rwkv_kernel,
        grid=(bsz // nb, seq // tt),
        in_specs=[pl.BlockSpec((nb, tt, RWKV_COLS), lambda b, i: (b, i, 0)),
                  pl.BlockSpec((nb, SUBLANES, RWKV_COLS),
                               lambda b, i: (b, jnp.maximum(i * per_tile - 1, 0), 0)),
                  full((1, RWKV_COLS)), full((SUBLANES, w)),
                  full(w2.shape), full(a2.shape), full(g2.shape),
                  full((tt, tt)), full((tt, tt)), full((MXU_TILE, MXU_TILE))],
        out_specs=pl.BlockSpec((nb, tt, w), lambda b, i: (b, i, 0)),
        out_shape=jax.ShapeDtypeStruct((bsz, seq, w), BF16),
        scratch_shapes=[pltpu.VMEM((nb * RWKV_HEADS, HEAD_DIM, HEAD_DIM), F32),
                        pltpu.VMEM((nb, tt, w), F32)],
        compiler_params=_params(("parallel", "arbitrary")),
        name="rwkv7_mix",
    )(p_r, p_r, mu.reshape(1, RWKV_COLS), vec, w2.astype(BF16), a2.astype(BF16),
      g2.astype(BF16), ltri, ustr, seg)


def _dsa_kernel(qt_ref, k_ref, vt_ref, qit_ref, ki_ref, wit_ref, low_ref, o_ref,
                key_ref, bias_ref, acc_ref, m_ref, l_ref, *, topk):
    qb = pl.program_id(1)
    nq = Q_BLOCK
    kt_sz = bias_ref.shape[0]
    hd = HEAD_DIM
    start = qb * nq
    nkt = (start + nq + kt_sz - 1) // kt_sz
    col = lax.broadcasted_iota(I32, (1, nq), 1)
    lim = start + (col // CHUNK + 1) * CHUNK
    wit = wit_ref[0]
    qit = qit_ref[0]
    qi_cat = jnp.concatenate([qit[h * IDX_DIM:(h + 1) * IDX_DIM] for h in range(IDX_HEADS)],
                             axis=1)

    def score_tile(j, carry):
        off = pl.multiple_of(j * kt_sz, kt_sz)
        d = jnp.dot(ki_ref[0, pl.ds(off, kt_sz), :], qi_cat, preferred_element_type=F32)
        s = jnp.zeros((kt_sz, nq), F32)
        for h in range(IDX_HEADS):
            s = s + wit[h:h + 1] * jnp.maximum(d[:, h * nq:(h + 1) * nq], 0.0)
        s = s + 0.0
        bits = pltpu.bitcast(s, I32)
        key = bits ^ ((bits >> 31) & 0x7FFFFFFF)
        sidx = off + lax.broadcasted_iota(I32, (kt_sz, nq), 0)
        key_ref[pl.ds(off, kt_sz), :] = jnp.where(sidx < lim, key, INT_MIN)
        return carry

    lax.fori_loop(0, nkt, score_tile, 0)

    def count(pred):
        def body(j, c):
            off = pl.multiple_of(j * kt_sz, kt_sz)
            hit = jnp.where(pred(key_ref[pl.ds(off, kt_sz), :]), 1, 0)
            return c + hit.reshape(kt_sz // RED_ROWS, RED_ROWS, nq).sum(axis=0)
        c = lax.fori_loop(0, nkt, body, jnp.zeros((RED_ROWS, nq), I32))
        return jnp.sum(c, axis=0, keepdims=True)

    def slot_max(j, c):
        off = pl.multiple_of(j * kt_sz, kt_sz)
        return jnp.maximum(c, key_ref[pl.ds(off, kt_sz), :].reshape(kt_sz // SLOTS, SLOTS, nq)
                           .max(axis=0))

    smax = lax.fori_loop(0, nkt, slot_max, jnp.full((SLOTS, nq), INT_MIN, I32))
    lo = jnp.min(smax, axis=0, keepdims=True)
    hi = jnp.max(smax, axis=0, keepdims=True) + 1
    zero = jnp.zeros((1, nq), I32)
    nonneg = count(lambda kv: kv >= zero) >= topk
    lo = jnp.where(nonneg, jnp.maximum(lo, 0), lo)
    hi = jnp.where(nonneg, hi, jnp.minimum(hi, 0))
    half_width = (hi >> 1) - (lo >> 1)
    steps = jnp.max(32 - lax.clz(half_width)) + 1

    def bisect(_, bounds):
        lo, hi = bounds
        mid = (lo >> 1) + (hi >> 1) + (lo & hi & 1)
        take = count(lambda kv: kv >= mid) >= topk
        return jnp.where(take, mid, lo), jnp.where(take, hi, mid)

    th, _ = lax.fori_loop(0, steps, bisect, (lo, hi))
    need = jnp.where(th == INT_MIN, 0, topk - count(lambda kv: kv > th)).astype(F32)

    acc_ref[...] = jnp.zeros_like(acc_ref)
    m_ref[...] = jnp.full_like(m_ref, NEG_INF)
    l_ref[...] = jnp.zeros_like(l_ref)
    low = low_ref[...]
    npair = DSA_HEADS // 2
    top_rows = lax.broadcasted_iota(I32, (2 * hd, nq), 0) < hd
    qt = qt_ref[0].astype(F32)
    q_bd = []
    for p in range(npair):
        qp = qt[p * 2 * hd:(p + 1) * 2 * hd]
        q_bd.append(jnp.concatenate([jnp.where(top_rows, qp, 0.0), jnp.where(top_rows, 0.0, qp)],
                                    axis=1).astype(BF16))

    pad_rows = 2 * SUBLANES
    ones_rows = (lax.broadcasted_iota(I32, (pad_rows, kt_sz), 0) == 0).astype(BF16)

    def attn_tile(j, eq_before):
        off = pl.multiple_of(j * kt_sz, kt_sz)
        eq_run = eq_before
        for t in range(kt_sz // TRI_TILE):
            kv = key_ref[pl.ds(off + t * TRI_TILE, TRI_TILE), :]
            eq = kv == th
            pre = eq_run + jnp.dot(low, jnp.where(eq, 1.0, 0.0).astype(BF16),
                                   preferred_element_type=F32)
            sel = (kv > th) | (eq & (pre <= need))
            bias_ref[t * TRI_TILE:(t + 1) * TRI_TILE, :] = jnp.where(sel, 0.0, NEG_INF)
            eq_run = pre[TRI_TILE - 1:TRI_TILE]
        bias = bias_ref[...]

        def logits(p):
            half = kt_sz // 2
            return jnp.concatenate(
                [jnp.dot(k_ref[0, pl.ds(off + r * half, half), p * 2 * hd:(p + 1) * 2 * hd],
                         q_bd[p], preferred_element_type=F32) for r in range(2)], axis=0)

        st_next = logits(0)
        for p in range(npair):
            cols = slice(p * 2 * hd, (p + 1) * 2 * hd)
            st = st_next
            if p + 1 < npair:
                st_next = logits(p + 1)
            prs, scales = [], []
            for i in range(2):
                h = 2 * p + i
                s = st[:, i * nq:(i + 1) * nq] + bias
                m_old = m_ref[h:h + 1]
                m_new = jnp.maximum(m_old, _fold_rows(s, jnp.max))
                prs.append(jnp.exp2(s - m_new).astype(BF16))
                scales.append(jnp.exp2(m_old - m_new))
                m_ref[h:h + 1] = m_new
            lhs = jnp.concatenate([vt_ref[0, cols, pl.ds(off, kt_sz)], ones_rows], axis=0)
            pv = jnp.dot(lhs, jnp.concatenate(prs, axis=1),
                         preferred_element_type=F32)
            for i in range(2):
                h = 2 * p + i
                l_ref[h:h + 1] = (scales[i] * l_ref[h:h + 1]
                                  + pv[2 * hd:2 * hd + 1, i * nq:(i + 1) * nq])
            new = jnp.where(top_rows, pv[:2 * hd, :nq], pv[:2 * hd, nq:])
            acc_ref[p] = jnp.where(top_rows, scales[0], scales[1]) * acc_ref[p] + new
        return eq_run

    lax.fori_loop(0, nkt, attn_tile, jnp.zeros((1, nq), F32))
    for p in range(npair):
        inv = jnp.where(top_rows, 1.0 / l_ref[2 * p:2 * p + 1], 1.0 / l_ref[2 * p + 1:2 * p + 2])
        o_ref[0, :, p * 2 * hd:(p + 1) * 2 * hd] = (acc_ref[p] * inv).T.astype(BF16)


def _dsa(q_t, k, v_t, qi_t, ki, wi_t):
    bsz, seq, w = k.shape
    nq = Q_BLOCK
    topk = min(INDEX_TOPK, seq // 4)
    kt_sz = min(KEY_TILE, seq)
    assert SLOTS >= topk and kt_sz % SLOTS == 0 and kt_sz % TRI_TILE == 0
    t_idx = np.arange(TRI_TILE)
    low = jnp.asarray(t_idx[None, :] <= t_idx[:, None], BF16)
    cols = lambda n: pl.BlockSpec((1, n, nq), lambda b, i: (b, 0, i))
    return pl.pallas_call(
        functools.partial(_dsa_kernel, topk=topk),
        grid=(bsz, seq // nq),
        in_specs=[cols(w),
                  pl.BlockSpec((1, seq, w), lambda b, i: (b, 0, 0)),
                  pl.BlockSpec((1, w, seq), lambda b, i: (b, 0, 0)),
                  cols(IDX_HEADS * IDX_DIM),
                  pl.BlockSpec((1, seq, IDX_DIM), lambda b, i: (b, 0, 0)),
                  cols(SUBLANES),
                  pl.BlockSpec((TRI_TILE, TRI_TILE), lambda b, i: (0, 0))],
        out_specs=pl.BlockSpec((1, nq, w), lambda b, i: (b, i, 0)),
        out_shape=jax.ShapeDtypeStruct((bsz, seq, w), BF16),
        scratch_shapes=[pltpu.VMEM((seq, nq), I32),
                        pltpu.VMEM((kt_sz, nq), F32),
                        pltpu.VMEM((DSA_HEADS // 2, 2 * HEAD_DIM, nq), F32),
                        pltpu.VMEM((DSA_HEADS, nq), F32),
                        pltpu.VMEM((DSA_HEADS, nq), F32)],
        compiler_params=_params(("parallel", "arbitrary")),
        name="dsa_mix",
    )(q_t, k, v_t, qi_t, ki, wi_t, low)


POOL_HALO = 16


def _odd_layer_kernel(x_ref, xp_ref, mod_ref, win_ref, pw_ref, ps_ref, lng_ref, lnb_ref, ws_ref,
                      bs_ref, wout_ref, g_ref, b_ref, o_ref, y_ref):
    i = pl.program_id(1)
    m = mod_ref[0]
    sc = 1.0 + m[1:2]
    sh = m[0:1]
    x = x_ref[0]
    tm = x.shape[0]
    gd = POOL_GROUP_DIM
    p = jnp.dot((x * sc + sh).astype(BF16), win_ref[...], preferred_element_type=F32)
    prev = jnp.dot((xp_ref[0] * sc + sh).astype(BF16), win_ref[:, :POOL_WIDTH],
                   preferred_element_type=F32) * (i > 0).astype(F32)
    t_glob = (i * tm + lax.broadcasted_iota(I32, (tm, 1), 0)).astype(F32)
    scale = ps_ref[...]
    for gi, win in enumerate(POOL_WINDOWS):
        cols = slice(gi * gd, (gi + 1) * gd)
        xg = p[:, cols]
        s = jnp.concatenate([prev[:, cols], xg], axis=0)
        span = 1
        while span < win:
            s = s[span:] + s[:-span]
            span *= 2
        first = POOL_HALO + 1 - win
        pooled = s[first:first + tm] / jnp.minimum(t_glob + 1.0, float(win)) - xg
        y_ref[:, cols] = (_dot(pooled, pw_ref[gi]) * scale[:, cols]).astype(BF16)

    u = _gelu(p[:, POOL_WIDTH:POOL_WIDTH + SG_WIDTH])
    v = _ln(_gelu(p[:, POOL_WIDTH + SG_WIDTH:]), lng_ref[...], lnb_ref[...], LN_EPS)
    ti = lax.broadcasted_iota(I32, (SG_CHUNK, SG_CHUNK), 0)
    si = lax.broadcasted_iota(I32, (SG_CHUNK, SG_CHUNK), 1)
    bs = bs_ref[...]
    for gi in range(SG_GROUPS):
        cols = slice(gi * SG_GROUP_DIM, (gi + 1) * SG_GROUP_DIM)
        ws = jnp.where(si <= ti, ws_ref[gi], 0.0)
        for n in range(tm // SG_CHUNK):
            rows = slice(n * SG_CHUNK, (n + 1) * SG_CHUNK)
            z = _dot(ws, v[rows, cols]) + bs[:, gi:gi + 1]
            y_ref[rows, POOL_WIDTH + gi * SG_GROUP_DIM:POOL_WIDTH + (gi + 1) * SG_GROUP_DIM] = (
                u[rows, cols] * z).astype(BF16)

    y = jnp.dot(y_ref[...], wout_ref[...], preferred_element_type=F32)
    o_ref[0] = _ln(ALPHA * x + m[2:3] * y, g_ref[...], b_ref[...], LN_EPS)


def _odd_layer(x, mod, w_in, pool_w, pool_scale, sg_ln_g, sg_ln_b, sg_w, sg_b, w_out, ln_g, ln_b):
    bsz, seq, d = x.shape
    tm = 256
    per_tile = tm // POOL_HALO
    full = lambda shape: pl.BlockSpec(shape, lambda b, i: (0,) * len(shape))
    once = lambda shape: pl.BlockSpec(shape, lambda b, i: (0,) * len(shape),
                                      pipeline_mode=pl.Buffered(1))
    return pl.pallas_call(
        _odd_layer_kernel,
        grid=(bsz, seq // tm),
        in_specs=[pl.BlockSpec((1, tm, d), lambda b, i: (b, i, 0)),
                  pl.BlockSpec((1, POOL_HALO, d),
                               lambda b, i: (b, jnp.maximum(i * per_tile - 1, 0), 0)),
                  pl.BlockSpec((1, 6, d), lambda b, i: (b, 0, 0)),
                  once(w_in.shape), full(pool_w.shape), full((1, POOL_WIDTH)),
                  full((1, SG_WIDTH)), full((1, SG_WIDTH)), full(sg_w.shape),
                  full((SG_CHUNK, SG_GROUPS)), once(w_out.shape), full((1, d)), full((1, d))],
        out_specs=pl.BlockSpec((1, tm, d), lambda b, i: (b, i, 0)),
        out_shape=jax.ShapeDtypeStruct((bsz, seq, d), F32),
        scratch_shapes=[pltpu.VMEM((tm, d), BF16)],
        compiler_params=_params(("parallel", "parallel")),
        name="odd_layer_mix",
    )(x, x, mod, w_in.astype(BF16), pool_w.astype(BF16), pool_scale.reshape(1, -1),
      sg_ln_g.reshape(1, -1), sg_ln_b.reshape(1, -1), sg_w, sg_b.T, w_out.astype(BF16),
      ln_g.reshape(1, d), ln_b.reshape(1, d))


def _proj_ln_kernel(*refs, n_in, gate_row):
    a_refs = refs[:n_in]
    w_refs = refs[n_in:2 * n_in]
    x_ref, mod_ref, g_ref, b_ref, o_ref = refs[2 * n_in:]
    y = None
    for a_ref, w_ref in zip(a_refs, w_refs):
        part = jnp.dot(a_ref[0], w_ref[...], preferred_element_type=F32)
        y = part if y is None else y + part
    gate = mod_ref[0][gate_row:gate_row + 1]
    o_ref[0] = _ln(ALPHA * x_ref[0] + gate * y, g_ref[...], b_ref[...], LN_EPS)


def _proj_ln(acts, weights, x, mod, gate_row, ln_g, ln_b, name):
    bsz, seq, d = x.shape
    tm = 512
    n_in = len(acts)
    in_specs = [pl.BlockSpec((1, tm, a.shape[2]), lambda b, i: (b, i, 0)) for a in acts]
    in_specs += [pl.BlockSpec(w.shape, lambda b, i: (0, 0)) for w in weights]
    in_specs += [pl.BlockSpec((1, tm, d), lambda b, i: (b, i, 0)),
                 pl.BlockSpec((1, 6, d), lambda b, i: (b, 0, 0)),
                 pl.BlockSpec((1, d), lambda b, i: (0, 0)),
                 pl.BlockSpec((1, d), lambda b, i: (0, 0))]
    return pl.pallas_call(
        functools.partial(_proj_ln_kernel, n_in=n_in, gate_row=gate_row),
        grid=(bsz, seq // tm),
        in_specs=in_specs,
        out_specs=pl.BlockSpec((1, tm, d), lambda b, i: (b, i, 0)),
        out_shape=jax.ShapeDtypeStruct((bsz, seq, d), F32),
        compiler_params=_params(("parallel", "parallel")),
        name=name,
    )(*acts, *[w.astype(BF16) for w in weights], x, mod, ln_g.reshape(1, d), ln_b.reshape(1, d))


def _ffn_kernel(x_ref, xp_ref, mod_ref, wg_ref, wv_ref, cwg_ref, cwv_ref, cbg_ref, cbv_ref,
                wd_ref, g_ref, b_ref, o_ref):
    i = pl.program_id(1)
    m = mod_ref[0]
    sc = 1.0 + m[4:5]
    sh = m[3:4]
    x = x_ref[0]
    live = (i > 0).astype(F32)
    h = jnp.concatenate([(xp_ref[0] * sc + sh) * live, x * sc + sh], axis=0).astype(BF16)
    halo = SUBLANES

    def conv(w_ref, cw_ref, cb_ref):
        u = jnp.dot(h, w_ref[...], preferred_element_type=F32)
        cw = cw_ref[...]
        u1 = pltpu.roll(u, 1, 0)
        u2 = pltpu.roll(u, 2, 0)
        return (cb_ref[...] + u2[halo:] * cw[0:1] + u1[halo:] * cw[1:2] + u[halo:] * cw[2:3])

    gate = conv(wg_ref, cwg_ref, cbg_ref)
    val = conv(wv_ref, cwv_ref, cbv_ref)
    act = (gate * jax.nn.sigmoid(gate) * val).astype(BF16)
    y = jnp.dot(act, wd_ref[...], preferred_element_type=F32)
    o_ref[0] = _ln(ALPHA * x + m[5:6] * y, g_ref[...], b_ref[...], LN_EPS)


def _conv_ffn(x, mod, w_up, conv_w, conv_b, w_down, ln_g, ln_b):
    bsz, seq, d = x.shape
    tm = 512
    per_tile = tm // SUBLANES
    wb = w_up.astype(BF16)
    cb = conv_b.reshape(1, -1)
    once = dict(pipeline_mode=pl.Buffered(1))
    return pl.pallas_call(
        _ffn_kernel,
        grid=(bsz, seq // tm),
        in_specs=[pl.BlockSpec((1, tm, d), lambda b, i: (b, i, 0)),
                  pl.BlockSpec((1, SUBLANES, d),
                               lambda b, i: (b, jnp.maximum(i * per_tile - 1, 0), 0)),
                  pl.BlockSpec((1, 6, d), lambda b, i: (b, 0, 0)),
                  pl.BlockSpec((d, D_FF), lambda b, i: (0, 0), **once),
                  pl.BlockSpec((d, D_FF), lambda b, i: (0, 1), **once),
                  pl.BlockSpec((3, D_FF), lambda b, i: (0, 0), **once),
                  pl.BlockSpec((3, D_FF), lambda b, i: (0, 1), **once),
                  pl.BlockSpec((1, D_FF), lambda b, i: (0, 0), **once),
                  pl.BlockSpec((1, D_FF), lambda b, i: (0, 1), **once),
                  pl.BlockSpec((D_FF, d), lambda b, i: (0, 0), **once),
                  pl.BlockSpec((1, d), lambda b, i: (0, 0)),
                  pl.BlockSpec((1, d), lambda b, i: (0, 0))],
        out_specs=pl.BlockSpec((1, tm, d), lambda b, i: (b, i, 0)),
        out_shape=jax.ShapeDtypeStruct((bsz, seq, d), F32),
        compiler_params=_params(("parallel", "parallel")),
        name="conv_ffn_ln",
    )(x, x, mod, wb, wb, conv_w, conv_w, cb, cb, w_down.astype(BF16),
      ln_g.reshape(1, d), ln_b.reshape(1, d))


def _rope_tables(seq):
    inv = ROPE_THETA ** (-jnp.arange(0, ROPE_DIM, 2, dtype=F32) / ROPE_DIM)
    ang = jnp.arange(seq, dtype=F32)[:, None] * inv[None, :]
    cos, sin = jnp.cos(ang), jnp.sin(ang)
    half = ROPE_DIM // 2
    rest = HEAD_DIM - ROPE_DIM
    one = jnp.ones((seq, rest), F32)
    zero = jnp.zeros((seq, rest), F32)
    zh = jnp.zeros((seq, half), F32)
    head = lambda *parts: jnp.concatenate(parts * (LANES // HEAD_DIM), axis=1)
    return head(cos, cos, one), head(-sin, zh, zero), head(zh, sin, zero)


def kernel(x, c, ada_w, ada_b, ln_g, ln_b, ffn_w_up, ffn_conv_w, ffn_conv_b, ffn_w_down, ev_w_in, ev_w_out, rw_mu, rw_w0, rw_w2, rw_a0, rw_a2, rw_g2, rw_k_k, rw_k_a, rw_r_k, rw_gn_g, rw_gn_b, ik_ln_g, ik_ln_b, od_w_in, od_w_out, pool_w, pool_scale, sg_ln_g, sg_ln_b, sg_w, sg_b):
    seq = x.shape[1]
    tables = _rope_tables(seq)
    mods = _modulation(c, ada_w, ada_b)
    ev_w = _split_even_weights(ev_w_in)
    for layer in range(DEPTH):
        mod = mods[layer]
        if layer % 2 == 0:
            e = layer // 2
            p_r, q_t, k, v_t, qi_t, ki, wi_t = _even_in_proj(x, mod, [w[e] for w in ev_w],
                                                             ik_ln_g[e], ik_ln_b[e], tables)
            ya = _rwkv(p_r, rw_mu[e], rw_w0[e], rw_w2[e], rw_a0[e], rw_a2[e], rw_g2[e],
                       rw_k_k[e], rw_k_a[e], rw_r_k[e], rw_gn_g[e], rw_gn_b[e])
            yb = _dsa(q_t, k, v_t, qi_t, ki, wi_t)
            w_out = ev_w_out[e]
            x = _proj_ln([ya, yb], [w_out[:RWKV_WIDTH], w_out[RWKV_WIDTH:]], x, mod, 2,
                         ln_g[layer, 0], ln_b[layer, 0], "even_out_proj_ln")
        else:
            o = layer // 2
            x = _odd_layer(x, mod, od_w_in[o], pool_w[o], pool_scale[o], sg_ln_g[o], sg_ln_b[o],
                           sg_w[o], sg_b[o], od_w_out[o], ln_g[layer, 0], ln_b[layer, 0])
        x = _conv_ffn(x, mod, ffn_w_up[layer], ffn_conv_w[layer], ffn_conv_b[layer],
                      ffn_w_down[layer], ln_g[layer, 1], ln_b[layer, 1])
    return x
```

```python
import functools

import numpy as np
import jax
import jax.numpy as jnp
from jax import lax
from jax.experimental import pallas as pl
from jax.experimental.pallas import tpu as pltpu

F32 = jnp.float32
BF16 = jnp.bfloat16
I32 = jnp.int32

D_MODEL = 1024
DEPTH = 4
CHUNK = 64
HEAD_DIM = 64
RWKV_WIDTH = D_MODEL // 2
RWKV_HEADS = RWKV_WIDTH // HEAD_DIM
RWKV_LORA_W = 64
RWKV_LORA_A = 64
RWKV_LORA_G = 128
RWKV_COLS = 3 * RWKV_WIDTH + RWKV_LORA_W + RWKV_LORA_A + RWKV_LORA_G
DSA_WIDTH = D_MODEL - RWKV_WIDTH
DSA_HEADS = DSA_WIDTH // HEAD_DIM
IDX_HEADS = 4
IDX_DIM = 64
INDEX_TOPK = 256
Q_BLOCK = 128
ROPE_THETA = 500000.0
ROPE_DIM = HEAD_DIM // 4
POOL_WINDOWS = (2, 4, 8, 16)
POOL_WIDTH = D_MODEL // 2
POOL_GROUP_DIM = POOL_WIDTH // len(POOL_WINDOWS)
SG_WIDTH = D_MODEL - POOL_WIDTH
SG_GROUPS = 4
SG_GROUP_DIM = SG_WIDTH // SG_GROUPS
SG_CHUNK = 128
D_FF = 2816
ALPHA = (2.0 * DEPTH) ** 0.25
LN_EPS = 1e-5
GN_EPS = 64e-5
NEG_INF = -1e30
INT_MIN = -(2 ** 31)
LOG2E = 1.4426950408889634

LANES = 128
SUBLANES = 8
MXU_TILE = 256
VMEM_LIMIT = 56 * 1024 * 1024

RW_TILE = 128
RW_CHUNK = 16
RW_ROWS = 4
KEY_TILE = 1024
TRI_TILE = 256
SLOTS = 256
SEARCH_UNROLL = 4
PAD_IDX = 384

def _dot(a, b):
    return jnp.dot(a.astype(BF16), b.astype(BF16), preferred_element_type=F32)


def _dot_nt(a, b):
    return lax.dot_general(a.astype(BF16), b.astype(BF16), (((1,), (1,)), ((), ())),
                           preferred_element_type=F32)


def _dot_split(x, m01, terms):
    acc = None
    rem = x
    for _ in range(terms):
        piece = rem.astype(BF16)
        rem = rem - piece.astype(F32)
        part = jnp.dot(piece, m01, preferred_element_type=F32)
        acc = part if acc is None else acc + part
    return acc


def _dot_split_left(m01, x, terms):
    acc = None
    rem = x
    for _ in range(terms):
        piece = rem.astype(BF16)
        rem = rem - piece.astype(F32)
        part = jnp.dot(m01, piece, preferred_element_type=F32)
        acc = part if acc is None else acc + part
    return acc


def _ln(x, g, b, eps):
    mu = jnp.mean(x, axis=-1, keepdims=True)
    xc = x - mu
    var = jnp.mean(xc * xc, axis=-1, keepdims=True)
    return xc * lax.rsqrt(var + eps) * g + b


def _gelu(x):
    return 0.5 * x * (1.0 + lax.erf(x * 0.7071067811865476))


RED_ROWS = 64


def _fold_rows(x, op):
    rows, n = x.shape
    part = op(x.reshape(rows // RED_ROWS, RED_ROWS, n), axis=0)
    return op(part, axis=0, keepdims=True)


def _params(sem):
    return pltpu.CompilerParams(dimension_semantics=sem, vmem_limit_bytes=VMEM_LIMIT)


def _mod_kernel(c_ref, w_ref, b_ref, o_ref):
    c = c_ref[...]
    ca = c * jax.nn.sigmoid(c)
    o_ref[0] = jnp.dot(ca, w_ref[0], preferred_element_type=F32,
                       precision=lax.Precision.HIGHEST) + b_ref[0]


def _modulation(c, ada_w, ada_b):
    bsz, d = c.shape
    depth = ada_w.shape[0]
    n = ada_w.shape[2]
    tn = 1536
    rows = -(-bsz // SUBLANES) * SUBLANES
    c8 = jnp.pad(c, ((0, rows - bsz), (0, 0)))
    out = pl.pallas_call(
        _mod_kernel,
        grid=(depth, n // tn),
        in_specs=[pl.BlockSpec((rows, d), lambda l, j: (0, 0)),
                  pl.BlockSpec((1, d, tn), lambda l, j: (l, 0, j)),
                  pl.BlockSpec((1, 1, tn), lambda l, j: (l, 0, j))],
        out_specs=pl.BlockSpec((1, rows, tn), lambda l, j: (l, 0, j)),
        out_shape=jax.ShapeDtypeStruct((depth, rows, n), F32),
        compiler_params=_params(("arbitrary", "arbitrary")),
        name="adaln_mod",
    )(c8, ada_w, ada_b.reshape(depth, 1, n))
    return out[:, :bsz].reshape(depth, bsz, 6, d)


def _rope(x, cos_t, sin_a, sin_b):
    n = x.shape[1] // LANES
    rep = (lambda t: jnp.concatenate([t] * n, axis=1)) if n > 1 else (lambda t: t)
    width = x.shape[1]
    half = ROPE_DIM // 2
    return (x * rep(cos_t) + pltpu.roll(x, width - half, 1) * rep(sin_a)
            + pltpu.roll(x, half, 1) * rep(sin_b))


def _even_in_kernel(x_ref, mod_ref, wr_ref, wqkv_ref, widx_ref, cos_ref, sa_ref, sb_ref,
                    ikg_ref, ikb_ref,
                    pr_ref, qt_ref, k_ref, vt_ref, qit_ref, ki_ref, wit_ref):
    m = mod_ref[0]
    h = (x_ref[0] * (1.0 + m[1:2]) + m[0:1]).astype(BF16)
    pr_ref[0] = jnp.dot(h, wr_ref[...], preferred_element_type=F32)
    qkv = jnp.dot(h, wqkv_ref[...], preferred_element_type=F32)
    cos_t = cos_ref[...]
    sin_a = sa_ref[...]
    sin_b = sb_ref[...]
    w = DSA_WIDTH
    q = _rope(qkv[:, :w], cos_t, sin_a, sin_b) * (HEAD_DIM ** -0.5 * LOG2E)
    qt_ref[0] = q.T.astype(BF16)
    k_ref[0] = _rope(qkv[:, w:2 * w], cos_t, sin_a, sin_b).astype(BF16)
    vt_ref[0] = qkv[:, 2 * w:].T.astype(BF16)
    idx = jnp.dot(h, widx_ref[...], preferred_element_type=F32)
    nq = IDX_HEADS * IDX_DIM
    qit_ref[0] = _rope(idx[:, :nq], cos_t, sin_a, sin_b).T.astype(BF16)
    blk = idx[:, nq:nq + LANES]
    lane = lax.broadcasted_iota(I32, blk.shape, 1)
    is_k = lane < IDX_DIM
    mu = jnp.sum(jnp.where(is_k, blk, 0.0), axis=1, keepdims=True) * (1.0 / IDX_DIM)
    xc = jnp.where(is_k, blk - mu, 0.0)
    var = jnp.sum(xc * xc, axis=1, keepdims=True) * (1.0 / IDX_DIM)
    kin = xc * lax.rsqrt(var + LN_EPS) * ikg_ref[...] + ikb_ref[...]
    ki_ref[0] = _rope(kin, cos_t, sin_a, sin_b)[:, :IDX_DIM].astype(BF16)
    wit = (blk * (IDX_HEADS ** -0.5 * IDX_DIM ** -0.5)).T
    wit_ref[0] = wit[IDX_DIM:IDX_DIM + SUBLANES]


def _split_w_kernel(w_ref, wr_ref, wqkv_ref, widx_ref):
    w = w_ref[0]
    c1 = RWKV_COLS
    c2 = RWKV_COLS + 3 * DSA_WIDTH
    wr_ref[0] = w[:, :c1].astype(BF16)
    wqkv_ref[0] = w[:, c1:c2].astype(BF16)
    tail = w[:, c2:]
    zeros = jnp.zeros((w.shape[0], PAD_IDX - tail.shape[1]), F32)
    widx_ref[0] = jnp.concatenate([tail, zeros], axis=1).astype(BF16)


def _split_even_weights(ev_w_in):
    n_even, d, n = ev_w_in.shape
    tr = 256
    widths = (RWKV_COLS, 3 * DSA_WIDTH, PAD_IDX)
    return pl.pallas_call(
        _split_w_kernel,
        grid=(n_even, d // tr),
        in_specs=[pl.BlockSpec((1, tr, n), lambda e, i: (e, i, 0))],
        out_specs=[pl.BlockSpec((1, tr, wd), lambda e, i: (e, i, 0)) for wd in widths],
        out_shape=[jax.ShapeDtypeStruct((n_even, d, wd), BF16) for wd in widths],
        compiler_params=_params(("parallel", "parallel")),
        name="split_even_weights",
    )(ev_w_in)


def _even_in_proj(x, mod, weights, ik_g, ik_b, tables):
    bsz, seq, d = x.shape
    tm = 256
    w_r, w_qkv, w_idx = weights
    pad = LANES - IDX_DIM
    ikg = jnp.pad(ik_g, (0, pad)).reshape(1, LANES)
    ikb = jnp.pad(ik_b, (0, pad)).reshape(1, LANES)
    cos_t, sin_a, sin_b = tables
    full = lambda shape: pl.BlockSpec(shape, lambda b, i: (0,) * len(shape))
    tab = pl.BlockSpec((tm, LANES), lambda b, i: (i, 0))
    nq = IDX_HEADS * IDX_DIM
    rows = lambda n: pl.BlockSpec((1, tm, n), lambda b, i: (b, i, 0))
    cols = lambda n: pl.BlockSpec((1, n, tm), lambda b, i: (b, 0, i))
    return pl.pallas_call(
        _even_in_kernel,
        grid=(bsz, seq // tm),
        in_specs=[pl.BlockSpec((1, tm, d), lambda b, i: (b, i, 0)),
                  pl.BlockSpec((1, 6, d), lambda b, i: (b, 0, 0)),
                  full(w_r.shape), full(w_qkv.shape), full(w_idx.shape),
                  tab, tab, tab, full((1, LANES)), full((1, LANES))],
        out_specs=[rows(RWKV_COLS), cols(DSA_WIDTH), rows(DSA_WIDTH), cols(DSA_WIDTH),
                   cols(nq), rows(IDX_DIM), cols(SUBLANES)],
        out_shape=[jax.ShapeDtypeStruct((bsz, seq, RWKV_COLS), F32),
                   jax.ShapeDtypeStruct((bsz, DSA_WIDTH, seq), BF16),
                   jax.ShapeDtypeStruct((bsz, seq, DSA_WIDTH), BF16),
                   jax.ShapeDtypeStruct((bsz, DSA_WIDTH, seq), BF16),
                   jax.ShapeDtypeStruct((bsz, nq, seq), BF16),
                   jax.ShapeDtypeStruct((bsz, seq, IDX_DIM), BF16),
                   jax.ShapeDtypeStruct((bsz, SUBLANES, seq), F32)],
        compiler_params=_params(("parallel", "parallel")),
        name="even_in_proj",
    )(x, mod, w_r, w_qkv, w_idx, cos_t, sin_a, sin_b, ikg, ikb)


def _rwkv_kernel(p_ref, pp_ref, mu_ref, vec_ref, w2_ref, a2_ref, g2_ref, ltri_ref, ustr_ref,
                 seg_ref, o_ref, s_ref, obuf_ref):
    i = pl.program_id(1)

    @pl.when(i == 0)
    def _():
        s_ref[...] = jnp.zeros_like(s_ref)

    tt = RW_TILE
    w = RWKV_WIDTH
    hd = HEAD_DIM
    nb = p_ref.shape[0]
    nchunk = tt // RW_CHUNK
    seg = seg_ref[...]
    half_w = seg.shape[0]

    def segsum(t):
        return jnp.concatenate([_dot_split(t[:, c:c + half_w], seg, 2)
                                for c in range(0, w, half_w)], axis=1)

    vec = vec_ref[...]
    w0, a0, k_k, k_a, r_k, gn_g, gn_b = (vec[j:j + 1] for j in range(7))
    rowi = lax.broadcasted_iota(I32, (tt, 1), 0)
    live = (i > 0).astype(F32)

    def prepare(b):
        p = p_ref[b]
        prow = pp_ref[b][SUBLANES - 1:SUBLANES] * live
        xprev = jnp.where(rowi == 0, prow, pltpu.roll(p, 1, 0))
        ps = p + (xprev - p) * mu_ref[...]
        r = ps[:, :w]
        k = ps[:, w:2 * w]
        v = ps[:, 2 * w:3 * w]
        o1 = 3 * w
        wd = ps[:, o1:o1 + RWKV_LORA_W]
        ad = ps[:, o1 + RWKV_LORA_W:o1 + RWKV_LORA_W + RWKV_LORA_A]
        gd = ps[:, o1 + RWKV_LORA_W + RWKV_LORA_A:]
        y = -(w0 + _dot(jnp.tanh(wd), w2_ref[...]))
        softplus = jnp.maximum(y, 0.0) + jnp.log1p(jnp.exp(-jnp.abs(y)))
        logw = -jnp.exp(-softplus - 0.5)
        a = jax.nn.sigmoid(a0 + _dot(ad, a2_ref[...]))
        g = _dot(jax.nn.sigmoid(gd), g2_ref[...])
        kk = k * k_k
        kk = kk / jnp.maximum(jnp.sqrt(segsum(kk * kk)), 1e-12)
        k2 = k * (1.0 + (a - 1.0) * k_a)
        bonus = segsum(r * k2 * r_k) * v
        cum = _dot_split_left(ltri_ref[...], logw, 3)
        rem = _dot_split_left(ustr_ref[...], logw, 3)
        pt = jnp.exp(cum)
        ipt = jnp.exp(-cum)
        erem = jnp.exp(rem)
        kka = kk * a
        return dict(at=-kk * jnp.exp(cum - logw), rt=r * pt, bt=kka * ipt, kt=k2 * ipt,
                    bp=kka * erem, kp=k2 * erem, v=v, pt=pt, bonus=bonus, g=g)

    rows_in = [prepare(b) for b in range(nb)]

    ti = lax.broadcasted_iota(I32, (tt, tt), 0)
    si = lax.broadcasted_iota(I32, (tt, tt), 1)
    same = (ti // RW_CHUNK) == (si // RW_CHUNK)
    strict = same & (si < ti)
    incl = same & (si <= ti)
    tb = lax.broadcasted_iota(I32, (tt, nchunk * hd), 0)
    cb = lax.broadcasted_iota(I32, (tt, nchunk * hd), 1)
    blkmask = (tb // RW_CHUNK) == (cb // hd)
    tile_chunks = lambda t: jnp.where(blkmask, jnp.concatenate([t] * nchunk, axis=1), 0.0)

    units = [(b, slice(h * hd, (h + 1) * hd)) for b in range(nb) for h in range(RWKV_HEADS)]
    idx = range(len(units))
    pick = lambda name: [rows_in[b][name][:, sl] for b, sl in units]
    at_h, rt_h, vh, bt_h, kt_h = pick("at"), pick("rt"), pick("v"), pick("bt"), pick("kt")
    x = [_dot_nt(jnp.concatenate([at_h[u], rt_h[u]], axis=0),
                 jnp.concatenate([bt_h[u], kt_h[u]], axis=0)) for u in idx]
    a_ak = [jnp.where(strict, x[u][:tt, tt:], 0.0) for u in idx]
    a_rb = [jnp.where(incl, x[u][tt:, :tt], 0.0) for u in idx]
    a_rk = [jnp.where(incl, x[u][tt:, tt:], 0.0) for u in idx]
    def bd(m0, m1):
        z0 = jnp.zeros((m0.shape[0], m1.shape[1]), m0.dtype)
        z1 = jnp.zeros((m1.shape[0], m0.shape[1]), m1.dtype)
        return jnp.concatenate([jnp.concatenate([m0, z0], axis=1),
                                jnp.concatenate([z1, m1], axis=1)], axis=0)

    def pair_dot(lhs, rhs):
        out = []
        for u in range(0, len(lhs), 2):
            res = _dot(jnp.concatenate([lhs[u], lhs[u + 1]], axis=1), bd(rhs[u], rhs[u + 1]))
            cut = rhs[u].shape[1]
            out += [res[:, :cut], res[:, cut:]]
        return out

    aak_v = pair_dot(a_ak, vh)
    eye = (ti == si).astype(F32)
    inv = None
    blk = 1
    while blk < RW_CHUNK:
        below = same & ((ti // blk) % 2 == 1) & ((si // blk) % 2 == 0) & (
            (ti // (2 * blk)) == (si // (2 * blk)))
        a21 = [jnp.where(below, x[u][:tt, :tt], 0.0) for u in idx]
        if inv is None:
            inv = [eye + a21[u] for u in idx]
        else:
            left = pair_dot(a21, inv)
            grow = pair_dot(inv, left)
            inv = [inv[u] + grow[u] for u in idx]
        blk *= 2
    yv = pair_dot(inv, [jnp.concatenate([at_h[u], aak_v[u]], axis=1) for u in idx])
    arb_y = pair_dot(a_rb, yv)
    ark_v = pair_dot(a_rk, vh)
    qt = [rt_h[u] + arb_y[u][:, :hd] for u in idx]
    o0 = [arb_y[u][:, hd:] + ark_v[u] for u in idx]
    yt = [yv[u].T for u in idx]
    bpb = [tile_chunks(t) for t in pick("bp")]
    kpb = [tile_chunks(t) for t in pick("kp")]
    g_all = [_dot(yt[u][:hd], bpb[u]) for u in idx]
    h_all = [_dot(jnp.concatenate([yt[u][hd:], vh[u].T], axis=1),
                  jnp.concatenate([bpb[u], kpb[u]], axis=0)) for u in idx]
    pt_h = pick("pt")
    pairs = range(0, len(units), 2)
    low_lanes = lax.broadcasted_iota(I32, (hd, 2 * hd), 1) < hd
    s = [jnp.concatenate([s_ref[u], s_ref[u + 1]], axis=1) for u in pairs]
    for n in range(nchunk):
        rows = slice(n * RW_CHUNK, (n + 1) * RW_CHUNK)
        cols = slice(n * hd, (n + 1) * hd)
        last = (n + 1) * RW_CHUNK - 1
        for j, u in enumerate(pairs):
            s_bd = jnp.concatenate([jnp.where(low_lanes, s[j], 0.0),
                                    jnp.where(low_lanes, 0.0, s[j])], axis=0)
            o_pair = _dot_nt(jnp.concatenate([qt[u][rows], qt[u + 1][rows]], axis=1), s_bd)
            for i in range(2):
                b, sl = units[u + i]
                obuf_ref[b, rows, sl] = o_pair[:, i * hd:(i + 1) * hd] + o0[u + i][rows]
        s = [s[j] * jnp.concatenate([pt_h[u][last:last + 1], pt_h[u + 1][last:last + 1]], axis=1)
             + _dot(s[j], bd(g_all[u][:, cols], g_all[u + 1][:, cols]))
             + jnp.concatenate([h_all[u][:, cols], h_all[u + 1][:, cols]], axis=1)
             for j, u in enumerate(pairs)]
    for j, u in enumerate(pairs):
        s_ref[u] = s[j][:, :hd]
        s_ref[u + 1] = s[j][:, hd:]

    for b in range(nb):
        o = obuf_ref[b]
        mean = segsum(o) * (1.0 / hd)
        oc = o - mean
        var = segsum(oc * oc) * (1.0 / hd)
        on = oc * lax.rsqrt(var + GN_EPS) * gn_g + gn_b
        o_ref[b] = ((on + rows_in[b]["bonus"]) * rows_in[b]["g"]).astype(BF16)


def _rwkv(p_r, mu, w0, w2, a0, a2, g2, k_k, k_a, r_k, gn_g, gn_b):
    bsz, seq, _ = p_r.shape
    tt = RW_TILE
    nb = RW_ROWS if bsz % RW_ROWS == 0 else 1
    w = RWKV_WIDTH
    vec = jnp.stack([w0, a0, k_k, k_a, r_k.reshape(w), gn_g, gn_b, jnp.zeros_like(w0)])
    t_idx = np.arange(tt)
    same = (t_idx[:, None] // RW_CHUNK) == (t_idx[None, :] // RW_CHUNK)
    ltri = jnp.asarray(same & (t_idx[None, :] <= t_idx[:, None]), BF16)
    ustr = jnp.asarray(same & (t_idx[None, :] > t_idx[:, None]), BF16)
    c_idx = np.arange(MXU_TILE)
    seg = jnp.asarray((c_idx[:, None] // HEAD_DIM) == (c_idx[None, :] // HEAD_DIM), BF16)
    full = lambda shape: pl.BlockSpec(shape, lambda b, i: (0,) * len(shape))
    per_tile = tt // SUBLANES
    return pl.pallas_call(
        _rwkv_kernel,
        grid=(bsz // nb, seq // tt),
        in_specs=[pl.BlockSpec((nb, tt, RWKV_COLS), lambda b, i: (b, i, 0)),
                  pl.BlockSpec((nb, SUBLANES, RWKV_COLS),
                               lambda b, i: (b, jnp.maximum(i * per_tile - 1, 0), 0)),
                  full((1, RWKV_COLS)), full((SUBLANES, w)),
                  full(w2.shape), full(a2.shape), full(g2.shape),
                  full((tt, tt)), full((tt, tt)), full((MXU_TILE, MXU_TILE))],
        out_specs=pl.BlockSpec((nb, tt, w), lambda b, i: (b, i, 0)),
        out_shape=jax.ShapeDtypeStruct((bsz, seq, w), BF16),
        scratch_shapes=[pltpu.VMEM((nb * RWKV_HEADS, HEAD_DIM, HEAD_DIM), F32),
                        pltpu.VMEM((nb, tt, w), F32)],
        compiler_params=_params(("parallel", "arbitrary")),
        name="rwkv7_mix",
    )(p_r, p_r, mu.reshape(1, RWKV_COLS), vec, w2.astype(BF16), a2.astype(BF16),
      g2.astype(BF16), ltri, ustr, seg)


def _dsa_kernel(qt_ref, k_ref, vt_ref, qit_ref, ki_ref, wit_ref, low_ref, o_ref,
                key_ref, bias_ref, acc_ref, m_ref, l_ref, *, topk):
    qb = pl.program_id(1)
    nq = Q_BLOCK
    kt_sz = bias_ref.shape[0]
    hd = HEAD_DIM
    start = qb * nq
    nkt = (start + nq + kt_sz - 1) // kt_sz
    col = lax.broadcasted_iota(I32, (1, nq), 1)
    lim = start + (col // CHUNK + 1) * CHUNK
    wit = wit_ref[0]
    qit = qit_ref[0]
    qi_cat = jnp.concatenate([qit[h * IDX_DIM:(h + 1) * IDX_DIM] for h in range(IDX_HEADS)],
                             axis=1)

    def score_tile(j, carry):
        off = pl.multiple_of(j * kt_sz, kt_sz)
        d = jnp.dot(ki_ref[0, pl.ds(off, kt_sz), :], qi_cat, preferred_element_type=F32)
        s = jnp.zeros((kt_sz, nq), F32)
        for h in range(IDX_HEADS):
            s = s + wit[h:h + 1] * jnp.maximum(d[:, h * nq:(h + 1) * nq], 0.0)
        s = s + 0.0
        bits = pltpu.bitcast(s, I32)
        key = bits ^ ((bits >> 31) & 0x7FFFFFFF)
        sidx = off + lax.broadcasted_iota(I32, (kt_sz, nq), 0)
        key_ref[pl.ds(off, kt_sz), :] = jnp.where(sidx < lim, key, INT_MIN)
        return carry

    lax.fori_loop(0, nkt, score_tile, 0)

    def count(pred):
        def body(j, c):
            off = pl.multiple_of(j * kt_sz, kt_sz)
            hit = jnp.where(pred(key_ref[pl.ds(off, kt_sz), :]), 1, 0)
            return c + hit.reshape(kt_sz // RED_ROWS, RED_ROWS, nq).sum(axis=0)
        c = lax.fori_loop(0, nkt, body, jnp.zeros((RED_ROWS, nq), I32))
        return jnp.sum(c, axis=0, keepdims=True)

    def slot_max(j, c):
        off = pl.multiple_of(j * kt_sz, kt_sz)
        return jnp.maximum(c, key_ref[pl.ds(off, kt_sz), :].reshape(kt_sz // SLOTS, SLOTS, nq)
                           .max(axis=0))

    smax = lax.fori_loop(0, nkt, slot_max, jnp.full((SLOTS, nq), INT_MIN, I32))
    lo = jnp.min(smax, axis=0, keepdims=True)
    hi = jnp.max(smax, axis=0, keepdims=True) + 1
    zero = jnp.zeros((1, nq), I32)
    positive = count(lambda kv: kv > zero) >= topk
    nonneg = count(lambda kv: kv >= zero) >= topk
    lo = jnp.where(positive, jnp.maximum(lo, 1), jnp.where(nonneg, 0, lo))
    hi = jnp.where(positive, hi, jnp.where(nonneg, 1, jnp.minimum(hi, 0)))
    n_lo = count(lambda kv: kv >= lo)
    half_width = (hi >> 1) - (lo >> 1)
    max_steps = jnp.max(32 - lax.clz(half_width)) + 1

    def unsettled(lo, hi, n_lo):
        return (n_lo != topk) & (hi > lo + 1)

    def bisect(bounds):
        lo, hi, n_lo = bounds
        live = unsettled(lo, hi, n_lo)
        mid = (lo >> 1) + (hi >> 1) + (lo & hi & 1)
        n_mid = count(lambda kv: kv >= mid)
        up = live & (n_mid >= topk)
        down = live & (n_mid < topk)
        return jnp.where(up, mid, lo), jnp.where(down, mid, hi), jnp.where(up, n_mid, n_lo)

    def search_cond(state):
        it, lo, hi, n_lo = state
        return (it < max_steps) & (jnp.max(unsettled(lo, hi, n_lo).astype(I32)) > 0)

    def search_body(state):
        it, lo, hi, n_lo = state
        for _ in range(SEARCH_UNROLL):
            lo, hi, n_lo = bisect((lo, hi, n_lo))
        return it + SEARCH_UNROLL, lo, hi, n_lo

    _, th, _, _ = lax.while_loop(search_cond, search_body, (jnp.int32(0), lo, hi, n_lo))
    need = jnp.where(th == INT_MIN, 0, topk - count(lambda kv: kv > th)).astype(F32)

    acc_ref[...] = jnp.zeros_like(acc_ref)
    m_ref[...] = jnp.full_like(m_ref, NEG_INF)
    l_ref[...] = jnp.zeros_like(l_ref)
    low = low_ref[...]
    npair = DSA_HEADS // 2
    top_rows = lax.broadcasted_iota(I32, (2 * hd, nq), 0) < hd
    qt = qt_ref[0].astype(F32)
    q_bd = []
    for p in range(npair):
        qp = qt[p * 2 * hd:(p + 1) * 2 * hd]
        q_bd.append(jnp.concatenate([jnp.where(top_rows, qp, 0.0), jnp.where(top_rows, 0.0, qp)],
                                    axis=1).astype(BF16))

    pad_rows = 2 * SUBLANES
    ones_rows = (lax.broadcasted_iota(I32, (pad_rows, kt_sz), 0) == 0).astype(BF16)

    def attn_tile(j, eq_before):
        off = pl.multiple_of(j * kt_sz, kt_sz)
        eq_run = eq_before
        for t in range(kt_sz // TRI_TILE):
            kv = key_ref[pl.ds(off + t * TRI_TILE, TRI_TILE), :]
            eq = kv == th
            pre = eq_run + jnp.dot(low, jnp.where(eq, 1.0, 0.0).astype(BF16),
                                   preferred_element_type=F32)
            sel = (kv > th) | (eq & (pre <= need))
            bias_ref[t * TRI_TILE:(t + 1) * TRI_TILE, :] = jnp.where(sel, 0.0, NEG_INF)
            eq_run = pre[TRI_TILE - 1:TRI_TILE]
        bias = bias_ref[...]

        def logits(p):
            half = kt_sz // 2
            return jnp.concatenate(
                [jnp.dot(k_ref[0, pl.ds(off + r * half, half), p * 2 * hd:(p + 1) * 2 * hd],
                         q_bd[p], preferred_element_type=F32) for r in range(2)], axis=0)

        st_next = logits(0)
        for p in range(npair):
            cols = slice(p * 2 * hd, (p + 1) * 2 * hd)
            st = st_next
            if p + 1 < npair:
                st_next = logits(p + 1)
            prs, scales = [], []
            for i in range(2):
                h = 2 * p + i
                s = st[:, i * nq:(i + 1) * nq] + bias
                m_old = m_ref[h:h + 1]
                m_new = jnp.maximum(m_old, _fold_rows(s, jnp.max))
                prs.append(jnp.exp2(s - m_new).astype(BF16))
                scales.append(jnp.exp2(m_old - m_new))
                m_ref[h:h + 1] = m_new
            lhs = jnp.concatenate([vt_ref[0, cols, pl.ds(off, kt_sz)], ones_rows], axis=0)
            pv = jnp.dot(lhs, jnp.concatenate(prs, axis=1),
                         preferred_element_type=F32)
            for i in range(2):
                h = 2 * p + i
                l_ref[h:h + 1] = (scales[i] * l_ref[h:h + 1]
                                  + pv[2 * hd:2 * hd + 1, i * nq:(i + 1) * nq])
            new = jnp.where(top_rows, pv[:2 * hd, :nq], pv[:2 * hd, nq:])
            acc_ref[p] = jnp.where(top_rows, scales[0], scales[1]) * acc_ref[p] + new
        return eq_run

    lax.fori_loop(0, nkt, attn_tile, jnp.zeros((1, nq), F32))
    for p in range(npair):
        inv = jnp.where(top_rows, 1.0 / l_ref[2 * p:2 * p + 1], 1.0 / l_ref[2 * p + 1:2 * p + 2])
        o_ref[0, :, p * 2 * hd:(p + 1) * 2 * hd] = (acc_ref[p] * inv).T.astype(BF16)


def _dsa(q_t, k, v_t, qi_t, ki, wi_t):
    bsz, seq, w = k.shape
    nq = Q_BLOCK
    topk = min(INDEX_TOPK, seq // 4)
    kt_sz = min(KEY_TILE, seq)
    assert SLOTS >= topk and kt_sz % SLOTS == 0 and kt_sz % TRI_TILE == 0
    t_idx = np.arange(TRI_TILE)
    low = jnp.asarray(t_idx[None, :] <= t_idx[:, None], BF16)
    cols = lambda n: pl.BlockSpec((1, n, nq), lambda b, i: (b, 0, i))
    return pl.pallas_call(
        functools.partial(_dsa_kernel, topk=topk),
        grid=(bsz, seq // nq),
        in_specs=[cols(w),
                  pl.BlockSpec((1, seq, w), lambda b, i: (b, 0, 0)),
                  pl.BlockSpec((1, w, seq), lambda b, i: (b, 0, 0)),
                  cols(IDX_HEADS * IDX_DIM),
                  pl.BlockSpec((1, seq, IDX_DIM), lambda b, i: (b, 0, 0)),
                  cols(SUBLANES),
                  pl.BlockSpec((TRI_TILE, TRI_TILE), lambda b, i: (0, 0))],
        out_specs=pl.BlockSpec((1, nq, w), lambda b, i: (b, i, 0)),
        out_shape=jax.ShapeDtypeStruct((bsz, seq, w), BF16),
        scratch_shapes=[pltpu.VMEM((seq, nq), I32),
                        pltpu.VMEM((kt_sz, nq), F32),
                        pltpu.VMEM((DSA_HEADS // 2, 2 * HEAD_DIM, nq), F32),
                        pltpu.VMEM((DSA_HEADS, nq), F32),
                        pltpu.VMEM((DSA_HEADS, nq), F32)],
        compiler_params=_params(("parallel", "arbitrary")),
        name="dsa_mix",
    )(q_t, k, v_t, qi_t, ki, wi_t, low)


POOL_HALO = 16


def _odd_layer_kernel(x_ref, xp_ref, mod_ref, win_ref, pw_ref, ps_ref, lng_ref, lnb_ref, ws_ref,
                      bs_ref, wout_ref, g_ref, b_ref, o_ref, y_ref):
    i = pl.program_id(1)
    m = mod_ref[0]
    sc = 1.0 + m[1:2]
    sh = m[0:1]
    x = x_ref[0]
    tm = x.shape[0]
    gd = POOL_GROUP_DIM
    p = jnp.dot((x * sc + sh).astype(BF16), win_ref[...], preferred_element_type=F32)
    prev = jnp.dot((xp_ref[0] * sc + sh).astype(BF16), win_ref[:, :POOL_WIDTH],
                   preferred_element_type=F32) * (i > 0).astype(F32)
    t_glob = (i * tm + lax.broadcasted_iota(I32, (tm, 1), 0)).astype(F32)
    scale = ps_ref[...]
    for gi, win in enumerate(POOL_WINDOWS):
        cols = slice(gi * gd, (gi + 1) * gd)
        xg = p[:, cols]
        s = jnp.concatenate([prev[:, cols], xg], axis=0)
        span = 1
        while span < win:
            s = s[span:] + s[:-span]
            span *= 2
        first = POOL_HALO + 1 - win
        pooled = s[first:first + tm] / jnp.minimum(t_glob + 1.0, float(win)) - xg
        y_ref[:, cols] = (_dot(pooled, pw_ref[gi]) * scale[:, cols]).astype(BF16)

    u = _gelu(p[:, POOL_WIDTH:POOL_WIDTH + SG_WIDTH])
    v = _ln(_gelu(p[:, POOL_WIDTH + SG_WIDTH:]), lng_ref[...], lnb_ref[...], LN_EPS)
    ti = lax.broadcasted_iota(I32, (SG_CHUNK, SG_CHUNK), 0)
    si = lax.broadcasted_iota(I32, (SG_CHUNK, SG_CHUNK), 1)
    bs = bs_ref[...]
    for gi in range(SG_GROUPS):
        cols = slice(gi * SG_GROUP_DIM, (gi + 1) * SG_GROUP_DIM)
        ws = jnp.where(si <= ti, ws_ref[gi], 0.0)
        for n in range(tm // SG_CHUNK):
            rows = slice(n * SG_CHUNK, (n + 1) * SG_CHUNK)
            z = _dot(ws, v[rows, cols]) + bs[:, gi:gi + 1]
            y_ref[rows, POOL_WIDTH + gi * SG_GROUP_DIM:POOL_WIDTH + (gi + 1) * SG_GROUP_DIM] = (
                u[rows, cols] * z).astype(BF16)

    y = jnp.dot(y_ref[...], wout_ref[...], preferred_element_type=F32)
    o_ref[0] = _ln(ALPHA * x + m[2:3] * y, g_ref[...], b_ref[...], LN_EPS)


def _odd_layer(x, mod, w_in, pool_w, pool_scale, sg_ln_g, sg_ln_b, sg_w, sg_b, w_out, ln_g, ln_b):
    bsz, seq, d = x.shape
    tm = 256
    per_tile = tm // POOL_HALO
    full = lambda shape: pl.BlockSpec(shape, lambda b, i: (0,) * len(shape))
    once = lambda shape: pl.BlockSpec(shape, lambda b, i: (0,) * len(shape),
                                      pipeline_mode=pl.Buffered(1))
    return pl.pallas_call(
        _odd_layer_kernel,
        grid=(bsz, seq // tm),
        in_specs=[pl.BlockSpec((1, tm, d), lambda b, i: (b, i, 0)),
                  pl.BlockSpec((1, POOL_HALO, d),
                               lambda b, i: (b, jnp.maximum(i * per_tile - 1, 0), 0)),
                  pl.BlockSpec((1, 6, d), lambda b, i: (b, 0, 0)),
                  once(w_in.shape), full(pool_w.shape), full((1, POOL_WIDTH)),
                  full((1, SG_WIDTH)), full((1, SG_WIDTH)), full(sg_w.shape),
                  full((SG_CHUNK, SG_GROUPS)), once(w_out.shape), full((1, d)), full((1, d))],
        out_specs=pl.BlockSpec((1, tm, d), lambda b, i: (b, i, 0)),
        out_shape=jax.ShapeDtypeStruct((bsz, seq, d), F32),
        scratch_shapes=[pltpu.VMEM((tm, d), BF16)],
        compiler_params=_params(("parallel", "parallel")),
        name="odd_layer_mix",
    )(x, x, mod, w_in.astype(BF16), pool_w.astype(BF16), pool_scale.reshape(1, -1),
      sg_ln_g.reshape(1, -1), sg_ln_b.reshape(1, -1), sg_w, sg_b.T, w_out.astype(BF16),
      ln_g.reshape(1, d), ln_b.reshape(1, d))


def _proj_ln_kernel(*refs, n_in, gate_row):
    a_refs = refs[:n_in]
    w_refs = refs[n_in:2 * n_in]
    x_ref, mod_ref, g_ref, b_ref, o_ref = refs[2 * n_in:]
    y = None
    for a_ref, w_ref in zip(a_refs, w_refs):
        part = jnp.dot(a_ref[0], w_ref[...], preferred_element_type=F32)
        y = part if y is None else y + part
    gate = mod_ref[0][gate_row:gate_row + 1]
    o_ref[0] = _ln(ALPHA * x_ref[0] + gate * y, g_ref[...], b_ref[...], LN_EPS)


def _proj_ln(acts, weights, x, mod, gate_row, ln_g, ln_b, name):
    bsz, seq, d = x.shape
    tm = 512
    n_in = len(acts)
    in_specs = [pl.BlockSpec((1, tm, a.shape[2]), lambda b, i: (b, i, 0)) for a in acts]
    in_specs += [pl.BlockSpec(w.shape, lambda b, i: (0, 0)) for w in weights]
    in_specs += [pl.BlockSpec((1, tm, d), lambda b, i: (b, i, 0)),
                 pl.BlockSpec((1, 6, d), lambda b, i: (b, 0, 0)),
                 pl.BlockSpec((1, d), lambda b, i: (0, 0)),
                 pl.BlockSpec((1, d), lambda b, i: (0, 0))]
    return pl.pallas_call(
        functools.partial(_proj_ln_kernel, n_in=n_in, gate_row=gate_row),
        grid=(bsz, seq // tm),
        in_specs=in_specs,
        out_specs=pl.BlockSpec((1, tm, d), lambda b, i: (b, i, 0)),
        out_shape=jax.ShapeDtypeStruct((bsz, seq, d), F32),
        compiler_params=_params(("parallel", "parallel")),
        name=name,
    )(*acts, *[w.astype(BF16) for w in weights], x, mod, ln_g.reshape(1, d), ln_b.reshape(1, d))


def _ffn_kernel(x_ref, xp_ref, mod_ref, wg_ref, wv_ref, cwg_ref, cwv_ref, cbg_ref, cbv_ref,
                wd_ref, g_ref, b_ref, o_ref):
    i = pl.program_id(1)
    m = mod_ref[0]
    sc = 1.0 + m[4:5]
    sh = m[3:4]
    x = x_ref[0]
    live = (i > 0).astype(F32)
    h = jnp.concatenate([(xp_ref[0] * sc + sh) * live, x * sc + sh], axis=0).astype(BF16)
    halo = SUBLANES

    def conv(w_ref, cw_ref, cb_ref):
        u = jnp.dot(h, w_ref[...], preferred_element_type=F32)
        cw = cw_ref[...]
        u1 = pltpu.roll(u, 1, 0)
        u2 = pltpu.roll(u, 2, 0)
        return (cb_ref[...] + u2[halo:] * cw[0:1] + u1[halo:] * cw[1:2] + u[halo:] * cw[2:3])

    gate = conv(wg_ref, cwg_ref, cbg_ref)
    val = conv(wv_ref, cwv_ref, cbv_ref)
    act = (gate * jax.nn.sigmoid(gate) * val).astype(BF16)
    y = jnp.dot(act, wd_ref[...], preferred_element_type=F32)
    o_ref[0] = _ln(ALPHA * x + m[5:6] * y, g_ref[...], b_ref[...], LN_EPS)


def _conv_ffn(x, mod, w_up, conv_w, conv_b, w_down, ln_g, ln_b):
    bsz, seq, d = x.shape
    tm = 512
    per_tile = tm // SUBLANES
    wb = w_up.astype(BF16)
    cb = conv_b.reshape(1, -1)
    once = dict(pipeline_mode=pl.Buffered(1))
    return pl.pallas_call(
        _ffn_kernel,
        grid=(bsz, seq // tm),
        in_specs=[pl.BlockSpec((1, tm, d), lambda b, i: (b, i, 0)),
                  pl.BlockSpec((1, SUBLANES, d),
                               lambda b, i: (b, jnp.maximum(i * per_tile - 1, 0), 0)),
                  pl.BlockSpec((1, 6, d), lambda b, i: (b, 0, 0)),
                  pl.BlockSpec((d, D_FF), lambda b, i: (0, 0), **once),
                  pl.BlockSpec((d, D_FF), lambda b, i: (0, 1), **once),
                  pl.BlockSpec((3, D_FF), lambda b, i: (0, 0), **once),
                  pl.BlockSpec((3, D_FF), lambda b, i: (0, 1), **once),
                  pl.BlockSpec((1, D_FF), lambda b, i: (0, 0), **once),
                  pl.BlockSpec((1, D_FF), lambda b, i: (0, 1), **once),
                  pl.BlockSpec((D_FF, d), lambda b, i: (0, 0), **once),
                  pl.BlockSpec((1, d), lambda b, i: (0, 0)),
                  pl.BlockSpec((1, d), lambda b, i: (0, 0))],
        out_specs=pl.BlockSpec((1, tm, d), lambda b, i: (b, i, 0)),
        out_shape=jax.ShapeDtypeStruct((bsz, seq, d), F32),
        compiler_params=_params(("parallel", "parallel")),
        name="conv_ffn_ln",
    )(x, x, mod, wb, wb, conv_w, conv_w, cb, cb, w_down.astype(BF16),
      ln_g.reshape(1, d), ln_b.reshape(1, d))


def _rope_tables(seq):
    inv = ROPE_THETA ** (-jnp.arange(0, ROPE_DIM, 2, dtype=F32) / ROPE_DIM)
    ang = jnp.arange(seq, dtype=F32)[:, None] * inv[None, :]
    cos, sin = jnp.cos(ang), jnp.sin(ang)
    half = ROPE_DIM // 2
    rest = HEAD_DIM - ROPE_DIM
    one = jnp.ones((seq, rest), F32)
    zero = jnp.zeros((seq, rest), F32)
    zh = jnp.zeros((seq, half), F32)
    head = lambda *parts: jnp.concatenate(parts * (LANES // HEAD_DIM), axis=1)
    return head(cos, cos, one), head(-sin, zh, zero), head(zh, sin, zero)


def kernel(x, c, ada_w, ada_b, ln_g, ln_b, ffn_w_up, ffn_conv_w, ffn_conv_b, ffn_w_down, ev_w_in, ev_w_out, rw_mu, rw_w0, rw_w2, rw_a0, rw_a2, rw_g2, rw_k_k, rw_k_a, rw_r_k, rw_gn_g, rw_gn_b, ik_ln_g, ik_ln_b, od_w_in, od_w_out, pool_w, pool_scale, sg_ln_g, sg_ln_b, sg_w, sg_b):
    seq = x.shape[1]
    tables = _rope_tables(seq)
    mods = _modulation(c, ada_w, ada_b)
    ev_w = _split_even_weights(ev_w_in)
    for layer in range(DEPTH):
        mod = mods[layer]
        if layer % 2 == 0:
            e = layer // 2
            p_r, q_t, k, v_t, qi_t, ki, wi_t = _even_in_proj(x, mod, [w[e] for w in ev_w],
                                                             ik_ln_g[e], ik_ln_b[e], tables)
            ya = _rwkv(p_r, rw_mu[e], rw_w0[e], rw_w2[e], rw_a0[e], rw_a2[e], rw_g2[e],
                       rw_k_k[e], rw_k_a[e], rw_r_k[e], rw_gn_g[e], rw_gn_b[e])
            yb = _dsa(q_t, k, v_t, qi_t, ki, wi_t)
            w_out = ev_w_out[e]
            x = _proj_ln([ya, yb], [w_out[:RWKV_WIDTH], w_out[RWKV_WIDTH:]], x, mod, 2,
                         ln_g[layer, 0], ln_b[layer, 0], "even_out_proj_ln")
        else:
            o = layer // 2
            x = _odd_layer(x, mod, od_w_in[o], pool_w[o], pool_scale[o], sg_ln_g[o], sg_ln_b[o],
                           sg_w[o], sg_b[o], od_w_out[o], ln_g[layer, 0], ln_b[layer, 0])
        x = _conv_ffn(x, mod, ffn_w_up[layer], ffn_conv_w[layer], ffn_conv_b[layer],
                      ffn_w_down[layer], ln_g[layer, 1], ln_b[layer, 1])
    return x
```

```python
import functools

import numpy as np
import jax
import jax.numpy as jnp
from jax import lax
from jax.experimental import pallas as pl
from jax.experimental.pallas import tpu as pltpu

F32 = jnp.float32
BF16 = jnp.bfloat16
I32 = jnp.int32

D_MODEL = 1024
DEPTH = 4
CHUNK = 64
HEAD_DIM = 64
RWKV_WIDTH = D_MODEL // 2
RWKV_HEADS = RWKV_WIDTH // HEAD_DIM
RWKV_LORA_W = 64
RWKV_LORA_A = 64
RWKV_LORA_G = 128
RWKV_COLS = 3 * RWKV_WIDTH + RWKV_LORA_W + RWKV_LORA_A + RWKV_LORA_G
DSA_WIDTH = D_MODEL - RWKV_WIDTH
DSA_HEADS = DSA_WIDTH // HEAD_DIM
IDX_HEADS = 4
IDX_DIM = 64
INDEX_TOPK = 256
Q_BLOCK = 128
ROPE_THETA = 500000.0
ROPE_DIM = HEAD_DIM // 4
POOL_WINDOWS = (2, 4, 8, 16)
POOL_WIDTH = D_MODEL // 2
POOL_GROUP_DIM = POOL_WIDTH // len(POOL_WINDOWS)
SG_WIDTH = D_MODEL - POOL_WIDTH
SG_GROUPS = 4
SG_GROUP_DIM = SG_WIDTH // SG_GROUPS
SG_CHUNK = 128
D_FF = 2816
ALPHA = (2.0 * DEPTH) ** 0.25
LN_EPS = 1e-5
GN_EPS = 64e-5
NEG_INF = -1e30
INT_MIN = -(2 ** 31)
LOG2E = 1.4426950408889634

LANES = 128
SUBLANES = 8
MXU_TILE = 256
VMEM_LIMIT = 56 * 1024 * 1024

MOD_COLS = 1536
SPLIT_ROWS = 256
IN_PROJ_ROWS = 256
ODD_ROWS = 256
OUT_PROJ_ROWS = 512
FFN_ROWS = 512
RW_TILE = 128
RW_CHUNK = 16
RW_ROWS = 4
KEY_TILE = 1024
TRI_TILE = 256
SLOTS = 256
PAD_IDX = 384

def _dot(a, b):
    return jnp.dot(a.astype(BF16), b.astype(BF16), preferred_element_type=F32)


def _dot_nt(a, b):
    return lax.dot_general(a.astype(BF16), b.astype(BF16), (((1,), (1,)), ((), ())),
                           preferred_element_type=F32)


def _dot_split(x, m01, terms):
    acc = None
    rem = x
    for _ in range(terms):
        piece = rem.astype(BF16)
        rem = rem - piece.astype(F32)
        part = jnp.dot(piece, m01, preferred_element_type=F32)
        acc = part if acc is None else acc + part
    return acc


def _dot_split_left(m01, x, terms):
    acc = None
    rem = x
    for _ in range(terms):
        piece = rem.astype(BF16)
        rem = rem - piece.astype(F32)
        part = jnp.dot(m01, piece, preferred_element_type=F32)
        acc = part if acc is None else acc + part
    return acc


def _ln(x, g, b, eps):
    mu = jnp.mean(x, axis=-1, keepdims=True)
    xc = x - mu
    var = jnp.mean(xc * xc, axis=-1, keepdims=True)
    return xc * lax.rsqrt(var + eps) * g + b


def _gelu(x):
    return 0.5 * x * (1.0 + lax.erf(x * 0.7071067811865476))


RED_ROWS = 64


def _fold_rows(x, op):
    rows, n = x.shape
    part = op(x.reshape(rows // RED_ROWS, RED_ROWS, n), axis=0)
    return op(part, axis=0, keepdims=True)


def _params(sem):
    return pltpu.CompilerParams(dimension_semantics=sem, vmem_limit_bytes=VMEM_LIMIT)


def _mod_kernel(c_ref, w_ref, b_ref, o_ref):
    c = c_ref[...]
    ca = c * jax.nn.sigmoid(c)
    o_ref[0] = jnp.dot(ca, w_ref[0], preferred_element_type=F32,
                       precision=lax.Precision.HIGHEST) + b_ref[0]


def _modulation(c, ada_w, ada_b):
    bsz, d = c.shape
    depth = ada_w.shape[0]
    n = ada_w.shape[2]
    tn = MOD_COLS
    rows = -(-bsz // SUBLANES) * SUBLANES
    c8 = jnp.pad(c, ((0, rows - bsz), (0, 0)))
    out = pl.pallas_call(
        _mod_kernel,
        grid=(depth, n // tn),
        in_specs=[pl.BlockSpec((rows, d), lambda l, j: (0, 0)),
                  pl.BlockSpec((1, d, tn), lambda l, j: (l, 0, j)),
                  pl.BlockSpec((1, 1, tn), lambda l, j: (l, 0, j))],
        out_specs=pl.BlockSpec((1, rows, tn), lambda l, j: (l, 0, j)),
        out_shape=jax.ShapeDtypeStruct((depth, rows, n), F32),
        compiler_params=_params(("arbitrary", "arbitrary")),
        name="adaln_mod",
    )(c8, ada_w, ada_b.reshape(depth, 1, n))
    return out[:, :bsz].reshape(depth, bsz, 6, d)


def _rope(x, cos_t, sin_a, sin_b):
    n = x.shape[1] // LANES
    rep = (lambda t: jnp.concatenate([t] * n, axis=1)) if n > 1 else (lambda t: t)
    width = x.shape[1]
    half = ROPE_DIM // 2
    return (x * rep(cos_t) + pltpu.roll(x, width - half, 1) * rep(sin_a)
            + pltpu.roll(x, half, 1) * rep(sin_b))


def _even_in_kernel(x_ref, mod_ref, wr_ref, wqkv_ref, widx_ref, cos_ref, sa_ref, sb_ref,
                    ikg_ref, ikb_ref,
                    pr_ref, qt_ref, k_ref, vt_ref, qit_ref, ki_ref, wit_ref):
    m = mod_ref[0]
    h = (x_ref[0] * (1.0 + m[1:2]) + m[0:1]).astype(BF16)
    pr_ref[0] = jnp.dot(h, wr_ref[...], preferred_element_type=F32)
    qkv = jnp.dot(h, wqkv_ref[...], preferred_element_type=F32)
    cos_t = cos_ref[...]
    sin_a = sa_ref[...]
    sin_b = sb_ref[...]
    w = DSA_WIDTH
    q = _rope(qkv[:, :w], cos_t, sin_a, sin_b) * (HEAD_DIM ** -0.5 * LOG2E)
    qt_ref[0] = q.T.astype(BF16)
    k_ref[0] = _rope(qkv[:, w:2 * w], cos_t, sin_a, sin_b).astype(BF16)
    vt_ref[0] = qkv[:, 2 * w:].T.astype(BF16)
    idx = jnp.dot(h, widx_ref[...], preferred_element_type=F32)
    nq = IDX_HEADS * IDX_DIM
    qit_ref[0] = _rope(idx[:, :nq], cos_t, sin_a, sin_b).T.astype(BF16)
    blk = idx[:, nq:nq + LANES]
    lane = lax.broadcasted_iota(I32, blk.shape, 1)
    is_k = lane < IDX_DIM
    mu = jnp.sum(jnp.where(is_k, blk, 0.0), axis=1, keepdims=True) * (1.0 / IDX_DIM)
    xc = jnp.where(is_k, blk - mu, 0.0)
    var = jnp.sum(xc * xc, axis=1, keepdims=True) * (1.0 / IDX_DIM)
    kin = xc * lax.rsqrt(var + LN_EPS) * ikg_ref[...] + ikb_ref[...]
    ki_ref[0] = _rope(kin, cos_t, sin_a, sin_b)[:, :IDX_DIM].astype(BF16)
    wit = (blk * (IDX_HEADS ** -0.5 * IDX_DIM ** -0.5)).T
    wit_ref[0] = wit[IDX_DIM:IDX_DIM + SUBLANES]


def _split_w_kernel(w_ref, wr_ref, wqkv_ref, widx_ref):
    w = w_ref[0]
    c1 = RWKV_COLS
    c2 = RWKV_COLS + 3 * DSA_WIDTH
    wr_ref[0] = w[:, :c1].astype(BF16)
    wqkv_ref[0] = w[:, c1:c2].astype(BF16)
    tail = w[:, c2:]
    zeros = jnp.zeros((w.shape[0], PAD_IDX - tail.shape[1]), F32)
    widx_ref[0] = jnp.concatenate([tail, zeros], axis=1).astype(BF16)


def _split_even_weights(ev_w_in):
    n_even, d, n = ev_w_in.shape
    tr = SPLIT_ROWS
    widths = (RWKV_COLS, 3 * DSA_WIDTH, PAD_IDX)
    return pl.pallas_call(
        _split_w_kernel,
        grid=(n_even, d // tr),
        in_specs=[pl.BlockSpec((1, tr, n), lambda e, i: (e, i, 0))],
        out_specs=[pl.BlockSpec((1, tr, wd), lambda e, i: (e, i, 0)) for wd in widths],
        out_shape=[jax.ShapeDtypeStruct((n_even, d, wd), BF16) for wd in widths],
        compiler_params=_params(("parallel", "parallel")),
        name="split_even_weights",
    )(ev_w_in)


def _even_in_proj(x, mod, weights, ik_g, ik_b, tables):
    bsz, seq, d = x.shape
    tm = IN_PROJ_ROWS
    w_r, w_qkv, w_idx = weights
    pad = LANES - IDX_DIM
    ikg = jnp.pad(ik_g, (0, pad)).reshape(1, LANES)
    ikb = jnp.pad(ik_b, (0, pad)).reshape(1, LANES)
    cos_t, sin_a, sin_b = tables
    full = lambda shape: pl.BlockSpec(shape, lambda b, i: (0,) * len(shape))
    tab = pl.BlockSpec((tm, LANES), lambda b, i: (i, 0))
    nq = IDX_HEADS * IDX_DIM
    rows = lambda n: pl.BlockSpec((1, tm, n), lambda b, i: (b, i, 0))
    cols = lambda n: pl.BlockSpec((1, n, tm), lambda b, i: (b, 0, i))
    return pl.pallas_call(
        _even_in_kernel,
        grid=(bsz, seq // tm),
        in_specs=[pl.BlockSpec((1, tm, d), lambda b, i: (b, i, 0)),
                  pl.BlockSpec((1, 6, d), lambda b, i: (b, 0, 0)),
                  full(w_r.shape), full(w_qkv.shape), full(w_idx.shape),
                  tab, tab, tab, full((1, LANES)), full((1, LANES))],
        out_specs=[rows(RWKV_COLS), cols(DSA_WIDTH), rows(DSA_WIDTH), cols(DSA_WIDTH),
                   cols(nq), rows(IDX_DIM), cols(SUBLANES)],
        out_shape=[jax.ShapeDtypeStruct((bsz, seq, RWKV_COLS), F32),
                   jax.ShapeDtypeStruct((bsz, DSA_WIDTH, seq), BF16),
                   jax.ShapeDtypeStruct((bsz, seq, DSA_WIDTH), BF16),
                   jax.ShapeDtypeStruct((bsz, DSA_WIDTH, seq), BF16),
                   jax.ShapeDtypeStruct((bsz, nq, seq), BF16),
                   jax.ShapeDtypeStruct((bsz, seq, IDX_DIM), BF16),
                   jax.ShapeDtypeStruct((bsz, SUBLANES, seq), F32)],
        compiler_params=_params(("parallel", "parallel")),
        name="even_in_proj",
    )(x, mod, w_r, w_qkv, w_idx, cos_t, sin_a, sin_b, ikg, ikb)


def _rwkv_kernel(p_ref, pp_ref, mu_ref, vec_ref, w2_ref, a2_ref, g2_ref, ltri_ref, ustr_ref,
                 seg_ref, o_ref, s_ref, obuf_ref):
    i = pl.program_id(1)

    @pl.when(i == 0)
    def _():
        s_ref[...] = jnp.zeros_like(s_ref)

    tt = RW_TILE
    w = RWKV_WIDTH
    hd = HEAD_DIM
    nb = p_ref.shape[0]
    nchunk = tt // RW_CHUNK
    seg = seg_ref[...]
    half_w = seg.shape[0]

    def segsum(t):
        return jnp.concatenate([_dot_split(t[:, c:c + half_w], seg, 2)
                                for c in range(0, w, half_w)], axis=1)

    vec = vec_ref[...]
    w0, a0, k_k, k_a, r_k, gn_g, gn_b = (vec[j:j + 1] for j in range(7))
    rowi = lax.broadcasted_iota(I32, (tt, 1), 0)
    live = (i > 0).astype(F32)

    def prepare(b):
        p = p_ref[b]
        prow = pp_ref[b][SUBLANES - 1:SUBLANES] * live
        xprev = jnp.where(rowi == 0, prow, pltpu.roll(p, 1, 0))
        ps = p + (xprev - p) * mu_ref[...]
        r = ps[:, :w]
        k = ps[:, w:2 * w]
        v = ps[:, 2 * w:3 * w]
        o1 = 3 * w
        wd = ps[:, o1:o1 + RWKV_LORA_W]
        ad = ps[:, o1 + RWKV_LORA_W:o1 + RWKV_LORA_W + RWKV_LORA_A]
        gd = ps[:, o1 + RWKV_LORA_W + RWKV_LORA_A:]
        y = -(w0 + _dot(jnp.tanh(wd), w2_ref[...]))
        softplus = jnp.maximum(y, 0.0) + jnp.log1p(jnp.exp(-jnp.abs(y)))
        logw = -jnp.exp(-softplus - 0.5)
        a = jax.nn.sigmoid(a0 + _dot(ad, a2_ref[...]))
        g = _dot(jax.nn.sigmoid(gd), g2_ref[...])
        kk = k * k_k
        kk = kk / jnp.maximum(jnp.sqrt(segsum(kk * kk)), 1e-12)
        k2 = k * (1.0 + (a - 1.0) * k_a)
        bonus = segsum(r * k2 * r_k) * v
        cum = _dot_split_left(ltri_ref[...], logw, 3)
        rem = _dot_split_left(ustr_ref[...], logw, 3)
        pt = jnp.exp(cum)
        ipt = jnp.exp(-cum)
        erem = jnp.exp(rem)
        kka = kk * a
        return dict(at=-kk * jnp.exp(cum - logw), rt=r * pt, bt=kka * ipt, kt=k2 * ipt,
                    bp=kka * erem, kp=k2 * erem, v=v, pt=pt, bonus=bonus, g=g)

    rows_in = [prepare(b) for b in range(nb)]

    ti = lax.broadcasted_iota(I32, (tt, tt), 0)
    si = lax.broadcasted_iota(I32, (tt, tt), 1)
    same = (ti // RW_CHUNK) == (si // RW_CHUNK)
    strict = same & (si < ti)
    incl = same & (si <= ti)
    tb = lax.broadcasted_iota(I32, (tt, nchunk * hd), 0)
    cb = lax.broadcasted_iota(I32, (tt, nchunk * hd), 1)
    blkmask = (tb // RW_CHUNK) == (cb // hd)
    tile_chunks = lambda t: jnp.where(blkmask, jnp.concatenate([t] * nchunk, axis=1), 0.0)

    units = [(b, slice(h * hd, (h + 1) * hd)) for b in range(nb) for h in range(RWKV_HEADS)]
    idx = range(len(units))
    pick = lambda name: [rows_in[b][name][:, sl] for b, sl in units]
    at_h, rt_h, vh, bt_h, kt_h = pick("at"), pick("rt"), pick("v"), pick("bt"), pick("kt")
    x = [_dot_nt(jnp.concatenate([at_h[u], rt_h[u]], axis=0),
                 jnp.concatenate([bt_h[u], kt_h[u]], axis=0)) for u in idx]
    a_ak = [jnp.where(strict, x[u][:tt, tt:], 0.0) for u in idx]
    a_rb = [jnp.where(incl, x[u][tt:, :tt], 0.0) for u in idx]
    a_rk = [jnp.where(incl, x[u][tt:, tt:], 0.0) for u in idx]
    def bd(m0, m1):
        z0 = jnp.zeros((m0.shape[0], m1.shape[1]), m0.dtype)
        z1 = jnp.zeros((m1.shape[0], m0.shape[1]), m1.dtype)
        return jnp.concatenate([jnp.concatenate([m0, z0], axis=1),
                                jnp.concatenate([z1, m1], axis=1)], axis=0)

    def pair_dot(lhs, rhs):
        out = []
        for u in range(0, len(lhs), 2):
            res = _dot(jnp.concatenate([lhs[u], lhs[u + 1]], axis=1), bd(rhs[u], rhs[u + 1]))
            cut = rhs[u].shape[1]
            out += [res[:, :cut], res[:, cut:]]
        return out

    aak_v = pair_dot(a_ak, vh)
    eye = (ti == si).astype(F32)
    inv = None
    blk = 1
    while blk < RW_CHUNK:
        below = same & ((ti // blk) % 2 == 1) & ((si // blk) % 2 == 0) & (
            (ti // (2 * blk)) == (si // (2 * blk)))
        a21 = [jnp.where(below, x[u][:tt, :tt], 0.0) for u in idx]
        if inv is None:
            inv = [eye + a21[u] for u in idx]
        else:
            left = pair_dot(a21, inv)
            grow = pair_dot(inv, left)
            inv = [inv[u] + grow[u] for u in idx]
        blk *= 2
    yv = pair_dot(inv, [jnp.concatenate([at_h[u], aak_v[u]], axis=1) for u in idx])
    arb_y = pair_dot(a_rb, yv)
    ark_v = pair_dot(a_rk, vh)
    qt = [rt_h[u] + arb_y[u][:, :hd] for u in idx]
    o0 = [arb_y[u][:, hd:] + ark_v[u] for u in idx]
    yt = [yv[u].T for u in idx]
    bpb = [tile_chunks(t) for t in pick("bp")]
    kpb = [tile_chunks(t) for t in pick("kp")]
    g_all = [_dot(yt[u][:hd], bpb[u]) for u in idx]
    h_all = [_dot(jnp.concatenate([yt[u][hd:], vh[u].T], axis=1),
                  jnp.concatenate([bpb[u], kpb[u]], axis=0)) for u in idx]
    pt_h = pick("pt")
    pairs = range(0, len(units), 2)
    low_lanes = lax.broadcasted_iota(I32, (hd, 2 * hd), 1) < hd
    s = [jnp.concatenate([s_ref[u], s_ref[u + 1]], axis=1) for u in pairs]
    for n in range(nchunk):
        rows = slice(n * RW_CHUNK, (n + 1) * RW_CHUNK)
        cols = slice(n * hd, (n + 1) * hd)
        last = (n + 1) * RW_CHUNK - 1
        for j, u in enumerate(pairs):
            s_bd = jnp.concatenate([jnp.where(low_lanes, s[j], 0.0),
                                    jnp.where(low_lanes, 0.0, s[j])], axis=0)
            o_pair = _dot_nt(jnp.concatenate([qt[u][rows], qt[u + 1][rows]], axis=1), s_bd)
            for i in range(2):
                b, sl = units[u + i]
                obuf_ref[b, rows, sl] = o_pair[:, i * hd:(i + 1) * hd] + o0[u + i][rows]
        s = [s[j] * jnp.concatenate([pt_h[u][last:last + 1], pt_h[u + 1][last:last + 1]], axis=1)
             + _dot(s[j], bd(g_all[u][:, cols], g_all[u + 1][:, cols]))
             + jnp.concatenate([h_all[u][:, cols], h_all[u + 1][:, cols]], axis=1)
             for j, u in enumerate(pairs)]
    for j, u in enumerate(pairs):
        s_ref[u] = s[j][:, :hd]
        s_ref[u + 1] = s[j][:, hd:]

    for b in range(nb):
        o = obuf_ref[b]
        mean = segsum(o) * (1.0 / hd)
        oc = o - mean
        var = segsum(oc * oc) * (1.0 / hd)
        on = oc * lax.rsqrt(var + GN_EPS) * gn_g + gn_b
        o_ref[b] = ((on + rows_in[b]["bonus"]) * rows_in[b]["g"]).astype(BF16)


def _rwkv(p_r, mu, w0, w2, a0, a2, g2, k_k, k_a, r_k, gn_g, gn_b):
    bsz, seq, _ = p_r.shape
    tt = RW_TILE
    nb = RW_ROWS if bsz % RW_ROWS == 0 else 1
    w = RWKV_WIDTH
    vec = jnp.stack([w0, a0, k_k, k_a, r_k.reshape(w), gn_g, gn_b, jnp.zeros_like(w0)])
    t_idx = np.arange(tt)
    same = (t_idx[:, None] // RW_CHUNK) == (t_idx[None, :] // RW_CHUNK)
    ltri = jnp.asarray(same & (t_idx[None, :] <= t_idx[:, None]), BF16)
    ustr = jnp.asarray(same & (t_idx[None, :] > t_idx[:, None]), BF16)
    c_idx = np.arange(MXU_TILE)
    seg = jnp.asarray((c_idx[:, None] // HEAD_DIM) == (c_idx[None, :] // HEAD_DIM), BF16)
    full = lambda shape: pl.BlockSpec(shape, lambda b, i: (0,) * len(shape))
    per_tile = tt // SUBLANES
    return pl.pallas_call(
        _rwkv_kernel,
        grid=(bsz // nb, seq // tt),
        in_specs=[pl.BlockSpec((nb, tt, RWKV_COLS), lambda b, i: (b, i, 0)),
                  pl.BlockSpec((nb, SUBLANES, RWKV_COLS),
                               lambda b, i: (b, jnp.maximum(i * per_tile - 1, 0), 0)),
                  full((1, RWKV_COLS)), full((SUBLANES, w)),
                  full(w2.shape), full(a2.shape), full(g2.shape),
                  full((tt, tt)), full((tt, tt)), full((MXU_TILE, MXU_TILE))],
        out_specs=pl.BlockSpec((nb, tt, w), lambda b, i: (b, i, 0)),
        out_shape=jax.ShapeDtypeStruct((bsz, seq, w), BF16),
        scratch_shapes=[pltpu.VMEM((nb * RWKV_HEADS, HEAD_DIM, HEAD_DIM), F32),
                        pltpu.VMEM((nb, tt, w), F32)],
        compiler_params=_params(("parallel", "arbitrary")),
        name="rwkv7_mix",
    )(p_r, p_r, mu.reshape(1, RWKV_COLS), vec, w2.astype(BF16), a2.astype(BF16),
      g2.astype(BF16), ltri, ustr, seg)


def _dsa_kernel(qt_ref, k_ref, vt_ref, qit_ref, ki_ref, wit_ref, low_ref, o_ref,
                key_ref, bias_ref, acc_ref, m_ref, l_ref, *, topk):
    qb = pl.program_id(1)
    nq = Q_BLOCK
    kt_sz = bias_ref.shape[0]
    hd = HEAD_DIM
    start = qb * nq
    nkt = (start + nq + kt_sz - 1) // kt_sz
    col = lax.broadcasted_iota(I32, (1, nq), 1)
    lim = start + (col // CHUNK + 1) * CHUNK
    wit = wit_ref[0]
    qit = qit_ref[0]
    qi_cat = jnp.concatenate([qit[h * IDX_DIM:(h + 1) * IDX_DIM] for h in range(IDX_HEADS)],
                             axis=1)

    def score_tile(j, carry):
        off = pl.multiple_of(j * kt_sz, kt_sz)
        d = jnp.dot(ki_ref[0, pl.ds(off, kt_sz), :], qi_cat, preferred_element_type=F32)
        s = jnp.zeros((kt_sz, nq), F32)
        for h in range(IDX_HEADS):
            s = s + wit[h:h + 1] * jnp.maximum(d[:, h * nq:(h + 1) * nq], 0.0)
        s = s + 0.0
        bits = pltpu.bitcast(s, I32)
        key = bits ^ ((bits >> 31) & 0x7FFFFFFF)
        sidx = off + lax.broadcasted_iota(I32, (kt_sz, nq), 0)
        key_ref[pl.ds(off, kt_sz), :] = jnp.where(sidx < lim, key, INT_MIN)
        return carry

    lax.fori_loop(0, nkt, score_tile, 0)

    def count(pred):
        def body(j, c):
            off = pl.multiple_of(j * kt_sz, kt_sz)
            hit = jnp.where(pred(key_ref[pl.ds(off, kt_sz), :]), 1, 0)
            return c + hit.reshape(kt_sz // RED_ROWS, RED_ROWS, nq).sum(axis=0)
        c = lax.fori_loop(0, nkt, body, jnp.zeros((RED_ROWS, nq), I32))
        return jnp.sum(c, axis=0, keepdims=True)

    def slot_max(j, c):
        off = pl.multiple_of(j * kt_sz, kt_sz)
        return jnp.maximum(c, key_ref[pl.ds(off, kt_sz), :].reshape(kt_sz // SLOTS, SLOTS, nq)
                           .max(axis=0))

    smax = lax.fori_loop(0, nkt, slot_max, jnp.full((SLOTS, nq), INT_MIN, I32))
    lo = jnp.min(smax, axis=0, keepdims=True)
    hi = jnp.max(smax, axis=0, keepdims=True) + 1
    zero = jnp.zeros((1, nq), I32)
    nonneg = count(lambda kv: kv >= zero) >= topk
    lo = jnp.where(nonneg, jnp.maximum(lo, 0), lo)
    hi = jnp.where(nonneg, hi, jnp.minimum(hi, 0))
    half_width = (hi >> 1) - (lo >> 1)
    steps = jnp.max(32 - lax.clz(half_width)) + 1

    def bisect(_, bounds):
        lo, hi = bounds
        mid = (lo >> 1) + (hi >> 1) + (lo & hi & 1)
        take = count(lambda kv: kv >= mid) >= topk
        return jnp.where(take, mid, lo), jnp.where(take, hi, mid)

    th, _ = lax.fori_loop(0, steps, bisect, (lo, hi))
    need = jnp.where(th == INT_MIN, 0, topk - count(lambda kv: kv > th)).astype(F32)

    acc_ref[...] = jnp.zeros_like(acc_ref)
    m_ref[...] = jnp.full_like(m_ref, NEG_INF)
    l_ref[...] = jnp.zeros_like(l_ref)
    low = low_ref[...]
    npair = DSA_HEADS // 2
    top_rows = lax.broadcasted_iota(I32, (2 * hd, nq), 0) < hd
    qt = qt_ref[0].astype(F32)
    q_bd = []
    for p in range(npair):
        qp = qt[p * 2 * hd:(p + 1) * 2 * hd]
        q_bd.append(jnp.concatenate([jnp.where(top_rows, qp, 0.0), jnp.where(top_rows, 0.0, qp)],
                                    axis=1).astype(BF16))

    pad_rows = 2 * SUBLANES
    ones_rows = (lax.broadcasted_iota(I32, (pad_rows, kt_sz), 0) == 0).astype(BF16)

    def attn_tile(j, eq_before):
        off = pl.multiple_of(j * kt_sz, kt_sz)
        eq_run = eq_before
        for t in range(kt_sz // TRI_TILE):
            kv = key_ref[pl.ds(off + t * TRI_TILE, TRI_TILE), :]
            eq = kv == th
            pre = eq_run + jnp.dot(low, jnp.where(eq, 1.0, 0.0).astype(BF16),
                                   preferred_element_type=F32)
            sel = (kv > th) | (eq & (pre <= need))
            bias_ref[t * TRI_TILE:(t + 1) * TRI_TILE, :] = jnp.where(sel, 0.0, NEG_INF)
            eq_run = pre[TRI_TILE - 1:TRI_TILE]
        bias = bias_ref[...]

        def logits(p):
            half = kt_sz // 2
            return jnp.concatenate(
                [jnp.dot(k_ref[0, pl.ds(off + r * half, half), p * 2 * hd:(p + 1) * 2 * hd],
                         q_bd[p], preferred_element_type=F32) for r in range(2)], axis=0)

        st_next = logits(0)
        for p in range(npair):
            cols = slice(p * 2 * hd, (p + 1) * 2 * hd)
            st = st_next
            if p + 1 < npair:
                st_next = logits(p + 1)
            prs, scales = [], []
            for i in range(2):
                h = 2 * p + i
                s = st[:, i * nq:(i + 1) * nq] + bias
                m_old = m_ref[h:h + 1]
                m_new = jnp.maximum(m_old, _fold_rows(s, jnp.max))
                prs.append(jnp.exp2(s - m_new).astype(BF16))
                scales.append(jnp.exp2(m_old - m_new))
                m_ref[h:h + 1] = m_new
            lhs = jnp.concatenate([vt_ref[0, cols, pl.ds(off, kt_sz)], ones_rows], axis=0)
            pv = jnp.dot(lhs, jnp.concatenate(prs, axis=1),
                         preferred_element_type=F32)
            for i in range(2):
                h = 2 * p + i
                l_ref[h:h + 1] = (scales[i] * l_ref[h:h + 1]
                                  + pv[2 * hd:2 * hd + 1, i * nq:(i + 1) * nq])
            new = jnp.where(top_rows, pv[:2 * hd, :nq], pv[:2 * hd, nq:])
            acc_ref[p] = jnp.where(top_rows, scales[0], scales[1]) * acc_ref[p] + new
        return eq_run

    lax.fori_loop(0, nkt, attn_tile, jnp.zeros((1, nq), F32))
    for p in range(npair):
        inv = jnp.where(top_rows, 1.0 / l_ref[2 * p:2 * p + 1], 1.0 / l_ref[2 * p + 1:2 * p + 2])
        o_ref[0, :, p * 2 * hd:(p + 1) * 2 * hd] = (acc_ref[p] * inv).T.astype(BF16)


def _dsa(q_t, k, v_t, qi_t, ki, wi_t):
    bsz, seq, w = k.shape
    nq = Q_BLOCK
    topk = min(INDEX_TOPK, seq // 4)
    kt_sz = min(KEY_TILE, seq)
    assert SLOTS >= topk and kt_sz % SLOTS == 0 and kt_sz % TRI_TILE == 0
    t_idx = np.arange(TRI_TILE)
    low = jnp.asarray(t_idx[None, :] <= t_idx[:, None], BF16)
    cols = lambda n: pl.BlockSpec((1, n, nq), lambda b, i: (b, 0, i))
    return pl.pallas_call(
        functools.partial(_dsa_kernel, topk=topk),
        grid=(bsz, seq // nq),
        in_specs=[cols(w),
                  pl.BlockSpec((1, seq, w), lambda b, i: (b, 0, 0)),
                  pl.BlockSpec((1, w, seq), lambda b, i: (b, 0, 0)),
                  cols(IDX_HEADS * IDX_DIM),
                  pl.BlockSpec((1, seq, IDX_DIM), lambda b, i: (b, 0, 0)),
                  cols(SUBLANES),
                  pl.BlockSpec((TRI_TILE, TRI_TILE), lambda b, i: (0, 0))],
        out_specs=pl.BlockSpec((1, nq, w), lambda b, i: (b, i, 0)),
        out_shape=jax.ShapeDtypeStruct((bsz, seq, w), BF16),
        scratch_shapes=[pltpu.VMEM((seq, nq), I32),
                        pltpu.VMEM((kt_sz, nq), F32),
                        pltpu.VMEM((DSA_HEADS // 2, 2 * HEAD_DIM, nq), F32),
                        pltpu.VMEM((DSA_HEADS, nq), F32),
                        pltpu.VMEM((DSA_HEADS, nq), F32)],
        compiler_params=_params(("parallel", "arbitrary")),
        name="dsa_mix",
    )(q_t, k, v_t, qi_t, ki, wi_t, low)


POOL_HALO = 16


def _odd_layer_kernel(x_ref, xp_ref, mod_ref, win_ref, pw_ref, ps_ref, lng_ref, lnb_ref, ws_ref,
                      bs_ref, wout_ref, g_ref, b_ref, o_ref, y_ref):
    i = pl.program_id(1)
    m = mod_ref[0]
    sc = 1.0 + m[1:2]
    sh = m[0:1]
    x = x_ref[0]
    tm = x.shape[0]
    gd = POOL_GROUP_DIM
    p = jnp.dot((x * sc + sh).astype(BF16), win_ref[...], preferred_element_type=F32)
    prev = jnp.dot((xp_ref[0] * sc + sh).astype(BF16), win_ref[:, :POOL_WIDTH],
                   preferred_element_type=F32) * (i > 0).astype(F32)
    t_glob = (i * tm + lax.broadcasted_iota(I32, (tm, 1), 0)).astype(F32)
    scale = ps_ref[...]
    for gi, win in enumerate(POOL_WINDOWS):
        cols = slice(gi * gd, (gi + 1) * gd)
        xg = p[:, cols]
        s = jnp.concatenate([prev[:, cols], xg], axis=0)
        span = 1
        while span < win:
            s = s[span:] + s[:-span]
            span *= 2
        first = POOL_HALO + 1 - win
        pooled = s[first:first + tm] / jnp.minimum(t_glob + 1.0, float(win)) - xg
        y_ref[:, cols] = (_dot(pooled, pw_ref[gi]) * scale[:, cols]).astype(BF16)

    u = _gelu(p[:, POOL_WIDTH:POOL_WIDTH + SG_WIDTH])
    v = _ln(_gelu(p[:, POOL_WIDTH + SG_WIDTH:]), lng_ref[...], lnb_ref[...], LN_EPS)
    ti = lax.broadcasted_iota(I32, (SG_CHUNK, SG_CHUNK), 0)
    si = lax.broadcasted_iota(I32, (SG_CHUNK, SG_CHUNK), 1)
    bs = bs_ref[...]
    for gi in range(SG_GROUPS):
        cols = slice(gi * SG_GROUP_DIM, (gi + 1) * SG_GROUP_DIM)
        ws = jnp.where(si <= ti, ws_ref[gi], 0.0)
        for n in range(tm // SG_CHUNK):
            rows = slice(n * SG_CHUNK, (n + 1) * SG_CHUNK)
            z = _dot(ws, v[rows, cols]) + bs[:, gi:gi + 1]
            y_ref[rows, POOL_WIDTH + gi * SG_GROUP_DIM:POOL_WIDTH + (gi + 1) * SG_GROUP_DIM] = (
                u[rows, cols] * z).astype(BF16)

    y = jnp.dot(y_ref[...], wout_ref[...], preferred_element_type=F32)
    o_ref[0] = _ln(ALPHA * x + m[2:3] * y, g_ref[...], b_ref[...], LN_EPS)


def _odd_layer(x, mod, w_in, pool_w, pool_scale, sg_ln_g, sg_ln_b, sg_w, sg_b, w_out, ln_g, ln_b):
    bsz, seq, d = x.shape
    tm = ODD_ROWS
    per_tile = tm // POOL_HALO
    full = lambda shape: pl.BlockSpec(shape, lambda b, i: (0,) * len(shape))
    once = lambda shape: pl.BlockSpec(shape, lambda b, i: (0,) * len(shape),
                                      pipeline_mode=pl.Buffered(1))
    return pl.pallas_call(
        _odd_layer_kernel,
        grid=(bsz, seq // tm),
        in_specs=[pl.BlockSpec((1, tm, d), lambda b, i: (b, i, 0)),
                  pl.BlockSpec((1, POOL_HALO, d),
                               lambda b, i: (b, jnp.maximum(i * per_tile - 1, 0), 0)),
                  pl.BlockSpec((1, 6, d), lambda b, i: (b, 0, 0)),
                  once(w_in.shape), full(pool_w.shape), full((1, POOL_WIDTH)),
                  full((1, SG_WIDTH)), full((1, SG_WIDTH)), full(sg_w.shape),
                  full((SG_CHUNK, SG_GROUPS)), once(w_out.shape), full((1, d)), full((1, d))],
        out_specs=pl.BlockSpec((1, tm, d), lambda b, i: (b, i, 0)),
        out_shape=jax.ShapeDtypeStruct((bsz, seq, d), F32),
        scratch_shapes=[pltpu.VMEM((tm, d), BF16)],
        compiler_params=_params(("parallel", "parallel")),
        name="odd_layer_mix",
    )(x, x, mod, w_in.astype(BF16), pool_w.astype(BF16), pool_scale.reshape(1, -1),
      sg_ln_g.reshape(1, -1), sg_ln_b.reshape(1, -1), sg_w, sg_b.T, w_out.astype(BF16),
      ln_g.reshape(1, d), ln_b.reshape(1, d))


def _proj_ln_kernel(*refs, n_in, gate_row):
    a_refs = refs[:n_in]
    w_refs = refs[n_in:2 * n_in]
    x_ref, mod_ref, g_ref, b_ref, o_ref = refs[2 * n_in:]
    y = None
    for a_ref, w_ref in zip(a_refs, w_refs):
        part = jnp.dot(a_ref[0], w_ref[...], preferred_element_type=F32)
        y = part if y is None else y + part
    gate = mod_ref[0][gate_row:gate_row + 1]
    o_ref[0] = _ln(ALPHA * x_ref[0] + gate * y, g_ref[...], b_ref[...], LN_EPS)


def _proj_ln(acts, weights, x, mod, gate_row, ln_g, ln_b, name):
    bsz, seq, d = x.shape
    tm = OUT_PROJ_ROWS
    n_in = len(acts)
    in_specs = [pl.BlockSpec((1, tm, a.shape[2]), lambda b, i: (b, i, 0)) for a in acts]
    in_specs += [pl.BlockSpec(w.shape, lambda b, i: (0, 0)) for w in weights]
    in_specs += [pl.BlockSpec((1, tm, d), lambda b, i: (b, i, 0)),
                 pl.BlockSpec((1, 6, d), lambda b, i: (b, 0, 0)),
                 pl.BlockSpec((1, d), lambda b, i: (0, 0)),
                 pl.BlockSpec((1, d), lambda b, i: (0, 0))]
    return pl.pallas_call(
        functools.partial(_proj_ln_kernel, n_in=n_in, gate_row=gate_row),
        grid=(bsz, seq // tm),
        in_specs=in_specs,
        out_specs=pl.BlockSpec((1, tm, d), lambda b, i: (b, i, 0)),
        out_shape=jax.ShapeDtypeStruct((bsz, seq, d), F32),
        compiler_params=_params(("parallel", "parallel")),
        name=name,
    )(*acts, *[w.astype(BF16) for w in weights], x, mod, ln_g.reshape(1, d), ln_b.reshape(1, d))


def _ffn_kernel(x_ref, xp_ref, mod_ref, wg_ref, wv_ref, cwg_ref, cwv_ref, cbg_ref, cbv_ref,
                wd_ref, g_ref, b_ref, o_ref):
    i = pl.program_id(1)
    m = mod_ref[0]
    sc = 1.0 + m[4:5]
    sh = m[3:4]
    x = x_ref[0]
    live = (i > 0).astype(F32)
    h = jnp.concatenate([(xp_ref[0] * sc + sh) * live, x * sc + sh], axis=0).astype(BF16)
    halo = SUBLANES

    def conv(w_ref, cw_ref, cb_ref):
        u = jnp.dot(h, w_ref[...], preferred_element_type=F32)
        cw = cw_ref[...]
        u1 = pltpu.roll(u, 1, 0)
        u2 = pltpu.roll(u, 2, 0)
        return (cb_ref[...] + u2[halo:] * cw[0:1] + u1[halo:] * cw[1:2] + u[halo:] * cw[2:3])

    gate = conv(wg_ref, cwg_ref, cbg_ref)
    val = conv(wv_ref, cwv_ref, cbv_ref)
    act = (gate * jax.nn.sigmoid(gate) * val).astype(BF16)
    y = jnp.dot(act, wd_ref[...], preferred_element_type=F32)
    o_ref[0] = _ln(ALPHA * x + m[5:6] * y, g_ref[...], b_ref[...], LN_EPS)


def _conv_ffn(x, mod, w_up, conv_w, conv_b, w_down, ln_g, ln_b):
    bsz, seq, d = x.shape
    tm = FFN_ROWS
    per_tile = tm // SUBLANES
    wb = w_up.astype(BF16)
    cb = conv_b.reshape(1, -1)
    once = dict(pipeline_mode=pl.Buffered(1))
    return pl.pallas_call(
        _ffn_kernel,
        grid=(bsz, seq // tm),
        in_specs=[pl.BlockSpec((1, tm, d), lambda b, i: (b, i, 0)),
                  pl.BlockSpec((1, SUBLANES, d),
                               lambda b, i: (b, jnp.maximum(i * per_tile - 1, 0), 0)),
                  pl.BlockSpec((1, 6, d), lambda b, i: (b, 0, 0)),
                  pl.BlockSpec((d, D_FF), lambda b, i: (0, 0), **once),
                  pl.BlockSpec((d, D_FF), lambda b, i: (0, 1), **once),
                  pl.BlockSpec((3, D_FF), lambda b, i: (0, 0), **once),
                  pl.BlockSpec((3, D_FF), lambda b, i: (0, 1), **once),
                  pl.BlockSpec((1, D_FF), lambda b, i: (0, 0), **once),
                  pl.BlockSpec((1, D_FF), lambda b, i: (0, 1), **once),
                  pl.BlockSpec((D_FF, d), lambda b, i: (0, 0), **once),
                  pl.BlockSpec((1, d), lambda b, i: (0, 0)),
                  pl.BlockSpec((1, d), lambda b, i: (0, 0))],
        out_specs=pl.BlockSpec((1, tm, d), lambda b, i: (b, i, 0)),
        out_shape=jax.ShapeDtypeStruct((bsz, seq, d), F32),
        compiler_params=_params(("parallel", "parallel")),
        name="conv_ffn_ln",
    )(x, x, mod, wb, wb, conv_w, conv_w, cb, cb, w_down.astype(BF16),
      ln_g.reshape(1, d), ln_b.reshape(1, d))


def _rope_tables(seq):
    inv = ROPE_THETA ** (-jnp.arange(0, ROPE_DIM, 2, dtype=F32) / ROPE_DIM)
    ang = jnp.arange(seq, dtype=F32)[:, None] * inv[None, :]
    cos, sin = jnp.cos(ang), jnp.sin(ang)
    half = ROPE_DIM // 2
    rest = HEAD_DIM - ROPE_DIM
    one = jnp.ones((seq, rest), F32)
    zero = jnp.zeros((seq, rest), F32)
    zh = jnp.zeros((seq, half), F32)
    head = lambda *parts: jnp.concatenate(parts * (LANES // HEAD_DIM), axis=1)
    return head(cos, cos, one), head(-sin, zh, zero), head(zh, sin, zero)


def kernel(x, c, ada_w, ada_b, ln_g, ln_b, ffn_w_up, ffn_conv_w, ffn_conv_b, ffn_w_down, ev_w_in, ev_w_out, rw_mu, rw_w0, rw_w2, rw_a0, rw_a2, rw_g2, rw_k_k, rw_k_a, rw_r_k, rw_gn_g, rw_gn_b, ik_ln_g, ik_ln_b, od_w_in, od_w_out, pool_w, pool_scale, sg_ln_g, sg_ln_b, sg_w, sg_b):
    seq = x.shape[1]
    tables = _rope_tables(seq)
    mods = _modulation(c, ada_w, ada_b)
    ev_w = _split_even_weights(ev_w_in)
    for layer in range(DEPTH):
        mod = mods[layer]
        if layer % 2 == 0:
            e = layer // 2
            p_r, q_t, k, v_t, qi_t, ki, wi_t = _even_in_proj(x, mod, [w[e] for w in ev_w],
                                                             ik_ln_g[e], ik_ln_b[e], tables)
            ya = _rwkv(p_r, rw_mu[e], rw_w0[e], rw_w2[e], rw_a0[e], rw_a2[e], rw_g2[e],
                       rw_k_k[e], rw_k_a[e], rw_r_k[e], rw_gn_g[e], rw_gn_b[e])
            yb = _dsa(q_t, k, v_t, qi_t, ki, wi_t)
            w_out = ev_w_out[e]
            x = _proj_ln([ya, yb], [w_out[:RWKV_WIDTH], w_out[RWKV_WIDTH:]], x, mod, 2,
                         ln_g[layer, 0], ln_b[layer, 0], "even_out_proj_ln")
        else:
            o = layer // 2
            x = _odd_layer(x, mod, od_w_in[o], pool_w[o], pool_scale[o], sg_ln_g[o], sg_ln_b[o],
                           sg_w[o], sg_b[o], od_w_out[o], ln_g[layer, 0], ln_b[layer, 0])
        x = _conv_ffn(x, mod, ffn_w_up[layer], ffn_conv_w[layer], ffn_conv_b[layer],
                      ffn_w_down[layer], ln_g[layer, 1], ln_b[layer, 1])
    return x
```

```python
import functools

import numpy as np
import jax
import jax.numpy as jnp
from jax import lax
from jax.experimental import pallas as pl
from jax.experimental.pallas import tpu as pltpu

F32 = jnp.float32
BF16 = jnp.bfloat16
I32 = jnp.int32

D_MODEL = 1024
DEPTH = 4
CHUNK = 64
HEAD_DIM = 64
RWKV_WIDTH = D_MODEL // 2
RWKV_HEADS = RWKV_WIDTH // HEAD_DIM
RWKV_LORA_W = 64
RWKV_LORA_A = 64
RWKV_LORA_G = 128
RWKV_COLS = 3 * RWKV_WIDTH + RWKV_LORA_W + RWKV_LORA_A + RWKV_LORA_G
DSA_WIDTH = D_MODEL - RWKV_WIDTH
DSA_HEADS = DSA_WIDTH // HEAD_DIM
IDX_HEADS = 4
IDX_DIM = 64
INDEX_TOPK = 256
Q_BLOCK = 128
ROPE_THETA = 500000.0
ROPE_DIM = HEAD_DIM // 4
POOL_WINDOWS = (2, 4, 8, 16)
POOL_WIDTH = D_MODEL // 2
POOL_GROUP_DIM = POOL_WIDTH // len(POOL_WINDOWS)
SG_WIDTH = D_MODEL - POOL_WIDTH
SG_GROUPS = 4
SG_GROUP_DIM = SG_WIDTH // SG_GROUPS
SG_CHUNK = 128
D_FF = 2816
ALPHA = (2.0 * DEPTH) ** 0.25
LN_EPS = 1e-5
GN_EPS = 64e-5
NEG_INF = -1e30
INT_MIN = -(2 ** 31)
LOG2E = 1.4426950408889634

LANES = 128
SUBLANES = 8
MXU_TILE = 256
VMEM_LIMIT = 56 * 1024 * 1024

MOD_COLS = 1536
SPLIT_ROWS = 256
IN_PROJ_ROWS = 256
ODD_ROWS = 256
OUT_PROJ_ROWS = 512
FFN_ROWS = 512
RW_TILE = 128
RW_CHUNK = 16
RW_ROWS = 4
KEY_TILE = 1024
TRI_TILE = 256
TOP_BITS = 15
PAD_IDX = 384

def _dot(a, b):
    return jnp.dot(a.astype(BF16), b.astype(BF16), preferred_element_type=F32)


def _dot_nt(a, b):
    return lax.dot_general(a.astype(BF16), b.astype(BF16), (((1,), (1,)), ((), ())),
                           preferred_element_type=F32)


def _dot_split(x, m01, terms):
    acc = None
    rem = x
    for _ in range(terms):
        piece = rem.astype(BF16)
        rem = rem - piece.astype(F32)
        part = jnp.dot(piece, m01, preferred_element_type=F32)
        acc = part if acc is None else acc + part
    return acc


def _dot_split_left(m01, x, terms):
    acc = None
    rem = x
    for _ in range(terms):
        piece = rem.astype(BF16)
        rem = rem - piece.astype(F32)
        part = jnp.dot(m01, piece, preferred_element_type=F32)
        acc = part if acc is None else acc + part
    return acc


def _ln(x, g, b, eps):
    mu = jnp.mean(x, axis=-1, keepdims=True)
    xc = x - mu
    var = jnp.mean(xc * xc, axis=-1, keepdims=True)
    return xc * lax.rsqrt(var + eps) * g + b


def _gelu(x):
    return 0.5 * x * (1.0 + lax.erf(x * 0.7071067811865476))


RED_ROWS = 64


def _fold_rows(x, op):
    rows, n = x.shape
    part = op(x.reshape(rows // RED_ROWS, RED_ROWS, n), axis=0)
    return op(part, axis=0, keepdims=True)


def _params(sem):
    return pltpu.CompilerParams(dimension_semantics=sem, vmem_limit_bytes=VMEM_LIMIT)


def _mod_kernel(c_ref, w_ref, b_ref, o_ref):
    c = c_ref[...]
    ca = c * jax.nn.sigmoid(c)
    o_ref[0] = jnp.dot(ca, w_ref[0], preferred_element_type=F32,
                       precision=lax.Precision.HIGHEST) + b_ref[0]


def _modulation(c, ada_w, ada_b):
    bsz, d = c.shape
    depth = ada_w.shape[0]
    n = ada_w.shape[2]
    tn = MOD_COLS
    rows = -(-bsz // SUBLANES) * SUBLANES
    c8 = jnp.pad(c, ((0, rows - bsz), (0, 0)))
    out = pl.pallas_call(
        _mod_kernel,
        grid=(depth, n // tn),
        in_specs=[pl.BlockSpec((rows, d), lambda l, j: (0, 0)),
                  pl.BlockSpec((1, d, tn), lambda l, j: (l, 0, j)),
                  pl.BlockSpec((1, 1, tn), lambda l, j: (l, 0, j))],
        out_specs=pl.BlockSpec((1, rows, tn), lambda l, j: (l, 0, j)),
        out_shape=jax.ShapeDtypeStruct((depth, rows, n), F32),
        compiler_params=_params(("arbitrary", "arbitrary")),
        name="adaln_mod",
    )(c8, ada_w, ada_b.reshape(depth, 1, n))
    return out[:, :bsz].reshape(depth, bsz, 6, d)


def _rope(x, cos_t, sin_a, sin_b):
    n = x.shape[1] // LANES
    rep = (lambda t: jnp.concatenate([t] * n, axis=1)) if n > 1 else (lambda t: t)
    width = x.shape[1]
    half = ROPE_DIM // 2
    return (x * rep(cos_t) + pltpu.roll(x, width - half, 1) * rep(sin_a)
            + pltpu.roll(x, half, 1) * rep(sin_b))


def _even_in_kernel(x_ref, mod_ref, wr_ref, wqkv_ref, widx_ref, cos_ref, sa_ref, sb_ref,
                    ikg_ref, ikb_ref,
                    pr_ref, qt_ref, k_ref, vt_ref, qit_ref, ki_ref, wit_ref):
    m = mod_ref[0]
    h = (x_ref[0] * (1.0 + m[1:2]) + m[0:1]).astype(BF16)
    pr_ref[0] = jnp.dot(h, wr_ref[...], preferred_element_type=F32)
    qkv = jnp.dot(h, wqkv_ref[...], preferred_element_type=F32)
    cos_t = cos_ref[...]
    sin_a = sa_ref[...]
    sin_b = sb_ref[...]
    w = DSA_WIDTH
    q = _rope(qkv[:, :w], cos_t, sin_a, sin_b) * (HEAD_DIM ** -0.5 * LOG2E)
    qt_ref[0] = q.T.astype(BF16)
    k_ref[0] = _rope(qkv[:, w:2 * w], cos_t, sin_a, sin_b).astype(BF16)
    vt_ref[0] = qkv[:, 2 * w:].T.astype(BF16)
    idx = jnp.dot(h, widx_ref[...], preferred_element_type=F32)
    nq = IDX_HEADS * IDX_DIM
    qit_ref[0] = _rope(idx[:, :nq], cos_t, sin_a, sin_b).T.astype(BF16)
    blk = idx[:, nq:nq + LANES]
    lane = lax.broadcasted_iota(I32, blk.shape, 1)
    is_k = lane < IDX_DIM
    mu = jnp.sum(jnp.where(is_k, blk, 0.0), axis=1, keepdims=True) * (1.0 / IDX_DIM)
    xc = jnp.where(is_k, blk - mu, 0.0)
    var = jnp.sum(xc * xc, axis=1, keepdims=True) * (1.0 / IDX_DIM)
    kin = xc * lax.rsqrt(var + LN_EPS) * ikg_ref[...] + ikb_ref[...]
    ki_ref[0] = _rope(kin, cos_t, sin_a, sin_b)[:, :IDX_DIM].astype(BF16)
    wit = (blk * (IDX_HEADS ** -0.5 * IDX_DIM ** -0.5)).T
    wit_ref[0] = wit[IDX_DIM:IDX_DIM + SUBLANES]


def _split_w_kernel(w_ref, wr_ref, wqkv_ref, widx_ref):
    w = w_ref[0]
    c1 = RWKV_COLS
    c2 = RWKV_COLS + 3 * DSA_WIDTH
    wr_ref[0] = w[:, :c1].astype(BF16)
    wqkv_ref[0] = w[:, c1:c2].astype(BF16)
    tail = w[:, c2:]
    zeros = jnp.zeros((w.shape[0], PAD_IDX - tail.shape[1]), F32)
    widx_ref[0] = jnp.concatenate([tail, zeros], axis=1).astype(BF16)


def _split_even_weights(ev_w_in):
    n_even, d, n = ev_w_in.shape
    tr = SPLIT_ROWS
    widths = (RWKV_COLS, 3 * DSA_WIDTH, PAD_IDX)
    return pl.pallas_call(
        _split_w_kernel,
        grid=(n_even, d // tr),
        in_specs=[pl.BlockSpec((1, tr, n), lambda e, i: (e, i, 0))],
        out_specs=[pl.BlockSpec((1, tr, wd), lambda e, i: (e, i, 0)) for wd in widths],
        out_shape=[jax.ShapeDtypeStruct((n_even, d, wd), BF16) for wd in widths],
        compiler_params=_params(("parallel", "parallel")),
        name="split_even_weights",
    )(ev_w_in)


def _even_in_proj(x, mod, weights, ik_g, ik_b, tables):
    bsz, seq, d = x.shape
    tm = IN_PROJ_ROWS
    w_r, w_qkv, w_idx = weights
    pad = LANES - IDX_DIM
    ikg = jnp.pad(ik_g, (0, pad)).reshape(1, LANES)
    ikb = jnp.pad(ik_b, (0, pad)).reshape(1, LANES)
    cos_t, sin_a, sin_b = tables
    full = lambda shape: pl.BlockSpec(shape, lambda b, i: (0,) * len(shape))
    tab = pl.BlockSpec((tm, LANES), lambda b, i: (i, 0))
    nq = IDX_HEADS * IDX_DIM
    rows = lambda n: pl.BlockSpec((1, tm, n), lambda b, i: (b, i, 0))
    cols = lambda n: pl.BlockSpec((1, n, tm), lambda b, i: (b, 0, i))
    return pl.pallas_call(
        _even_in_kernel,
        grid=(bsz, seq // tm),
        in_specs=[pl.BlockSpec((1, tm, d), lambda b, i: (b, i, 0)),
                  pl.BlockSpec((1, 6, d), lambda b, i: (b, 0, 0)),
                  full(w_r.shape), full(w_qkv.shape), full(w_idx.shape),
                  tab, tab, tab, full((1, LANES)), full((1, LANES))],
        out_specs=[rows(RWKV_COLS), cols(DSA_WIDTH), rows(DSA_WIDTH), cols(DSA_WIDTH),
                   cols(nq), rows(IDX_DIM), cols(SUBLANES)],
        out_shape=[jax.ShapeDtypeStruct((bsz, seq, RWKV_COLS), F32),
                   jax.ShapeDtypeStruct((bsz, DSA_WIDTH, seq), BF16),
                   jax.ShapeDtypeStruct((bsz, seq, DSA_WIDTH), BF16),
                   jax.ShapeDtypeStruct((bsz, DSA_WIDTH, seq), BF16),
                   jax.ShapeDtypeStruct((bsz, nq, seq), BF16),
                   jax.ShapeDtypeStruct((bsz, seq, IDX_DIM), BF16),
                   jax.ShapeDtypeStruct((bsz, SUBLANES, seq), F32)],
        compiler_params=_params(("parallel", "parallel")),
        name="even_in_proj",
    )(x, mod, w_r, w_qkv, w_idx, cos_t, sin_a, sin_b, ikg, ikb)


def _rwkv_kernel(p_ref, pp_ref, mu_ref, vec_ref, w2_ref, a2_ref, g2_ref, ltri_ref, ustr_ref,
                 seg_ref, o_ref, s_ref, obuf_ref):
    i = pl.program_id(1)

    @pl.when(i == 0)
    def _():
        s_ref[...] = jnp.zeros_like(s_ref)

    tt = RW_TILE
    w = RWKV_WIDTH
    hd = HEAD_DIM
    nb = p_ref.shape[0]
    nchunk = tt // RW_CHUNK
    seg = seg_ref[...]
    half_w = seg.shape[0]

    def segsum(t):
        return jnp.concatenate([_dot_split(t[:, c:c + half_w], seg, 2)
                                for c in range(0, w, half_w)], axis=1)

    vec = vec_ref[...]
    w0, a0, k_k, k_a, r_k, gn_g, gn_b = (vec[j:j + 1] for j in range(7))
    rowi = lax.broadcasted_iota(I32, (tt, 1), 0)
    live = (i > 0).astype(F32)

    def prepare(b):
        p = p_ref[b]
        prow = pp_ref[b][SUBLANES - 1:SUBLANES] * live
        xprev = jnp.where(rowi == 0, prow, pltpu.roll(p, 1, 0))
        ps = p + (xprev - p) * mu_ref[...]
        r = ps[:, :w]
        k = ps[:, w:2 * w]
        v = ps[:, 2 * w:3 * w]
        o1 = 3 * w
        wd = ps[:, o1:o1 + RWKV_LORA_W]
        ad = ps[:, o1 + RWKV_LORA_W:o1 + RWKV_LORA_W + RWKV_LORA_A]
        gd = ps[:, o1 + RWKV_LORA_W + RWKV_LORA_A:]
        y = -(w0 + _dot(jnp.tanh(wd), w2_ref[...]))
        softplus = jnp.maximum(y, 0.0) + jnp.log1p(jnp.exp(-jnp.abs(y)))
        logw = -jnp.exp(-softplus - 0.5)
        a = jax.nn.sigmoid(a0 + _dot(ad, a2_ref[...]))
        g = _dot(jax.nn.sigmoid(gd), g2_ref[...])
        kk = k * k_k
        kk = kk / jnp.maximum(jnp.sqrt(segsum(kk * kk)), 1e-12)
        k2 = k * (1.0 + (a - 1.0) * k_a)
        bonus = segsum(r * k2 * r_k) * v
        cum = _dot_split_left(ltri_ref[...], logw, 3)
        rem = _dot_split_left(ustr_ref[...], logw, 3)
        pt = jnp.exp(cum)
        ipt = jnp.exp(-cum)
        erem = jnp.exp(rem)
        kka = kk * a
        return dict(at=-kk * jnp.exp(cum - logw), rt=r * pt, bt=kka * ipt, kt=k2 * ipt,
                    bp=kka * erem, kp=k2 * erem, v=v, pt=pt, bonus=bonus, g=g)

    rows_in = [prepare(b) for b in range(nb)]

    ti = lax.broadcasted_iota(I32, (tt, tt), 0)
    si = lax.broadcasted_iota(I32, (tt, tt), 1)
    same = (ti // RW_CHUNK) == (si // RW_CHUNK)
    strict = same & (si < ti)
    incl = same & (si <= ti)
    tb = lax.broadcasted_iota(I32, (tt, nchunk * hd), 0)
    cb = lax.broadcasted_iota(I32, (tt, nchunk * hd), 1)
    blkmask = (tb // RW_CHUNK) == (cb // hd)
    tile_chunks = lambda t: jnp.where(blkmask, jnp.concatenate([t] * nchunk, axis=1), 0.0)

    units = [(b, slice(h * hd, (h + 1) * hd)) for b in range(nb) for h in range(RWKV_HEADS)]
    idx = range(len(units))
    pick = lambda name: [rows_in[b][name][:, sl] for b, sl in units]
    at_h, rt_h, vh, bt_h, kt_h = pick("at"), pick("rt"), pick("v"), pick("bt"), pick("kt")
    x = [_dot_nt(jnp.concatenate([at_h[u], rt_h[u]], axis=0),
                 jnp.concatenate([bt_h[u], kt_h[u]], axis=0)) for u in idx]
    a_ak = [jnp.where(strict, x[u][:tt, tt:], 0.0) for u in idx]
    a_rb = [jnp.where(incl, x[u][tt:, :tt], 0.0) for u in idx]
    a_rk = [jnp.where(incl, x[u][tt:, tt:], 0.0) for u in idx]
    def bd(m0, m1):
        z0 = jnp.zeros((m0.shape[0], m1.shape[1]), m0.dtype)
        z1 = jnp.zeros((m1.shape[0], m0.shape[1]), m1.dtype)
        return jnp.concatenate([jnp.concatenate([m0, z0], axis=1),
                                jnp.concatenate([z1, m1], axis=1)], axis=0)

    def pair_dot(lhs, rhs):
        out = []
        for u in range(0, len(lhs), 2):
            res = _dot(jnp.concatenate([lhs[u], lhs[u + 1]], axis=1), bd(rhs[u], rhs[u + 1]))
            cut = rhs[u].shape[1]
            out += [res[:, :cut], res[:, cut:]]
        return out

    aak_v = pair_dot(a_ak, vh)
    eye = (ti == si).astype(F32)
    inv = None
    blk = 1
    while blk < RW_CHUNK:
        below = same & ((ti // blk) % 2 == 1) & ((si // blk) % 2 == 0) & (
            (ti // (2 * blk)) == (si // (2 * blk)))
        a21 = [jnp.where(below, x[u][:tt, :tt], 0.0) for u in idx]
        if inv is None:
            inv = [eye + a21[u] for u in idx]
        else:
            left = pair_dot(a21, inv)
            grow = pair_dot(inv, left)
            inv = [inv[u] + grow[u] for u in idx]
        blk *= 2
    yv = pair_dot(inv, [jnp.concatenate([at_h[u], aak_v[u]], axis=1) for u in idx])
    arb_y = pair_dot(a_rb, yv)
    ark_v = pair_dot(a_rk, vh)
    qt = [rt_h[u] + arb_y[u][:, :hd] for u in idx]
    o0 = [arb_y[u][:, hd:] + ark_v[u] for u in idx]
    yt = [yv[u].T for u in idx]
    bpb = [tile_chunks(t) for t in pick("bp")]
    kpb = [tile_chunks(t) for t in pick("kp")]
    g_all = [_dot(yt[u][:hd], bpb[u]) for u in idx]
    h_all = [_dot(jnp.concatenate([yt[u][hd:], vh[u].T], axis=1),
                  jnp.concatenate([bpb[u], kpb[u]], axis=0)) for u in idx]
    pt_h = pick("pt")
    pairs = range(0, len(units), 2)
    low_lanes = lax.broadcasted_iota(I32, (hd, 2 * hd), 1) < hd
    s = [jnp.concatenate([s_ref[u], s_ref[u + 1]], axis=1) for u in pairs]
    for n in range(nchunk):
        rows = slice(n * RW_CHUNK, (n + 1) * RW_CHUNK)
        cols = slice(n * hd, (n + 1) * hd)
        last = (n + 1) * RW_CHUNK - 1
        for j, u in enumerate(pairs):
            s_bd = jnp.concatenate([jnp.where(low_lanes, s[j], 0.0),
                                    jnp.where(low_lanes, 0.0, s[j])], axis=0)
            o_pair = _dot_nt(jnp.concatenate([qt[u][rows], qt[u + 1][rows]], axis=1), s_bd)
            for i in range(2):
                b, sl = units[u + i]
                obuf_ref[b, rows, sl] = o_pair[:, i * hd:(i + 1) * hd] + o0[u + i][rows]
        s = [s[j] * jnp.concatenate([pt_h[u][last:last + 1], pt_h[u + 1][last:last + 1]], axis=1)
             + _dot(s[j], bd(g_all[u][:, cols], g_all[u + 1][:, cols]))
             + jnp.concatenate([h_all[u][:, cols], h_all[u + 1][:, cols]], axis=1)
             for j, u in enumerate(pairs)]
    for j, u in enumerate(pairs):
        s_ref[u] = s[j][:, :hd]
        s_ref[u + 1] = s[j][:, hd:]

    for b in range(nb):
        o = obuf_ref[b]
        mean = segsum(o) * (1.0 / hd)
        oc = o - mean
        var = segsum(oc * oc) * (1.0 / hd)
        on = oc * lax.rsqrt(var + GN_EPS) * gn_g + gn_b
        o_ref[b] = ((on + rows_in[b]["bonus"]) * rows_in[b]["g"]).astype(BF16)


def _rwkv(p_r, mu, w0, w2, a0, a2, g2, k_k, k_a, r_k, gn_g, gn_b):
    bsz, seq, _ = p_r.shape
    tt = RW_TILE
    nb = RW_ROWS if bsz % RW_ROWS == 0 else 1
    w = RWKV_WIDTH
    vec = jnp.stack([w0, a0, k_k, k_a, r_k.reshape(w), gn_g, gn_b, jnp.zeros_like(w0)])
    t_idx = np.arange(tt)
    same = (t_idx[:, None] // RW_CHUNK) == (t_idx[None, :] // RW_CHUNK)
    ltri = jnp.asarray(same & (t_idx[None, :] <= t_idx[:, None]), BF16)
    ustr = jnp.asarray(same & (t_idx[None, :] > t_idx[:, None]), BF16)
    c_idx = np.arange(MXU_TILE)
    seg = jnp.asarray((c_idx[:, None] // HEAD_DIM) == (c_idx[None, :] // HEAD_DIM), BF16)
    full = lambda shape: pl.BlockSpec(shape, lambda b, i: (0,) * len(shape))
    per_tile = tt // SUBLANES
    return pl.pallas_call(
        _rwkv_kernel,
        grid=(bsz // nb, seq // tt),
        in_specs=[pl.BlockSpec((nb, tt, RWKV_COLS), lambda b, i: (b, i, 0)),
                  pl.BlockSpec((nb, SUBLANES, RWKV_COLS),
                               lambda b, i: (b, jnp.maximum(i * per_tile - 1, 0), 0)),
                  full((1, RWKV_COLS)), full((SUBLANES, w)),
                  full(w2.shape), full(a2.shape), full(g2.shape),
                  full((tt, tt)), full((tt, tt)), full((MXU_TILE, MXU_TILE))],
        out_specs=pl.BlockSpec((nb, tt, w), lambda b, i: (b, i, 0)),
        out_shape=jax.ShapeDtypeStruct((bsz, seq, w), BF16),
        scratch_shapes=[pltpu.VMEM((nb * RWKV_HEADS, HEAD_DIM, HEAD_DIM), F32),
                        pltpu.VMEM((nb, tt, w), F32)],
        compiler_params=_params(("parallel", "arbitrary")),
        name="rwkv7_mix",
    )(p_r, p_r, mu.reshape(1, RWKV_COLS), vec, w2.astype(BF16), a2.astype(BF16),
      g2.astype(BF16), ltri, ustr, seg)


def _dsa_kernel(qt_ref, k_ref, vt_ref, qit_ref, ki_ref, wit_ref, low_ref, o_ref,
                key_ref, top_ref, bias_ref, acc_ref, m_ref, l_ref, *, topk):
    qb = pl.program_id(1)
    nq = Q_BLOCK
    kt_sz = bias_ref.shape[0]
    half = kt_sz // 2
    hd = HEAD_DIM
    start = qb * nq
    nkt = (start + nq + kt_sz - 1) // kt_sz
    col = lax.broadcasted_iota(I32, (1, nq), 1)
    lim = start + (col // CHUNK + 1) * CHUNK
    wit = wit_ref[0]
    qit = qit_ref[0]
    qi_cat = jnp.concatenate([qit[h * IDX_DIM:(h + 1) * IDX_DIM] for h in range(IDX_HEADS)],
                             axis=1)

    def score_tile(j, carry):
        off = pl.multiple_of(j * kt_sz, kt_sz)
        d = jnp.dot(ki_ref[0, pl.ds(off, kt_sz), :], qi_cat, preferred_element_type=F32)
        s = jnp.zeros((kt_sz, nq), F32)
        for h in range(IDX_HEADS):
            s = s + wit[h:h + 1] * jnp.maximum(d[:, h * nq:(h + 1) * nq], 0.0)
        s = s + 0.0
        bits = pltpu.bitcast(s, I32)
        key = bits ^ ((bits >> 31) & 0x7FFFFFFF)
        sidx = off + lax.broadcasted_iota(I32, (kt_sz, nq), 0)
        key = jnp.where(sidx < lim, key, INT_MIN)
        key_ref[pl.ds(off, kt_sz), :] = key
        field = (key >> (32 - TOP_BITS)) + (3 << (TOP_BITS - 1))
        top_ref[pl.ds(pl.multiple_of(j * half, half), half), :] = (field[:half] << 16) | field[half:]
        return carry

    lax.fori_loop(0, nkt, score_tile, 0)

    def count(pred):
        def body(j, c):
            off = pl.multiple_of(j * kt_sz, kt_sz)
            hit = jnp.where(pred(key_ref[pl.ds(off, kt_sz), :]), 1, 0)
            return c + hit.reshape(kt_sz // RED_ROWS, RED_ROWS, nq).sum(axis=0)
        c = lax.fori_loop(0, nkt, body, jnp.zeros((RED_ROWS, nq), I32))
        return jnp.sum(c, axis=0, keepdims=True)

    def count_top(cand):
        both = (cand << 16) | cand

        def body(j, c):
            off = pl.multiple_of(j * half, half)
            hit = lax.shift_right_logical(top_ref[pl.ds(off, half), :] - both, 15) & 0x00010001
            return c + hit.reshape(half // RED_ROWS, RED_ROWS, nq).sum(axis=0)

        c = jnp.sum(lax.fori_loop(0, nkt, body, jnp.zeros((RED_ROWS, nq), I32)),
                    axis=0, keepdims=True)
        return (c & 0xFFFF) + lax.shift_right_logical(c, 16)

    def top_step(b, prefix):
        cand = prefix | lax.shift_left(jnp.int32(1), TOP_BITS - 1 - b)
        return jnp.where(count_top(cand) >= topk, cand, prefix)

    prefix = lax.fori_loop(0, TOP_BITS, top_step, jnp.zeros((1, nq), I32))
    lo = (prefix - (1 << (TOP_BITS - 1))) << (32 - TOP_BITS)

    def bit_step(b, lo):
        cand = lo + lax.shift_left(jnp.int32(1), 31 - TOP_BITS - b)
        return jnp.where(count(lambda kv: kv >= cand) >= topk, cand, lo)

    th = lax.fori_loop(0, 32 - TOP_BITS, bit_step, lo)
    need = jnp.where(th == INT_MIN, 0, topk - count(lambda kv: kv > th)).astype(F32)

    acc_ref[...] = jnp.zeros_like(acc_ref)
    m_ref[...] = jnp.full_like(m_ref, NEG_INF)
    l_ref[...] = jnp.zeros_like(l_ref)
    low = low_ref[...]
    npair = DSA_HEADS // 2
    top_rows = lax.broadcasted_iota(I32, (2 * hd, nq), 0) < hd
    qt = qt_ref[0].astype(F32)
    q_bd = []
    for p in range(npair):
        qp = qt[p * 2 * hd:(p + 1) * 2 * hd]
        q_bd.append(jnp.concatenate([jnp.where(top_rows, qp, 0.0), jnp.where(top_rows, 0.0, qp)],
                                    axis=1).astype(BF16))

    pad_rows = 2 * SUBLANES
    ones_rows = (lax.broadcasted_iota(I32, (pad_rows, kt_sz), 0) == 0).astype(BF16)

    def attn_tile(j, eq_before):
        off = pl.multiple_of(j * kt_sz, kt_sz)
        eq_run = eq_before
        for t in range(kt_sz // TRI_TILE):
            kv = key_ref[pl.ds(off + t * TRI_TILE, TRI_TILE), :]
            eq = kv == th
            pre = eq_run + jnp.dot(low, jnp.where(eq, 1.0, 0.0).astype(BF16),
                                   preferred_element_type=F32)
            sel = (kv > th) | (eq & (pre <= need))
            bias_ref[t * TRI_TILE:(t + 1) * TRI_TILE, :] = jnp.where(sel, 0.0, NEG_INF)
            eq_run = pre[TRI_TILE - 1:TRI_TILE]
        bias = bias_ref[...]

        def logits(p):
            half = kt_sz // 2
            return jnp.concatenate(
                [jnp.dot(k_ref[0, pl.ds(off + r * half, half), p * 2 * hd:(p + 1) * 2 * hd],
                         q_bd[p], preferred_element_type=F32) for r in range(2)], axis=0)

        st_next = logits(0)
        for p in range(npair):
            cols = slice(p * 2 * hd, (p + 1) * 2 * hd)
            st = st_next
            if p + 1 < npair:
                st_next = logits(p + 1)
            prs, scales = [], []
            for i in range(2):
                h = 2 * p + i
                s = st[:, i * nq:(i + 1) * nq] + bias
                m_old = m_ref[h:h + 1]
                m_new = jnp.maximum(m_old, _fold_rows(s, jnp.max))
                prs.append(jnp.exp2(s - m_new).astype(BF16))
                scales.append(jnp.exp2(m_old - m_new))
                m_ref[h:h + 1] = m_new
            lhs = jnp.concatenate([vt_ref[0, cols, pl.ds(off, kt_sz)], ones_rows], axis=0)
            pv = jnp.dot(lhs, jnp.concatenate(prs, axis=1),
                         preferred_element_type=F32)
            for i in range(2):
                h = 2 * p + i
                l_ref[h:h + 1] = (scales[i] * l_ref[h:h + 1]
                                  + pv[2 * hd:2 * hd + 1, i * nq:(i + 1) * nq])
            new = jnp.where(top_rows, pv[:2 * hd, :nq], pv[:2 * hd, nq:])
            acc_ref[p] = jnp.where(top_rows, scales[0], scales[1]) * acc_ref[p] + new
        return eq_run

    lax.fori_loop(0, nkt, attn_tile, jnp.zeros((1, nq), F32))
    for p in range(npair):
        inv = jnp.where(top_rows, 1.0 / l_ref[2 * p:2 * p + 1], 1.0 / l_ref[2 * p + 1:2 * p + 2])
        o_ref[0, :, p * 2 * hd:(p + 1) * 2 * hd] = (acc_ref[p] * inv).T.astype(BF16)


def _dsa(q_t, k, v_t, qi_t, ki, wi_t):
    bsz, seq, w = k.shape
    nq = Q_BLOCK
    topk = min(INDEX_TOPK, seq // 4)
    kt_sz = min(KEY_TILE, seq)
    assert kt_sz % (2 * RED_ROWS) == 0 and kt_sz % TRI_TILE == 0 and seq // 2 < 2 ** 16
    t_idx = np.arange(TRI_TILE)
    low = jnp.asarray(t_idx[None, :] <= t_idx[:, None], BF16)
    cols = lambda n: pl.BlockSpec((1, n, nq), lambda b, i: (b, 0, i))
    return pl.pallas_call(
        functools.partial(_dsa_kernel, topk=topk),
        grid=(bsz, seq // nq),
        in_specs=[cols(w),
                  pl.BlockSpec((1, seq, w), lambda b, i: (b, 0, 0)),
                  pl.BlockSpec((1, w, seq), lambda b, i: (b, 0, 0)),
                  cols(IDX_HEADS * IDX_DIM),
                  pl.BlockSpec((1, seq, IDX_DIM), lambda b, i: (b, 0, 0)),
                  cols(SUBLANES),
                  pl.BlockSpec((TRI_TILE, TRI_TILE), lambda b, i: (0, 0))],
        out_specs=pl.BlockSpec((1, nq, w), lambda b, i: (b, i, 0)),
        out_shape=jax.ShapeDtypeStruct((bsz, seq, w), BF16),
        scratch_shapes=[pltpu.VMEM((seq, nq), I32),
                        pltpu.VMEM((seq // 2, nq), I32),
                        pltpu.VMEM((kt_sz, nq), F32),
                        pltpu.VMEM((DSA_HEADS // 2, 2 * HEAD_DIM, nq), F32),
                        pltpu.VMEM((DSA_HEADS, nq), F32),
                        pltpu.VMEM((DSA_HEADS, nq), F32)],
        compiler_params=_params(("parallel", "arbitrary")),
        name="dsa_mix",
    )(q_t, k, v_t, qi_t, ki, wi_t, low)


POOL_HALO = 16


def _odd_layer_kernel(x_ref, xp_ref, mod_ref, win_ref, pw_ref, ps_ref, lng_ref, lnb_ref, ws_ref,
                      bs_ref, wout_ref, g_ref, b_ref, o_ref, y_ref):
    i = pl.program_id(1)
    m = mod_ref[0]
    sc = 1.0 + m[1:2]
    sh = m[0:1]
    x = x_ref[0]
    tm = x.shape[0]
    gd = POOL_GROUP_DIM
    p = jnp.dot((x * sc + sh).astype(BF16), win_ref[...], preferred_element_type=F32)
    prev = jnp.dot((xp_ref[0] * sc + sh).astype(BF16), win_ref[:, :POOL_WIDTH],
                   preferred_element_type=F32) * (i > 0).astype(F32)
    t_glob = (i * tm + lax.broadcasted_iota(I32, (tm, 1), 0)).astype(F32)
    scale = ps_ref[...]
    for gi, win in enumerate(POOL_WINDOWS):
        cols = slice(gi * gd, (gi + 1) * gd)
        xg = p[:, cols]
        s = jnp.concatenate([prev[:, cols], xg], axis=0)
        span = 1
        while span < win:
            s = s[span:] + s[:-span]
            span *= 2
        first = POOL_HALO + 1 - win
        pooled = s[first:first + tm] / jnp.minimum(t_glob + 1.0, float(win)) - xg
        y_ref[:, cols] = (_dot(pooled, pw_ref[gi]) * scale[:, cols]).astype(BF16)

    u = _gelu(p[:, POOL_WIDTH:POOL_WIDTH + SG_WIDTH])
    v = _ln(_gelu(p[:, POOL_WIDTH + SG_WIDTH:]), lng_ref[...], lnb_ref[...], LN_EPS)
    ti = lax.broadcasted_iota(I32, (SG_CHUNK, SG_CHUNK), 0)
    si = lax.broadcasted_iota(I32, (SG_CHUNK, SG_CHUNK), 1)
    bs = bs_ref[...]
    for gi in range(SG_GROUPS):
        cols = slice(gi * SG_GROUP_DIM, (gi + 1) * SG_GROUP_DIM)
        ws = jnp.where(si <= ti, ws_ref[gi], 0.0)
        for n in range(tm // SG_CHUNK):
            rows = slice(n * SG_CHUNK, (n + 1) * SG_CHUNK)
            z = _dot(ws, v[rows, cols]) + bs[:, gi:gi + 1]
            y_ref[rows, POOL_WIDTH + gi * SG_GROUP_DIM:POOL_WIDTH + (gi + 1) * SG_GROUP_DIM] = (
                u[rows, cols] * z).astype(BF16)

    y = jnp.dot(y_ref[...], wout_ref[...], preferred_element_type=F32)
    o_ref[0] = _ln(ALPHA * x + m[2:3] * y, g_ref[...], b_ref[...], LN_EPS)


def _odd_layer(x, mod, w_in, pool_w, pool_scale, sg_ln_g, sg_ln_b, sg_w, sg_b, w_out, ln_g, ln_b):
    bsz, seq, d = x.shape
    tm = ODD_ROWS
    per_tile = tm // POOL_HALO
    full = lambda shape: pl.BlockSpec(shape, lambda b, i: (0,) * len(shape))
    once = lambda shape: pl.BlockSpec(shape, lambda b, i: (0,) * len(shape),
                                      pipeline_mode=pl.Buffered(1))
    return pl.pallas_call(
        _odd_layer_kernel,
        grid=(bsz, seq // tm),
        in_specs=[pl.BlockSpec((1, tm, d), lambda b, i: (b, i, 0)),
                  pl.BlockSpec((1, POOL_HALO, d),
                               lambda b, i: (b, jnp.maximum(i * per_tile - 1, 0), 0)),
                  pl.BlockSpec((1, 6, d), lambda b, i: (b, 0, 0)),
                  once(w_in.shape), full(pool_w.shape), full((1, POOL_WIDTH)),
                  full((1, SG_WIDTH)), full((1, SG_WIDTH)), full(sg_w.shape),
                  full((SG_CHUNK, SG_GROUPS)), once(w_out.shape), full((1, d)), full((1, d))],
        out_specs=pl.BlockSpec((1, tm, d), lambda b, i: (b, i, 0)),
        out_shape=jax.ShapeDtypeStruct((bsz, seq, d), F32),
        scratch_shapes=[pltpu.VMEM((tm, d), BF16)],
        compiler_params=_params(("parallel", "parallel")),
        name="odd_layer_mix",
    )(x, x, mod, w_in.astype(BF16), pool_w.astype(BF16), pool_scale.reshape(1, -1),
      sg_ln_g.reshape(1, -1), sg_ln_b.reshape(1, -1), sg_w, sg_b.T, w_out.astype(BF16),
      ln_g.reshape(1, d), ln_b.reshape(1, d))


def _proj_ln_kernel(*refs, n_in, gate_row):
    a_refs = refs[:n_in]
    w_refs = refs[n_in:2 * n_in]
    x_ref, mod_ref, g_ref, b_ref, o_ref = refs[2 * n_in:]
    y = None
    for a_ref, w_ref in zip(a_refs, w_refs):
        part = jnp.dot(a_ref[0], w_ref[...], preferred_element_type=F32)
        y = part if y is None else y + part
    gate = mod_ref[0][gate_row:gate_row + 1]
    o_ref[0] = _ln(ALPHA * x_ref[0] + gate * y, g_ref[...], b_ref[...], LN_EPS)


def _proj_ln(acts, weights, x, mod, gate_row, ln_g, ln_b, name):
    bsz, seq, d = x.shape
    tm = OUT_PROJ_ROWS
    n_in = len(acts)
    in_specs = [pl.BlockSpec((1, tm, a.shape[2]), lambda b, i: (b, i, 0)) for a in acts]
    in_specs += [pl.BlockSpec(w.shape, lambda b, i: (0, 0)) for w in weights]
    in_specs += [pl.BlockSpec((1, tm, d), lambda b, i: (b, i, 0)),
                 pl.BlockSpec((1, 6, d), lambda b, i: (b, 0, 0)),
                 pl.BlockSpec((1, d), lambda b, i: (0, 0)),
                 pl.BlockSpec((1, d), lambda b, i: (0, 0))]
    return pl.pallas_call(
        functools.partial(_proj_ln_kernel, n_in=n_in, gate_row=gate_row),
        grid=(bsz, seq // tm),
        in_specs=in_specs,
        out_specs=pl.BlockSpec((1, tm, d), lambda b, i: (b, i, 0)),
        out_shape=jax.ShapeDtypeStruct((bsz, seq, d), F32),
        compiler_params=_params(("parallel", "parallel")),
        name=name,
    )(*acts, *[w.astype(BF16) for w in weights], x, mod, ln_g.reshape(1, d), ln_b.reshape(1, d))


def _ffn_kernel(x_ref, xp_ref, mod_ref, wg_ref, wv_ref, cwg_ref, cwv_ref, cbg_ref, cbv_ref,
                wd_ref, g_ref, b_ref, o_ref):
    i = pl.program_id(1)
    m = mod_ref[0]
    sc = 1.0 + m[4:5]
    sh = m[3:4]
    x = x_ref[0]
    live = (i > 0).astype(F32)
    h = jnp.concatenate([(xp_ref[0] * sc + sh) * live, x * sc + sh], axis=0).astype(BF16)
    halo = SUBLANES

    def conv(w_ref, cw_ref, cb_ref):
        u = jnp.dot(h, w_ref[...], preferred_element_type=F32)
        cw = cw_ref[...]
        u1 = pltpu.roll(u, 1, 0)
        u2 = pltpu.roll(u, 2, 0)
        return (cb_ref[...] + u2[halo:] * cw[0:1] + u1[halo:] * cw[1:2] + u[halo:] * cw[2:3])

    gate = conv(wg_ref, cwg_ref, cbg_ref)
    val = conv(wv_ref, cwv_ref, cbv_ref)
    act = (gate * jax.nn.sigmoid(gate) * val).astype(BF16)
    y = jnp.dot(act, wd_ref[...], preferred_element_type=F32)
    o_ref[0] = _ln(ALPHA * x + m[5:6] * y, g_ref[...], b_ref[...], LN_EPS)


def _conv_ffn(x, mod, w_up, conv_w, conv_b, w_down, ln_g, ln_b):
    bsz, seq, d = x.shape
    tm = FFN_ROWS
    per_tile = tm // SUBLANES
    wb = w_up.astype(BF16)
    cb = conv_b.reshape(1, -1)
    once = dict(pipeline_mode=pl.Buffered(1))
    return pl.pallas_call(
        _ffn_kernel,
        grid=(bsz, seq // tm),
        in_specs=[pl.BlockSpec((1, tm, d), lambda b, i: (b, i, 0)),
                  pl.BlockSpec((1, SUBLANES, d),
                               lambda b, i: (b, jnp.maximum(i * per_tile - 1, 0), 0)),
                  pl.BlockSpec((1, 6, d), lambda b, i: (b, 0, 0)),
                  pl.BlockSpec((d, D_FF), lambda b, i: (0, 0), **once),
                  pl.BlockSpec((d, D_FF), lambda b, i: (0, 1), **once),
                  pl.BlockSpec((3, D_FF), lambda b, i: (0, 0), **once),
                  pl.BlockSpec((3, D_FF), lambda b, i: (0, 1), **once),
                  pl.BlockSpec((1, D_FF), lambda b, i: (0, 0), **once),
                  pl.BlockSpec((1, D_FF), lambda b, i: (0, 1), **once),
                  pl.BlockSpec((D_FF, d), lambda b, i: (0, 0), **once),
                  pl.BlockSpec((1, d), lambda b, i: (0, 0)),
                  pl.BlockSpec((1, d), lambda b, i: (0, 0))],
        out_specs=pl.BlockSpec((1, tm, d), lambda b, i: (b, i, 0)),
        out_shape=jax.ShapeDtypeStruct((bsz, seq, d), F32),
        compiler_params=_params(("parallel", "parallel")),
        name="conv_ffn_ln",
    )(x, x, mod, wb, wb, conv_w, conv_w, cb, cb, w_down.astype(BF16),
      ln_g.reshape(1, d), ln_b.reshape(1, d))


def _rope_tables(seq):
    inv = ROPE_THETA ** (-jnp.arange(0, ROPE_DIM, 2, dtype=F32) / ROPE_DIM)
    ang = jnp.arange(seq, dtype=F32)[:, None] * inv[None, :]
    cos, sin = jnp.cos(ang), jnp.sin(ang)
    half = ROPE_DIM // 2
    rest = HEAD_DIM - ROPE_DIM
    one = jnp.ones((seq, rest), F32)
    zero = jnp.zeros((seq, rest), F32)
    zh = jnp.zeros((seq, half), F32)
    head = lambda *parts: jnp.concatenate(parts * (LANES // HEAD_DIM), axis=1)
    return head(cos, cos, one), head(-sin, zh, zero), head(zh, sin, zero)


def kernel(x, c, ada_w, ada_b, ln_g, ln_b, ffn_w_up, ffn_conv_w, ffn_conv_b, ffn_w_down, ev_w_in, ev_w_out, rw_mu, rw_w0, rw_w2, rw_a0, rw_a2, rw_g2, rw_k_k, rw_k_a, rw_r_k, rw_gn_g, rw_gn_b, ik_ln_g, ik_ln_b, od_w_in, od_w_out, pool_w, pool_scale, sg_ln_g, sg_ln_b, sg_w, sg_b):
    seq = x.shape[1]
    tables = _rope_tables(seq)
    mods = _modulation(c, ada_w, ada_b)
    ev_w = _split_even_weights(ev_w_in)
    for layer in range(DEPTH):
        mod = mods[layer]
        if layer % 2 == 0:
            e = layer // 2
            p_r, q_t, k, v_t, qi_t, ki, wi_t = _even_in_proj(x, mod, [w[e] for w in ev_w],
                                                             ik_ln_g[e], ik_ln_b[e], tables)
            ya = _rwkv(p_r, rw_mu[e], rw_w0[e], rw_w2[e], rw_a0[e], rw_a2[e], rw_g2[e],
                       rw_k_k[e], rw_k_a[e], rw_r_k[e], rw_gn_g[e], rw_gn_b[e])
            yb = _dsa(q_t, k, v_t, qi_t, ki, wi_t)
            w_out = ev_w_out[e]
            x = _proj_ln([ya, yb], [w_out[:RWKV_WIDTH], w_out[RWKV_WIDTH:]], x, mod, 2,
                         ln_g[layer, 0], ln_b[layer, 0], "even_out_proj_ln")
        else:
            o = layer // 2
            x = _odd_layer(x, mod, od_w_in[o], pool_w[o], pool_scale[o], sg_ln_g[o], sg_ln_b[o],
                           sg_w[o], sg_b[o], od_w_out[o], ln_g[layer, 0], ln_b[layer, 0])
        x = _conv_ffn(x, mod, ffn_w_up[layer], ffn_conv_w[layer], ffn_conv_b[layer],
                      ffn_w_down[layer], ln_g[layer, 1], ln_b[layer, 1])
    return x
```

```python
import functools

import numpy as np
import jax
import jax.numpy as jnp
from jax import lax
from jax.experimental import pallas as pl
from jax.experimental.pallas import tpu as pltpu

F32 = jnp.float32
BF16 = jnp.bfloat16
I32 = jnp.int32

D_MODEL = 1024
DEPTH = 4
CHUNK = 64
HEAD_DIM = 64
RWKV_WIDTH = D_MODEL // 2
RWKV_HEADS = RWKV_WIDTH // HEAD_DIM
RWKV_LORA_W = 64
RWKV_LORA_A = 64
RWKV_LORA_G = 128
RWKV_COLS = 3 * RWKV_WIDTH + RWKV_LORA_W + RWKV_LORA_A + RWKV_LORA_G
DSA_WIDTH = D_MODEL - RWKV_WIDTH
DSA_HEADS = DSA_WIDTH // HEAD_DIM
IDX_HEADS = 4
IDX_DIM = 64
INDEX_TOPK = 256
Q_BLOCK = 128
ROPE_THETA = 500000.0
ROPE_DIM = HEAD_DIM // 4
POOL_WINDOWS = (2, 4, 8, 16)
POOL_WIDTH = D_MODEL // 2
POOL_GROUP_DIM = POOL_WIDTH // len(POOL_WINDOWS)
SG_WIDTH = D_MODEL - POOL_WIDTH
SG_GROUPS = 4
SG_GROUP_DIM = SG_WIDTH // SG_GROUPS
SG_CHUNK = 128
D_FF = 2816
ALPHA = (2.0 * DEPTH) ** 0.25
LN_EPS = 1e-5
GN_EPS = 64e-5
NEG_INF = -1e30
INT_MIN = -(2 ** 31)
LOG2E = 1.4426950408889634

LANES = 128
SUBLANES = 8
MXU_TILE = 256
VMEM_LIMIT = 56 * 1024 * 1024

MOD_COLS = 1536
SPLIT_ROWS = 256
IN_PROJ_ROWS = 256
ODD_ROWS = 256
OUT_PROJ_ROWS = 512
FFN_ROWS = 512
RW_TILE = 128
RW_CHUNK = 16
RW_ROWS = 4
KEY_TILE = 1024
TRI_TILE = 256
TOP_BITS = 15
PAD_IDX = 384

def _dot(a, b):
    return jnp.dot(a.astype(BF16), b.astype(BF16), preferred_element_type=F32)


def _dot_nt(a, b):
    return lax.dot_general(a.astype(BF16), b.astype(BF16), (((1,), (1,)), ((), ())),
                           preferred_element_type=F32)


def _dot_split(x, m01, terms):
    acc = None
    rem = x
    for _ in range(terms):
        piece = rem.astype(BF16)
        rem = rem - piece.astype(F32)
        part = jnp.dot(piece, m01, preferred_element_type=F32)
        acc = part if acc is None else acc + part
    return acc


def _dot_split_left(m01, x, terms):
    acc = None
    rem = x
    for _ in range(terms):
        piece = rem.astype(BF16)
        rem = rem - piece.astype(F32)
        part = jnp.dot(m01, piece, preferred_element_type=F32)
        acc = part if acc is None else acc + part
    return acc


def _ln(x, g, b, eps):
    mu = jnp.mean(x, axis=-1, keepdims=True)
    xc = x - mu
    var = jnp.mean(xc * xc, axis=-1, keepdims=True)
    return xc * lax.rsqrt(var + eps) * g + b


def _gelu(x):
    return 0.5 * x * (1.0 + lax.erf(x * 0.7071067811865476))


RED_ROWS = 64


def _fold_rows(x, op):
    rows, n = x.shape
    part = op(x.reshape(rows // RED_ROWS, RED_ROWS, n), axis=0)
    return op(part, axis=0, keepdims=True)


def _params(sem):
    return pltpu.CompilerParams(dimension_semantics=sem, vmem_limit_bytes=VMEM_LIMIT)


def _mod_kernel(c_ref, w_ref, b_ref, o_ref):
    c = c_ref[...]
    ca = c * jax.nn.sigmoid(c)
    o_ref[0] = jnp.dot(ca, w_ref[0], preferred_element_type=F32,
                       precision=lax.Precision.HIGHEST) + b_ref[0]


def _modulation(c, ada_w, ada_b):
    bsz, d = c.shape
    depth = ada_w.shape[0]
    n = ada_w.shape[2]
    tn = MOD_COLS
    rows = -(-bsz // SUBLANES) * SUBLANES
    c8 = jnp.pad(c, ((0, rows - bsz), (0, 0)))
    out = pl.pallas_call(
        _mod_kernel,
        grid=(depth, n // tn),
        in_specs=[pl.BlockSpec((rows, d), lambda l, j: (0, 0)),
                  pl.BlockSpec((1, d, tn), lambda l, j: (l, 0, j)),
                  pl.BlockSpec((1, 1, tn), lambda l, j: (l, 0, j))],
        out_specs=pl.BlockSpec((1, rows, tn), lambda l, j: (l, 0, j)),
        out_shape=jax.ShapeDtypeStruct((depth, rows, n), F32),
        compiler_params=_params(("arbitrary", "arbitrary")),
        name="adaln_mod",
    )(c8, ada_w, ada_b.reshape(depth, 1, n))
    return out[:, :bsz].reshape(depth, bsz, 6, d)


def _rope(x, cos_t, sin_a, sin_b):
    n = x.shape[1] // LANES
    rep = (lambda t: jnp.concatenate([t] * n, axis=1)) if n > 1 else (lambda t: t)
    width = x.shape[1]
    half = ROPE_DIM // 2
    return (x * rep(cos_t) + pltpu.roll(x, width - half, 1) * rep(sin_a)
            + pltpu.roll(x, half, 1) * rep(sin_b))


def _even_in_kernel(x_ref, mod_ref, wr_ref, wqkv_ref, widx_ref, cos_ref, sa_ref, sb_ref,
                    ikg_ref, ikb_ref,
                    pr_ref, qt_ref, k_ref, vt_ref, qit_ref, ki_ref, wit_ref):
    m = mod_ref[0]
    h = (x_ref[0] * (1.0 + m[1:2]) + m[0:1]).astype(BF16)
    pr_ref[0] = jnp.dot(h, wr_ref[...], preferred_element_type=F32)
    qkv = jnp.dot(h, wqkv_ref[...], preferred_element_type=F32)
    cos_t = cos_ref[...]
    sin_a = sa_ref[...]
    sin_b = sb_ref[...]
    w = DSA_WIDTH
    q = _rope(qkv[:, :w], cos_t, sin_a, sin_b) * (HEAD_DIM ** -0.5 * LOG2E)
    qt_ref[0] = q.T.astype(BF16)
    k_ref[0] = _rope(qkv[:, w:2 * w], cos_t, sin_a, sin_b).astype(BF16)
    vt_ref[0] = qkv[:, 2 * w:].T.astype(BF16)
    idx = jnp.dot(h, widx_ref[...], preferred_element_type=F32)
    nq = IDX_HEADS * IDX_DIM
    qit_ref[0] = _rope(idx[:, :nq], cos_t, sin_a, sin_b).T.astype(BF16)
    blk = idx[:, nq:nq + LANES]
    lane = lax.broadcasted_iota(I32, blk.shape, 1)
    is_k = lane < IDX_DIM
    mu = jnp.sum(jnp.where(is_k, blk, 0.0), axis=1, keepdims=True) * (1.0 / IDX_DIM)
    xc = jnp.where(is_k, blk - mu, 0.0)
    var = jnp.sum(xc * xc, axis=1, keepdims=True) * (1.0 / IDX_DIM)
    kin = xc * lax.rsqrt(var + LN_EPS) * ikg_ref[...] + ikb_ref[...]
    ki_ref[0] = _rope(kin, cos_t, sin_a, sin_b)[:, :IDX_DIM].astype(BF16)
    wit = (blk * (IDX_HEADS ** -0.5 * IDX_DIM ** -0.5)).T
    wit_ref[0] = wit[IDX_DIM:IDX_DIM + SUBLANES]


def _split_w_kernel(w_ref, wr_ref, wqkv_ref, widx_ref):
    w = w_ref[0]
    c1 = RWKV_COLS
    c2 = RWKV_COLS + 3 * DSA_WIDTH
    wr_ref[0] = w[:, :c1].astype(BF16)
    wqkv_ref[0] = w[:, c1:c2].astype(BF16)
    tail = w[:, c2:]
    zeros = jnp.zeros((w.shape[0], PAD_IDX - tail.shape[1]), F32)
    widx_ref[0] = jnp.concatenate([tail, zeros], axis=1).astype(BF16)


def _split_even_weights(ev_w_in):
    n_even, d, n = ev_w_in.shape
    tr = SPLIT_ROWS
    widths = (RWKV_COLS, 3 * DSA_WIDTH, PAD_IDX)
    return pl.pallas_call(
        _split_w_kernel,
        grid=(n_even, d // tr),
        in_specs=[pl.BlockSpec((1, tr, n), lambda e, i: (e, i, 0))],
        out_specs=[pl.BlockSpec((1, tr, wd), lambda e, i: (e, i, 0)) for wd in widths],
        out_shape=[jax.ShapeDtypeStruct((n_even, d, wd), BF16) for wd in widths],
        compiler_params=_params(("parallel", "parallel")),
        name="split_even_weights",
    )(ev_w_in)


def _even_in_proj(x, mod, weights, ik_g, ik_b, tables):
    bsz, seq, d = x.shape
    tm = IN_PROJ_ROWS
    w_r, w_qkv, w_idx = weights
    pad = LANES - IDX_DIM
    ikg = jnp.pad(ik_g, (0, pad)).reshape(1, LANES)
    ikb = jnp.pad(ik_b, (0, pad)).reshape(1, LANES)
    cos_t, sin_a, sin_b = tables
    full = lambda shape: pl.BlockSpec(shape, lambda b, i: (0,) * len(shape))
    tab = pl.BlockSpec((tm, LANES), lambda b, i: (i, 0))
    nq = IDX_HEADS * IDX_DIM
    rows = lambda n: pl.BlockSpec((1, tm, n), lambda b, i: (b, i, 0))
    cols = lambda n: pl.BlockSpec((1, n, tm), lambda b, i: (b, 0, i))
    return pl.pallas_call(
        _even_in_kernel,
        grid=(bsz, seq // tm),
        in_specs=[pl.BlockSpec((1, tm, d), lambda b, i: (b, i, 0)),
                  pl.BlockSpec((1, 6, d), lambda b, i: (b, 0, 0)),
                  full(w_r.shape), full(w_qkv.shape), full(w_idx.shape),
                  tab, tab, tab, full((1, LANES)), full((1, LANES))],
        out_specs=[rows(RWKV_COLS), cols(DSA_WIDTH), rows(DSA_WIDTH), cols(DSA_WIDTH),
                   cols(nq), rows(IDX_DIM), cols(SUBLANES)],
        out_shape=[jax.ShapeDtypeStruct((bsz, seq, RWKV_COLS), F32),
                   jax.ShapeDtypeStruct((bsz, DSA_WIDTH, seq), BF16),
                   jax.ShapeDtypeStruct((bsz, seq, DSA_WIDTH), BF16),
                   jax.ShapeDtypeStruct((bsz, DSA_WIDTH, seq), BF16),
                   jax.ShapeDtypeStruct((bsz, nq, seq), BF16),
                   jax.ShapeDtypeStruct((bsz, seq, IDX_DIM), BF16),
                   jax.ShapeDtypeStruct((bsz, SUBLANES, seq), F32)],
        compiler_params=_params(("parallel", "parallel")),
        name="even_in_proj",
    )(x, mod, w_r, w_qkv, w_idx, cos_t, sin_a, sin_b, ikg, ikb)


def _rwkv_kernel(p_ref, pp_ref, mu_ref, vec_ref, w2_ref, a2_ref, g2_ref, ltri_ref, ustr_ref,
                 seg_ref, o_ref, s_ref, obuf_ref):
    i = pl.program_id(1)

    @pl.when(i == 0)
    def _():
        s_ref[...] = jnp.zeros_like(s_ref)

    tt = RW_TILE
    w = RWKV_WIDTH
    hd = HEAD_DIM
    nb = p_ref.shape[0]
    nchunk = tt // RW_CHUNK
    seg = seg_ref[...]
    half_w = seg.shape[0]

    def segsum(t):
        return jnp.concatenate([_dot_split(t[:, c:c + half_w], seg, 2)
                                for c in range(0, w, half_w)], axis=1)

    vec = vec_ref[...]
    w0, a0, k_k, k_a, r_k, gn_g, gn_b = (vec[j:j + 1] for j in range(7))
    rowi = lax.broadcasted_iota(I32, (tt, 1), 0)
    live = (i > 0).astype(F32)

    def prepare(b):
        p = p_ref[b]
        prow = pp_ref[b][SUBLANES - 1:SUBLANES] * live
        xprev = jnp.where(rowi == 0, prow, pltpu.roll(p, 1, 0))
        ps = p + (xprev - p) * mu_ref[...]
        r = ps[:, :w]
        k = ps[:, w:2 * w]
        v = ps[:, 2 * w:3 * w]
        o1 = 3 * w
        wd = ps[:, o1:o1 + RWKV_LORA_W]
        ad = ps[:, o1 + RWKV_LORA_W:o1 + RWKV_LORA_W + RWKV_LORA_A]
        gd = ps[:, o1 + RWKV_LORA_W + RWKV_LORA_A:]
        y = -(w0 + _dot(jnp.tanh(wd), w2_ref[...]))
        softplus = jnp.maximum(y, 0.0) + jnp.log1p(jnp.exp(-jnp.abs(y)))
        logw = -jnp.exp(-softplus - 0.5)
        a = jax.nn.sigmoid(a0 + _dot(ad, a2_ref[...]))
        g = _dot(jax.nn.sigmoid(gd), g2_ref[...])
        kk = k * k_k
        kk = kk / jnp.maximum(jnp.sqrt(segsum(kk * kk)), 1e-12)
        k2 = k * (1.0 + (a - 1.0) * k_a)
        bonus = segsum(r * k2 * r_k) * v
        cum = _dot_split_left(ltri_ref[...], logw, 3)
        rem = _dot_split_left(ustr_ref[...], logw, 3)
        pt = jnp.exp(cum)
        ipt = jnp.exp(-cum)
        erem = jnp.exp(rem)
        kka = kk * a
        return dict(at=-kk * jnp.exp(cum - logw), rt=r * pt, bt=kka * ipt, kt=k2 * ipt,
                    bp=kka * erem, kp=k2 * erem, v=v, pt=pt, bonus=bonus, g=g)

    rows_in = [prepare(b) for b in range(nb)]

    ti = lax.broadcasted_iota(I32, (tt, tt), 0)
    si = lax.broadcasted_iota(I32, (tt, tt), 1)
    same = (ti // RW_CHUNK) == (si // RW_CHUNK)
    strict = same & (si < ti)
    incl = same & (si <= ti)
    tb = lax.broadcasted_iota(I32, (tt, nchunk * hd), 0)
    cb = lax.broadcasted_iota(I32, (tt, nchunk * hd), 1)
    blkmask = (tb // RW_CHUNK) == (cb // hd)
    tile_chunks = lambda t: jnp.where(blkmask, jnp.concatenate([t] * nchunk, axis=1), 0.0)

    units = [(b, slice(h * hd, (h + 1) * hd)) for b in range(nb) for h in range(RWKV_HEADS)]
    idx = range(len(units))
    pick = lambda name: [rows_in[b][name][:, sl] for b, sl in units]
    at_h, rt_h, vh, bt_h, kt_h = pick("at"), pick("rt"), pick("v"), pick("bt"), pick("kt")
    x = [_dot_nt(jnp.concatenate([at_h[u], rt_h[u]], axis=0),
                 jnp.concatenate([bt_h[u], kt_h[u]], axis=0)) for u in idx]
    a_ak = [jnp.where(strict, x[u][:tt, tt:], 0.0) for u in idx]
    a_rb = [jnp.where(incl, x[u][tt:, :tt], 0.0) for u in idx]
    a_rk = [jnp.where(incl, x[u][tt:, tt:], 0.0) for u in idx]
    def bd(m0, m1):
        z0 = jnp.zeros((m0.shape[0], m1.shape[1]), m0.dtype)
        z1 = jnp.zeros((m1.shape[0], m0.shape[1]), m1.dtype)
        return jnp.concatenate([jnp.concatenate([m0, z0], axis=1),
                                jnp.concatenate([z1, m1], axis=1)], axis=0)

    def pair_dot(lhs, rhs):
        out = []
        for u in range(0, len(lhs), 2):
            res = _dot(jnp.concatenate([lhs[u], lhs[u + 1]], axis=1), bd(rhs[u], rhs[u + 1]))
            cut = rhs[u].shape[1]
            out += [res[:, :cut], res[:, cut:]]
        return out

    aak_v = pair_dot(a_ak, vh)
    eye = (ti == si).astype(F32)
    inv = None
    blk = 1
    while blk < RW_CHUNK:
        below = same & ((ti // blk) % 2 == 1) & ((si // blk) % 2 == 0) & (
            (ti // (2 * blk)) == (si // (2 * blk)))
        a21 = [jnp.where(below, x[u][:tt, :tt], 0.0) for u in idx]
        if inv is None:
            inv = [eye + a21[u] for u in idx]
        else:
            left = pair_dot(a21, inv)
            grow = pair_dot(inv, left)
            inv = [inv[u] + grow[u] for u in idx]
        blk *= 2
    yv = pair_dot(inv, [jnp.concatenate([at_h[u], aak_v[u]], axis=1) for u in idx])
    arb_y = pair_dot(a_rb, yv)
    ark_v = pair_dot(a_rk, vh)
    qt = [rt_h[u] + arb_y[u][:, :hd] for u in idx]
    o0 = [arb_y[u][:, hd:] + ark_v[u] for u in idx]
    yt = [yv[u].T for u in idx]
    bpb = [tile_chunks(t) for t in pick("bp")]
    kpb = [tile_chunks(t) for t in pick("kp")]
    g_all = [_dot(yt[u][:hd], bpb[u]) for u in idx]
    h_all = [_dot(jnp.concatenate([yt[u][hd:], vh[u].T], axis=1),
                  jnp.concatenate([bpb[u], kpb[u]], axis=0)) for u in idx]
    pt_h = pick("pt")
    pairs = range(0, len(units), 2)
    low_lanes = lax.broadcasted_iota(I32, (hd, 2 * hd), 1) < hd
    s = [jnp.concatenate([s_ref[u], s_ref[u + 1]], axis=1) for u in pairs]
    for n in range(nchunk):
        rows = slice(n * RW_CHUNK, (n + 1) * RW_CHUNK)
        cols = slice(n * hd, (n + 1) * hd)
        last = (n + 1) * RW_CHUNK - 1
        for j, u in enumerate(pairs):
            s_bd = jnp.concatenate([jnp.where(low_lanes, s[j], 0.0),
                                    jnp.where(low_lanes, 0.0, s[j])], axis=0)
            o_pair = _dot_nt(jnp.concatenate([qt[u][rows], qt[u + 1][rows]], axis=1), s_bd)
            for i in range(2):
                b, sl = units[u + i]
                obuf_ref[b, rows, sl] = o_pair[:, i * hd:(i + 1) * hd] + o0[u + i][rows]
        s = [s[j] * jnp.concatenate([pt_h[u][last:last + 1], pt_h[u + 1][last:last + 1]], axis=1)
             + _dot(s[j], bd(g_all[u][:, cols], g_all[u + 1][:, cols]))
             + jnp.concatenate([h_all[u][:, cols], h_all[u + 1][:, cols]], axis=1)
             for j, u in enumerate(pairs)]
    for j, u in enumerate(pairs):
        s_ref[u] = s[j][:, :hd]
        s_ref[u + 1] = s[j][:, hd:]

    for b in range(nb):
        o = obuf_ref[b]
        mean = segsum(o) * (1.0 / hd)
        oc = o - mean
        var = segsum(oc * oc) * (1.0 / hd)
        on = oc * lax.rsqrt(var + GN_EPS) * gn_g + gn_b
        o_ref[b] = ((on + rows_in[b]["bonus"]) * rows_in[b]["g"]).astype(BF16)


def _rwkv(p_r, mu, w0, w2, a0, a2, g2, k_k, k_a, r_k, gn_g, gn_b):
    bsz, seq, _ = p_r.shape
    tt = RW_TILE
    nb = RW_ROWS if bsz % RW_ROWS == 0 else 1
    w = RWKV_WIDTH
    vec = jnp.stack([w0, a0, k_k, k_a, r_k.reshape(w), gn_g, gn_b, jnp.zeros_like(w0)])
    t_idx = np.arange(tt)
    same = (t_idx[:, None] // RW_CHUNK) == (t_idx[None, :] // RW_CHUNK)
    ltri = jnp.asarray(same & (t_idx[None, :] <= t_idx[:, None]), BF16)
    ustr = jnp.asarray(same & (t_idx[None, :] > t_idx[:, None]), BF16)
    c_idx = np.arange(MXU_TILE)
    seg = jnp.asarray((c_idx[:, None] // HEAD_DIM) == (c_idx[None, :] // HEAD_DIM), BF16)
    full = lambda shape: pl.BlockSpec(shape, lambda b, i: (0,) * len(shape))
    per_tile = tt // SUBLANES
    return pl.pallas_call(
        _rwkv_kernel,
        grid=(bsz // nb, seq // tt),
        in_specs=[pl.BlockSpec((nb, tt, RWKV_COLS), lambda b, i: (b, i, 0)),
                  pl.BlockSpec((nb, SUBLANES, RWKV_COLS),
                               lambda b, i: (b, jnp.maximum(i * per_tile - 1, 0), 0)),
                  full((1, RWKV_COLS)), full((SUBLANES, w)),
                  full(w2.shape), full(a2.shape), full(g2.shape),
                  full((tt, tt)), full((tt, tt)), full((MXU_TILE, MXU_TILE))],
        out_specs=pl.BlockSpec((nb, tt, w), lambda b, i: (b, i, 0)),
        out_shape=jax.ShapeDtypeStruct((bsz, seq, w), BF16),
        scratch_shapes=[pltpu.VMEM((nb * RWKV_HEADS, HEAD_DIM, HEAD_DIM), F32),
                        pltpu.VMEM((nb, tt, w), F32)],
        compiler_params=_params(("parallel", "arbitrary")),
        name="rwkv7_mix",
    )(p_r, p_r, mu.reshape(1, RWKV_COLS), vec, w2.astype(BF16), a2.astype(BF16),
      g2.astype(BF16), ltri, ustr, seg)


def _dsa_kernel(qt_ref, k_ref, vt_ref, qit_ref, ki_ref, wit_ref, low_ref, o_ref,
                key_ref, top_ref, bias_ref, acc_ref, m_ref, l_ref, *, topk):
    qb = pl.program_id(1)
    nq = Q_BLOCK
    kt_sz = bias_ref.shape[0]
    half = kt_sz // 2
    hd = HEAD_DIM
    start = qb * nq
    nkt = (start + nq + kt_sz - 1) // kt_sz
    nun = (start + nq + half - 1) // half
    col = lax.broadcasted_iota(I32, (1, nq), 1)
    lim = start + (col // CHUNK + 1) * CHUNK
    wit = wit_ref[0]
    qit = qit_ref[0]
    qi_cat = jnp.concatenate([qit[h * IDX_DIM:(h + 1) * IDX_DIM] for h in range(IDX_HEADS)],
                             axis=1)

    def score_tile(j, carry):
        off = pl.multiple_of(j * kt_sz, kt_sz)
        d = jnp.dot(ki_ref[0, pl.ds(off, kt_sz), :], qi_cat, preferred_element_type=F32)
        s = jnp.zeros((kt_sz, nq), F32)
        for h in range(IDX_HEADS):
            s = s + wit[h:h + 1] * jnp.maximum(d[:, h * nq:(h + 1) * nq], 0.0)
        s = s + 0.0
        bits = pltpu.bitcast(s, I32)
        key = bits ^ ((bits >> 31) & 0x7FFFFFFF)
        sidx = off + lax.broadcasted_iota(I32, (kt_sz, nq), 0)
        key = jnp.where(sidx < lim, key, INT_MIN)
        key_ref[pl.ds(off, kt_sz), :] = key
        field = (key >> (32 - TOP_BITS)) + (3 << (TOP_BITS - 1))
        top_ref[pl.ds(pl.multiple_of(j * half, half), half), :] = (field[:half] << 16) | field[half:]
        return carry

    lax.fori_loop(0, nkt, score_tile, 0)

    def count(pred):
        def body(j, c):
            off = pl.multiple_of(j * kt_sz, kt_sz)
            hit = jnp.where(pred(key_ref[pl.ds(off, kt_sz), :]), 1, 0)
            return c + hit.reshape(kt_sz // RED_ROWS, RED_ROWS, nq).sum(axis=0)
        c = lax.fori_loop(0, nkt, body, jnp.zeros((RED_ROWS, nq), I32))
        return jnp.sum(c, axis=0, keepdims=True)

    def count_top(cand):
        both = (cand << 16) | cand

        def body(j, c):
            off = pl.multiple_of(j * half, half)
            hit = lax.shift_right_logical(top_ref[pl.ds(off, half), :] - both, 15) & 0x00010001
            return c + hit.reshape(half // RED_ROWS, RED_ROWS, nq).sum(axis=0)

        c = jnp.sum(lax.fori_loop(0, nkt, body, jnp.zeros((RED_ROWS, nq), I32)),
                    axis=0, keepdims=True)
        return (c & 0xFFFF) + lax.shift_right_logical(c, 16)

    def top_step(b, prefix):
        cand = prefix | lax.shift_left(jnp.int32(1), TOP_BITS - 1 - b)
        return jnp.where(count_top(cand) >= topk, cand, prefix)

    prefix = lax.fori_loop(0, TOP_BITS, top_step, jnp.zeros((1, nq), I32))
    lo = (prefix - (1 << (TOP_BITS - 1))) << (32 - TOP_BITS)

    def bit_step(b, lo):
        cand = lo + lax.shift_left(jnp.int32(1), 31 - TOP_BITS - b)
        return jnp.where(count(lambda kv: kv >= cand) >= topk, cand, lo)

    th = lax.fori_loop(0, 32 - TOP_BITS, bit_step, lo)
    need = jnp.where(th == INT_MIN, 0, topk - count(lambda kv: kv > th)).astype(F32)

    acc_ref[...] = jnp.zeros_like(acc_ref)
    m_ref[...] = jnp.full_like(m_ref, NEG_INF)
    l_ref[...] = jnp.zeros_like(l_ref)
    low = low_ref[...]
    npair = DSA_HEADS // 2
    top_rows = lax.broadcasted_iota(I32, (2 * hd, nq), 0) < hd
    qt = qt_ref[0].astype(F32)
    q_bd = []
    for p in range(npair):
        qp = qt[p * 2 * hd:(p + 1) * 2 * hd]
        q_bd.append(jnp.concatenate([jnp.where(top_rows, qp, 0.0), jnp.where(top_rows, 0.0, qp)],
                                    axis=1).astype(BF16))

    pad_rows = 2 * SUBLANES

    def attn_tile(off, rows, eq_before):
        ones_rows = (lax.broadcasted_iota(I32, (pad_rows, rows), 0) == 0).astype(BF16)
        eq_run = eq_before
        for t in range(rows // TRI_TILE):
            kv = key_ref[pl.ds(off + t * TRI_TILE, TRI_TILE), :]
            eq = kv == th
            pre = eq_run + jnp.dot(low, jnp.where(eq, 1.0, 0.0).astype(BF16),
                                   preferred_element_type=F32)
            sel = (kv > th) | (eq & (pre <= need))
            bias_ref[t * TRI_TILE:(t + 1) * TRI_TILE, :] = jnp.where(sel, 0.0, NEG_INF)
            eq_run = pre[TRI_TILE - 1:TRI_TILE]
        bias = bias_ref[:rows]

        def logits(p):
            part = rows // 2
            return jnp.concatenate(
                [jnp.dot(k_ref[0, pl.ds(off + r * part, part), p * 2 * hd:(p + 1) * 2 * hd],
                         q_bd[p], preferred_element_type=F32) for r in range(2)], axis=0)

        st_next = logits(0)
        for p in range(npair):
            cols = slice(p * 2 * hd, (p + 1) * 2 * hd)
            st = st_next
            if p + 1 < npair:
                st_next = logits(p + 1)
            prs, scales = [], []
            for i in range(2):
                h = 2 * p + i
                s = st[:, i * nq:(i + 1) * nq] + bias
                m_old = m_ref[h:h + 1]
                m_new = jnp.maximum(m_old, _fold_rows(s, jnp.max))
                prs.append(jnp.exp2(s - m_new).astype(BF16))
                scales.append(jnp.exp2(m_old - m_new))
                m_ref[h:h + 1] = m_new
            lhs = jnp.concatenate([vt_ref[0, cols, pl.ds(off, rows)], ones_rows], axis=0)
            pv = jnp.dot(lhs, jnp.concatenate(prs, axis=1),
                         preferred_element_type=F32)
            for i in range(2):
                h = 2 * p + i
                l_ref[h:h + 1] = (scales[i] * l_ref[h:h + 1]
                                  + pv[2 * hd:2 * hd + 1, i * nq:(i + 1) * nq])
            new = jnp.where(top_rows, pv[:2 * hd, :nq], pv[:2 * hd, nq:])
            acc_ref[p] = jnp.where(top_rows, scales[0], scales[1]) * acc_ref[p] + new
        return eq_run

    ties_seen = lax.fori_loop(
        0, nun // 2, lambda j, e: attn_tile(pl.multiple_of(j * kt_sz, kt_sz), kt_sz, e),
        jnp.zeros((1, nq), F32))

    @pl.when(nun % 2 == 1)
    def _():
        attn_tile(pl.multiple_of((nun - 1) * half, half), half, ties_seen)

    for p in range(npair):
        inv = jnp.where(top_rows, 1.0 / l_ref[2 * p:2 * p + 1], 1.0 / l_ref[2 * p + 1:2 * p + 2])
        o_ref[0, :, p * 2 * hd:(p + 1) * 2 * hd] = (acc_ref[p] * inv).T.astype(BF16)


def _dsa(q_t, k, v_t, qi_t, ki, wi_t):
    bsz, seq, w = k.shape
    nq = Q_BLOCK
    topk = min(INDEX_TOPK, seq // 4)
    kt_sz = min(KEY_TILE, seq)
    assert kt_sz % (2 * RED_ROWS) == 0 and kt_sz % (2 * TRI_TILE) == 0 and seq // 2 < 2 ** 16
    t_idx = np.arange(TRI_TILE)
    low = jnp.asarray(t_idx[None, :] <= t_idx[:, None], BF16)
    cols = lambda n: pl.BlockSpec((1, n, nq), lambda b, i: (b, 0, i))
    return pl.pallas_call(
        functools.partial(_dsa_kernel, topk=topk),
        grid=(bsz, seq // nq),
        in_specs=[cols(w),
                  pl.BlockSpec((1, seq, w), lambda b, i: (b, 0, 0)),
                  pl.BlockSpec((1, w, seq), lambda b, i: (b, 0, 0)),
                  cols(IDX_HEADS * IDX_DIM),
                  pl.BlockSpec((1, seq, IDX_DIM), lambda b, i: (b, 0, 0)),
                  cols(SUBLANES),
                  pl.BlockSpec((TRI_TILE, TRI_TILE), lambda b, i: (0, 0))],
        out_specs=pl.BlockSpec((1, nq, w), lambda b, i: (b, i, 0)),
        out_shape=jax.ShapeDtypeStruct((bsz, seq, w), BF16),
        scratch_shapes=[pltpu.VMEM((seq, nq), I32),
                        pltpu.VMEM((seq // 2, nq), I32),
                        pltpu.VMEM((kt_sz, nq), F32),
                        pltpu.VMEM((DSA_HEADS // 2, 2 * HEAD_DIM, nq), F32),
                        pltpu.VMEM((DSA_HEADS, nq), F32),
                        pltpu.VMEM((DSA_HEADS, nq), F32)],
        compiler_params=_params(("parallel", "arbitrary")),
        name="dsa_mix",
    )(q_t, k, v_t, qi_t, ki, wi_t, low)


POOL_HALO = 16


def _odd_layer_kernel(x_ref, xp_ref, mod_ref, win_ref, pw_ref, ps_ref, lng_ref, lnb_ref, ws_ref,
                      bs_ref, wout_ref, g_ref, b_ref, o_ref, y_ref):
    i = pl.program_id(1)
    m = mod_ref[0]
    sc = 1.0 + m[1:2]
    sh = m[0:1]
    x = x_ref[0]
    tm = x.shape[0]
    gd = POOL_GROUP_DIM
    p = jnp.dot((x * sc + sh).astype(BF16), win_ref[...], preferred_element_type=F32)
    prev = jnp.dot((xp_ref[0] * sc + sh).astype(BF16), win_ref[:, :POOL_WIDTH],
                   preferred_element_type=F32) * (i > 0).astype(F32)
    t_glob = (i * tm + lax.broadcasted_iota(I32, (tm, 1), 0)).astype(F32)
    scale = ps_ref[...]
    for gi, win in enumerate(POOL_WINDOWS):
        cols = slice(gi * gd, (gi + 1) * gd)
        xg = p[:, cols]
        s = jnp.concatenate([prev[:, cols], xg], axis=0)
        span = 1
        while span < win:
            s = s[span:] + s[:-span]
            span *= 2
        first = POOL_HALO + 1 - win
        pooled = s[first:first + tm] / jnp.minimum(t_glob + 1.0, float(win)) - xg
        y_ref[:, cols] = (_dot(pooled, pw_ref[gi]) * scale[:, cols]).astype(BF16)

    u = _gelu(p[:, POOL_WIDTH:POOL_WIDTH + SG_WIDTH])
    v = _ln(_gelu(p[:, POOL_WIDTH + SG_WIDTH:]), lng_ref[...], lnb_ref[...], LN_EPS)
    ti = lax.broadcasted_iota(I32, (SG_CHUNK, SG_CHUNK), 0)
    si = lax.broadcasted_iota(I32, (SG_CHUNK, SG_CHUNK), 1)
    bs = bs_ref[...]
    for gi in range(SG_GROUPS):
        cols = slice(gi * SG_GROUP_DIM, (gi + 1) * SG_GROUP_DIM)
        ws = jnp.where(si <= ti, ws_ref[gi], 0.0)
        for n in range(tm // SG_CHUNK):
            rows = slice(n * SG_CHUNK, (n + 1) * SG_CHUNK)
            z = _dot(ws, v[rows, cols]) + bs[:, gi:gi + 1]
            y_ref[rows, POOL_WIDTH + gi * SG_GROUP_DIM:POOL_WIDTH + (gi + 1) * SG_GROUP_DIM] = (
                u[rows, cols] * z).astype(BF16)

    y = jnp.dot(y_ref[...], wout_ref[...], preferred_element_type=F32)
    o_ref[0] = _ln(ALPHA * x + m[2:3] * y, g_ref[...], b_ref[...], LN_EPS)


def _odd_layer(x, mod, w_in, pool_w, pool_scale, sg_ln_g, sg_ln_b, sg_w, sg_b, w_out, ln_g, ln_b):
    bsz, seq, d = x.shape
    tm = ODD_ROWS
    per_tile = tm // POOL_HALO
    full = lambda shape: pl.BlockSpec(shape, lambda b, i: (0,) * len(shape))
    once = lambda shape: pl.BlockSpec(shape, lambda b, i: (0,) * len(shape),
                                      pipeline_mode=pl.Buffered(1))
    return pl.pallas_call(
        _odd_layer_kernel,
        grid=(bsz, seq // tm),
        in_specs=[pl.BlockSpec((1, tm, d), lambda b, i: (b, i, 0)),
                  pl.BlockSpec((1, POOL_HALO, d),
                               lambda b, i: (b, jnp.maximum(i * per_tile - 1, 0), 0)),
                  pl.BlockSpec((1, 6, d), lambda b, i: (b, 0, 0)),
                  once(w_in.shape), full(pool_w.shape), full((1, POOL_WIDTH)),
                  full((1, SG_WIDTH)), full((1, SG_WIDTH)), full(sg_w.shape),
                  full((SG_CHUNK, SG_GROUPS)), once(w_out.shape), full((1, d)), full((1, d))],
        out_specs=pl.BlockSpec((1, tm, d), lambda b, i: (b, i, 0)),
        out_shape=jax.ShapeDtypeStruct((bsz, seq, d), F32),
        scratch_shapes=[pltpu.VMEM((tm, d), BF16)],
        compiler_params=_params(("parallel", "parallel")),
        name="odd_layer_mix",
    )(x, x, mod, w_in.astype(BF16), pool_w.astype(BF16), pool_scale.reshape(1, -1),
      sg_ln_g.reshape(1, -1), sg_ln_b.reshape(1, -1), sg_w, sg_b.T, w_out.astype(BF16),
      ln_g.reshape(1, d), ln_b.reshape(1, d))


def _proj_ln_kernel(*refs, n_in, gate_row):
    a_refs = refs[:n_in]
    w_refs = refs[n_in:2 * n_in]
    x_ref, mod_ref, g_ref, b_ref, o_ref = refs[2 * n_in:]
    y = None
    for a_ref, w_ref in zip(a_refs, w_refs):
        part = jnp.dot(a_ref[0], w_ref[...], preferred_element_type=F32)
        y = part if y is None else y + part
    gate = mod_ref[0][gate_row:gate_row + 1]
    o_ref[0] = _ln(ALPHA * x_ref[0] + gate * y, g_ref[...], b_ref[...], LN_EPS)


def _proj_ln(acts, weights, x, mod, gate_row, ln_g, ln_b, name):
    bsz, seq, d = x.shape
    tm = OUT_PROJ_ROWS
    n_in = len(acts)
    in_specs = [pl.BlockSpec((1, tm, a.shape[2]), lambda b, i: (b, i, 0)) for a in acts]
    in_specs += [pl.BlockSpec(w.shape, lambda b, i: (0, 0)) for w in weights]
    in_specs += [pl.BlockSpec((1, tm, d), lambda b, i: (b, i, 0)),
                 pl.BlockSpec((1, 6, d), lambda b, i: (b, 0, 0)),
                 pl.BlockSpec((1, d), lambda b, i: (0, 0)),
                 pl.BlockSpec((1, d), lambda b, i: (0, 0))]
    return pl.pallas_call(
        functools.partial(_proj_ln_kernel, n_in=n_in, gate_row=gate_row),
        grid=(bsz, seq // tm),
        in_specs=in_specs,
        out_specs=pl.BlockSpec((1, tm, d), lambda b, i: (b, i, 0)),
        out_shape=jax.ShapeDtypeStruct((bsz, seq, d), F32),
        compiler_params=_params(("parallel", "parallel")),
        name=name,
    )(*acts, *[w.astype(BF16) for w in weights], x, mod, ln_g.reshape(1, d), ln_b.reshape(1, d))


def _ffn_kernel(x_ref, xp_ref, mod_ref, wg_ref, wv_ref, cwg_ref, cwv_ref, cbg_ref, cbv_ref,
                wd_ref, g_ref, b_ref, o_ref):
    i = pl.program_id(1)
    m = mod_ref[0]
    sc = 1.0 + m[4:5]
    sh = m[3:4]
    x = x_ref[0]
    live = (i > 0).astype(F32)
    h = jnp.concatenate([(xp_ref[0] * sc + sh) * live, x * sc + sh], axis=0).astype(BF16)
    halo = SUBLANES

    def conv(w_ref, cw_ref, cb_ref):
        u = jnp.dot(h, w_ref[...], preferred_element_type=F32)
        cw = cw_ref[...]
        u1 = pltpu.roll(u, 1, 0)
        u2 = pltpu.roll(u, 2, 0)
        return (cb_ref[...] + u2[halo:] * cw[0:1] + u1[halo:] * cw[1:2] + u[halo:] * cw[2:3])

    gate = conv(wg_ref, cwg_ref, cbg_ref)
    val = conv(wv_ref, cwv_ref, cbv_ref)
    act = (gate * jax.nn.sigmoid(gate) * val).astype(BF16)
    y = jnp.dot(act, wd_ref[...], preferred_element_type=F32)
    o_ref[0] = _ln(ALPHA * x + m[5:6] * y, g_ref[...], b_ref[...], LN_EPS)


def _conv_ffn(x, mod, w_up, conv_w, conv_b, w_down, ln_g, ln_b):
    bsz, seq, d = x.shape
    tm = FFN_ROWS
    per_tile = tm // SUBLANES
    wb = w_up.astype(BF16)
    cb = conv_b.reshape(1, -1)
    once = dict(pipeline_mode=pl.Buffered(1))
    return pl.pallas_call(
        _ffn_kernel,
        grid=(bsz, seq // tm),
        in_specs=[pl.BlockSpec((1, tm, d), lambda b, i: (b, i, 0)),
                  pl.BlockSpec((1, SUBLANES, d),
                               lambda b, i: (b, jnp.maximum(i * per_tile - 1, 0), 0)),
                  pl.BlockSpec((1, 6, d), lambda b, i: (b, 0, 0)),
                  pl.BlockSpec((d, D_FF), lambda b, i: (0, 0), **once),
                  pl.BlockSpec((d, D_FF), lambda b, i: (0, 1), **once),
                  pl.BlockSpec((3, D_FF), lambda b, i: (0, 0), **once),
                  pl.BlockSpec((3, D_FF), lambda b, i: (0, 1), **once),
                  pl.BlockSpec((1, D_FF), lambda b, i: (0, 0), **once),
                  pl.BlockSpec((1, D_FF), lambda b, i: (0, 1), **once),
                  pl.BlockSpec((D_FF, d), lambda b, i: (0, 0), **once),
                  pl.BlockSpec((1, d), lambda b, i: (0, 0)),
                  pl.BlockSpec((1, d), lambda b, i: (0, 0))],
        out_specs=pl.BlockSpec((1, tm, d), lambda b, i: (b, i, 0)),
        out_shape=jax.ShapeDtypeStruct((bsz, seq, d), F32),
        compiler_params=_params(("parallel", "parallel")),
        name="conv_ffn_ln",
    )(x, x, mod, wb, wb, conv_w, conv_w, cb, cb, w_down.astype(BF16),
      ln_g.reshape(1, d), ln_b.reshape(1, d))


def _rope_tables(seq):
    inv = ROPE_THETA ** (-jnp.arange(0, ROPE_DIM, 2, dtype=F32) / ROPE_DIM)
    ang = jnp.arange(seq, dtype=F32)[:, None] * inv[None, :]
    cos, sin = jnp.cos(ang), jnp.sin(ang)
    half = ROPE_DIM // 2
    rest = HEAD_DIM - ROPE_DIM
    one = jnp.ones((seq, rest), F32)
    zero = jnp.zeros((seq, rest), F32)
    zh = jnp.zeros((seq, half), F32)
    head = lambda *parts: jnp.concatenate(parts * (LANES // HEAD_DIM), axis=1)
    return head(cos, cos, one), head(-sin, zh, zero), head(zh, sin, zero)


def kernel(x, c, ada_w, ada_b, ln_g, ln_b, ffn_w_up, ffn_conv_w, ffn_conv_b, ffn_w_down, ev_w_in, ev_w_out, rw_mu, rw_w0, rw_w2, rw_a0, rw_a2, rw_g2, rw_k_k, rw_k_a, rw_r_k, rw_gn_g, rw_gn_b, ik_ln_g, ik_ln_b, od_w_in, od_w_out, pool_w, pool_scale, sg_ln_g, sg_ln_b, sg_w, sg_b):
    seq = x.shape[1]
    tables = _rope_tables(seq)
    mods = _modulation(c, ada_w, ada_b)
    ev_w = _split_even_weights(ev_w_in)
    for layer in range(DEPTH):
        mod = mods[layer]
        if layer % 2 == 0:
            e = layer // 2
            p_r, q_t, k, v_t, qi_t, ki, wi_t = _even_in_proj(x, mod, [w[e] for w in ev_w],
                                                             ik_ln_g[e], ik_ln_b[e], tables)
            ya = _rwkv(p_r, rw_mu[e], rw_w0[e], rw_w2[e], rw_a0[e], rw_a2[e], rw_g2[e],
                       rw_k_k[e], rw_k_a[e], rw_r_k[e], rw_gn_g[e], rw_gn_b[e])
            yb = _dsa(q_t, k, v_t, qi_t, ki, wi_t)
            w_out = ev_w_out[e]
            x = _proj_ln([ya, yb], [w_out[:RWKV_WIDTH], w_out[RWKV_WIDTH:]], x, mod, 2,
                         ln_g[layer, 0], ln_b[layer, 0], "even_out_proj_ln")
        else:
            o = layer // 2
            x = _odd_layer(x, mod, od_w_in[o], pool_w[o], pool_scale[o], sg_ln_g[o], sg_ln_b[o],
                           sg_w[o], sg_b[o], od_w_out[o], ln_g[layer, 0], ln_b[layer, 0])
        x = _conv_ffn(x, mod, ffn_w_up[layer], ffn_conv_w[layer], ffn_conv_b[layer],
                      ffn_w_down[layer], ln_g[layer, 1], ln_b[layer, 1])
    return x
```

```python
import functools

import numpy as np
import jax
import jax.numpy as jnp
from jax import lax
from jax.experimental import pallas as pl
from jax.experimental.pallas import tpu as pltpu

F32 = jnp.float32
BF16 = jnp.bfloat16
I32 = jnp.int32

D_MODEL = 1024
DEPTH = 4
CHUNK = 64
HEAD_DIM = 64
RWKV_WIDTH = D_MODEL // 2
RWKV_HEADS = RWKV_WIDTH // HEAD_DIM
RWKV_LORA_W = 64
RWKV_LORA_A = 64
RWKV_LORA_G = 128
RWKV_COLS = 3 * RWKV_WIDTH + RWKV_LORA_W + RWKV_LORA_A + RWKV_LORA_G
DSA_WIDTH = D_MODEL - RWKV_WIDTH
DSA_HEADS = DSA_WIDTH // HEAD_DIM
IDX_HEADS = 4
IDX_DIM = 64
INDEX_TOPK = 256
Q_BLOCK = 128
ROPE_THETA = 500000.0
ROPE_DIM = HEAD_DIM // 4
POOL_WINDOWS = (2, 4, 8, 16)
POOL_WIDTH = D_MODEL // 2
POOL_GROUP_DIM = POOL_WIDTH // len(POOL_WINDOWS)
SG_WIDTH = D_MODEL - POOL_WIDTH
SG_GROUPS = 4
SG_GROUP_DIM = SG_WIDTH // SG_GROUPS
SG_CHUNK = 128
D_FF = 2816
ALPHA = (2.0 * DEPTH) ** 0.25
LN_EPS = 1e-5
GN_EPS = 64e-5
NEG_INF = -1e30
INT_MIN = -(2 ** 31)
LOG2E = 1.4426950408889634

LANES = 128
SUBLANES = 8
MXU_TILE = 256
VMEM_LIMIT = 56 * 1024 * 1024

MOD_COLS = 1536
SPLIT_ROWS = 256
IN_PROJ_ROWS = 256
ODD_ROWS = 256
OUT_PROJ_ROWS = 512
FFN_ROWS = 512
RW_TILE = 128
RW_CHUNK = 16
RW_ROWS = 4
KEY_TILE = 1024
TRI_TILE = 256
TOP_BITS = 15
PAD_IDX = 384

def _dot(a, b):
    return jnp.dot(a.astype(BF16), b.astype(BF16), preferred_element_type=F32)


def _dot_nt(a, b):
    return lax.dot_general(a.astype(BF16), b.astype(BF16), (((1,), (1,)), ((), ())),
                           preferred_element_type=F32)


def _dot_split(x, m01, terms):
    acc = None
    rem = x
    for _ in range(terms):
        piece = rem.astype(BF16)
        rem = rem - piece.astype(F32)
        part = jnp.dot(piece, m01, preferred_element_type=F32)
        acc = part if acc is None else acc + part
    return acc


def _dot_split_left(m01, x, terms):
    acc = None
    rem = x
    for _ in range(terms):
        piece = rem.astype(BF16)
        rem = rem - piece.astype(F32)
        part = jnp.dot(m01, piece, preferred_element_type=F32)
        acc = part if acc is None else acc + part
    return acc


def _ln(x, g, b, eps):
    mu = jnp.mean(x, axis=-1, keepdims=True)
    xc = x - mu
    var = jnp.mean(xc * xc, axis=-1, keepdims=True)
    return xc * lax.rsqrt(var + eps) * g + b


def _gelu(x):
    return 0.5 * x * (1.0 + lax.erf(x * 0.7071067811865476))


RED_ROWS = 64


def _fold_rows(x, op):
    rows, n = x.shape
    part = op(x.reshape(rows // RED_ROWS, RED_ROWS, n), axis=0)
    return op(part, axis=0, keepdims=True)


def _params(sem):
    return pltpu.CompilerParams(dimension_semantics=sem, vmem_limit_bytes=VMEM_LIMIT)


def _mod_kernel(c_ref, w_ref, b_ref, o_ref):
    c = c_ref[...]
    ca = c * jax.nn.sigmoid(c)
    o_ref[0] = jnp.dot(ca, w_ref[0], preferred_element_type=F32,
                       precision=lax.Precision.HIGHEST) + b_ref[0]


def _modulation(c, ada_w, ada_b):
    bsz, d = c.shape
    depth = ada_w.shape[0]
    n = ada_w.shape[2]
    tn = MOD_COLS
    rows = -(-bsz // SUBLANES) * SUBLANES
    c8 = jnp.pad(c, ((0, rows - bsz), (0, 0)))
    out = pl.pallas_call(
        _mod_kernel,
        grid=(depth, n // tn),
        in_specs=[pl.BlockSpec((rows, d), lambda l, j: (0, 0)),
                  pl.BlockSpec((1, d, tn), lambda l, j: (l, 0, j)),
                  pl.BlockSpec((1, 1, tn), lambda l, j: (l, 0, j))],
        out_specs=pl.BlockSpec((1, rows, tn), lambda l, j: (l, 0, j)),
        out_shape=jax.ShapeDtypeStruct((depth, rows, n), F32),
        compiler_params=_params(("arbitrary", "arbitrary")),
        name="adaln_mod",
    )(c8, ada_w, ada_b.reshape(depth, 1, n))
    return out[:, :bsz].reshape(depth, bsz, 6, d)


def _rope(x, cos_t, sin_a, sin_b):
    n = x.shape[1] // LANES
    rep = (lambda t: jnp.concatenate([t] * n, axis=1)) if n > 1 else (lambda t: t)
    width = x.shape[1]
    half = ROPE_DIM // 2
    return (x * rep(cos_t) + pltpu.roll(x, width - half, 1) * rep(sin_a)
            + pltpu.roll(x, half, 1) * rep(sin_b))


def _even_in_kernel(x_ref, mod_ref, wr_ref, wqkv_ref, widx_ref, cos_ref, sa_ref, sb_ref,
                    ikg_ref, ikb_ref,
                    pr_ref, qt_ref, k_ref, vt_ref, qit_ref, ki_ref, wit_ref):
    m = mod_ref[0]
    h = (x_ref[0] * (1.0 + m[1:2]) + m[0:1]).astype(BF16)
    pr_ref[0] = jnp.dot(h, wr_ref[...], preferred_element_type=F32)
    qkv = jnp.dot(h, wqkv_ref[...], preferred_element_type=F32)
    cos_t = cos_ref[...]
    sin_a = sa_ref[...]
    sin_b = sb_ref[...]
    w = DSA_WIDTH
    q = _rope(qkv[:, :w], cos_t, sin_a, sin_b) * (HEAD_DIM ** -0.5 * LOG2E)
    qt_ref[0] = q.T.astype(BF16)
    k_ref[0] = _rope(qkv[:, w:2 * w], cos_t, sin_a, sin_b).astype(BF16)
    vt_ref[0] = qkv[:, 2 * w:].T.astype(BF16)
    idx = jnp.dot(h, widx_ref[...], preferred_element_type=F32)
    nq = IDX_HEADS * IDX_DIM
    qit_ref[0] = _rope(idx[:, :nq], cos_t, sin_a, sin_b).T.astype(BF16)
    blk = idx[:, nq:nq + LANES]
    lane = lax.broadcasted_iota(I32, blk.shape, 1)
    is_k = lane < IDX_DIM
    mu = jnp.sum(jnp.where(is_k, blk, 0.0), axis=1, keepdims=True) * (1.0 / IDX_DIM)
    xc = jnp.where(is_k, blk - mu, 0.0)
    var = jnp.sum(xc * xc, axis=1, keepdims=True) * (1.0 / IDX_DIM)
    kin = xc * lax.rsqrt(var + LN_EPS) * ikg_ref[...] + ikb_ref[...]
    ki_ref[0] = _rope(kin, cos_t, sin_a, sin_b)[:, :IDX_DIM].astype(BF16)
    wit = (blk * (IDX_HEADS ** -0.5 * IDX_DIM ** -0.5)).T
    wit_ref[0] = wit[IDX_DIM:IDX_DIM + SUBLANES]


def _split_w_kernel(w_ref, wr_ref, wqkv_ref, widx_ref):
    w = w_ref[0]
    c1 = RWKV_COLS
    c2 = RWKV_COLS + 3 * DSA_WIDTH
    wr_ref[0] = w[:, :c1].astype(BF16)
    wqkv_ref[0] = w[:, c1:c2].astype(BF16)
    tail = w[:, c2:]
    zeros = jnp.zeros((w.shape[0], PAD_IDX - tail.shape[1]), F32)
    widx_ref[0] = jnp.concatenate([tail, zeros], axis=1).astype(BF16)


def _split_even_weights(ev_w_in):
    n_even, d, n = ev_w_in.shape
    tr = SPLIT_ROWS
    widths = (RWKV_COLS, 3 * DSA_WIDTH, PAD_IDX)
    return pl.pallas_call(
        _split_w_kernel,
        grid=(n_even, d // tr),
        in_specs=[pl.BlockSpec((1, tr, n), lambda e, i: (e, i, 0))],
        out_specs=[pl.BlockSpec((1, tr, wd), lambda e, i: (e, i, 0)) for wd in widths],
        out_shape=[jax.ShapeDtypeStruct((n_even, d, wd), BF16) for wd in widths],
        compiler_params=_params(("parallel", "parallel")),
        name="split_even_weights",
    )(ev_w_in)


def _even_in_proj(x, mod, weights, ik_g, ik_b, tables):
    bsz, seq, d = x.shape
    tm = IN_PROJ_ROWS
    w_r, w_qkv, w_idx = weights
    pad = LANES - IDX_DIM
    ikg = jnp.pad(ik_g, (0, pad)).reshape(1, LANES)
    ikb = jnp.pad(ik_b, (0, pad)).reshape(1, LANES)
    cos_t, sin_a, sin_b = tables
    full = lambda shape: pl.BlockSpec(shape, lambda b, i: (0,) * len(shape))
    tab = pl.BlockSpec((tm, LANES), lambda b, i: (i, 0))
    nq = IDX_HEADS * IDX_DIM
    rows = lambda n: pl.BlockSpec((1, tm, n), lambda b, i: (b, i, 0))
    cols = lambda n: pl.BlockSpec((1, n, tm), lambda b, i: (b, 0, i))
    return pl.pallas_call(
        _even_in_kernel,
        grid=(bsz, seq // tm),
        in_specs=[pl.BlockSpec((1, tm, d), lambda b, i: (b, i, 0)),
                  pl.BlockSpec((1, 6, d), lambda b, i: (b, 0, 0)),
                  full(w_r.shape), full(w_qkv.shape), full(w_idx.shape),
                  tab, tab, tab, full((1, LANES)), full((1, LANES))],
        out_specs=[rows(RWKV_COLS), cols(DSA_WIDTH), rows(DSA_WIDTH), cols(DSA_WIDTH),
                   cols(nq), rows(IDX_DIM), cols(SUBLANES)],
        out_shape=[jax.ShapeDtypeStruct((bsz, seq, RWKV_COLS), F32),
                   jax.ShapeDtypeStruct((bsz, DSA_WIDTH, seq), BF16),
                   jax.ShapeDtypeStruct((bsz, seq, DSA_WIDTH), BF16),
                   jax.ShapeDtypeStruct((bsz, DSA_WIDTH, seq), BF16),
                   jax.ShapeDtypeStruct((bsz, nq, seq), BF16),
                   jax.ShapeDtypeStruct((bsz, seq, IDX_DIM), BF16),
                   jax.ShapeDtypeStruct((bsz, SUBLANES, seq), F32)],
        compiler_params=_params(("parallel", "parallel")),
        name="even_in_proj",
    )(x, mod, w_r, w_qkv, w_idx, cos_t, sin_a, sin_b, ikg, ikb)


def _rwkv_kernel(p_ref, pp_ref, mu_ref, vec_ref, w2_ref, a2_ref, g2_ref, ltri_ref, ustr_ref,
                 seg_ref, o_ref, s_ref, obuf_ref):
    i = pl.program_id(1)

    @pl.when(i == 0)
    def _():
        s_ref[...] = jnp.zeros_like(s_ref)

    tt = RW_TILE
    w = RWKV_WIDTH
    hd = HEAD_DIM
    nb = p_ref.shape[0]
    nchunk = tt // RW_CHUNK
    seg = seg_ref[...]
    half_w = seg.shape[0]

    def segsum(t):
        return jnp.concatenate([_dot_split(t[:, c:c + half_w], seg, 2)
                                for c in range(0, w, half_w)], axis=1)

    vec = vec_ref[...]
    w0, a0, k_k, k_a, r_k, gn_g, gn_b = (vec[j:j + 1] for j in range(7))
    rowi = lax.broadcasted_iota(I32, (tt, 1), 0)
    live = (i > 0).astype(F32)

    def prepare(b):
        p = p_ref[b]
        prow = pp_ref[b][SUBLANES - 1:SUBLANES] * live
        xprev = jnp.where(rowi == 0, prow, pltpu.roll(p, 1, 0))
        ps = p + (xprev - p) * mu_ref[...]
        r = ps[:, :w]
        k = ps[:, w:2 * w]
        v = ps[:, 2 * w:3 * w]
        o1 = 3 * w
        wd = ps[:, o1:o1 + RWKV_LORA_W]
        ad = ps[:, o1 + RWKV_LORA_W:o1 + RWKV_LORA_W + RWKV_LORA_A]
        gd = ps[:, o1 + RWKV_LORA_W + RWKV_LORA_A:]
        y = -(w0 + _dot(jnp.tanh(wd), w2_ref[...]))
        softplus = jnp.maximum(y, 0.0) + jnp.log1p(jnp.exp(-jnp.abs(y)))
        logw = -jnp.exp(-softplus - 0.5)
        a = jax.nn.sigmoid(a0 + _dot(ad, a2_ref[...]))
        g = _dot(jax.nn.sigmoid(gd), g2_ref[...])
        kk = k * k_k
        kk = kk / jnp.maximum(jnp.sqrt(segsum(kk * kk)), 1e-12)
        k2 = k * (1.0 + (a - 1.0) * k_a)
        bonus = segsum(r * k2 * r_k) * v
        cum = _dot_split_left(ltri_ref[...], logw, 3)
        rem = _dot_split_left(ustr_ref[...], logw, 3)
        pt = jnp.exp(cum)
        ipt = jnp.exp(-cum)
        erem = jnp.exp(rem)
        kka = kk * a
        return dict(at=-kk * jnp.exp(cum - logw), rt=r * pt, bt=kka * ipt, kt=k2 * ipt,
                    bp=kka * erem, kp=k2 * erem, v=v, pt=pt, bonus=bonus, g=g)

    rows_in = [prepare(b) for b in range(nb)]

    ti = lax.broadcasted_iota(I32, (tt, tt), 0)
    si = lax.broadcasted_iota(I32, (tt, tt), 1)
    same = (ti // RW_CHUNK) == (si // RW_CHUNK)
    strict = same & (si < ti)
    incl = same & (si <= ti)
    tb = lax.broadcasted_iota(I32, (tt, nchunk * hd), 0)
    cb = lax.broadcasted_iota(I32, (tt, nchunk * hd), 1)
    blkmask = (tb // RW_CHUNK) == (cb // hd)
    tile_chunks = lambda t: jnp.where(blkmask, jnp.concatenate([t] * nchunk, axis=1), 0.0)

    units = [(b, slice(h * hd, (h + 1) * hd)) for b in range(nb) for h in range(RWKV_HEADS)]
    idx = range(len(units))
    pick = lambda name: [rows_in[b][name][:, sl] for b, sl in units]
    at_h, rt_h, vh, bt_h, kt_h = pick("at"), pick("rt"), pick("v"), pick("bt"), pick("kt")
    x = [_dot_nt(jnp.concatenate([at_h[u], rt_h[u]], axis=0),
                 jnp.concatenate([bt_h[u], kt_h[u]], axis=0)) for u in idx]
    a_ak = [jnp.where(strict, x[u][:tt, tt:], 0.0) for u in idx]
    a_rb = [jnp.where(incl, x[u][tt:, :tt], 0.0) for u in idx]
    a_rk = [jnp.where(incl, x[u][tt:, tt:], 0.0) for u in idx]
    def bd(m0, m1):
        z0 = jnp.zeros((m0.shape[0], m1.shape[1]), m0.dtype)
        z1 = jnp.zeros((m1.shape[0], m0.shape[1]), m1.dtype)
        return jnp.concatenate([jnp.concatenate([m0, z0], axis=1),
                                jnp.concatenate([z1, m1], axis=1)], axis=0)

    def pair_dot(lhs, rhs):
        out = []
        for u in range(0, len(lhs), 2):
            res = _dot(jnp.concatenate([lhs[u], lhs[u + 1]], axis=1), bd(rhs[u], rhs[u + 1]))
            cut = rhs[u].shape[1]
            out += [res[:, :cut], res[:, cut:]]
        return out

    aak_v = pair_dot(a_ak, vh)
    eye = (ti == si).astype(F32)
    inv = None
    blk = 1
    while blk < RW_CHUNK:
        below = same & ((ti // blk) % 2 == 1) & ((si // blk) % 2 == 0) & (
            (ti // (2 * blk)) == (si // (2 * blk)))
        a21 = [jnp.where(below, x[u][:tt, :tt], 0.0) for u in idx]
        if inv is None:
            inv = [eye + a21[u] for u in idx]
        else:
            left = pair_dot(a21, inv)
            grow = pair_dot(inv, left)
            inv = [inv[u] + grow[u] for u in idx]
        blk *= 2
    yv = pair_dot(inv, [jnp.concatenate([at_h[u], aak_v[u]], axis=1) for u in idx])
    arb_y = pair_dot(a_rb, yv)
    ark_v = pair_dot(a_rk, vh)
    qt = [rt_h[u] + arb_y[u][:, :hd] for u in idx]
    o0 = [arb_y[u][:, hd:] + ark_v[u] for u in idx]
    yt = [yv[u].T for u in idx]
    bpb = [tile_chunks(t) for t in pick("bp")]
    kpb = [tile_chunks(t) for t in pick("kp")]
    g_all = [_dot(yt[u][:hd], bpb[u]) for u in idx]
    h_all = [_dot(jnp.concatenate([yt[u][hd:], vh[u].T], axis=1),
                  jnp.concatenate([bpb[u], kpb[u]], axis=0)) for u in idx]
    pt_h = pick("pt")
    pairs = range(0, len(units), 2)
    low_lanes = lax.broadcasted_iota(I32, (hd, 2 * hd), 1) < hd
    s = [jnp.concatenate([s_ref[u], s_ref[u + 1]], axis=1) for u in pairs]
    for n in range(nchunk):
        rows = slice(n * RW_CHUNK, (n + 1) * RW_CHUNK)
        cols = slice(n * hd, (n + 1) * hd)
        last = (n + 1) * RW_CHUNK - 1
        for j, u in enumerate(pairs):
            s_bd = jnp.concatenate([jnp.where(low_lanes, s[j], 0.0),
                                    jnp.where(low_lanes, 0.0, s[j])], axis=0)
            o_pair = _dot_nt(jnp.concatenate([qt[u][rows], qt[u + 1][rows]], axis=1), s_bd)
            for i in range(2):
                b, sl = units[u + i]
                obuf_ref[b, rows, sl] = o_pair[:, i * hd:(i + 1) * hd] + o0[u + i][rows]
        s = [s[j] * jnp.concatenate([pt_h[u][last:last + 1], pt_h[u + 1][last:last + 1]], axis=1)
             + _dot(s[j], bd(g_all[u][:, cols], g_all[u + 1][:, cols]))
             + jnp.concatenate([h_all[u][:, cols], h_all[u + 1][:, cols]], axis=1)
             for j, u in enumerate(pairs)]
    for j, u in enumerate(pairs):
        s_ref[u] = s[j][:, :hd]
        s_ref[u + 1] = s[j][:, hd:]

    for b in range(nb):
        o = obuf_ref[b]
        mean = segsum(o) * (1.0 / hd)
        oc = o - mean
        var = segsum(oc * oc) * (1.0 / hd)
        on = oc * lax.rsqrt(var + GN_EPS) * gn_g + gn_b
        o_ref[b] = ((on + rows_in[b]["bonus"]) * rows_in[b]["g"]).astype(BF16)


def _rwkv(p_r, mu, w0, w2, a0, a2, g2, k_k, k_a, r_k, gn_g, gn_b):
    bsz, seq, _ = p_r.shape
    tt = RW_TILE
    nb = RW_ROWS if bsz % RW_ROWS == 0 else 1
    w = RWKV_WIDTH
    vec = jnp.stack([w0, a0, k_k, k_a, r_k.reshape(w), gn_g, gn_b, jnp.zeros_like(w0)])
    t_idx = np.arange(tt)
    same = (t_idx[:, None] // RW_CHUNK) == (t_idx[None, :] // RW_CHUNK)
    ltri = jnp.asarray(same & (t_idx[None, :] <= t_idx[:, None]), BF16)
    ustr = jnp.asarray(same & (t_idx[None, :] > t_idx[:, None]), BF16)
    c_idx = np.arange(MXU_TILE)
    seg = jnp.asarray((c_idx[:, None] // HEAD_DIM) == (c_idx[None, :] // HEAD_DIM), BF16)
    full = lambda shape: pl.BlockSpec(shape, lambda b, i: (0,) * len(shape))
    per_tile = tt // SUBLANES
    return pl.pallas_call(
        _rwkv_kernel,
        grid=(bsz // nb, seq // tt),
        in_specs=[pl.BlockSpec((nb, tt, RWKV_COLS), lambda b, i: (b, i, 0)),
                  pl.BlockSpec((nb, SUBLANES, RWKV_COLS),
                               lambda b, i: (b, jnp.maximum(i * per_tile - 1, 0), 0)),
                  full((1, RWKV_COLS)), full((SUBLANES, w)),
                  full(w2.shape), full(a2.shape), full(g2.shape),
                  full((tt, tt)), full((tt, tt)), full((MXU_TILE, MXU_TILE))],
        out_specs=pl.BlockSpec((nb, tt, w), lambda b, i: (b, i, 0)),
        out_shape=jax.ShapeDtypeStruct((bsz, seq, w), BF16),
        scratch_shapes=[pltpu.VMEM((nb * RWKV_HEADS, HEAD_DIM, HEAD_DIM), F32),
                        pltpu.VMEM((nb, tt, w), F32)],
        compiler_params=_params(("parallel", "arbitrary")),
        name="rwkv7_mix",
    )(p_r, p_r, mu.reshape(1, RWKV_COLS), vec, w2.astype(BF16), a2.astype(BF16),
      g2.astype(BF16), ltri, ustr, seg)


def _dsa_kernel(qt_ref, k_ref, vt_ref, qit_ref, ki_ref, wit_ref, low_ref, o_ref,
                key_ref, top_ref, bias_ref, acc_ref, m_ref, l_ref, *, topk):
    qb = pl.program_id(1)
    nq = Q_BLOCK
    kt_sz = bias_ref.shape[0]
    half = kt_sz // 2
    quarter = kt_sz // 4
    hd = HEAD_DIM
    start = qb * nq
    nun = (start + nq + half - 1) // half
    col = lax.broadcasted_iota(I32, (1, nq), 1)
    lim = start + (col // CHUNK + 1) * CHUNK
    wit = wit_ref[0]
    qit = qit_ref[0]
    qi_cat = jnp.concatenate([qit[h * IDX_DIM:(h + 1) * IDX_DIM] for h in range(IDX_HEADS)],
                             axis=1)

    def score_rows(off, rows):
        d = jnp.dot(ki_ref[0, pl.ds(off, rows), :], qi_cat, preferred_element_type=F32)
        s = jnp.zeros((rows, nq), F32)
        for h in range(IDX_HEADS):
            s = s + wit[h:h + 1] * jnp.maximum(d[:, h * nq:(h + 1) * nq], 0.0)
        s = s + 0.0
        bits = pltpu.bitcast(s, I32)
        key = bits ^ ((bits >> 31) & 0x7FFFFFFF)
        sidx = off + lax.broadcasted_iota(I32, (rows, nq), 0)
        key = jnp.where(sidx < lim, key, INT_MIN)
        key_ref[pl.ds(off, rows), :] = key
        field = (key >> (32 - TOP_BITS)) + (3 << (TOP_BITS - 1))
        for c0 in range(0, rows, half):
            top_ref[pl.ds(pl.multiple_of((off + c0) // 2, quarter), quarter), :] = (
                (field[c0:c0 + quarter] << 16) | field[c0 + quarter:c0 + half])

    def over_keys(step, init):
        c = lax.fori_loop(
            0, nun // 2, lambda j, c: step(pl.multiple_of(j * kt_sz, kt_sz), kt_sz, c), init)
        return lax.cond(nun % 2 == 1,
                        lambda c: step(pl.multiple_of((nun - 1) * half, half), half, c),
                        lambda c: c, c)

    def score_step(off, rows, carry):
        score_rows(off, rows)
        return carry

    over_keys(score_step, jnp.int32(0))

    def count(pred):
        def step(off, rows, c):
            hit = jnp.where(pred(key_ref[pl.ds(off, rows), :]), 1, 0)
            return c + hit.reshape(rows // RED_ROWS, RED_ROWS, nq).sum(axis=0)
        return jnp.sum(over_keys(step, jnp.zeros((RED_ROWS, nq), I32)), axis=0, keepdims=True)

    def count_top(cand):
        both = (cand << 16) | cand

        def step(off, rows, c):
            words = top_ref[pl.ds(pl.multiple_of(off // 2, quarter), rows // 2), :]
            hit = lax.shift_right_logical(words - both, 15) & 0x00010001
            return c + hit.reshape(rows // 2 // RED_ROWS, RED_ROWS, nq).sum(axis=0)

        c = jnp.sum(over_keys(step, jnp.zeros((RED_ROWS, nq), I32)), axis=0, keepdims=True)
        return (c & 0xFFFF) + lax.shift_right_logical(c, 16)

    def top_step(b, prefix):
        cand = prefix | lax.shift_left(jnp.int32(1), TOP_BITS - 1 - b)
        return jnp.where(count_top(cand) >= topk, cand, prefix)

    prefix = lax.fori_loop(0, TOP_BITS, top_step, jnp.zeros((1, nq), I32))
    lo = (prefix - (1 << (TOP_BITS - 1))) << (32 - TOP_BITS)

    def bit_step(b, lo):
        cand = lo + lax.shift_left(jnp.int32(1), 31 - TOP_BITS - b)
        return jnp.where(count(lambda kv: kv >= cand) >= topk, cand, lo)

    th = lax.fori_loop(0, 32 - TOP_BITS, bit_step, lo)
    need = jnp.where(th == INT_MIN, 0, topk - count(lambda kv: kv > th)).astype(F32)

    acc_ref[...] = jnp.zeros_like(acc_ref)
    m_ref[...] = jnp.full_like(m_ref, NEG_INF)
    l_ref[...] = jnp.zeros_like(l_ref)
    low = low_ref[...]
    npair = DSA_HEADS // 2
    top_rows = lax.broadcasted_iota(I32, (2 * hd, nq), 0) < hd
    qt = qt_ref[0].astype(F32)
    q_bd = []
    for p in range(npair):
        qp = qt[p * 2 * hd:(p + 1) * 2 * hd]
        q_bd.append(jnp.concatenate([jnp.where(top_rows, qp, 0.0), jnp.where(top_rows, 0.0, qp)],
                                    axis=1).astype(BF16))

    pad_rows = 2 * SUBLANES

    def attn_tile(off, rows, eq_before):
        ones_rows = (lax.broadcasted_iota(I32, (pad_rows, rows), 0) == 0).astype(BF16)
        eq_run = eq_before
        for t in range(rows // TRI_TILE):
            kv = key_ref[pl.ds(off + t * TRI_TILE, TRI_TILE), :]
            eq = kv == th
            pre = eq_run + jnp.dot(low, jnp.where(eq, 1.0, 0.0).astype(BF16),
                                   preferred_element_type=F32)
            sel = (kv > th) | (eq & (pre <= need))
            bias_ref[t * TRI_TILE:(t + 1) * TRI_TILE, :] = jnp.where(sel, 0.0, NEG_INF)
            eq_run = pre[TRI_TILE - 1:TRI_TILE]
        bias = bias_ref[:rows]

        def logits(p):
            part = rows // 2
            return jnp.concatenate(
                [jnp.dot(k_ref[0, pl.ds(off + r * part, part), p * 2 * hd:(p + 1) * 2 * hd],
                         q_bd[p], preferred_element_type=F32) for r in range(2)], axis=0)

        st_next = logits(0)
        for p in range(npair):
            cols = slice(p * 2 * hd, (p + 1) * 2 * hd)
            st = st_next
            if p + 1 < npair:
                st_next = logits(p + 1)
            prs, scales = [], []
            for i in range(2):
                h = 2 * p + i
                s = st[:, i * nq:(i + 1) * nq] + bias
                m_old = m_ref[h:h + 1]
                m_new = jnp.maximum(m_old, _fold_rows(s, jnp.max))
                prs.append(jnp.exp2(s - m_new).astype(BF16))
                scales.append(jnp.exp2(m_old - m_new))
                m_ref[h:h + 1] = m_new
            lhs = jnp.concatenate([vt_ref[0, cols, pl.ds(off, rows)], ones_rows], axis=0)
            pv = jnp.dot(lhs, jnp.concatenate(prs, axis=1),
                         preferred_element_type=F32)
            for i in range(2):
                h = 2 * p + i
                l_ref[h:h + 1] = (scales[i] * l_ref[h:h + 1]
                                  + pv[2 * hd:2 * hd + 1, i * nq:(i + 1) * nq])
            new = jnp.where(top_rows, pv[:2 * hd, :nq], pv[:2 * hd, nq:])
            acc_ref[p] = jnp.where(top_rows, scales[0], scales[1]) * acc_ref[p] + new
        return eq_run

    over_keys(attn_tile, jnp.zeros((1, nq), F32))
    for p in range(npair):
        inv = jnp.where(top_rows, 1.0 / l_ref[2 * p:2 * p + 1], 1.0 / l_ref[2 * p + 1:2 * p + 2])
        o_ref[0, :, p * 2 * hd:(p + 1) * 2 * hd] = (acc_ref[p] * inv).T.astype(BF16)


def _dsa(q_t, k, v_t, qi_t, ki, wi_t):
    bsz, seq, w = k.shape
    nq = Q_BLOCK
    topk = min(INDEX_TOPK, seq // 4)
    kt_sz = min(KEY_TILE, seq)
    assert kt_sz % (4 * RED_ROWS) == 0 and kt_sz % (2 * TRI_TILE) == 0 and seq // 2 < 2 ** 16
    t_idx = np.arange(TRI_TILE)
    low = jnp.asarray(t_idx[None, :] <= t_idx[:, None], BF16)
    cols = lambda n: pl.BlockSpec((1, n, nq), lambda b, i: (b, 0, i))
    return pl.pallas_call(
        functools.partial(_dsa_kernel, topk=topk),
        grid=(bsz, seq // nq),
        in_specs=[cols(w),
                  pl.BlockSpec((1, seq, w), lambda b, i: (b, 0, 0)),
                  pl.BlockSpec((1, w, seq), lambda b, i: (b, 0, 0)),
                  cols(IDX_HEADS * IDX_DIM),
                  pl.BlockSpec((1, seq, IDX_DIM), lambda b, i: (b, 0, 0)),
                  cols(SUBLANES),
                  pl.BlockSpec((TRI_TILE, TRI_TILE), lambda b, i: (0, 0))],
        out_specs=pl.BlockSpec((1, nq, w), lambda b, i: (b, i, 0)),
        out_shape=jax.ShapeDtypeStruct((bsz, seq, w), BF16),
        scratch_shapes=[pltpu.VMEM((seq, nq), I32),
                        pltpu.VMEM((seq // 2, nq), I32),
                        pltpu.VMEM((kt_sz, nq), F32),
                        pltpu.VMEM((DSA_HEADS // 2, 2 * HEAD_DIM, nq), F32),
                        pltpu.VMEM((DSA_HEADS, nq), F32),
                        pltpu.VMEM((DSA_HEADS, nq), F32)],
        compiler_params=_params(("parallel", "arbitrary")),
        name="dsa_mix",
    )(q_t, k, v_t, qi_t, ki, wi_t, low)


POOL_HALO = 16


def _odd_layer_kernel(x_ref, xp_ref, mod_ref, win_ref, pw_ref, ps_ref, lng_ref, lnb_ref, ws_ref,
                      bs_ref, wout_ref, g_ref, b_ref, o_ref, y_ref):
    i = pl.program_id(1)
    m = mod_ref[0]
    sc = 1.0 + m[1:2]
    sh = m[0:1]
    x = x_ref[0]
    tm = x.shape[0]
    gd = POOL_GROUP_DIM
    p = jnp.dot((x * sc + sh).astype(BF16), win_ref[...], preferred_element_type=F32)
    prev = jnp.dot((xp_ref[0] * sc + sh).astype(BF16), win_ref[:, :POOL_WIDTH],
                   preferred_element_type=F32) * (i > 0).astype(F32)
    t_glob = (i * tm + lax.broadcasted_iota(I32, (tm, 1), 0)).astype(F32)
    scale = ps_ref[...]
    for gi, win in enumerate(POOL_WINDOWS):
        cols = slice(gi * gd, (gi + 1) * gd)
        xg = p[:, cols]
        s = jnp.concatenate([prev[:, cols], xg], axis=0)
        span = 1
        while span < win:
            s = s[span:] + s[:-span]
            span *= 2
        first = POOL_HALO + 1 - win
        pooled = s[first:first + tm] / jnp.minimum(t_glob + 1.0, float(win)) - xg
        y_ref[:, cols] = (_dot(pooled, pw_ref[gi]) * scale[:, cols]).astype(BF16)

    u = _gelu(p[:, POOL_WIDTH:POOL_WIDTH + SG_WIDTH])
    v = _ln(_gelu(p[:, POOL_WIDTH + SG_WIDTH:]), lng_ref[...], lnb_ref[...], LN_EPS)
    ti = lax.broadcasted_iota(I32, (SG_CHUNK, SG_CHUNK), 0)
    si = lax.broadcasted_iota(I32, (SG_CHUNK, SG_CHUNK), 1)
    bs = bs_ref[...]
    for gi in range(SG_GROUPS):
        cols = slice(gi * SG_GROUP_DIM, (gi + 1) * SG_GROUP_DIM)
        ws = jnp.where(si <= ti, ws_ref[gi], 0.0)
        for n in range(tm // SG_CHUNK):
            rows = slice(n * SG_CHUNK, (n + 1) * SG_CHUNK)
            z = _dot(ws, v[rows, cols]) + bs[:, gi:gi + 1]
            y_ref[rows, POOL_WIDTH + gi * SG_GROUP_DIM:POOL_WIDTH + (gi + 1) * SG_GROUP_DIM] = (
                u[rows, cols] * z).astype(BF16)

    y = jnp.dot(y_ref[...], wout_ref[...], preferred_element_type=F32)
    o_ref[0] = _ln(ALPHA * x + m[2:3] * y, g_ref[...], b_ref[...], LN_EPS)


def _odd_layer(x, mod, w_in, pool_w, pool_scale, sg_ln_g, sg_ln_b, sg_w, sg_b, w_out, ln_g, ln_b):
    bsz, seq, d = x.shape
    tm = ODD_ROWS
    per_tile = tm // POOL_HALO
    full = lambda shape: pl.BlockSpec(shape, lambda b, i: (0,) * len(shape))
    once = lambda shape: pl.BlockSpec(shape, lambda b, i: (0,) * len(shape),
                                      pipeline_mode=pl.Buffered(1))
    return pl.pallas_call(
        _odd_layer_kernel,
        grid=(bsz, seq // tm),
        in_specs=[pl.BlockSpec((1, tm, d), lambda b, i: (b, i, 0)),
                  pl.BlockSpec((1, POOL_HALO, d),
                               lambda b, i: (b, jnp.maximum(i * per_tile - 1, 0), 0)),
                  pl.BlockSpec((1, 6, d), lambda b, i: (b, 0, 0)),
                  once(w_in.shape), full(pool_w.shape), full((1, POOL_WIDTH)),
                  full((1, SG_WIDTH)), full((1, SG_WIDTH)), full(sg_w.shape),
                  full((SG_CHUNK, SG_GROUPS)), once(w_out.shape), full((1, d)), full((1, d))],
        out_specs=pl.BlockSpec((1, tm, d), lambda b, i: (b, i, 0)),
        out_shape=jax.ShapeDtypeStruct((bsz, seq, d), F32),
        scratch_shapes=[pltpu.VMEM((tm, d), BF16)],
        compiler_params=_params(("parallel", "parallel")),
        name="odd_layer_mix",
    )(x, x, mod, w_in.astype(BF16), pool_w.astype(BF16), pool_scale.reshape(1, -1),
      sg_ln_g.reshape(1, -1), sg_ln_b.reshape(1, -1), sg_w, sg_b.T, w_out.astype(BF16),
      ln_g.reshape(1, d), ln_b.reshape(1, d))


def _proj_ln_kernel(*refs, n_in, gate_row):
    a_refs = refs[:n_in]
    w_refs = refs[n_in:2 * n_in]
    x_ref, mod_ref, g_ref, b_ref, o_ref = refs[2 * n_in:]
    y = None
    for a_ref, w_ref in zip(a_refs, w_refs):
        part = jnp.dot(a_ref[0], w_ref[...], preferred_element_type=F32)
        y = part if y is None else y + part
    gate = mod_ref[0][gate_row:gate_row + 1]
    o_ref[0] = _ln(ALPHA * x_ref[0] + gate * y, g_ref[...], b_ref[...], LN_EPS)


def _proj_ln(acts, weights, x, mod, gate_row, ln_g, ln_b, name):
    bsz, seq, d = x.shape
    tm = OUT_PROJ_ROWS
    n_in = len(acts)
    in_specs = [pl.BlockSpec((1, tm, a.shape[2]), lambda b, i: (b, i, 0)) for a in acts]
    in_specs += [pl.BlockSpec(w.shape, lambda b, i: (0, 0)) for w in weights]
    in_specs += [pl.BlockSpec((1, tm, d), lambda b, i: (b, i, 0)),
                 pl.BlockSpec((1, 6, d), lambda b, i: (b, 0, 0)),
                 pl.BlockSpec((1, d), lambda b, i: (0, 0)),
                 pl.BlockSpec((1, d), lambda b, i: (0, 0))]
    return pl.pallas_call(
        functools.partial(_proj_ln_kernel, n_in=n_in, gate_row=gate_row),
        grid=(bsz, seq // tm),
        in_specs=in_specs,
        out_specs=pl.BlockSpec((1, tm, d), lambda b, i: (b, i, 0)),
        out_shape=jax.ShapeDtypeStruct((bsz, seq, d), F32),
        compiler_params=_params(("parallel", "parallel")),
        name=name,
    )(*acts, *[w.astype(BF16) for w in weights], x, mod, ln_g.reshape(1, d), ln_b.reshape(1, d))


def _ffn_kernel(x_ref, xp_ref, mod_ref, wg_ref, wv_ref, cwg_ref, cwv_ref, cbg_ref, cbv_ref,
                wd_ref, g_ref, b_ref, o_ref):
    i = pl.program_id(1)
    m = mod_ref[0]
    sc = 1.0 + m[4:5]
    sh = m[3:4]
    x = x_ref[0]
    live = (i > 0).astype(F32)
    h = jnp.concatenate([(xp_ref[0] * sc + sh) * live, x * sc + sh], axis=0).astype(BF16)
    halo = SUBLANES

    def conv(w_ref, cw_ref, cb_ref):
        u = jnp.dot(h, w_ref[...], preferred_element_type=F32)
        cw = cw_ref[...]
        u1 = pltpu.roll(u, 1, 0)
        u2 = pltpu.roll(u, 2, 0)
        return (cb_ref[...] + u2[halo:] * cw[0:1] + u1[halo:] * cw[1:2] + u[halo:] * cw[2:3])

    gate = conv(wg_ref, cwg_ref, cbg_ref)
    val = conv(wv_ref, cwv_ref, cbv_ref)
    act = (gate * jax.nn.sigmoid(gate) * val).astype(BF16)
    y = jnp.dot(act, wd_ref[...], preferred_element_type=F32)
    o_ref[0] = _ln(ALPHA * x + m[5:6] * y, g_ref[...], b_ref[...], LN_EPS)


def _conv_ffn(x, mod, w_up, conv_w, conv_b, w_down, ln_g, ln_b):
    bsz, seq, d = x.shape
    tm = FFN_ROWS
    per_tile = tm // SUBLANES
    wb = w_up.astype(BF16)
    cb = conv_b.reshape(1, -1)
    once = dict(pipeline_mode=pl.Buffered(1))
    return pl.pallas_call(
        _ffn_kernel,
        grid=(bsz, seq // tm),
        in_specs=[pl.BlockSpec((1, tm, d), lambda b, i: (b, i, 0)),
                  pl.BlockSpec((1, SUBLANES, d),
                               lambda b, i: (b, jnp.maximum(i * per_tile - 1, 0), 0)),
                  pl.BlockSpec((1, 6, d), lambda b, i: (b, 0, 0)),
                  pl.BlockSpec((d, D_FF), lambda b, i: (0, 0), **once),
                  pl.BlockSpec((d, D_FF), lambda b, i: (0, 1), **once),
                  pl.BlockSpec((3, D_FF), lambda b, i: (0, 0), **once),
                  pl.BlockSpec((3, D_FF), lambda b, i: (0, 1), **once),
                  pl.BlockSpec((1, D_FF), lambda b, i: (0, 0), **once),
                  pl.BlockSpec((1, D_FF), lambda b, i: (0, 1), **once),
                  pl.BlockSpec((D_FF, d), lambda b, i: (0, 0), **once),
                  pl.BlockSpec((1, d), lambda b, i: (0, 0)),
                  pl.BlockSpec((1, d), lambda b, i: (0, 0))],
        out_specs=pl.BlockSpec((1, tm, d), lambda b, i: (b, i, 0)),
        out_shape=jax.ShapeDtypeStruct((bsz, seq, d), F32),
        compiler_params=_params(("parallel", "parallel")),
        name="conv_ffn_ln",
    )(x, x, mod, wb, wb, conv_w, conv_w, cb, cb, w_down.astype(BF16),
      ln_g.reshape(1, d), ln_b.reshape(1, d))


def _rope_tables(seq):
    inv = ROPE_THETA ** (-jnp.arange(0, ROPE_DIM, 2, dtype=F32) / ROPE_DIM)
    ang = jnp.arange(seq, dtype=F32)[:, None] * inv[None, :]
    cos, sin = jnp.cos(ang), jnp.sin(ang)
    half = ROPE_DIM // 2
    rest = HEAD_DIM - ROPE_DIM
    one = jnp.ones((seq, rest), F32)
    zero = jnp.zeros((seq, rest), F32)
    zh = jnp.zeros((seq, half), F32)
    head = lambda *parts: jnp.concatenate(parts * (LANES // HEAD_DIM), axis=1)
    return head(cos, cos, one), head(-sin, zh, zero), head(zh, sin, zero)


def kernel(x, c, ada_w, ada_b, ln_g, ln_b, ffn_w_up, ffn_conv_w, ffn_conv_b, ffn_w_down, ev_w_in, ev_w_out, rw_mu, rw_w0, rw_w2, rw_a0, rw_a2, rw_g2, rw_k_k, rw_k_a, rw_r_k, rw_gn_g, rw_gn_b, ik_ln_g, ik_ln_b, od_w_in, od_w_out, pool_w, pool_scale, sg_ln_g, sg_ln_b, sg_w, sg_b):
    seq = x.shape[1]
    tables = _rope_tables(seq)
    mods = _modulation(c, ada_w, ada_b)
    ev_w = _split_even_weights(ev_w_in)
    for layer in range(DEPTH):
        mod = mods[layer]
        if layer % 2 == 0:
            e = layer // 2
            p_r, q_t, k, v_t, qi_t, ki, wi_t = _even_in_proj(x, mod, [w[e] for w in ev_w],
                                                             ik_ln_g[e], ik_ln_b[e], tables)
            ya = _rwkv(p_r, rw_mu[e], rw_w0[e], rw_w2[e], rw_a0[e], rw_a2[e], rw_g2[e],
                       rw_k_k[e], rw_k_a[e], rw_r_k[e], rw_gn_g[e], rw_gn_b[e])
            yb = _dsa(q_t, k, v_t, qi_t, ki, wi_t)
            w_out = ev_w_out[e]
            x = _proj_ln([ya, yb], [w_out[:RWKV_WIDTH], w_out[RWKV_WIDTH:]], x, mod, 2,
                         ln_g[layer, 0], ln_b[layer, 0], "even_out_proj_ln")
        else:
            o = layer // 2
            x = _odd_layer(x, mod, od_w_in[o], pool_w[o], pool_scale[o], sg_ln_g[o], sg_ln_b[o],
                           sg_w[o], sg_b[o], od_w_out[o], ln_g[layer, 0], ln_b[layer, 0])
        x = _conv_ffn(x, mod, ffn_w_up[layer], ffn_conv_w[layer], ffn_conv_b[layer],
                      ffn_w_down[layer], ln_g[layer, 1], ln_b[layer, 1])
    return x
```

```python
import functools

import numpy as np
import jax
import jax.numpy as jnp
from jax import lax
from jax.experimental import pallas as pl
from jax.experimental.pallas import tpu as pltpu

F32 = jnp.float32
BF16 = jnp.bfloat16
I32 = jnp.int32

D_MODEL = 1024
DEPTH = 4
CHUNK = 64
HEAD_DIM = 64
RWKV_WIDTH = D_MODEL // 2
RWKV_HEADS = RWKV_WIDTH // HEAD_DIM
RWKV_LORA_W = 64
RWKV_LORA_A = 64
RWKV_LORA_G = 128
RWKV_COLS = 3 * RWKV_WIDTH + RWKV_LORA_W + RWKV_LORA_A + RWKV_LORA_G
DSA_WIDTH = D_MODEL - RWKV_WIDTH
DSA_HEADS = DSA_WIDTH // HEAD_DIM
IDX_HEADS = 4
IDX_DIM = 64
INDEX_TOPK = 256
Q_BLOCK = 128
ROPE_THETA = 500000.0
ROPE_DIM = HEAD_DIM // 4
POOL_WINDOWS = (2, 4, 8, 16)
POOL_WIDTH = D_MODEL // 2
POOL_GROUP_DIM = POOL_WIDTH // len(POOL_WINDOWS)
SG_WIDTH = D_MODEL - POOL_WIDTH
SG_GROUPS = 4
SG_GROUP_DIM = SG_WIDTH // SG_GROUPS
SG_CHUNK = 128
D_FF = 2816
ALPHA = (2.0 * DEPTH) ** 0.25
LN_EPS = 1e-5
GN_EPS = 64e-5
NEG_INF = -1e30
INT_MIN = -(2 ** 31)
LOG2E = 1.4426950408889634

LANES = 128
SUBLANES = 8
MXU_TILE = 256
VMEM_LIMIT = 56 * 1024 * 1024

MOD_COLS = 1536
SPLIT_ROWS = 256
IN_PROJ_ROWS = 256
ODD_ROWS = 256
OUT_PROJ_ROWS = 512
FFN_ROWS = 512
RW_TILE = 128
RW_CHUNK = 16
RW_ROWS = 4
KEY_TILE = 1024
TRI_TILE = 256
TOP_BITS = 15
PAD_IDX = 384

def _dot(a, b):
    return jnp.dot(a.astype(BF16), b.astype(BF16), preferred_element_type=F32)


def _dot_nt(a, b):
    return lax.dot_general(a.astype(BF16), b.astype(BF16), (((1,), (1,)), ((), ())),
                           preferred_element_type=F32)


def _dot_split(x, m01, terms):
    acc = None
    rem = x
    for _ in range(terms):
        piece = rem.astype(BF16)
        rem = rem - piece.astype(F32)
        part = jnp.dot(piece, m01, preferred_element_type=F32)
        acc = part if acc is None else acc + part
    return acc


def _dot_split_left(m01, x, terms):
    acc = None
    rem = x
    for _ in range(terms):
        piece = rem.astype(BF16)
        rem = rem - piece.astype(F32)
        part = jnp.dot(m01, piece, preferred_element_type=F32)
        acc = part if acc is None else acc + part
    return acc


def _ln(x, g, b, eps):
    mu = jnp.mean(x, axis=-1, keepdims=True)
    xc = x - mu
    var = jnp.mean(xc * xc, axis=-1, keepdims=True)
    return xc * lax.rsqrt(var + eps) * g + b


def _gelu(x):
    return 0.5 * x * (1.0 + lax.erf(x * 0.7071067811865476))


RED_ROWS = 64


def _fold_rows(x, op):
    rows, n = x.shape
    part = op(x.reshape(rows // RED_ROWS, RED_ROWS, n), axis=0)
    return op(part, axis=0, keepdims=True)


def _params(sem):
    return pltpu.CompilerParams(dimension_semantics=sem, vmem_limit_bytes=VMEM_LIMIT)


def _mod_kernel(c_ref, w_ref, b_ref, o_ref):
    c = c_ref[...]
    ca = c * jax.nn.sigmoid(c)
    o_ref[0] = jnp.dot(ca, w_ref[0], preferred_element_type=F32,
                       precision=lax.Precision.HIGHEST) + b_ref[0]


def _modulation(c, ada_w, ada_b):
    bsz, d = c.shape
    depth = ada_w.shape[0]
    n = ada_w.shape[2]
    tn = MOD_COLS
    rows = -(-bsz // SUBLANES) * SUBLANES
    c8 = jnp.pad(c, ((0, rows - bsz), (0, 0)))
    out = pl.pallas_call(
        _mod_kernel,
        grid=(depth, n // tn),
        in_specs=[pl.BlockSpec((rows, d), lambda l, j: (0, 0)),
                  pl.BlockSpec((1, d, tn), lambda l, j: (l, 0, j)),
                  pl.BlockSpec((1, 1, tn), lambda l, j: (l, 0, j))],
        out_specs=pl.BlockSpec((1, rows, tn), lambda l, j: (l, 0, j)),
        out_shape=jax.ShapeDtypeStruct((depth, rows, n), F32),
        compiler_params=_params(("arbitrary", "arbitrary")),
        name="adaln_mod",
    )(c8, ada_w, ada_b.reshape(depth, 1, n))
    return out[:, :bsz].reshape(depth, bsz, 6, d)


def _rope(x, cos_t, sin_a, sin_b):
    n = x.shape[1] // LANES
    rep = (lambda t: jnp.concatenate([t] * n, axis=1)) if n > 1 else (lambda t: t)
    width = x.shape[1]
    half = ROPE_DIM // 2
    return (x * rep(cos_t) + pltpu.roll(x, width - half, 1) * rep(sin_a)
            + pltpu.roll(x, half, 1) * rep(sin_b))


def _even_in_kernel(x_ref, mod_ref, wr_ref, wqkv_ref, widx_ref, cos_ref, sa_ref, sb_ref,
                    ikg_ref, ikb_ref,
                    pr_ref, qt_ref, k_ref, vt_ref, qit_ref, ki_ref, wit_ref):
    m = mod_ref[0]
    h = (x_ref[0] * (1.0 + m[1:2]) + m[0:1]).astype(BF16)
    pr_ref[0] = jnp.dot(h, wr_ref[...], preferred_element_type=F32)
    qkv = jnp.dot(h, wqkv_ref[...], preferred_element_type=F32)
    cos_t = cos_ref[...]
    sin_a = sa_ref[...]
    sin_b = sb_ref[...]
    w = DSA_WIDTH
    q = _rope(qkv[:, :w], cos_t, sin_a, sin_b) * (HEAD_DIM ** -0.5 * LOG2E)
    qt_ref[0] = q.T.astype(BF16)
    k_ref[0] = _rope(qkv[:, w:2 * w], cos_t, sin_a, sin_b).astype(BF16)
    vt_ref[0] = qkv[:, 2 * w:].T.astype(BF16)
    idx = jnp.dot(h, widx_ref[...], preferred_element_type=F32)
    nq = IDX_HEADS * IDX_DIM
    qit_ref[0] = _rope(idx[:, :nq], cos_t, sin_a, sin_b).T.astype(BF16)
    blk = idx[:, nq:nq + LANES]
    lane = lax.broadcasted_iota(I32, blk.shape, 1)
    is_k = lane < IDX_DIM
    mu = jnp.sum(jnp.where(is_k, blk, 0.0), axis=1, keepdims=True) * (1.0 / IDX_DIM)
    xc = jnp.where(is_k, blk - mu, 0.0)
    var = jnp.sum(xc * xc, axis=1, keepdims=True) * (1.0 / IDX_DIM)
    kin = xc * lax.rsqrt(var + LN_EPS) * ikg_ref[...] + ikb_ref[...]
    ki_ref[0] = _rope(kin, cos_t, sin_a, sin_b)[:, :IDX_DIM].astype(BF16)
    wit = (blk * (IDX_HEADS ** -0.5 * IDX_DIM ** -0.5)).T
    wit_ref[0] = wit[IDX_DIM:IDX_DIM + SUBLANES]


def _split_w_kernel(w_ref, wr_ref, wqkv_ref, widx_ref):
    w = w_ref[0]
    c1 = RWKV_COLS
    c2 = RWKV_COLS + 3 * DSA_WIDTH
    wr_ref[0] = w[:, :c1].astype(BF16)
    wqkv_ref[0] = w[:, c1:c2].astype(BF16)
    tail = w[:, c2:]
    zeros = jnp.zeros((w.shape[0], PAD_IDX - tail.shape[1]), F32)
    widx_ref[0] = jnp.concatenate([tail, zeros], axis=1).astype(BF16)


def _split_even_weights(ev_w_in):
    n_even, d, n = ev_w_in.shape
    tr = SPLIT_ROWS
    widths = (RWKV_COLS, 3 * DSA_WIDTH, PAD_IDX)
    return pl.pallas_call(
        _split_w_kernel,
        grid=(n_even, d // tr),
        in_specs=[pl.BlockSpec((1, tr, n), lambda e, i: (e, i, 0))],
        out_specs=[pl.BlockSpec((1, tr, wd), lambda e, i: (e, i, 0)) for wd in widths],
        out_shape=[jax.ShapeDtypeStruct((n_even, d, wd), BF16) for wd in widths],
        compiler_params=_params(("parallel", "parallel")),
        name="split_even_weights",
    )(ev_w_in)


def _even_in_proj(x, mod, weights, ik_g, ik_b, tables):
    bsz, seq, d = x.shape
    tm = IN_PROJ_ROWS
    w_r, w_qkv, w_idx = weights
    pad = LANES - IDX_DIM
    ikg = jnp.pad(ik_g, (0, pad)).reshape(1, LANES)
    ikb = jnp.pad(ik_b, (0, pad)).reshape(1, LANES)
    cos_t, sin_a, sin_b = tables
    full = lambda shape: pl.BlockSpec(shape, lambda b, i: (0,) * len(shape))
    tab = pl.BlockSpec((tm, LANES), lambda b, i: (i, 0))
    nq = IDX_HEADS * IDX_DIM
    rows = lambda n: pl.BlockSpec((1, tm, n), lambda b, i: (b, i, 0))
    cols = lambda n: pl.BlockSpec((1, n, tm), lambda b, i: (b, 0, i))
    return pl.pallas_call(
        _even_in_kernel,
        grid=(bsz, seq // tm),
        in_specs=[pl.BlockSpec((1, tm, d), lambda b, i: (b, i, 0)),
                  pl.BlockSpec((1, 6, d), lambda b, i: (b, 0, 0)),
                  full(w_r.shape), full(w_qkv.shape), full(w_idx.shape),
                  tab, tab, tab, full((1, LANES)), full((1, LANES))],
        out_specs=[rows(RWKV_COLS), cols(DSA_WIDTH), rows(DSA_WIDTH), cols(DSA_WIDTH),
                   cols(nq), rows(IDX_DIM), cols(SUBLANES)],
        out_shape=[jax.ShapeDtypeStruct((bsz, seq, RWKV_COLS), F32),
                   jax.ShapeDtypeStruct((bsz, DSA_WIDTH, seq), BF16),
                   jax.ShapeDtypeStruct((bsz, seq, DSA_WIDTH), BF16),
                   jax.ShapeDtypeStruct((bsz, DSA_WIDTH, seq), BF16),
                   jax.ShapeDtypeStruct((bsz, nq, seq), BF16),
                   jax.ShapeDtypeStruct((bsz, seq, IDX_DIM), BF16),
                   jax.ShapeDtypeStruct((bsz, SUBLANES, seq), F32)],
        compiler_params=_params(("parallel", "parallel")),
        name="even_in_proj",
    )(x, mod, w_r, w_qkv, w_idx, cos_t, sin_a, sin_b, ikg, ikb)


def _rwkv_kernel(p_ref, pp_ref, mu_ref, vec_ref, w2_ref, a2_ref, g2_ref, ltri_ref, ustr_ref,
                 seg_ref, o_ref, s_ref, obuf_ref):
    i = pl.program_id(1)

    @pl.when(i == 0)
    def _():
        s_ref[...] = jnp.zeros_like(s_ref)

    tt = RW_TILE
    w = RWKV_WIDTH
    hd = HEAD_DIM
    nb = p_ref.shape[0]
    nchunk = tt // RW_CHUNK
    seg = seg_ref[...]
    half_w = seg.shape[0]

    def segsum(t):
        return jnp.concatenate([_dot_split(t[:, c:c + half_w], seg, 2)
                                for c in range(0, w, half_w)], axis=1)

    vec = vec_ref[...]
    w0, a0, k_k, k_a, r_k, gn_g, gn_b = (vec[j:j + 1] for j in range(7))
    rowi = lax.broadcasted_iota(I32, (tt, 1), 0)
    live = (i > 0).astype(F32)

    def prepare(b):
        p = p_ref[b]
        prow = pp_ref[b][SUBLANES - 1:SUBLANES] * live
        xprev = jnp.where(rowi == 0, prow, pltpu.roll(p, 1, 0))
        ps = p + (xprev - p) * mu_ref[...]
        r = ps[:, :w]
        k = ps[:, w:2 * w]
        v = ps[:, 2 * w:3 * w]
        o1 = 3 * w
        wd = ps[:, o1:o1 + RWKV_LORA_W]
        ad = ps[:, o1 + RWKV_LORA_W:o1 + RWKV_LORA_W + RWKV_LORA_A]
        gd = ps[:, o1 + RWKV_LORA_W + RWKV_LORA_A:]
        y = -(w0 + _dot(jnp.tanh(wd), w2_ref[...]))
        softplus = jnp.maximum(y, 0.0) + jnp.log1p(jnp.exp(-jnp.abs(y)))
        logw = -jnp.exp(-softplus - 0.5)
        a = jax.nn.sigmoid(a0 + _dot(ad, a2_ref[...]))
        g = _dot(jax.nn.sigmoid(gd), g2_ref[...])
        kk = k * k_k
        kk = kk / jnp.maximum(jnp.sqrt(segsum(kk * kk)), 1e-12)
        k2 = k * (1.0 + (a - 1.0) * k_a)
        bonus = segsum(r * k2 * r_k) * v
        cum = _dot_split_left(ltri_ref[...], logw, 3)
        rem = _dot_split_left(ustr_ref[...], logw, 3)
        pt = jnp.exp(cum)
        ipt = jnp.exp(-cum)
        erem = jnp.exp(rem)
        kka = kk * a
        return dict(at=-kk * jnp.exp(cum - logw), rt=r * pt, bt=kka * ipt, kt=k2 * ipt,
                    bp=kka * erem, kp=k2 * erem, v=v, pt=pt, bonus=bonus, g=g)

    rows_in = [prepare(b) for b in range(nb)]

    ti = lax.broadcasted_iota(I32, (tt, tt), 0)
    si = lax.broadcasted_iota(I32, (tt, tt), 1)
    same = (ti // RW_CHUNK) == (si // RW_CHUNK)
    strict = same & (si < ti)
    incl = same & (si <= ti)
    tb = lax.broadcasted_iota(I32, (tt, nchunk * hd), 0)
    cb = lax.broadcasted_iota(I32, (tt, nchunk * hd), 1)
    blkmask = (tb // RW_CHUNK) == (cb // hd)
    tile_chunks = lambda t: jnp.where(blkmask, jnp.concatenate([t] * nchunk, axis=1), 0.0)

    units = [(b, slice(h * hd, (h + 1) * hd)) for b in range(nb) for h in range(RWKV_HEADS)]
    idx = range(len(units))
    pick = lambda name: [rows_in[b][name][:, sl] for b, sl in units]
    at_h, rt_h, vh, bt_h, kt_h = pick("at"), pick("rt"), pick("v"), pick("bt"), pick("kt")
    x = [_dot_nt(jnp.concatenate([at_h[u], rt_h[u]], axis=0),
                 jnp.concatenate([bt_h[u], kt_h[u]], axis=0)) for u in idx]
    a_ak = [jnp.where(strict, x[u][:tt, tt:], 0.0) for u in idx]
    a_rb = [jnp.where(incl, x[u][tt:, :tt], 0.0) for u in idx]
    a_rk = [jnp.where(incl, x[u][tt:, tt:], 0.0) for u in idx]
    def bd(m0, m1):
        z0 = jnp.zeros((m0.shape[0], m1.shape[1]), m0.dtype)
        z1 = jnp.zeros((m1.shape[0], m0.shape[1]), m1.dtype)
        return jnp.concatenate([jnp.concatenate([m0, z0], axis=1),
                                jnp.concatenate([z1, m1], axis=1)], axis=0)

    def pair_dot(lhs, rhs):
        out = []
        for u in range(0, len(lhs), 2):
            res = _dot(jnp.concatenate([lhs[u], lhs[u + 1]], axis=1), bd(rhs[u], rhs[u + 1]))
            cut = rhs[u].shape[1]
            out += [res[:, :cut], res[:, cut:]]
        return out

    aak_v = pair_dot(a_ak, vh)
    eye = (ti == si).astype(F32)
    inv = None
    blk = 1
    while blk < RW_CHUNK:
        below = same & ((ti // blk) % 2 == 1) & ((si // blk) % 2 == 0) & (
            (ti // (2 * blk)) == (si // (2 * blk)))
        a21 = [jnp.where(below, x[u][:tt, :tt], 0.0) for u in idx]
        if inv is None:
            inv = [eye + a21[u] for u in idx]
        else:
            left = pair_dot(a21, inv)
            grow = pair_dot(inv, left)
            inv = [inv[u] + grow[u] for u in idx]
        blk *= 2
    yv = pair_dot(inv, [jnp.concatenate([at_h[u], aak_v[u]], axis=1) for u in idx])
    arb_y = pair_dot(a_rb, yv)
    ark_v = pair_dot(a_rk, vh)
    qt = [rt_h[u] + arb_y[u][:, :hd] for u in idx]
    o0 = [arb_y[u][:, hd:] + ark_v[u] for u in idx]
    yt = [yv[u].T for u in idx]
    bpb = [tile_chunks(t) for t in pick("bp")]
    kpb = [tile_chunks(t) for t in pick("kp")]
    g_all = [_dot(yt[u][:hd], bpb[u]) for u in idx]
    h_all = [_dot(jnp.concatenate([yt[u][hd:], vh[u].T], axis=1),
                  jnp.concatenate([bpb[u], kpb[u]], axis=0)) for u in idx]
    pt_h = pick("pt")
    pairs = range(0, len(units), 2)
    low_lanes = lax.broadcasted_iota(I32, (hd, 2 * hd), 1) < hd
    s = [jnp.concatenate([s_ref[u], s_ref[u + 1]], axis=1) for u in pairs]
    for n in range(nchunk):
        rows = slice(n * RW_CHUNK, (n + 1) * RW_CHUNK)
        cols = slice(n * hd, (n + 1) * hd)
        last = (n + 1) * RW_CHUNK - 1
        for j, u in enumerate(pairs):
            s_bd = jnp.concatenate([jnp.where(low_lanes, s[j], 0.0),
                                    jnp.where(low_lanes, 0.0, s[j])], axis=0)
            o_pair = _dot_nt(jnp.concatenate([qt[u][rows], qt[u + 1][rows]], axis=1), s_bd)
            for i in range(2):
                b, sl = units[u + i]
                obuf_ref[b, rows, sl] = o_pair[:, i * hd:(i + 1) * hd] + o0[u + i][rows]
        s = [s[j] * jnp.concatenate([pt_h[u][last:last + 1], pt_h[u + 1][last:last + 1]], axis=1)
             + _dot(s[j], bd(g_all[u][:, cols], g_all[u + 1][:, cols]))
             + jnp.concatenate([h_all[u][:, cols], h_all[u + 1][:, cols]], axis=1)
             for j, u in enumerate(pairs)]
    for j, u in enumerate(pairs):
        s_ref[u] = s[j][:, :hd]
        s_ref[u + 1] = s[j][:, hd:]

    for b in range(nb):
        o = obuf_ref[b]
        mean = segsum(o) * (1.0 / hd)
        oc = o - mean
        var = segsum(oc * oc) * (1.0 / hd)
        on = oc * lax.rsqrt(var + GN_EPS) * gn_g + gn_b
        o_ref[b] = ((on + rows_in[b]["bonus"]) * rows_in[b]["g"]).astype(BF16)


def _rwkv(p_r, mu, w0, w2, a0, a2, g2, k_k, k_a, r_k, gn_g, gn_b):
    bsz, seq, _ = p_r.shape
    tt = RW_TILE
    nb = RW_ROWS if bsz % RW_ROWS == 0 else 1
    w = RWKV_WIDTH
    vec = jnp.stack([w0, a0, k_k, k_a, r_k.reshape(w), gn_g, gn_b, jnp.zeros_like(w0)])
    t_idx = np.arange(tt)
    same = (t_idx[:, None] // RW_CHUNK) == (t_idx[None, :] // RW_CHUNK)
    ltri = jnp.asarray(same & (t_idx[None, :] <= t_idx[:, None]), BF16)
    ustr = jnp.asarray(same & (t_idx[None, :] > t_idx[:, None]), BF16)
    c_idx = np.arange(MXU_TILE)
    seg = jnp.asarray((c_idx[:, None] // HEAD_DIM) == (c_idx[None, :] // HEAD_DIM), BF16)
    full = lambda shape: pl.BlockSpec(shape, lambda b, i: (0,) * len(shape))
    per_tile = tt // SUBLANES
    return pl.pallas_call(
        _rwkv_kernel,
        grid=(bsz // nb, seq // tt),
        in_specs=[pl.BlockSpec((nb, tt, RWKV_COLS), lambda b, i: (b, i, 0)),
                  pl.BlockSpec((nb, SUBLANES, RWKV_COLS),
                               lambda b, i: (b, jnp.maximum(i * per_tile - 1, 0), 0)),
                  full((1, RWKV_COLS)), full((SUBLANES, w)),
                  full(w2.shape), full(a2.shape), full(g2.shape),
                  full((tt, tt)), full((tt, tt)), full((MXU_TILE, MXU_TILE))],
        out_specs=pl.BlockSpec((nb, tt, w), lambda b, i: (b, i, 0)),
        out_shape=jax.ShapeDtypeStruct((bsz, seq, w), BF16),
        scratch_shapes=[pltpu.VMEM((nb * RWKV_HEADS, HEAD_DIM, HEAD_DIM), F32),
                        pltpu.VMEM((nb, tt, w), F32)],
        compiler_params=_params(("parallel", "arbitrary")),
        name="rwkv7_mix",
    )(p_r, p_r, mu.reshape(1, RWKV_COLS), vec, w2.astype(BF16), a2.astype(BF16),
      g2.astype(BF16), ltri, ustr, seg)


def _dsa_kernel(qt_ref, k_ref, vt_ref, qit_ref, ki_ref, wit_ref, low_ref, o_ref,
                key_ref, top_ref, bias_ref, acc_ref, m_ref, l_ref, *, topk):
    qb = pl.program_id(1)
    nq = Q_BLOCK
    kt_sz = bias_ref.shape[0]
    half = kt_sz // 2
    quarter = kt_sz // 4
    hd = HEAD_DIM
    start = qb * nq
    nun = (start + nq + half - 1) // half
    col = lax.broadcasted_iota(I32, (1, nq), 1)
    lim = start + (col // CHUNK + 1) * CHUNK
    wit = wit_ref[0]
    qit = qit_ref[0]
    qi_cat = jnp.concatenate([qit[h * IDX_DIM:(h + 1) * IDX_DIM] for h in range(IDX_HEADS)],
                             axis=1)

    def score_rows(off, rows, masked=True):
        d = jnp.dot(ki_ref[0, pl.ds(off, rows), :], qi_cat, preferred_element_type=F32)
        s = jnp.zeros((rows, nq), F32)
        for h in range(IDX_HEADS):
            s = s + wit[h:h + 1] * jnp.maximum(d[:, h * nq:(h + 1) * nq], 0.0)
        s = s + 0.0
        bits = pltpu.bitcast(s, I32)
        key = bits ^ ((bits >> 31) & 0x7FFFFFFF)
        if masked:
            sidx = off + lax.broadcasted_iota(I32, (rows, nq), 0)
            key = jnp.where(sidx < lim, key, INT_MIN)
        key_ref[pl.ds(off, rows), :] = key
        field = (key >> (32 - TOP_BITS)) + (3 << (TOP_BITS - 1))
        for c0 in range(0, rows, half):
            top_ref[pl.ds(pl.multiple_of((off + c0) // 2, quarter), quarter), :] = (
                (field[c0:c0 + quarter] << 16) | field[c0 + quarter:c0 + half])

    def over_keys(step, init, first=0):
        c = lax.fori_loop(
            first, nun // 2, lambda j, c: step(pl.multiple_of(j * kt_sz, kt_sz), kt_sz, c), init)
        return lax.cond(nun % 2 == 1,
                        lambda c: step(pl.multiple_of((nun - 1) * half, half), half, c),
                        lambda c: c, c)

    def score_step(off, rows, carry):
        score_rows(off, rows)
        return carry

    def open_tile(j, carry):
        score_rows(pl.multiple_of(j * kt_sz, kt_sz), kt_sz, masked=False)
        return carry

    n_open = (start + CHUNK) // kt_sz
    lax.fori_loop(0, n_open, open_tile, 0)
    over_keys(score_step, jnp.int32(0), first=n_open)

    def count(pred):
        def step(off, rows, c):
            hit = jnp.where(pred(key_ref[pl.ds(off, rows), :]), 1, 0)
            return c + hit.reshape(rows // RED_ROWS, RED_ROWS, nq).sum(axis=0)
        return jnp.sum(over_keys(step, jnp.zeros((RED_ROWS, nq), I32)), axis=0, keepdims=True)

    def count_top(cand):
        both = (cand << 16) | cand

        def step(off, rows, c):
            words = top_ref[pl.ds(pl.multiple_of(off // 2, quarter), rows // 2), :]
            hit = lax.shift_right_logical(words - both, 15) & 0x00010001
            return c + hit.reshape(rows // 2 // RED_ROWS, RED_ROWS, nq).sum(axis=0)

        c = jnp.sum(over_keys(step, jnp.zeros((RED_ROWS, nq), I32)), axis=0, keepdims=True)
        return (c & 0xFFFF) + lax.shift_right_logical(c, 16)

    def top_step(b, prefix):
        cand = prefix | lax.shift_left(jnp.int32(1), TOP_BITS - 1 - b)
        return jnp.where(count_top(cand) >= topk, cand, prefix)

    prefix = lax.fori_loop(0, TOP_BITS, top_step, jnp.zeros((1, nq), I32))
    lo = (prefix - (1 << (TOP_BITS - 1))) << (32 - TOP_BITS)

    def bit_step(b, lo):
        cand = lo + lax.shift_left(jnp.int32(1), 31 - TOP_BITS - b)
        return jnp.where(count(lambda kv: kv >= cand) >= topk, cand, lo)

    th = lax.fori_loop(0, 32 - TOP_BITS, bit_step, lo)
    need = jnp.where(th == INT_MIN, 0, topk - count(lambda kv: kv > th)).astype(F32)

    acc_ref[...] = jnp.zeros_like(acc_ref)
    m_ref[...] = jnp.full_like(m_ref, NEG_INF)
    l_ref[...] = jnp.zeros_like(l_ref)
    low = low_ref[...]
    npair = DSA_HEADS // 2
    top_rows = lax.broadcasted_iota(I32, (2 * hd, nq), 0) < hd
    qt = qt_ref[0].astype(F32)
    q_bd = []
    for p in range(npair):
        qp = qt[p * 2 * hd:(p + 1) * 2 * hd]
        q_bd.append(jnp.concatenate([jnp.where(top_rows, qp, 0.0), jnp.where(top_rows, 0.0, qp)],
                                    axis=1).astype(BF16))

    pad_rows = 2 * SUBLANES

    def attn_tile(off, rows, eq_before):
        ones_rows = (lax.broadcasted_iota(I32, (pad_rows, rows), 0) == 0).astype(BF16)
        eq_run = eq_before
        for t in range(rows // TRI_TILE):
            kv = key_ref[pl.ds(off + t * TRI_TILE, TRI_TILE), :]
            eq = kv == th
            pre = eq_run + jnp.dot(low, jnp.where(eq, 1.0, 0.0).astype(BF16),
                                   preferred_element_type=F32)
            sel = (kv > th) | (eq & (pre <= need))
            bias_ref[t * TRI_TILE:(t + 1) * TRI_TILE, :] = jnp.where(sel, 0.0, NEG_INF)
            eq_run = pre[TRI_TILE - 1:TRI_TILE]
        bias = bias_ref[:rows]

        def logits(p):
            part = rows // 2
            return jnp.concatenate(
                [jnp.dot(k_ref[0, pl.ds(off + r * part, part), p * 2 * hd:(p + 1) * 2 * hd],
                         q_bd[p], preferred_element_type=F32) for r in range(2)], axis=0)

        st_next = logits(0)
        for p in range(npair):
            cols = slice(p * 2 * hd, (p + 1) * 2 * hd)
            st = st_next
            if p + 1 < npair:
                st_next = logits(p + 1)
            prs, scales = [], []
            for i in range(2):
                h = 2 * p + i
                s = st[:, i * nq:(i + 1) * nq] + bias
                m_old = m_ref[h:h + 1]
                m_new = jnp.maximum(m_old, _fold_rows(s, jnp.max))
                prs.append(jnp.exp2(s - m_new).astype(BF16))
                scales.append(jnp.exp2(m_old - m_new))
                m_ref[h:h + 1] = m_new
            lhs = jnp.concatenate([vt_ref[0, cols, pl.ds(off, rows)], ones_rows], axis=0)
            pv = jnp.dot(lhs, jnp.concatenate(prs, axis=1),
                         preferred_element_type=F32)
            for i in range(2):
                h = 2 * p + i
                l_ref[h:h + 1] = (scales[i] * l_ref[h:h + 1]
                                  + pv[2 * hd:2 * hd + 1, i * nq:(i + 1) * nq])
            new = jnp.where(top_rows, pv[:2 * hd, :nq], pv[:2 * hd, nq:])
            acc_ref[p] = jnp.where(top_rows, scales[0], scales[1]) * acc_ref[p] + new
        return eq_run

    over_keys(attn_tile, jnp.zeros((1, nq), F32))
    for p in range(npair):
        inv = jnp.where(top_rows, 1.0 / l_ref[2 * p:2 * p + 1], 1.0 / l_ref[2 * p + 1:2 * p + 2])
        o_ref[0, :, p * 2 * hd:(p + 1) * 2 * hd] = (acc_ref[p] * inv).T.astype(BF16)


def _dsa(q_t, k, v_t, qi_t, ki, wi_t):
    bsz, seq, w = k.shape
    nq = Q_BLOCK
    topk = min(INDEX_TOPK, seq // 4)
    kt_sz = min(KEY_TILE, seq)
    assert kt_sz % (4 * RED_ROWS) == 0 and kt_sz % (2 * TRI_TILE) == 0 and seq // 2 < 2 ** 16
    t_idx = np.arange(TRI_TILE)
    low = jnp.asarray(t_idx[None, :] <= t_idx[:, None], BF16)
    cols = lambda n: pl.BlockSpec((1, n, nq), lambda b, i: (b, 0, i))
    return pl.pallas_call(
        functools.partial(_dsa_kernel, topk=topk),
        grid=(bsz, seq // nq),
        in_specs=[cols(w),
                  pl.BlockSpec((1, seq, w), lambda b, i: (b, 0, 0)),
                  pl.BlockSpec((1, w, seq), lambda b, i: (b, 0, 0)),
                  cols(IDX_HEADS * IDX_DIM),
                  pl.BlockSpec((1, seq, IDX_DIM), lambda b, i: (b, 0, 0)),
                  cols(SUBLANES),
                  pl.BlockSpec((TRI_TILE, TRI_TILE), lambda b, i: (0, 0))],
        out_specs=pl.BlockSpec((1, nq, w), lambda b, i: (b, i, 0)),
        out_shape=jax.ShapeDtypeStruct((bsz, seq, w), BF16),
        scratch_shapes=[pltpu.VMEM((seq, nq), I32),
                        pltpu.VMEM((seq // 2, nq), I32),
                        pltpu.VMEM((kt_sz, nq), F32),
                        pltpu.VMEM((DSA_HEADS // 2, 2 * HEAD_DIM, nq), F32),
                        pltpu.VMEM((DSA_HEADS, nq), F32),
                        pltpu.VMEM((DSA_HEADS, nq), F32)],
        compiler_params=_params(("parallel", "arbitrary")),
        name="dsa_mix",
    )(q_t, k, v_t, qi_t, ki, wi_t, low)


POOL_HALO = 16


def _odd_layer_kernel(x_ref, xp_ref, mod_ref, win_ref, pw_ref, ps_ref, lng_ref, lnb_ref, ws_ref,
                      bs_ref, wout_ref, g_ref, b_ref, o_ref, y_ref):
    i = pl.program_id(1)
    m = mod_ref[0]
    sc = 1.0 + m[1:2]
    sh = m[0:1]
    x = x_ref[0]
    tm = x.shape[0]
    gd = POOL_GROUP_DIM
    p = jnp.dot((x * sc + sh).astype(BF16), win_ref[...], preferred_element_type=F32)
    prev = jnp.dot((xp_ref[0] * sc + sh).astype(BF16), win_ref[:, :POOL_WIDTH],
                   preferred_element_type=F32) * (i > 0).astype(F32)
    t_glob = (i * tm + lax.broadcasted_iota(I32, (tm, 1), 0)).astype(F32)
    scale = ps_ref[...]
    for gi, win in enumerate(POOL_WINDOWS):
        cols = slice(gi * gd, (gi + 1) * gd)
        xg = p[:, cols]
        s = jnp.concatenate([prev[:, cols], xg], axis=0)
        span = 1
        while span < win:
            s = s[span:] + s[:-span]
            span *= 2
        first = POOL_HALO + 1 - win
        pooled = s[first:first + tm] / jnp.minimum(t_glob + 1.0, float(win)) - xg
        y_ref[:, cols] = (_dot(pooled, pw_ref[gi]) * scale[:, cols]).astype(BF16)

    u = _gelu(p[:, POOL_WIDTH:POOL_WIDTH + SG_WIDTH])
    v = _ln(_gelu(p[:, POOL_WIDTH + SG_WIDTH:]), lng_ref[...], lnb_ref[...], LN_EPS)
    ti = lax.broadcasted_iota(I32, (SG_CHUNK, SG_CHUNK), 0)
    si = lax.broadcasted_iota(I32, (SG_CHUNK, SG_CHUNK), 1)
    bs = bs_ref[...]
    for gi in range(SG_GROUPS):
        cols = slice(gi * SG_GROUP_DIM, (gi + 1) * SG_GROUP_DIM)
        ws = jnp.where(si <= ti, ws_ref[gi], 0.0)
        for n in range(tm // SG_CHUNK):
            rows = slice(n * SG_CHUNK, (n + 1) * SG_CHUNK)
            z = _dot(ws, v[rows, cols]) + bs[:, gi:gi + 1]
            y_ref[rows, POOL_WIDTH + gi * SG_GROUP_DIM:POOL_WIDTH + (gi + 1) * SG_GROUP_DIM] = (
                u[rows, cols] * z).astype(BF16)

    y = jnp.dot(y_ref[...], wout_ref[...], preferred_element_type=F32)
    o_ref[0] = _ln(ALPHA * x + m[2:3] * y, g_ref[...], b_ref[...], LN_EPS)


def _odd_layer(x, mod, w_in, pool_w, pool_scale, sg_ln_g, sg_ln_b, sg_w, sg_b, w_out, ln_g, ln_b):
    bsz, seq, d = x.shape
    tm = ODD_ROWS
    per_tile = tm // POOL_HALO
    full = lambda shape: pl.BlockSpec(shape, lambda b, i: (0,) * len(shape))
    once = lambda shape: pl.BlockSpec(shape, lambda b, i: (0,) * len(shape),
                                      pipeline_mode=pl.Buffered(1))
    return pl.pallas_call(
        _odd_layer_kernel,
        grid=(bsz, seq // tm),
        in_specs=[pl.BlockSpec((1, tm, d), lambda b, i: (b, i, 0)),
                  pl.BlockSpec((1, POOL_HALO, d),
                               lambda b, i: (b, jnp.maximum(i * per_tile - 1, 0), 0)),
                  pl.BlockSpec((1, 6, d), lambda b, i: (b, 0, 0)),
                  once(w_in.shape), full(pool_w.shape), full((1, POOL_WIDTH)),
                  full((1, SG_WIDTH)), full((1, SG_WIDTH)), full(sg_w.shape),
                  full((SG_CHUNK, SG_GROUPS)), once(w_out.shape), full((1, d)), full((1, d))],
        out_specs=pl.BlockSpec((1, tm, d), lambda b, i: (b, i, 0)),
        out_shape=jax.ShapeDtypeStruct((bsz, seq, d), F32),
        scratch_shapes=[pltpu.VMEM((tm, d), BF16)],
        compiler_params=_params(("parallel", "parallel")),
        name="odd_layer_mix",
    )(x, x, mod, w_in.astype(BF16), pool_w.astype(BF16), pool_scale.reshape(1, -1),
      sg_ln_g.reshape(1, -1), sg_ln_b.reshape(1, -1), sg_w, sg_b.T, w_out.astype(BF16),
      ln_g.reshape(1, d), ln_b.reshape(1, d))


def _proj_ln_kernel(*refs, n_in, gate_row):
    a_refs = refs[:n_in]
    w_refs = refs[n_in:2 * n_in]
    x_ref, mod_ref, g_ref, b_ref, o_ref = refs[2 * n_in:]
    y = None
    for a_ref, w_ref in zip(a_refs, w_refs):
        part = jnp.dot(a_ref[0], w_ref[...], preferred_element_type=F32)
        y = part if y is None else y + part
    gate = mod_ref[0][gate_row:gate_row + 1]
    o_ref[0] = _ln(ALPHA * x_ref[0] + gate * y, g_ref[...], b_ref[...], LN_EPS)


def _proj_ln(acts, weights, x, mod, gate_row, ln_g, ln_b, name):
    bsz, seq, d = x.shape
    tm = OUT_PROJ_ROWS
    n_in = len(acts)
    in_specs = [pl.BlockSpec((1, tm, a.shape[2]), lambda b, i: (b, i, 0)) for a in acts]
    in_specs += [pl.BlockSpec(w.shape, lambda b, i: (0, 0)) for w in weights]
    in_specs += [pl.BlockSpec((1, tm, d), lambda b, i: (b, i, 0)),
                 pl.BlockSpec((1, 6, d), lambda b, i: (b, 0, 0)),
                 pl.BlockSpec((1, d), lambda b, i: (0, 0)),
                 pl.BlockSpec((1, d), lambda b, i: (0, 0))]
    return pl.pallas_call(
        functools.partial(_proj_ln_kernel, n_in=n_in, gate_row=gate_row),
        grid=(bsz, seq // tm),
        in_specs=in_specs,
        out_specs=pl.BlockSpec((1, tm, d), lambda b, i: (b, i, 0)),
        out_shape=jax.ShapeDtypeStruct((bsz, seq, d), F32),
        compiler_params=_params(("parallel", "parallel")),
        name=name,
    )(*acts, *[w.astype(BF16) for w in weights], x, mod, ln_g.reshape(1, d), ln_b.reshape(1, d))


def _ffn_kernel(x_ref, xp_ref, mod_ref, wg_ref, wv_ref, cwg_ref, cwv_ref, cbg_ref, cbv_ref,
                wd_ref, g_ref, b_ref, o_ref):
    i = pl.program_id(1)
    m = mod_ref[0]
    sc = 1.0 + m[4:5]
    sh = m[3:4]
    x = x_ref[0]
    live = (i > 0).astype(F32)
    h = jnp.concatenate([(xp_ref[0] * sc + sh) * live, x * sc + sh], axis=0).astype(BF16)
    halo = SUBLANES

    def conv(w_ref, cw_ref, cb_ref):
        u = jnp.dot(h, w_ref[...], preferred_element_type=F32)
        cw = cw_ref[...]
        u1 = pltpu.roll(u, 1, 0)
        u2 = pltpu.roll(u, 2, 0)
        return (cb_ref[...] + u2[halo:] * cw[0:1] + u1[halo:] * cw[1:2] + u[halo:] * cw[2:3])

    gate = conv(wg_ref, cwg_ref, cbg_ref)
    val = conv(wv_ref, cwv_ref, cbv_ref)
    act = (gate * jax.nn.sigmoid(gate) * val).astype(BF16)
    y = jnp.dot(act, wd_ref[...], preferred_element_type=F32)
    o_ref[0] = _ln(ALPHA * x + m[5:6] * y, g_ref[...], b_ref[...], LN_EPS)


def _conv_ffn(x, mod, w_up, conv_w, conv_b, w_down, ln_g, ln_b):
    bsz, seq, d = x.shape
    tm = FFN_ROWS
    per_tile = tm // SUBLANES
    wb = w_up.astype(BF16)
    cb = conv_b.reshape(1, -1)
    once = dict(pipeline_mode=pl.Buffered(1))
    return pl.pallas_call(
        _ffn_kernel,
        grid=(bsz, seq // tm),
        in_specs=[pl.BlockSpec((1, tm, d), lambda b, i: (b, i, 0)),
                  pl.BlockSpec((1, SUBLANES, d),
                               lambda b, i: (b, jnp.maximum(i * per_tile - 1, 0), 0)),
                  pl.BlockSpec((1, 6, d), lambda b, i: (b, 0, 0)),
                  pl.BlockSpec((d, D_FF), lambda b, i: (0, 0), **once),
                  pl.BlockSpec((d, D_FF), lambda b, i: (0, 1), **once),
                  pl.BlockSpec((3, D_FF), lambda b, i: (0, 0), **once),
                  pl.BlockSpec((3, D_FF), lambda b, i: (0, 1), **once),
                  pl.BlockSpec((1, D_FF), lambda b, i: (0, 0), **once),
                  pl.BlockSpec((1, D_FF), lambda b, i: (0, 1), **once),
                  pl.BlockSpec((D_FF, d), lambda b, i: (0, 0), **once),
                  pl.BlockSpec((1, d), lambda b, i: (0, 0)),
                  pl.BlockSpec((1, d), lambda b, i: (0, 0))],
        out_specs=pl.BlockSpec((1, tm, d), lambda b, i: (b, i, 0)),
        out_shape=jax.ShapeDtypeStruct((bsz, seq, d), F32),
        compiler_params=_params(("parallel", "parallel")),
        name="conv_ffn_ln",
    )(x, x, mod, wb, wb, conv_w, conv_w, cb, cb, w_down.astype(BF16),
      ln_g.reshape(1, d), ln_b.reshape(1, d))


def _rope_tables(seq):
    inv = ROPE_THETA ** (-jnp.arange(0, ROPE_DIM, 2, dtype=F32) / ROPE_DIM)
    ang = jnp.arange(seq, dtype=F32)[:, None] * inv[None, :]
    cos, sin = jnp.cos(ang), jnp.sin(ang)
    half = ROPE_DIM // 2
    rest = HEAD_DIM - ROPE_DIM
    one = jnp.ones((seq, rest), F32)
    zero = jnp.zeros((seq, rest), F32)
    zh = jnp.zeros((seq, half), F32)
    head = lambda *parts: jnp.concatenate(parts * (LANES // HEAD_DIM), axis=1)
    return head(cos, cos, one), head(-sin, zh, zero), head(zh, sin, zero)


def kernel(x, c, ada_w, ada_b, ln_g, ln_b, ffn_w_up, ffn_conv_w, ffn_conv_b, ffn_w_down, ev_w_in, ev_w_out, rw_mu, rw_w0, rw_w2, rw_a0, rw_a2, rw_g2, rw_k_k, rw_k_a, rw_r_k, rw_gn_g, rw_gn_b, ik_ln_g, ik_ln_b, od_w_in, od_w_out, pool_w, pool_scale, sg_ln_g, sg_ln_b, sg_w, sg_b):
    seq = x.shape[1]
    tables = _rope_tables(seq)
    mods = _modulation(c, ada_w, ada_b)
    ev_w = _split_even_weights(ev_w_in)
    for layer in range(DEPTH):
        mod = mods[layer]
        if layer % 2 == 0:
            e = layer // 2
            p_r, q_t, k, v_t, qi_t, ki, wi_t = _even_in_proj(x, mod, [w[e] for w in ev_w],
                                                             ik_ln_g[e], ik_ln_b[e], tables)
            ya = _rwkv(p_r, rw_mu[e], rw_w0[e], rw_w2[e], rw_a0[e], rw_a2[e], rw_g2[e],
                       rw_k_k[e], rw_k_a[e], rw_r_k[e], rw_gn_g[e], rw_gn_b[e])
            yb = _dsa(q_t, k, v_t, qi_t, ki, wi_t)
            w_out = ev_w_out[e]
            x = _proj_ln([ya, yb], [w_out[:RWKV_WIDTH], w_out[RWKV_WIDTH:]], x, mod, 2,
                         ln_g[layer, 0], ln_b[layer, 0], "even_out_proj_ln")
        else:
            o = layer // 2
            x = _odd_layer(x, mod, od_w_in[o], pool_w[o], pool_scale[o], sg_ln_g[o], sg_ln_b[o],
                           sg_w[o], sg_b[o], od_w_out[o], ln_g[layer, 0], ln_b[layer, 0])
        x = _conv_ffn(x, mod, ffn_w_up[layer], ffn_conv_w[layer], ffn_conv_b[layer],
                      ffn_w_down[layer], ln_g[layer, 1], ln_b[layer, 1])
    return x
```

```python
import functools

import numpy as np
import jax
import jax.numpy as jnp
from jax import lax
from jax.experimental import pallas as pl
from jax.experimental.pallas import tpu as pltpu

F32 = jnp.float32
BF16 = jnp.bfloat16
I32 = jnp.int32

D_MODEL = 1024
DEPTH = 4
CHUNK = 64
HEAD_DIM = 64
RWKV_WIDTH = D_MODEL // 2
RWKV_HEADS = RWKV_WIDTH // HEAD_DIM
RWKV_LORA_W = 64
RWKV_LORA_A = 64
RWKV_LORA_G = 128
RWKV_COLS = 3 * RWKV_WIDTH + RWKV_LORA_W + RWKV_LORA_A + RWKV_LORA_G
DSA_WIDTH = D_MODEL - RWKV_WIDTH
DSA_HEADS = DSA_WIDTH // HEAD_DIM
IDX_HEADS = 4
IDX_DIM = 64
INDEX_TOPK = 256
Q_BLOCK = 128
ROPE_THETA = 500000.0
ROPE_DIM = HEAD_DIM // 4
POOL_WINDOWS = (2, 4, 8, 16)
POOL_WIDTH = D_MODEL // 2
POOL_GROUP_DIM = POOL_WIDTH // len(POOL_WINDOWS)
SG_WIDTH = D_MODEL - POOL_WIDTH
SG_GROUPS = 4
SG_GROUP_DIM = SG_WIDTH // SG_GROUPS
SG_CHUNK = 128
D_FF = 2816
ALPHA = (2.0 * DEPTH) ** 0.25
LN_EPS = 1e-5
GN_EPS = 64e-5
NEG_INF = -1e30
INT_MIN = -(2 ** 31)
LOG2E = 1.4426950408889634

LANES = 128
SUBLANES = 8
MXU_TILE = 256
VMEM_LIMIT = 56 * 1024 * 1024

MOD_COLS = 1536
SPLIT_ROWS = 256
IN_PROJ_ROWS = 256
ODD_ROWS = 256
OUT_PROJ_ROWS = 512
FFN_ROWS = 512
RW_TILE = 128
RW_CHUNK = 16
RW_ROWS = 4
KEY_TILE = 1024
TRI_TILE = 256
TOP_BITS = 15
PAD_IDX = 384

def _dot(a, b):
    return jnp.dot(a.astype(BF16), b.astype(BF16), preferred_element_type=F32)


def _dot_nt(a, b):
    return lax.dot_general(a.astype(BF16), b.astype(BF16), (((1,), (1,)), ((), ())),
                           preferred_element_type=F32)


def _dot_split(x, m01, terms):
    acc = None
    rem = x
    for _ in range(terms):
        piece = rem.astype(BF16)
        rem = rem - piece.astype(F32)
        part = jnp.dot(piece, m01, preferred_element_type=F32)
        acc = part if acc is None else acc + part
    return acc


def _dot_split_left(m01, x, terms):
    acc = None
    rem = x
    for _ in range(terms):
        piece = rem.astype(BF16)
        rem = rem - piece.astype(F32)
        part = jnp.dot(m01, piece, preferred_element_type=F32)
        acc = part if acc is None else acc + part
    return acc


def _ln(x, g, b, eps):
    mu = jnp.mean(x, axis=-1, keepdims=True)
    xc = x - mu
    var = jnp.mean(xc * xc, axis=-1, keepdims=True)
    return xc * lax.rsqrt(var + eps) * g + b


def _gelu(x):
    return 0.5 * x * (1.0 + lax.erf(x * 0.7071067811865476))


RED_ROWS = 64


def _fold_rows(x, op):
    rows, n = x.shape
    part = op(x.reshape(rows // RED_ROWS, RED_ROWS, n), axis=0)
    return op(part, axis=0, keepdims=True)


def _params(sem):
    return pltpu.CompilerParams(dimension_semantics=sem, vmem_limit_bytes=VMEM_LIMIT)


def _mod_kernel(c_ref, w_ref, b_ref, o_ref):
    c = c_ref[...]
    ca = c * jax.nn.sigmoid(c)
    o_ref[0] = jnp.dot(ca, w_ref[0], preferred_element_type=F32,
                       precision=lax.Precision.HIGHEST) + b_ref[0]


def _modulation(c, ada_w, ada_b):
    bsz, d = c.shape
    depth = ada_w.shape[0]
    n = ada_w.shape[2]
    tn = MOD_COLS
    rows = -(-bsz // SUBLANES) * SUBLANES
    c8 = jnp.pad(c, ((0, rows - bsz), (0, 0)))
    out = pl.pallas_call(
        _mod_kernel,
        grid=(depth, n // tn),
        in_specs=[pl.BlockSpec((rows, d), lambda l, j: (0, 0)),
                  pl.BlockSpec((1, d, tn), lambda l, j: (l, 0, j)),
                  pl.BlockSpec((1, 1, tn), lambda l, j: (l, 0, j))],
        out_specs=pl.BlockSpec((1, rows, tn), lambda l, j: (l, 0, j)),
        out_shape=jax.ShapeDtypeStruct((depth, rows, n), F32),
        compiler_params=_params(("arbitrary", "arbitrary")),
        name="adaln_mod",
    )(c8, ada_w, ada_b.reshape(depth, 1, n))
    return out[:, :bsz].reshape(depth, bsz, 6, d)


def _rope(x, cos_t, sin_a, sin_b):
    n = x.shape[1] // LANES
    rep = (lambda t: jnp.concatenate([t] * n, axis=1)) if n > 1 else (lambda t: t)
    width = x.shape[1]
    half = ROPE_DIM // 2
    return (x * rep(cos_t) + pltpu.roll(x, width - half, 1) * rep(sin_a)
            + pltpu.roll(x, half, 1) * rep(sin_b))


def _even_in_kernel(x_ref, mod_ref, wr_ref, wqkv_ref, widx_ref, cos_ref, sa_ref, sb_ref,
                    ikg_ref, ikb_ref,
                    pr_ref, qt_ref, k_ref, vt_ref, qit_ref, ki_ref, wit_ref):
    m = mod_ref[0]
    h = (x_ref[0] * (1.0 + m[1:2]) + m[0:1]).astype(BF16)
    pr_ref[0] = jnp.dot(h, wr_ref[...], preferred_element_type=F32)
    qkv = jnp.dot(h, wqkv_ref[...], preferred_element_type=F32)
    cos_t = cos_ref[...]
    sin_a = sa_ref[...]
    sin_b = sb_ref[...]
    w = DSA_WIDTH
    q = _rope(qkv[:, :w], cos_t, sin_a, sin_b) * (HEAD_DIM ** -0.5 * LOG2E)
    qt_ref[0] = q.T.astype(BF16)
    k_ref[0] = _rope(qkv[:, w:2 * w], cos_t, sin_a, sin_b).astype(BF16)
    vt_ref[0] = qkv[:, 2 * w:].T.astype(BF16)
    idx = jnp.dot(h, widx_ref[...], preferred_element_type=F32)
    nq = IDX_HEADS * IDX_DIM
    qit_ref[0] = _rope(idx[:, :nq], cos_t, sin_a, sin_b).T.astype(BF16)
    blk = idx[:, nq:nq + LANES]
    lane = lax.broadcasted_iota(I32, blk.shape, 1)
    is_k = lane < IDX_DIM
    mu = jnp.sum(jnp.where(is_k, blk, 0.0), axis=1, keepdims=True) * (1.0 / IDX_DIM)
    xc = jnp.where(is_k, blk - mu, 0.0)
    var = jnp.sum(xc * xc, axis=1, keepdims=True) * (1.0 / IDX_DIM)
    kin = xc * lax.rsqrt(var + LN_EPS) * ikg_ref[...] + ikb_ref[...]
    ki_ref[0] = _rope(kin, cos_t, sin_a, sin_b)[:, :IDX_DIM].astype(BF16)
    wit = (blk * (IDX_HEADS ** -0.5 * IDX_DIM ** -0.5)).T
    wit_ref[0] = wit[IDX_DIM:IDX_DIM + SUBLANES]


def _split_w_kernel(w_ref, wr_ref, wqkv_ref, widx_ref):
    w = w_ref[0]
    c1 = RWKV_COLS
    c2 = RWKV_COLS + 3 * DSA_WIDTH
    wr_ref[0] = w[:, :c1].astype(BF16)
    wqkv_ref[0] = w[:, c1:c2].astype(BF16)
    tail = w[:, c2:]
    zeros = jnp.zeros((w.shape[0], PAD_IDX - tail.shape[1]), F32)
    widx_ref[0] = jnp.concatenate([tail, zeros], axis=1).astype(BF16)


def _split_even_weights(ev_w_in):
    n_even, d, n = ev_w_in.shape
    tr = SPLIT_ROWS
    widths = (RWKV_COLS, 3 * DSA_WIDTH, PAD_IDX)
    return pl.pallas_call(
        _split_w_kernel,
        grid=(n_even, d // tr),
        in_specs=[pl.BlockSpec((1, tr, n), lambda e, i: (e, i, 0))],
        out_specs=[pl.BlockSpec((1, tr, wd), lambda e, i: (e, i, 0)) for wd in widths],
        out_shape=[jax.ShapeDtypeStruct((n_even, d, wd), BF16) for wd in widths],
        compiler_params=_params(("parallel", "parallel")),
        name="split_even_weights",
    )(ev_w_in)


def _even_in_proj(x, mod, weights, ik_g, ik_b, tables):
    bsz, seq, d = x.shape
    tm = IN_PROJ_ROWS
    w_r, w_qkv, w_idx = weights
    pad = LANES - IDX_DIM
    ikg = jnp.pad(ik_g, (0, pad)).reshape(1, LANES)
    ikb = jnp.pad(ik_b, (0, pad)).reshape(1, LANES)
    cos_t, sin_a, sin_b = tables
    full = lambda shape: pl.BlockSpec(shape, lambda b, i: (0,) * len(shape))
    tab = pl.BlockSpec((tm, LANES), lambda b, i: (i, 0))
    nq = IDX_HEADS * IDX_DIM
    rows = lambda n: pl.BlockSpec((1, tm, n), lambda b, i: (b, i, 0))
    cols = lambda n: pl.BlockSpec((1, n, tm), lambda b, i: (b, 0, i))
    return pl.pallas_call(
        _even_in_kernel,
        grid=(bsz, seq // tm),
        in_specs=[pl.BlockSpec((1, tm, d), lambda b, i: (b, i, 0)),
                  pl.BlockSpec((1, 6, d), lambda b, i: (b, 0, 0)),
                  full(w_r.shape), full(w_qkv.shape), full(w_idx.shape),
                  tab, tab, tab, full((1, LANES)), full((1, LANES))],
        out_specs=[rows(RWKV_COLS), cols(DSA_WIDTH), rows(DSA_WIDTH), cols(DSA_WIDTH),
                   cols(nq), rows(IDX_DIM), cols(SUBLANES)],
        out_shape=[jax.ShapeDtypeStruct((bsz, seq, RWKV_COLS), F32),
                   jax.ShapeDtypeStruct((bsz, DSA_WIDTH, seq), BF16),
                   jax.ShapeDtypeStruct((bsz, seq, DSA_WIDTH), BF16),
                   jax.ShapeDtypeStruct((bsz, DSA_WIDTH, seq), BF16),
                   jax.ShapeDtypeStruct((bsz, nq, seq), BF16),
                   jax.ShapeDtypeStruct((bsz, seq, IDX_DIM), BF16),
                   jax.ShapeDtypeStruct((bsz, SUBLANES, seq), F32)],
        compiler_params=_params(("parallel", "parallel")),
        name="even_in_proj",
    )(x, mod, w_r, w_qkv, w_idx, cos_t, sin_a, sin_b, ikg, ikb)


def _rwkv_kernel(p_ref, pp_ref, mu_ref, vec_ref, w2_ref, a2_ref, g2_ref, ltri_ref, ustr_ref,
                 seg_ref, o_ref, s_ref, obuf_ref):
    i = pl.program_id(1)

    @pl.when(i == 0)
    def _():
        s_ref[...] = jnp.zeros_like(s_ref)

    tt = RW_TILE
    w = RWKV_WIDTH
    hd = HEAD_DIM
    nb = p_ref.shape[0]
    nchunk = tt // RW_CHUNK
    seg = seg_ref[...]
    half_w = seg.shape[0]

    def segsum(t):
        return jnp.concatenate([_dot_split(t[:, c:c + half_w], seg, 2)
                                for c in range(0, w, half_w)], axis=1)

    vec = vec_ref[...]
    w0, a0, k_k, k_a, r_k, gn_g, gn_b = (vec[j:j + 1] for j in range(7))
    rowi = lax.broadcasted_iota(I32, (tt, 1), 0)
    live = (i > 0).astype(F32)

    def prepare(b):
        p = p_ref[b]
        prow = pp_ref[b][SUBLANES - 1:SUBLANES] * live
        xprev = jnp.where(rowi == 0, prow, pltpu.roll(p, 1, 0))
        ps = p + (xprev - p) * mu_ref[...]
        r = ps[:, :w]
        k = ps[:, w:2 * w]
        v = ps[:, 2 * w:3 * w]
        o1 = 3 * w
        wd = ps[:, o1:o1 + RWKV_LORA_W]
        ad = ps[:, o1 + RWKV_LORA_W:o1 + RWKV_LORA_W + RWKV_LORA_A]
        gd = ps[:, o1 + RWKV_LORA_W + RWKV_LORA_A:]
        y = -(w0 + _dot(jnp.tanh(wd), w2_ref[...]))
        softplus = jnp.maximum(y, 0.0) + jnp.log1p(jnp.exp(-jnp.abs(y)))
        logw = -jnp.exp(-softplus - 0.5)
        a = jax.nn.sigmoid(a0 + _dot(ad, a2_ref[...]))
        g = _dot(jax.nn.sigmoid(gd), g2_ref[...])
        kk = k * k_k
        kk = kk / jnp.maximum(jnp.sqrt(segsum(kk * kk)), 1e-12)
        k2 = k * (1.0 + (a - 1.0) * k_a)
        bonus = segsum(r * k2 * r_k) * v
        cum = _dot_split_left(ltri_ref[...], logw, 3)
        rem = _dot_split_left(ustr_ref[...], logw, 3)
        pt = jnp.exp(cum)
        ipt = jnp.exp(-cum)
        erem = jnp.exp(rem)
        kka = kk * a
        return dict(at=-kk * jnp.exp(cum - logw), rt=r * pt, bt=kka * ipt, kt=k2 * ipt,
                    bp=kka * erem, kp=k2 * erem, v=v, pt=pt, bonus=bonus, g=g)

    rows_in = [prepare(b) for b in range(nb)]

    ti = lax.broadcasted_iota(I32, (tt, tt), 0)
    si = lax.broadcasted_iota(I32, (tt, tt), 1)
    same = (ti // RW_CHUNK) == (si // RW_CHUNK)
    strict = same & (si < ti)
    incl = same & (si <= ti)
    tb = lax.broadcasted_iota(I32, (tt, nchunk * hd), 0)
    cb = lax.broadcasted_iota(I32, (tt, nchunk * hd), 1)
    blkmask = (tb // RW_CHUNK) == (cb // hd)
    tile_chunks = lambda t: jnp.where(blkmask, jnp.concatenate([t] * nchunk, axis=1), 0.0)

    units = [(b, slice(h * hd, (h + 1) * hd)) for b in range(nb) for h in range(RWKV_HEADS)]
    idx = range(len(units))
    pick = lambda name: [rows_in[b][name][:, sl] for b, sl in units]
    at_h, rt_h, vh, bt_h, kt_h = pick("at"), pick("rt"), pick("v"), pick("bt"), pick("kt")
    x = [_dot_nt(jnp.concatenate([at_h[u], rt_h[u]], axis=0),
                 jnp.concatenate([bt_h[u], kt_h[u]], axis=0)) for u in idx]
    a_ak = [jnp.where(strict, x[u][:tt, tt:], 0.0) for u in idx]
    a_rb = [jnp.where(incl, x[u][tt:, :tt], 0.0) for u in idx]
    a_rk = [jnp.where(incl, x[u][tt:, tt:], 0.0) for u in idx]
    def bd(m0, m1):
        z0 = jnp.zeros((m0.shape[0], m1.shape[1]), m0.dtype)
        z1 = jnp.zeros((m1.shape[0], m0.shape[1]), m1.dtype)
        return jnp.concatenate([jnp.concatenate([m0, z0], axis=1),
                                jnp.concatenate([z1, m1], axis=1)], axis=0)

    def pair_dot(lhs, rhs):
        out = []
        for u in range(0, len(lhs), 2):
            res = _dot(jnp.concatenate([lhs[u], lhs[u + 1]], axis=1), bd(rhs[u], rhs[u + 1]))
            cut = rhs[u].shape[1]
            out += [res[:, :cut], res[:, cut:]]
        return out

    aak_v = pair_dot(a_ak, vh)
    eye = (ti == si).astype(F32)
    inv = None
    blk = 1
    while blk < RW_CHUNK:
        below = same & ((ti // blk) % 2 == 1) & ((si // blk) % 2 == 0) & (
            (ti // (2 * blk)) == (si // (2 * blk)))
        a21 = [jnp.where(below, x[u][:tt, :tt], 0.0) for u in idx]
        if inv is None:
            inv = [eye + a21[u] for u in idx]
        else:
            left = pair_dot(a21, inv)
            grow = pair_dot(inv, left)
            inv = [inv[u] + grow[u] for u in idx]
        blk *= 2
    yv = pair_dot(inv, [jnp.concatenate([at_h[u], aak_v[u]], axis=1) for u in idx])
    arb_y = pair_dot(a_rb, yv)
    ark_v = pair_dot(a_rk, vh)
    qt = [rt_h[u] + arb_y[u][:, :hd] for u in idx]
    o0 = [arb_y[u][:, hd:] + ark_v[u] for u in idx]
    yt = [yv[u].T for u in idx]
    bpb = [tile_chunks(t) for t in pick("bp")]
    kpb = [tile_chunks(t) for t in pick("kp")]
    g_all = [_dot(yt[u][:hd], bpb[u]) for u in idx]
    h_all = [_dot(jnp.concatenate([yt[u][hd:], vh[u].T], axis=1),
                  jnp.concatenate([bpb[u], kpb[u]], axis=0)) for u in idx]
    pt_h = pick("pt")
    pairs = range(0, len(units), 2)
    low_lanes = lax.broadcasted_iota(I32, (hd, 2 * hd), 1) < hd
    s = [jnp.concatenate([s_ref[u], s_ref[u + 1]], axis=1) for u in pairs]
    for n in range(nchunk):
        rows = slice(n * RW_CHUNK, (n + 1) * RW_CHUNK)
        cols = slice(n * hd, (n + 1) * hd)
        last = (n + 1) * RW_CHUNK - 1
        for j, u in enumerate(pairs):
            s_bd = jnp.concatenate([jnp.where(low_lanes, s[j], 0.0),
                                    jnp.where(low_lanes, 0.0, s[j])], axis=0)
            o_pair = _dot_nt(jnp.concatenate([qt[u][rows], qt[u + 1][rows]], axis=1), s_bd)
            for i in range(2):
                b, sl = units[u + i]
                obuf_ref[b, rows, sl] = o_pair[:, i * hd:(i + 1) * hd] + o0[u + i][rows]
        s = [s[j] * jnp.concatenate([pt_h[u][last:last + 1], pt_h[u + 1][last:last + 1]], axis=1)
             + _dot(s[j], bd(g_all[u][:, cols], g_all[u + 1][:, cols]))
             + jnp.concatenate([h_all[u][:, cols], h_all[u + 1][:, cols]], axis=1)
             for j, u in enumerate(pairs)]
    for j, u in enumerate(pairs):
        s_ref[u] = s[j][:, :hd]
        s_ref[u + 1] = s[j][:, hd:]

    for b in range(nb):
        o = obuf_ref[b]
        mean = segsum(o) * (1.0 / hd)
        oc = o - mean
        var = segsum(oc * oc) * (1.0 / hd)
        on = oc * lax.rsqrt(var + GN_EPS) * gn_g + gn_b
        o_ref[b] = ((on + rows_in[b]["bonus"]) * rows_in[b]["g"]).astype(BF16)


def _rwkv(p_r, mu, w0, w2, a0, a2, g2, k_k, k_a, r_k, gn_g, gn_b):
    bsz, seq, _ = p_r.shape
    tt = RW_TILE
    nb = RW_ROWS if bsz % RW_ROWS == 0 else 1
    w = RWKV_WIDTH
    vec = jnp.stack([w0, a0, k_k, k_a, r_k.reshape(w), gn_g, gn_b, jnp.zeros_like(w0)])
    t_idx = np.arange(tt)
    same = (t_idx[:, None] // RW_CHUNK) == (t_idx[None, :] // RW_CHUNK)
    ltri = jnp.asarray(same & (t_idx[None, :] <= t_idx[:, None]), BF16)
    ustr = jnp.asarray(same & (t_idx[None, :] > t_idx[:, None]), BF16)
    c_idx = np.arange(MXU_TILE)
    seg = jnp.asarray((c_idx[:, None] // HEAD_DIM) == (c_idx[None, :] // HEAD_DIM), BF16)
    full = lambda shape: pl.BlockSpec(shape, lambda b, i: (0,) * len(shape))
    per_tile = tt // SUBLANES
    return pl.pallas_call(
        _rwkv_kernel,
        grid=(bsz // nb, seq // tt),
        in_specs=[pl.BlockSpec((nb, tt, RWKV_COLS), lambda b, i: (b, i, 0)),
                  pl.BlockSpec((nb, SUBLANES, RWKV_COLS),
                               lambda b, i: (b, jnp.maximum(i * per_tile - 1, 0), 0)),
                  full((1, RWKV_COLS)), full((SUBLANES, w)),
                  full(w2.shape), full(a2.shape), full(g2.shape),
                  full((tt, tt)), full((tt, tt)), full((MXU_TILE, MXU_TILE))],
        out_specs=pl.BlockSpec((nb, tt, w), lambda b, i: (b, i, 0)),
        out_shape=jax.ShapeDtypeStruct((bsz, seq, w), BF16),
        scratch_shapes=[pltpu.VMEM((nb * RWKV_HEADS, HEAD_DIM, HEAD_DIM), F32),
                        pltpu.VMEM((nb, tt, w), F32)],
        compiler_params=_params(("parallel", "arbitrary")),
        name="rwkv7_mix",
    )(p_r, p_r, mu.reshape(1, RWKV_COLS), vec, w2.astype(BF16), a2.astype(BF16),
      g2.astype(BF16), ltri, ustr, seg)


def _dsa_kernel(qt_ref, k_ref, vt_ref, qit_ref, ki_ref, wit_ref, low_ref, o_ref,
                key_ref, top_ref, bias_ref, acc_ref, m_ref, l_ref, *, topk):
    qb = pl.program_id(1)
    nq = Q_BLOCK
    kt_sz = bias_ref.shape[0]
    half = kt_sz // 2
    quarter = kt_sz // 4
    eighth = kt_sz // 8
    hd = HEAD_DIM
    start = qb * nq
    nun = (start + nq + quarter - 1) // quarter
    col = lax.broadcasted_iota(I32, (1, nq), 1)
    lim = start + (col // CHUNK + 1) * CHUNK
    wit = wit_ref[0]
    qit = qit_ref[0]
    qi_cat = jnp.concatenate([qit[h * IDX_DIM:(h + 1) * IDX_DIM] for h in range(IDX_HEADS)],
                             axis=1)

    def score_rows(off, rows, masked=True):
        d = jnp.dot(ki_ref[0, pl.ds(off, rows), :], qi_cat, preferred_element_type=F32)
        s = jnp.zeros((rows, nq), F32)
        for h in range(IDX_HEADS):
            s = s + wit[h:h + 1] * jnp.maximum(d[:, h * nq:(h + 1) * nq], 0.0)
        s = s + 0.0
        bits = pltpu.bitcast(s, I32)
        key = bits ^ ((bits >> 31) & 0x7FFFFFFF)
        if masked:
            sidx = off + lax.broadcasted_iota(I32, (rows, nq), 0)
            key = jnp.where(sidx < lim, key, INT_MIN)
        key_ref[pl.ds(off, rows), :] = key
        field = (key >> (32 - TOP_BITS)) + (3 << (TOP_BITS - 1))
        for c0 in range(0, rows, quarter):
            top_ref[pl.ds(pl.multiple_of((off + c0) // 2, eighth), eighth), :] = (
                (field[c0:c0 + eighth] << 16) | field[c0 + eighth:c0 + quarter])

    def over_keys(step, init, first=0):
        whole = nun // 4
        c = lax.fori_loop(
            first, whole, lambda j, c: step(pl.multiple_of(j * kt_sz, kt_sz), kt_sz, c), init)
        c = lax.cond(nun % 4 >= 2,
                     lambda c: step(pl.multiple_of(whole * kt_sz, kt_sz), half, c),
                     lambda c: c, c)
        return lax.cond(nun % 2 == 1,
                        lambda c: step(pl.multiple_of((nun - 1) * quarter, quarter), quarter, c),
                        lambda c: c, c)

    def score_step(off, rows, carry):
        score_rows(off, rows)
        return carry

    def open_tile(j, carry):
        score_rows(pl.multiple_of(j * kt_sz, kt_sz), kt_sz, masked=False)
        return carry

    n_open = (start + CHUNK) // kt_sz
    lax.fori_loop(0, n_open, open_tile, 0)
    over_keys(score_step, jnp.int32(0), first=n_open)

    def count(pred):
        def step(off, rows, c):
            hit = jnp.where(pred(key_ref[pl.ds(off, rows), :]), 1, 0)
            return c + hit.reshape(rows // RED_ROWS, RED_ROWS, nq).sum(axis=0)
        return jnp.sum(over_keys(step, jnp.zeros((RED_ROWS, nq), I32)), axis=0, keepdims=True)

    def count_top(cand):
        both = (cand << 16) | cand

        def step(off, rows, c):
            words = top_ref[pl.ds(pl.multiple_of(off // 2, eighth), rows // 2), :]
            hit = lax.shift_right_logical(words - both, 15) & 0x00010001
            return c + hit.reshape(rows // 2 // RED_ROWS, RED_ROWS, nq).sum(axis=0)

        c = jnp.sum(over_keys(step, jnp.zeros((RED_ROWS, nq), I32)), axis=0, keepdims=True)
        return (c & 0xFFFF) + lax.shift_right_logical(c, 16)

    def top_step(b, prefix):
        cand = prefix | lax.shift_left(jnp.int32(1), TOP_BITS - 1 - b)
        return jnp.where(count_top(cand) >= topk, cand, prefix)

    prefix = lax.fori_loop(0, TOP_BITS, top_step, jnp.zeros((1, nq), I32))
    lo = (prefix - (1 << (TOP_BITS - 1))) << (32 - TOP_BITS)

    def bit_step(b, lo):
        cand = lo + lax.shift_left(jnp.int32(1), 31 - TOP_BITS - b)
        return jnp.where(count(lambda kv: kv >= cand) >= topk, cand, lo)

    th = lax.fori_loop(0, 32 - TOP_BITS, bit_step, lo)
    need = jnp.where(th == INT_MIN, 0, topk - count(lambda kv: kv > th)).astype(F32)

    acc_ref[...] = jnp.zeros_like(acc_ref)
    m_ref[...] = jnp.full_like(m_ref, NEG_INF)
    l_ref[...] = jnp.zeros_like(l_ref)
    low = low_ref[...]
    npair = DSA_HEADS // 2
    top_rows = lax.broadcasted_iota(I32, (2 * hd, nq), 0) < hd
    qt = qt_ref[0].astype(F32)
    q_bd = []
    for p in range(npair):
        qp = qt[p * 2 * hd:(p + 1) * 2 * hd]
        q_bd.append(jnp.concatenate([jnp.where(top_rows, qp, 0.0), jnp.where(top_rows, 0.0, qp)],
                                    axis=1).astype(BF16))

    pad_rows = 2 * SUBLANES

    def attn_tile(off, rows, eq_before):
        ones_rows = (lax.broadcasted_iota(I32, (pad_rows, rows), 0) == 0).astype(BF16)
        eq_run = eq_before
        tb = min(TRI_TILE, rows)
        for t in range(rows // tb):
            kv = key_ref[pl.ds(off + t * tb, tb), :]
            eq = kv == th
            pre = eq_run + jnp.dot(low[:tb, :tb], jnp.where(eq, 1.0, 0.0).astype(BF16),
                                   preferred_element_type=F32)
            sel = (kv > th) | (eq & (pre <= need))
            bias_ref[t * tb:(t + 1) * tb, :] = jnp.where(sel, 0.0, NEG_INF)
            eq_run = pre[tb - 1:tb]
        bias = bias_ref[:rows]

        def logits(p):
            part = rows // 2
            return jnp.concatenate(
                [jnp.dot(k_ref[0, pl.ds(off + r * part, part), p * 2 * hd:(p + 1) * 2 * hd],
                         q_bd[p], preferred_element_type=F32) for r in range(2)], axis=0)

        st_next = logits(0)
        for p in range(npair):
            cols = slice(p * 2 * hd, (p + 1) * 2 * hd)
            st = st_next
            if p + 1 < npair:
                st_next = logits(p + 1)
            prs, scales = [], []
            for i in range(2):
                h = 2 * p + i
                s = st[:, i * nq:(i + 1) * nq] + bias
                m_old = m_ref[h:h + 1]
                m_new = jnp.maximum(m_old, _fold_rows(s, jnp.max))
                prs.append(jnp.exp2(s - m_new).astype(BF16))
                scales.append(jnp.exp2(m_old - m_new))
                m_ref[h:h + 1] = m_new
            lhs = jnp.concatenate([vt_ref[0, cols, pl.ds(off, rows)], ones_rows], axis=0)
            pv = jnp.dot(lhs, jnp.concatenate(prs, axis=1),
                         preferred_element_type=F32)
            for i in range(2):
                h = 2 * p + i
                l_ref[h:h + 1] = (scales[i] * l_ref[h:h + 1]
                                  + pv[2 * hd:2 * hd + 1, i * nq:(i + 1) * nq])
            new = jnp.where(top_rows, pv[:2 * hd, :nq], pv[:2 * hd, nq:])
            acc_ref[p] = jnp.where(top_rows, scales[0], scales[1]) * acc_ref[p] + new
        return eq_run

    over_keys(attn_tile, jnp.zeros((1, nq), F32))
    for p in range(npair):
        inv = jnp.where(top_rows, 1.0 / l_ref[2 * p:2 * p + 1], 1.0 / l_ref[2 * p + 1:2 * p + 2])
        o_ref[0, :, p * 2 * hd:(p + 1) * 2 * hd] = (acc_ref[p] * inv).T.astype(BF16)


def _dsa(q_t, k, v_t, qi_t, ki, wi_t):
    bsz, seq, w = k.shape
    nq = Q_BLOCK
    topk = min(INDEX_TOPK, seq // 4)
    kt_sz = min(KEY_TILE, seq)
    assert kt_sz % (8 * RED_ROWS) == 0 and seq // 2 < 2 ** 16
    t_idx = np.arange(TRI_TILE)
    low = jnp.asarray(t_idx[None, :] <= t_idx[:, None], BF16)
    cols = lambda n: pl.BlockSpec((1, n, nq), lambda b, i: (b, 0, i))
    return pl.pallas_call(
        functools.partial(_dsa_kernel, topk=topk),
        grid=(bsz, seq // nq),
        in_specs=[cols(w),
                  pl.BlockSpec((1, seq, w), lambda b, i: (b, 0, 0)),
                  pl.BlockSpec((1, w, seq), lambda b, i: (b, 0, 0)),
                  cols(IDX_HEADS * IDX_DIM),
                  pl.BlockSpec((1, seq, IDX_DIM), lambda b, i: (b, 0, 0)),
                  cols(SUBLANES),
                  pl.BlockSpec((TRI_TILE, TRI_TILE), lambda b, i: (0, 0))],
        out_specs=pl.BlockSpec((1, nq, w), lambda b, i: (b, i, 0)),
        out_shape=jax.ShapeDtypeStruct((bsz, seq, w), BF16),
        scratch_shapes=[pltpu.VMEM((seq, nq), I32),
                        pltpu.VMEM((seq // 2, nq), I32),
                        pltpu.VMEM((kt_sz, nq), F32),
                        pltpu.VMEM((DSA_HEADS // 2, 2 * HEAD_DIM, nq), F32),
                        pltpu.VMEM((DSA_HEADS, nq), F32),
                        pltpu.VMEM((DSA_HEADS, nq), F32)],
        compiler_params=_params(("parallel", "arbitrary")),
        name="dsa_mix",
    )(q_t, k, v_t, qi_t, ki, wi_t, low)


POOL_HALO = 16


def _odd_layer_kernel(x_ref, xp_ref, mod_ref, win_ref, pw_ref, ps_ref, lng_ref, lnb_ref, ws_ref,
                      bs_ref, wout_ref, g_ref, b_ref, o_ref, y_ref):
    i = pl.program_id(1)
    m = mod_ref[0]
    sc = 1.0 + m[1:2]
    sh = m[0:1]
    x = x_ref[0]
    tm = x.shape[0]
    gd = POOL_GROUP_DIM
    p = jnp.dot((x * sc + sh).astype(BF16), win_ref[...], preferred_element_type=F32)
    prev = jnp.dot((xp_ref[0] * sc + sh).astype(BF16), win_ref[:, :POOL_WIDTH],
                   preferred_element_type=F32) * (i > 0).astype(F32)
    t_glob = (i * tm + lax.broadcasted_iota(I32, (tm, 1), 0)).astype(F32)
    scale = ps_ref[...]
    for gi, win in enumerate(POOL_WINDOWS):
        cols = slice(gi * gd, (gi + 1) * gd)
        xg = p[:, cols]
        s = jnp.concatenate([prev[:, cols], xg], axis=0)
        span = 1
        while span < win:
            s = s[span:] + s[:-span]
            span *= 2
        first = POOL_HALO + 1 - win
        pooled = s[first:first + tm] / jnp.minimum(t_glob + 1.0, float(win)) - xg
        y_ref[:, cols] = (_dot(pooled, pw_ref[gi]) * scale[:, cols]).astype(BF16)

    u = _gelu(p[:, POOL_WIDTH:POOL_WIDTH + SG_WIDTH])
    v = _ln(_gelu(p[:, POOL_WIDTH + SG_WIDTH:]), lng_ref[...], lnb_ref[...], LN_EPS)
    ti = lax.broadcasted_iota(I32, (SG_CHUNK, SG_CHUNK), 0)
    si = lax.broadcasted_iota(I32, (SG_CHUNK, SG_CHUNK), 1)
    bs = bs_ref[...]
    for gi in range(SG_GROUPS):
        cols = slice(gi * SG_GROUP_DIM, (gi + 1) * SG_GROUP_DIM)
        ws = jnp.where(si <= ti, ws_ref[gi], 0.0)
        for n in range(tm // SG_CHUNK):
            rows = slice(n * SG_CHUNK, (n + 1) * SG_CHUNK)
            z = _dot(ws, v[rows, cols]) + bs[:, gi:gi + 1]
            y_ref[rows, POOL_WIDTH + gi * SG_GROUP_DIM:POOL_WIDTH + (gi + 1) * SG_GROUP_DIM] = (
                u[rows, cols] * z).astype(BF16)

    y = jnp.dot(y_ref[...], wout_ref[...], preferred_element_type=F32)
    o_ref[0] = _ln(ALPHA * x + m[2:3] * y, g_ref[...], b_ref[...], LN_EPS)


def _odd_layer(x, mod, w_in, pool_w, pool_scale, sg_ln_g, sg_ln_b, sg_w, sg_b, w_out, ln_g, ln_b):
    bsz, seq, d = x.shape
    tm = ODD_ROWS
    per_tile = tm // POOL_HALO
    full = lambda shape: pl.BlockSpec(shape, lambda b, i: (0,) * len(shape))
    once = lambda shape: pl.BlockSpec(shape, lambda b, i: (0,) * len(shape),
                                      pipeline_mode=pl.Buffered(1))
    return pl.pallas_call(
        _odd_layer_kernel,
        grid=(bsz, seq // tm),
        in_specs=[pl.BlockSpec((1, tm, d), lambda b, i: (b, i, 0)),
                  pl.BlockSpec((1, POOL_HALO, d),
                               lambda b, i: (b, jnp.maximum(i * per_tile - 1, 0), 0)),
                  pl.BlockSpec((1, 6, d), lambda b, i: (b, 0, 0)),
                  once(w_in.shape), full(pool_w.shape), full((1, POOL_WIDTH)),
                  full((1, SG_WIDTH)), full((1, SG_WIDTH)), full(sg_w.shape),
                  full((SG_CHUNK, SG_GROUPS)), once(w_out.shape), full((1, d)), full((1, d))],
        out_specs=pl.BlockSpec((1, tm, d), lambda b, i: (b, i, 0)),
        out_shape=jax.ShapeDtypeStruct((bsz, seq, d), F32),
        scratch_shapes=[pltpu.VMEM((tm, d), BF16)],
        compiler_params=_params(("parallel", "parallel")),
        name="odd_layer_mix",
    )(x, x, mod, w_in.astype(BF16), pool_w.astype(BF16), pool_scale.reshape(1, -1),
      sg_ln_g.reshape(1, -1), sg_ln_b.reshape(1, -1), sg_w, sg_b.T, w_out.astype(BF16),
      ln_g.reshape(1, d), ln_b.reshape(1, d))


def _proj_ln_kernel(*refs, n_in, gate_row):
    a_refs = refs[:n_in]
    w_refs = refs[n_in:2 * n_in]
    x_ref, mod_ref, g_ref, b_ref, o_ref = refs[2 * n_in:]
    y = None
    for a_ref, w_ref in zip(a_refs, w_refs):
        part = jnp.dot(a_ref[0], w_ref[...], preferred_element_type=F32)
        y = part if y is None else y + part
    gate = mod_ref[0][gate_row:gate_row + 1]
    o_ref[0] = _ln(ALPHA * x_ref[0] + gate * y, g_ref[...], b_ref[...], LN_EPS)


def _proj_ln(acts, weights, x, mod, gate_row, ln_g, ln_b, name):
    bsz, seq, d = x.shape
    tm = OUT_PROJ_ROWS
    n_in = len(acts)
    in_specs = [pl.BlockSpec((1, tm, a.shape[2]), lambda b, i: (b, i, 0)) for a in acts]
    in_specs += [pl.BlockSpec(w.shape, lambda b, i: (0, 0)) for w in weights]
    in_specs += [pl.BlockSpec((1, tm, d), lambda b, i: (b, i, 0)),
                 pl.BlockSpec((1, 6, d), lambda b, i: (b, 0, 0)),
                 pl.BlockSpec((1, d), lambda b, i: (0, 0)),
                 pl.BlockSpec((1, d), lambda b, i: (0, 0))]
    return pl.pallas_call(
        functools.partial(_proj_ln_kernel, n_in=n_in, gate_row=gate_row),
        grid=(bsz, seq // tm),
        in_specs=in_specs,
        out_specs=pl.BlockSpec((1, tm, d), lambda b, i: (b, i, 0)),
        out_shape=jax.ShapeDtypeStruct((bsz, seq, d), F32),
        compiler_params=_params(("parallel", "parallel")),
        name=name,
    )(*acts, *[w.astype(BF16) for w in weights], x, mod, ln_g.reshape(1, d), ln_b.reshape(1, d))


def _ffn_kernel(x_ref, xp_ref, mod_ref, wg_ref, wv_ref, cwg_ref, cwv_ref, cbg_ref, cbv_ref,
                wd_ref, g_ref, b_ref, o_ref):
    i = pl.program_id(1)
    m = mod_ref[0]
    sc = 1.0 + m[4:5]
    sh = m[3:4]
    x = x_ref[0]
    live = (i > 0).astype(F32)
    h = jnp.concatenate([(xp_ref[0] * sc + sh) * live, x * sc + sh], axis=0).astype(BF16)
    halo = SUBLANES

    def conv(w_ref, cw_ref, cb_ref):
        u = jnp.dot(h, w_ref[...], preferred_element_type=F32)
        cw = cw_ref[...]
        u1 = pltpu.roll(u, 1, 0)
        u2 = pltpu.roll(u, 2, 0)
        return (cb_ref[...] + u2[halo:] * cw[0:1] + u1[halo:] * cw[1:2] + u[halo:] * cw[2:3])

    gate = conv(wg_ref, cwg_ref, cbg_ref)
    val = conv(wv_ref, cwv_ref, cbv_ref)
    act = (gate * jax.nn.sigmoid(gate) * val).astype(BF16)
    y = jnp.dot(act, wd_ref[...], preferred_element_type=F32)
    o_ref[0] = _ln(ALPHA * x + m[5:6] * y, g_ref[...], b_ref[...], LN_EPS)


def _conv_ffn(x, mod, w_up, conv_w, conv_b, w_down, ln_g, ln_b):
    bsz, seq, d = x.shape
    tm = FFN_ROWS
    per_tile = tm // SUBLANES
    wb = w_up.astype(BF16)
    cb = conv_b.reshape(1, -1)
    once = dict(pipeline_mode=pl.Buffered(1))
    return pl.pallas_call(
        _ffn_kernel,
        grid=(bsz, seq // tm),
        in_specs=[pl.BlockSpec((1, tm, d), lambda b, i: (b, i, 0)),
                  pl.BlockSpec((1, SUBLANES, d),
                               lambda b, i: (b, jnp.maximum(i * per_tile - 1, 0), 0)),
                  pl.BlockSpec((1, 6, d), lambda b, i: (b, 0, 0)),
                  pl.BlockSpec((d, D_FF), lambda b, i: (0, 0), **once),
                  pl.BlockSpec((d, D_FF), lambda b, i: (0, 1), **once),
                  pl.BlockSpec((3, D_FF), lambda b, i: (0, 0), **once),
                  pl.BlockSpec((3, D_FF), lambda b, i: (0, 1), **once),
                  pl.BlockSpec((1, D_FF), lambda b, i: (0, 0), **once),
                  pl.BlockSpec((1, D_FF), lambda b, i: (0, 1), **once),
                  pl.BlockSpec((D_FF, d), lambda b, i: (0, 0), **once),
                  pl.BlockSpec((1, d), lambda b, i: (0, 0)),
                  pl.BlockSpec((1, d), lambda b, i: (0, 0))],
        out_specs=pl.BlockSpec((1, tm, d), lambda b, i: (b, i, 0)),
        out_shape=jax.ShapeDtypeStruct((bsz, seq, d), F32),
        compiler_params=_params(("parallel", "parallel")),
        name="conv_ffn_ln",
    )(x, x, mod, wb, wb, conv_w, conv_w, cb, cb, w_down.astype(BF16),
      ln_g.reshape(1, d), ln_b.reshape(1, d))


def _rope_tables(seq):
    inv = ROPE_THETA ** (-jnp.arange(0, ROPE_DIM, 2, dtype=F32) / ROPE_DIM)
    ang = jnp.arange(seq, dtype=F32)[:, None] * inv[None, :]
    cos, sin = jnp.cos(ang), jnp.sin(ang)
    half = ROPE_DIM // 2
    rest = HEAD_DIM - ROPE_DIM
    one = jnp.ones((seq, rest), F32)
    zero = jnp.zeros((seq, rest), F32)
    zh = jnp.zeros((seq, half), F32)
    head = lambda *parts: jnp.concatenate(parts * (LANES // HEAD_DIM), axis=1)
    return head(cos, cos, one), head(-sin, zh, zero), head(zh, sin, zero)


def kernel(x, c, ada_w, ada_b, ln_g, ln_b, ffn_w_up, ffn_conv_w, ffn_conv_b, ffn_w_down, ev_w_in, ev_w_out, rw_mu, rw_w0, rw_w2, rw_a0, rw_a2, rw_g2, rw_k_k, rw_k_a, rw_r_k, rw_gn_g, rw_gn_b, ik_ln_g, ik_ln_b, od_w_in, od_w_out, pool_w, pool_scale, sg_ln_g, sg_ln_b, sg_w, sg_b):
    seq = x.shape[1]
    tables = _rope_tables(seq)
    mods = _modulation(c, ada_w, ada_b)
    ev_w = _split_even_weights(ev_w_in)
    for layer in range(DEPTH):
        mod = mods[layer]
        if layer % 2 == 0:
            e = layer // 2
            p_r, q_t, k, v_t, qi_t, ki, wi_t = _even_in_proj(x, mod, [w[e] for w in ev_w],
                                                             ik_ln_g[e], ik_ln_b[e], tables)
            ya = _rwkv(p_r, rw_mu[e], rw_w0[e], rw_w2[e], rw_a0[e], rw_a2[e], rw_g2[e],
                       rw_k_k[e], rw_k_a[e], rw_r_k[e], rw_gn_g[e], rw_gn_b[e])
            yb = _dsa(q_t, k, v_t, qi_t, ki, wi_t)
            w_out = ev_w_out[e]
            x = _proj_ln([ya, yb], [w_out[:RWKV_WIDTH], w_out[RWKV_WIDTH:]], x, mod, 2,
                         ln_g[layer, 0], ln_b[layer, 0], "even_out_proj_ln")
        else:
            o = layer // 2
            x = _odd_layer(x, mod, od_w_in[o], pool_w[o], pool_scale[o], sg_ln_g[o], sg_ln_b[o],
                           sg_w[o], sg_b[o], od_w_out[o], ln_g[layer, 0], ln_b[layer, 0])
        x = _conv_ffn(x, mod, ffn_w_up[layer], ffn_conv_w[layer], ffn_conv_b[layer],
                      ffn_w_down[layer], ln_g[layer, 1], ln_b[layer, 1])
    return x
```
